```python
import math
import jax, jax.numpy as jnp
from jax import lax
import numpy as np

D_MODEL = 1024
BATCH = 16
SEQ = 2048
DEPTH = 1

HEAD_DIM = 64
NSA_HEADS = 8
NSA_KV_GROUPS = 2
NSA_Q_PER_GROUP = NSA_HEADS // NSA_KV_GROUPS
CMP_BLOCK = 32
CMP_STRIDE = 16
CMP_HIDDEN = 128
SLC_BLOCK = 64
SLC_TOPK = 8
WINDOW = 512
FOX_HEADS = 8
Q_BLOCK = 128
D_FF = 2816
N_BUCKETS = 32
MAX_DISTANCE = 128
RMS_EPS = 1e-6
NEG_INF = -1.0e30
FORCE_BONUS = 1.0e4

NSA_W = NSA_HEADS * HEAD_DIM
NSA_KV_W = NSA_KV_GROUPS * HEAD_DIM
FOX_W = FOX_HEADS * HEAD_DIM
IN_SPLITS = (NSA_W, NSA_KV_W, NSA_KV_W, NSA_KV_W, NSA_KV_W, NSA_KV_W, NSA_KV_W, 3 * NSA_HEADS,
             FOX_W, FOX_W, FOX_W, FOX_HEADS, 2 * D_MODEL)
D_IN = sum(IN_SPLITS)

kernel_name = 'hybrid_nsa_fox_macaron'


def _rms(x, g):
    xf = x.astype(jnp.float32)
    y = xf * lax.rsqrt(jnp.mean(xf * xf, axis=-1, keepdims=True) + RMS_EPS)
    return (y * g.astype(jnp.float32)).astype(x.dtype)


def _swiglu(x, w_up, w_down):
    gate, up = jnp.split(x @ w_up, 2, axis=-1)
    return (jax.nn.silu(gate) * up) @ w_down


def _t5_bucket(dist):
    n = jnp.maximum(dist, 0)
    max_exact = N_BUCKETS // 2
    nf = jnp.maximum(n, 1).astype(jnp.float32)
    large = max_exact + (jnp.log(nf / max_exact) / math.log(MAX_DISTANCE / max_exact)
                         * (N_BUCKETS - max_exact)).astype(jnp.int32)
    large = jnp.minimum(large, N_BUCKETS - 1)
    return jnp.where(n < max_exact, n, large)


def _masked_softmax(logits, mask):
    p = jax.nn.softmax(jnp.where(mask, logits, NEG_INF), axis=-1)
    return p * mask


def _compress(kv, pos, w1, w2):
    B, S, G, dh = kv.shape
    nc = (S - CMP_BLOCK) // CMP_STRIDE + 1
    idx = np.arange(nc)[:, None] * CMP_STRIDE + np.arange(CMP_BLOCK)[None, :]
    blocks = kv[:, idx] + pos[None, None, :, None, :]
    blocks = blocks.transpose(0, 3, 1, 2, 4).reshape(B, G, nc, CMP_BLOCK * dh)
    return jax.nn.silu(blocks @ w1) @ w2


def _hybrid_layer(x, ffn1_norm, ffn1_w_up, ffn1_w_down, mix_norm, w_in, b_forget, nsa_q_gain, nsa_k_gain,
                  fox_q_gain, fox_k_gain, cmp_pos_k, cmp_pos_v, cmp_k_w1, cmp_k_w2, cmp_v_w1, cmp_v_w2,
                  w_o_nsa, w_o_fox, w_out, ffn2_norm, ffn2_w_up, ffn2_w_down, rel_bias_table):
    B, S, D = x.shape
    G, R, H, dh, HB = NSA_KV_GROUPS, NSA_Q_PER_GROUP, NSA_HEADS, HEAD_DIM, FOX_HEADS
    nq = S // Q_BLOCK
    scale = 1.0 / math.sqrt(dh)
    f32 = jnp.float32
    t_pos = jnp.arange(S)
    starts = jnp.arange(nq) * Q_BLOCK

    x = x + 0.5 * _swiglu(_rms(x, ffn1_norm), ffn1_w_up, ffn1_w_down)

    u = _rms(x, mix_norm)
    split_pts = np.cumsum(np.array(IN_SPLITS))[:-1].tolist()
    qa, kc, vc, ks, vs, kw, vw, ga, qb, kb, vb, fb, gm = jnp.split(u @ w_in, split_pts, axis=-1)

    qa = _rms(qa.reshape(B, S, H, dh), nsa_q_gain) * scale
    qa = qa.reshape(B, S, G, R, dh).transpose(0, 2, 3, 1, 4)
    kc = _rms(_compress(kc.reshape(B, S, G, dh), cmp_pos_k, cmp_k_w1, cmp_k_w2), nsa_k_gain[0])
    vc = _compress(vc.reshape(B, S, G, dh), cmp_pos_v, cmp_v_w1, cmp_v_w2)
    ks = _rms(ks.reshape(B, S, G, dh), nsa_k_gain[1]).transpose(0, 2, 1, 3)
    vs = vs.reshape(B, S, G, dh).transpose(0, 2, 1, 3)
    kw = _rms(kw.reshape(B, S, G, dh), nsa_k_gain[2]).transpose(0, 2, 1, 3)
    vw = vw.reshape(B, S, G, dh).transpose(0, 2, 1, 3)

    dist_bias = rel_bias_table[_t5_bucket(t_pos)].T.reshape(G, R, S).astype(f32)

    nc = kc.shape[2]
    cmp_end = jnp.arange(nc) * CMP_STRIDE + CMP_BLOCK - 1
    dist_c = t_pos[:, None] - cmp_end[None, :]
    logit_c = (jnp.einsum('bgrsd,bgcd->bgrsc', qa, kc).astype(f32)
               + dist_bias[:, :, jnp.clip(dist_c, 0, S - 1)])
    p_c = _masked_softmax(logit_c, dist_c >= 0)
    o_cmp = jnp.einsum('bgrsc,bgcd->bgrsd', p_c.astype(vc.dtype), vc)

    ns = S // SLC_BLOCK
    ci = np.arange(nc)[:, None] * CMP_STRIDE
    sj = np.arange(ns)[None, :] * SLC_BLOCK
    overlap = ((ci <= sj + SLC_BLOCK - 1) & (ci + CMP_BLOCK - 1 >= sj)).astype(np.float32)
    imp = jnp.einsum('bgrsc,cj->bgsj', p_c, jnp.asarray(overlap))
    cur = t_pos // SLC_BLOCK
    blk = jnp.arange(ns)
    forced = (blk[None, :] == 0) | (blk[None, :] == cur[:, None]) | (blk[None, :] == cur[:, None] - 1)
    imp = jnp.where(blk[None, :] <= cur[:, None], imp + FORCE_BONUS * forced, NEG_INF)
    n_sel = min(SLC_TOPK, ns)
    _, sel_idx = lax.top_k(imp, n_sel)

    ks_blk = ks.reshape(B, G, ns, SLC_BLOCK, dh)
    vs_blk = vs.reshape(B, G, ns, SLC_BLOCK, dh)
    kw_pad = jnp.pad(kw, ((0, 0), (0, 0), (WINDOW, 0), (0, 0)))
    vw_pad = jnp.pad(vw, ((0, 0), (0, 0), (WINDOW, 0), (0, 0)))
    q_chunks = qa.reshape(B, G, R, nq, Q_BLOCK, dh).transpose(3, 0, 1, 2, 4, 5)
    idx_chunks = sel_idx.reshape(B, G, nq, Q_BLOCK, n_sel).transpose(2, 0, 1, 3, 4)
    bi = jnp.arange(B)[:, None, None, None]
    gi = jnp.arange(G)[None, :, None, None]
    n_keys = n_sel * SLC_BLOCK

    def nsa_block(args):
        q, idx, start = args
        t = start + jnp.arange(Q_BLOCK)
        k_sel = ks_blk[bi, gi, idx]
        v_sel = vs_blk[bi, gi, idx]
        tok = idx[..., None] * SLC_BLOCK + jnp.arange(SLC_BLOCK)
        dist = t[None, None, :, None, None] - tok
        bias = jax.vmap(lambda tab, d: tab[:, d], in_axes=(0, 1), out_axes=0)(
            dist_bias, jnp.clip(dist, 0, S - 1))
        bias = bias.transpose(2, 0, 1, 3, 4, 5)
        logit = jnp.einsum('bgrtd,bgtnld->bgrtnl', q, k_sel).astype(f32) + bias
        logit = logit.reshape(B, G, R, Q_BLOCK, n_keys)
        mask = (dist >= 0).reshape(B, G, 1, Q_BLOCK, n_keys)
        p = _masked_softmax(logit, mask)
        o_s = jnp.einsum('bgrtk,bgtkd->bgrtd', p.astype(v_sel.dtype),
                         v_sel.reshape(B, G, Q_BLOCK, n_keys, dh))
        k_win = lax.dynamic_slice_in_dim(kw_pad, start, WINDOW + Q_BLOCK, axis=2)
        v_win = lax.dynamic_slice_in_dim(vw_pad, start, WINDOW + Q_BLOCK, axis=2)
        s_pos = start - WINDOW + jnp.arange(WINDOW + Q_BLOCK)
        dist_w = t[:, None] - s_pos[None, :]
        mask_w = (dist_w >= 0) & (dist_w < WINDOW) & (s_pos[None, :] >= 0)
        logit_w = (jnp.einsum('bgrtd,bgsd->bgrts', q, k_win).astype(f32)
                   + dist_bias[:, :, jnp.clip(dist_w, 0, S - 1)])
        p_w = _masked_softmax(logit_w, mask_w)
        o_w = jnp.einsum('bgrts,bgsd->bgrtd', p_w.astype(v_win.dtype), v_win)
        return o_s, o_w

    o_slc, o_win = lax.map(nsa_block, (q_chunks, idx_chunks, starts))
    o_slc = o_slc.transpose(1, 0, 4, 2, 3, 5).reshape(B, S, H, dh)
    o_win = o_win.transpose(1, 0, 4, 2, 3, 5).reshape(B, S, H, dh)
    o_cmp = o_cmp.transpose(0, 3, 1, 2, 4).reshape(B, S, H, dh)
    g_nsa = jax.nn.sigmoid(ga).reshape(B, S, 3, H)[..., None]
    o_nsa = (g_nsa[:, :, 0] * o_cmp + g_nsa[:, :, 1] * o_slc + g_nsa[:, :, 2] * o_win).reshape(B, S, NSA_W)

    qb = (_rms(qb.reshape(B, S, HB, dh), fox_q_gain) * scale).transpose(0, 2, 1, 3)
    kb = _rms(kb.reshape(B, S, HB, dh), fox_k_gain).transpose(0, 2, 1, 3)
    vb = vb.reshape(B, S, HB, dh).transpose(0, 2, 1, 3)
    log_f = jax.nn.log_sigmoid((fb + b_forget).astype(f32))
    cum = jnp.cumsum(log_f, axis=1).transpose(0, 2, 1)
    qb_chunks = qb.reshape(B, HB, nq, Q_BLOCK, dh).transpose(2, 0, 1, 3, 4)
    cum_chunks = cum.reshape(B, HB, nq, Q_BLOCK).transpose(2, 0, 1, 3)

    def fox_block(args):
        q, cq, start = args
        t = start + jnp.arange(Q_BLOCK)
        logit = (jnp.einsum('bhtd,bhsd->bhts', q, kb).astype(f32)
                 + cq[..., None] - cum[:, :, None, :])
        p = _masked_softmax(logit, t_pos[None, :] <= t[:, None])
        return jnp.einsum('bhts,bhsd->bhtd', p.astype(vb.dtype), vb)

    o_fox = lax.map(fox_block, (qb_chunks, cum_chunks, starts))
    o_fox = o_fox.transpose(1, 0, 3, 2, 4).reshape(B, S, FOX_W)

    gate_a, gate_b = jnp.split(jax.nn.sigmoid(gm), 2, axis=-1)
    merged = gate_a * (o_nsa @ w_o_nsa) + gate_b * (o_fox @ w_o_fox)
    x = x + merged @ w_out

    x = x + 0.5 * _swiglu(_rms(x, ffn2_norm), ffn2_w_up, ffn2_w_down)
    return x


def setup_inputs(seed: int = 0) -> dict:
    key = jax.random.key(seed)
    ks = jax.random.split(key, 24)
    f32 = jnp.float32
    L = DEPTH

    def nrm(k, shape, fan_in):
        return jax.random.normal(k, shape, f32) * (fan_in ** -0.5)

    def gain(k, shape):
        return 1.0 + 0.1 * jax.random.normal(k, shape, f32)

    return {
        'x': jax.random.normal(ks[0], (BATCH, SEQ, D_MODEL), f32),
        'ffn1_norm': gain(ks[1], (L, D_MODEL)),
        'ffn1_w_up': nrm(ks[2], (L, D_MODEL, 2 * D_FF), D_MODEL),
        'ffn1_w_down': nrm(ks[3], (L, D_FF, D_MODEL), D_FF),
        'mix_norm': gain(ks[4], (L, D_MODEL)),
        'w_in': nrm(ks[5], (L, D_MODEL, D_IN), D_MODEL),
        'b_forget': jax.random.uniform(ks[6], (L, FOX_HEADS), f32, 1.0, 5.0),
        'nsa_q_gain': gain(ks[7], (L, HEAD_DIM)),
        'nsa_k_gain': gain(ks[8], (L, 3, HEAD_DIM)),
        'fox_q_gain': gain(ks[9], (L, HEAD_DIM)),
        'fox_k_gain': gain(ks[10], (L, HEAD_DIM)),
        'cmp_pos_k': 0.1 * jax.random.normal(ks[11], (L, CMP_BLOCK, HEAD_DIM), f32),
        'cmp_pos_v': 0.1 * jax.random.normal(ks[12], (L, CMP_BLOCK, HEAD_DIM), f32),
        'cmp_k_w1': nrm(ks[13], (L, CMP_BLOCK * HEAD_DIM, CMP_HIDDEN), CMP_BLOCK * HEAD_DIM),
        'cmp_k_w2': nrm(ks[14], (L, CMP_HIDDEN, HEAD_DIM), CMP_HIDDEN),
        'cmp_v_w1': nrm(ks[15], (L, CMP_BLOCK * HEAD_DIM, CMP_HIDDEN), CMP_BLOCK * HEAD_DIM),
        'cmp_v_w2': nrm(ks[16], (L, CMP_HIDDEN, HEAD_DIM), CMP_HIDDEN),
        'w_o_nsa': nrm(ks[17], (L, NSA_W, D_MODEL), NSA_W),
        'w_o_fox': nrm(ks[18], (L, FOX_W, D_MODEL), FOX_W),
        'w_out': nrm(ks[19], (L, D_MODEL, D_MODEL), D_MODEL),
        'ffn2_norm': gain(ks[20], (L, D_MODEL)),
        'ffn2_w_up': nrm(ks[21], (L, D_MODEL, 2 * D_FF), D_MODEL),
        'ffn2_w_down': nrm(ks[22], (L, D_FF, D_MODEL), D_FF),
        'rel_bias_table': 0.5 * jax.random.normal(ks[23], (N_BUCKETS, NSA_HEADS), f32),
    }


def reference(x, ffn1_norm, ffn1_w_up, ffn1_w_down, mix_norm, w_in, b_forget, nsa_q_gain, nsa_k_gain,
              fox_q_gain, fox_k_gain, cmp_pos_k, cmp_pos_v, cmp_k_w1, cmp_k_w2, cmp_v_w1, cmp_v_w2,
              w_o_nsa, w_o_fox, w_out, ffn2_norm, ffn2_w_up, ffn2_w_down, rel_bias_table):
    for layer in range(DEPTH):
        x = _hybrid_layer(x, ffn1_norm[layer], ffn1_w_up[layer], ffn1_w_down[layer], mix_norm[layer],
                          w_in[layer], b_forget[layer], nsa_q_gain[layer], nsa_k_gain[layer],
                          fox_q_gain[layer], fox_k_gain[layer], cmp_pos_k[layer], cmp_pos_v[layer],
                          cmp_k_w1[layer], cmp_k_w2[layer], cmp_v_w1[layer], cmp_v_w2[layer],
                          w_o_nsa[layer], w_o_fox[layer], w_out[layer], ffn2_norm[layer],
                          ffn2_w_up[layer], ffn2_w_down[layer], rel_bias_table)
    return x
```

```python
import functools
import math

import numpy as np
import jax
import jax.numpy as jnp
from jax import lax
from jax.experimental import pallas as pl
from jax.experimental.pallas import tpu as pltpu

F32 = jnp.float32
BF16 = jnp.bfloat16

D_MODEL = 1024
HEAD_DIM = 64
NSA_HEADS = 8
NSA_KV_GROUPS = 2
NSA_Q_PER_GROUP = NSA_HEADS // NSA_KV_GROUPS
CMP_BLOCK = 32
CMP_STRIDE = 16
CMP_HIDDEN = 128
SLC_BLOCK = 64
SLC_TOPK = 8
WINDOW = 512
FOX_HEADS = 8
D_FF = 2816
N_BUCKETS = 32
MAX_DISTANCE = 128
RMS_EPS = 1e-6
NEG_INF = -1.0e30
FORCE_BONUS = 1.0e4

NSA_W = NSA_HEADS * HEAD_DIM
NSA_KV_W = NSA_KV_GROUPS * HEAD_DIM
FOX_W = FOX_HEADS * HEAD_DIM
IN_SPLITS = (NSA_W, NSA_KV_W, NSA_KV_W, NSA_KV_W, NSA_KV_W, NSA_KV_W, NSA_KV_W, 3 * NSA_HEADS,
             FOX_W, FOX_W, FOX_W, FOX_HEADS, 2 * D_MODEL)

LANES = 128
TOKEN_TILE = 512
FFN_CHUNK = D_FF // 2
NSA_TQ = 128
N_WIN_TILES = WINDOW // NSA_TQ + 1
FOX_TQ = 256
VMEM_LIMIT = 56 * 1024 * 1024

PQ_A = 0
PQ_B = PQ_A + NSA_HEADS * LANES
PK_B = PQ_B + FOX_HEADS * LANES
PK_SW = PK_B + FOX_W
P_NORM_END = PK_SW + 2 * NSA_KV_W
PV_B = P_NORM_END
PV_SW = PV_B + FOX_W
PKV_C = PV_SW + 2 * NSA_KV_W
P_SMALL = PKV_C + 2 * NSA_KV_W
P_END = P_SMALL + LANES
NORM_CHUNK = 256


def _dot(a, b):
    return jnp.dot(a, b, preferred_element_type=F32)


def _dot_nt(a, b):
    return lax.dot_general(a, b, (((1,), (1,)), ((), ())), preferred_element_type=F32)


def _split_dot(x, w):
    hi = x.astype(BF16)
    lo = (x - hi.astype(F32)).astype(BF16)
    return _dot(hi, w) + _dot(lo, w)


def _rms_rows(x, gain_row):
    ms = jnp.mean(x * x, axis=-1, keepdims=True)
    return x * lax.rsqrt(ms + RMS_EPS) * gain_row


def _const_spec(shape):
    nd = len(shape)
    return pl.BlockSpec(shape, lambda *_: (0,) * nd, pipeline_mode=pl.Buffered(1))


def _params(n_axes):
    return pltpu.CompilerParams(dimension_semantics=("arbitrary",) * n_axes,
                                vmem_limit_bytes=VMEM_LIMIT)


def _swiglu_residual(x, gain_row, wup_ref, wdn_ref):
    xn = _rms_rows(x, gain_row).astype(BF16)
    acc = jnp.zeros(x.shape, F32)
    for c in range(D_FF // FFN_CHUNK):
        lo = c * FFN_CHUNK
        gate = _dot(xn, wup_ref[:, lo:lo + FFN_CHUNK])
        up = _dot(xn, wup_ref[:, D_FF + lo:D_FF + lo + FFN_CHUNK])
        h = (gate * jax.nn.sigmoid(gate) * up).astype(BF16)
        acc = acc + _dot(h, wdn_ref[lo:lo + FFN_CHUNK, :])
    return x + 0.5 * acc


def _ffn_kernel(x_ref, g_ref, wup_ref, wdn_ref, o_ref):
    o_ref[...] = _swiglu_residual(x_ref[...], g_ref[...], wup_ref, wdn_ref)


def _ffn(x2d, gain, w_up, w_down):
    n = x2d.shape[0]
    row = pl.BlockSpec((TOKEN_TILE, D_MODEL), lambda i: (i, 0))
    return pl.pallas_call(
        _ffn_kernel,
        grid=(n // TOKEN_TILE,),
        in_specs=[row, _const_spec((1, D_MODEL)), _const_spec((D_MODEL, 2 * D_FF)),
                  _const_spec((D_FF, D_MODEL))],
        out_specs=row,
        out_shape=jax.ShapeDtypeStruct((n, D_MODEL), F32),
        compiler_params=_params(1),
        name="ffn",
    )(x2d, gain.reshape(1, D_MODEL), w_up.astype(BF16), w_down.astype(BF16))


def _block_ones(width, size=NORM_CHUNK):
    idx = np.arange(size) // width
    return jnp.asarray((idx[:, None] == idx[None, :]).astype(np.float32), BF16)


def _pack_in_proj(w_in, b_forget, nsa_q_gain, nsa_k_gain, fox_q_gain, fox_k_gain):
    scale = 1.0 / math.sqrt(HEAD_DIM)
    pts = np.cumsum(np.array(IN_SPLITS))[:-1].tolist()
    qa, kc, vc, ks, vs, kw, vw, ga, qb, kb, vb, fb, gm = jnp.split(w_in, pts, axis=1)
    z_w = jnp.zeros((D_MODEL, HEAD_DIM), F32)
    z_g = jnp.zeros((HEAD_DIM,), F32)
    cols, gains = [], []
    for h in range(NSA_HEADS):
        wh = qa[:, h * HEAD_DIM:(h + 1) * HEAD_DIM]
        first = (h // NSA_Q_PER_GROUP) == 0
        cols += [wh, z_w] if first else [z_w, wh]
        gains += [nsa_q_gain * scale, z_g] if first else [z_g, nsa_q_gain * scale]
    for h in range(FOX_HEADS):
        wh = qb[:, h * HEAD_DIM:(h + 1) * HEAD_DIM]
        first = (h % 2) == 0
        cols += [wh, z_w] if first else [z_w, wh]
        gains += [fox_q_gain * scale, z_g] if first else [z_g, fox_q_gain * scale]
    cols += [kb, ks, kw]
    gains += [jnp.tile(fox_k_gain, FOX_HEADS), jnp.tile(nsa_k_gain[1], NSA_KV_GROUPS),
              jnp.tile(nsa_k_gain[2], NSA_KV_GROUPS)]
    n_small_pad = LANES - ga.shape[1] - fb.shape[1]
    cols += [vb, vs, vw, kc, vc, ga, fb, jnp.zeros((D_MODEL, n_small_pad), F32)]
    w_packed = jnp.concatenate(cols, axis=1).astype(BF16)
    gain_row = jnp.concatenate(gains).reshape(1, P_NORM_END)
    small_bias = jnp.concatenate([jnp.zeros((ga.shape[1],), F32), b_forget,
                                  jnp.zeros((n_small_pad,), F32)]).reshape(1, LANES)
    return w_packed, gain_row, small_bias, gm.astype(BF16)


def _in_proj_kernel(x_ref, g_ref, w_ref, gain_ref, sbias_ref, bd128_ref, bd64_ref,
                    qa_ref, qb_ref, kb_ref, ksw_ref, vb_ref, vsw_ref, kc_ref, vc_ref, small_ref):
    u = _rms_rows(x_ref[...], g_ref[...]).astype(BF16)

    def normed(lo, width, bd_ref):
        parts = []
        for c in range(lo, lo + width, NORM_CHUNK):
            y = _dot(u, w_ref[:, c:c + NORM_CHUNK])
            ss = _split_dot(y * y, bd_ref[...])
            parts.append(y * lax.rsqrt(ss * (1.0 / HEAD_DIM) + RMS_EPS) * gain_ref[:, c:c + NORM_CHUNK])
        return parts

    for i, y in enumerate(normed(PQ_A, NSA_HEADS * LANES, bd128_ref)):
        qa_ref[:, i * NORM_CHUNK:(i + 1) * NORM_CHUNK] = y.astype(BF16)
    for i, y in enumerate(normed(PQ_B, FOX_HEADS * LANES, bd128_ref)):
        qb_ref[:, i * NORM_CHUNK:(i + 1) * NORM_CHUNK] = y.astype(BF16)
    for i, y in enumerate(normed(PK_B, FOX_W, bd64_ref)):
        kb_ref[:, i * NORM_CHUNK:(i + 1) * NORM_CHUNK] = y.astype(BF16)
    ksw_ref[...] = normed(PK_SW, 2 * NSA_KV_W, bd64_ref)[0].astype(BF16)
    vb_ref[...] = _dot(u, w_ref[:, PV_B:PV_B + FOX_W]).astype(BF16)
    vsw_ref[...] = _dot(u, w_ref[:, PV_SW:PV_SW + 2 * NSA_KV_W]).astype(BF16)
    kvc = _dot(u, w_ref[:, PKV_C:PKV_C + 2 * NSA_KV_W])
    kc_ref[...] = kvc[:, :NSA_KV_W]
    vc_ref[...] = kvc[:, NSA_KV_W:]
    z = _dot(u, w_ref[:, P_SMALL:P_END]) + sbias_ref[...]
    lane = lax.broadcasted_iota(jnp.int32, z.shape, 1)
    log_sig = jnp.minimum(z, 0.0) - jnp.log1p(jnp.exp(-jnp.abs(z)))
    small_ref[...] = jnp.where(lane < 3 * NSA_HEADS, jax.nn.sigmoid(z), log_sig)


def _in_proj(x2d, mix_norm, w_packed, gain_row, small_bias):
    n = x2d.shape[0]

    def row(width):
        return pl.BlockSpec((TOKEN_TILE, width), lambda i: (i, 0))

    widths = (NSA_HEADS * LANES, FOX_HEADS * LANES, FOX_W, 2 * NSA_KV_W, FOX_W, 2 * NSA_KV_W,
              NSA_KV_W, NSA_KV_W, LANES)
    dtypes = (BF16, BF16, BF16, BF16, BF16, BF16, F32, F32, F32)
    return pl.pallas_call(
        _in_proj_kernel,
        grid=(n // TOKEN_TILE,),
        in_specs=[row(D_MODEL), _const_spec((1, D_MODEL)), _const_spec((D_MODEL, P_END)),
                  _const_spec((1, P_NORM_END)), _const_spec((1, LANES)),
                  _const_spec((NORM_CHUNK, NORM_CHUNK)), _const_spec((NORM_CHUNK, NORM_CHUNK))],
        out_specs=[row(w) for w in widths],
        out_shape=[jax.ShapeDtypeStruct((n, w), d) for w, d in zip(widths, dtypes)],
        compiler_params=_params(1),
        name="in_proj",
    )(x2d, mix_norm.reshape(1, D_MODEL), w_packed, gain_row, small_bias,
      _block_ones(LANES), _block_ones(HEAD_DIM))


def _pack_compress(pos, w1, w2):
    half = CMP_BLOCK // 2
    eye = jnp.eye(NSA_KV_GROUPS, dtype=F32)
    w1r = w1.reshape(CMP_BLOCK, HEAD_DIM, CMP_HIDDEN)

    def big(w):
        return jnp.einsum('ldh,pg->lpdgh', w, eye).reshape(half * NSA_KV_W, NSA_KV_GROUPS * CMP_HIDDEN)

    def posrow(p):
        return jnp.broadcast_to(p[:, None, :], (half, NSA_KV_GROUPS, HEAD_DIM)).reshape(1, half * NSA_KV_W)

    w2big = jnp.einsum('hd,pg->phgd', w2, eye).reshape(NSA_KV_GROUPS * CMP_HIDDEN, NSA_KV_W)
    return (posrow(pos[:half]), posrow(pos[half:]), big(w1r[:half]).astype(BF16),
            big(w1r[half:]).astype(BF16), w2big.astype(BF16))


def _compress_kernel(kc_ref, vc_ref, kp_lo, kp_hi, kw_lo, kw_hi, kw2, vp_lo, vp_hi, vw_lo, vw_hi, vw2,
                     kgain_ref, bd64_ref, ko_ref, vo_ref):
    def mlp(r, p_lo, p_hi, w_lo, w_hi, w2):
        first = _dot((r + p_lo[...]).astype(BF16), w_lo[...])
        second = _dot((r + p_hi[...]).astype(BF16), w_hi[...])
        h = first + pltpu.roll(second, second.shape[0] - 1, axis=0)
        return _dot((h * jax.nn.sigmoid(h)).astype(BF16), w2[...])

    k = mlp(kc_ref[0], kp_lo, kp_hi, kw_lo, kw_hi, kw2)
    ss = _split_dot(k * k, bd64_ref[...])
    ko_ref[0] = (k * lax.rsqrt(ss * (1.0 / HEAD_DIM) + RMS_EPS) * kgain_ref[...]).astype(BF16)
    vo_ref[0] = mlp(vc_ref[0], vp_lo, vp_hi, vw_lo, vw_hi, vw2).astype(BF16)


def _compress(kc, vc, batch, seq, k_pack, v_pack, k_gain):
    rows = seq // (CMP_BLOCK // 2)
    width = (CMP_BLOCK // 2) * NSA_KV_W
    blk = pl.BlockSpec((1, rows, width), lambda b: (b, 0, 0))
    out = pl.BlockSpec((1, rows, NSA_KV_W), lambda b: (b, 0, 0))
    pack_specs = [_const_spec((1, width)), _const_spec((1, width)),
                  _const_spec((width, NSA_KV_GROUPS * CMP_HIDDEN)),
                  _const_spec((width, NSA_KV_GROUPS * CMP_HIDDEN)),
                  _const_spec((NSA_KV_GROUPS * CMP_HIDDEN, NSA_KV_W))]
    bd = _block_ones(HEAD_DIM, NSA_KV_W)
    return pl.pallas_call(
        _compress_kernel,
        grid=(batch,),
        in_specs=[blk, blk] + pack_specs + pack_specs + [_const_spec((1, NSA_KV_W)),
                                                         _const_spec((NSA_KV_W, NSA_KV_W))],
        out_specs=[out, out],
        out_shape=[jax.ShapeDtypeStruct((batch, rows, NSA_KV_W), BF16)] * 2,
        compiler_params=_params(1),
        name="compress",
    )(kc.reshape(batch, rows, width), vc.reshape(batch, rows, width), *k_pack, *v_pack,
      jnp.tile(k_gain, NSA_KV_GROUPS).reshape(1, NSA_KV_W), bd)


def _cumsum_kernel(x_ref, o_ref):
    x = x_ref[0]
    lane = lax.broadcasted_iota(jnp.int32, x.shape, 1)
    k = 1
    while k < x.shape[1]:
        x = x + jnp.where(lane >= k, pltpu.roll(x, k, axis=1), 0.0)
        k *= 2
    o_ref[0] = x


def _cumsum(logf_t):
    b, h, s = logf_t.shape
    spec = pl.BlockSpec((1, h, s), lambda i: (i, 0, 0))
    return pl.pallas_call(
        _cumsum_kernel, grid=(b,), in_specs=[spec], out_specs=spec,
        out_shape=jax.ShapeDtypeStruct((b, h, s), F32), compiler_params=_params(1), name="cumsum",
    )(logf_t)


def _bias_tables_kernel(tab_ref, o_ref):
    s = pl.program_id(0)
    tq = NSA_TQ
    i = lax.broadcasted_iota(jnp.int32, (tq, LANES), 0)
    j = lax.broadcasted_iota(jnp.int32, (tq, LANES), 1)
    toep = s < N_WIN_TILES
    n_cmp_valid = LANES - 1
    d = jnp.where(toep, s * tq + i - j,
                  (s - N_WIN_TILES) * tq + i - CMP_STRIDE * j - (CMP_BLOCK - 1))
    d_hi = jnp.where(toep, WINDOW, jnp.int32(1 << 30))
    j_hi = jnp.where(toep, LANES, n_cmp_valid)
    valid = (d >= 0) & (d < d_hi) & (j < j_hi)
    max_exact = N_BUCKETS // 2
    n = jnp.maximum(d, 0)
    nf = jnp.maximum(n, 1).astype(F32)
    large = max_exact + (jnp.log(nf / max_exact) / math.log(MAX_DISTANCE / max_exact)
                         * (N_BUCKETS - max_exact)).astype(jnp.int32)
    bucket = jnp.where(n < max_exact, n, jnp.minimum(large, N_BUCKETS - 1))
    acc = [jnp.zeros((tq, LANES), F32) for _ in range(NSA_HEADS)]
    for b in range(N_BUCKETS):
        hit = bucket == b
        for h in range(NSA_HEADS):
            acc[h] = jnp.where(hit, tab_ref[b, h], acc[h])
    for h in range(NSA_HEADS):
        g, r = divmod(h, NSA_Q_PER_GROUP)
        o_ref[0, g, r * tq:(r + 1) * tq, :] = jnp.where(valid, acc[h], NEG_INF)


def _bias_tables(rel_bias_table, n_q_tiles):
    m = NSA_Q_PER_GROUP * NSA_TQ
    steps = N_WIN_TILES + n_q_tiles
    return pl.pallas_call(
        _bias_tables_kernel,
        grid=(steps,),
        in_specs=[pl.BlockSpec(memory_space=pltpu.SMEM)],
        out_specs=pl.BlockSpec((1, NSA_KV_GROUPS, m, LANES), lambda s: (s, 0, 0, 0)),
        out_shape=jax.ShapeDtypeStruct((steps, NSA_KV_GROUPS, m, LANES), F32),
        compiler_params=_params(1),
        name="bias_tables",
    )(rel_bias_table)


def _flash_step(q, k, v, bias, state):
    m, l, acc = state
    s = _dot_nt(q, k) + bias
    m_new = jnp.maximum(m, jnp.max(s, axis=-1, keepdims=True))
    alpha = jnp.exp(m - m_new)
    p = jnp.exp(s - m_new)
    l = alpha * l + jnp.sum(p, axis=-1, keepdims=True)
    acc = alpha * acc + _dot(p.astype(BF16), v)
    return m_new, l, acc


def _flash_init(rows):
    return (jnp.full((rows, 1), NEG_INF, F32), jnp.zeros((rows, 1), F32), jnp.zeros((rows, LANES), F32))


def _select_blocks(imp, t0):
    tq = imp.shape[0]
    lane = lax.broadcasted_iota(jnp.int32, (tq, LANES), 1)
    cur = (t0 + lax.broadcasted_iota(jnp.int32, (tq, LANES), 0)) // SLC_BLOCK
    forced = (lane == 0) | (lane == cur) | (lane == cur - 1)
    score = jnp.where(lane <= cur, imp + jnp.where(forced, FORCE_BONUS, 0.0), NEG_INF)
    lane_f = lane.astype(F32)
    dead = -3.0e38
    picked = jnp.zeros((tq, LANES), jnp.bool_)
    for _ in range(SLC_TOPK):
        best = jnp.max(score, axis=-1, keepdims=True)
        first = jnp.min(jnp.where(score == best, lane_f, float(LANES)), axis=-1, keepdims=True)
        hit = lane_f == first
        picked = picked | hit
        score = jnp.where(hit, dead, score)
    return jnp.where(picked, 0.0, NEG_INF)


def _nsa_kernel(qa_ref, kcmp_ref, vcmp_ref, ksw_ref, vsw_ref, small_ref, tb_ref, bc_ref, ov_ref, oh_ref,
                o_ref, *, n_blocks):
    qt = pl.program_id(1)
    tq = NSA_TQ
    rq = NSA_Q_PER_GROUP
    gates = small_ref[...]
    lane = lax.broadcasted_iota(jnp.int32, (tq, LANES), 1)

    def stacked_gate(branch, g):
        c0 = branch * NSA_HEADS + g * rq
        return jnp.concatenate([gates[:, c0 + r:c0 + r + 1] for r in range(rq)], axis=0)

    outs = []
    for g in range(NSA_KV_GROUPS):
        q = jnp.concatenate([qa_ref[:, (g * rq + r) * LANES:(g * rq + r + 1) * LANES] for r in range(rq)],
                            axis=0)

        s = _dot_nt(q, kcmp_ref[0]) + bc_ref[0, g]
        m = jnp.max(s, axis=-1, keepdims=True)
        p = jnp.where(s > 0.5 * NEG_INF, jnp.exp(s - m), 0.0)
        l = jnp.sum(p, axis=-1, keepdims=True)
        p_c = p * (1.0 / jnp.where(l > 0.0, l, 1.0))
        o_cmp = _dot(p_c.astype(BF16), vcmp_ref[0])

        p_sum = p_c[0:tq] + p_c[tq:2 * tq] + p_c[2 * tq:3 * tq] + p_c[3 * tq:4 * tq]
        imp = _split_dot(p_sum, ov_ref[...])
        sel_bias = _select_blocks(imp, qt * tq).astype(BF16)
        q_sel = jnp.concatenate([q, jnp.concatenate([sel_bias] * rq, axis=0)], axis=1)

        def slc_step(dl, state):
            row0 = pl.multiple_of((qt - dl) * tq, tq)
            k = jnp.concatenate([ksw_ref[pl.ds(row0, tq), 0:LANES], oh_ref[pl.ds(row0, tq), :]], axis=1)
            v = vsw_ref[pl.ds(row0, tq), 0:LANES]
            return _flash_step(q_sel, k, v, tb_ref[jnp.minimum(dl, 2), g], state)

        m_s, l_s, acc_s = lax.fori_loop(0, qt + 1, slc_step, _flash_init(rq * tq))

        def win_step(dl, state):
            row0 = pl.multiple_of((qt - dl) * tq, tq)
            k = ksw_ref[pl.ds(row0, tq), LANES:2 * LANES]
            v = vsw_ref[pl.ds(row0, tq), LANES:2 * LANES]
            return _flash_step(q, k, v, tb_ref[dl, g], state)

        m_w, l_w, acc_w = lax.fori_loop(0, jnp.minimum(qt, N_WIN_TILES - 1) + 1, win_step,
                                        _flash_init(rq * tq))

        outs.append(stacked_gate(0, g) * o_cmp + stacked_gate(1, g) * (acc_s * (1.0 / l_s))
                    + stacked_gate(2, g) * (acc_w * (1.0 / l_w)))

    for r in range(rq):
        pair = jnp.where(lane < HEAD_DIM, outs[0][r * tq:(r + 1) * tq], outs[1][r * tq:(r + 1) * tq])
        o_ref[:, r * LANES:(r + 1) * LANES] = pair.astype(BF16)


def _nsa(qa, kcmp, vcmp, ksw, vsw, small, tables, batch, seq):
    tq = NSA_TQ
    nq = seq // tq
    m = NSA_Q_PER_GROUP * tq
    n_blocks = seq // SLC_BLOCK
    n_cmp = (seq - CMP_BLOCK) // CMP_STRIDE + 1
    ci = np.arange(LANES)[:, None] * CMP_STRIDE
    sj = np.arange(LANES)[None, :] * SLC_BLOCK
    overlap = ((ci <= sj + SLC_BLOCK - 1) & (ci + CMP_BLOCK - 1 >= sj)
               & (np.arange(LANES)[:, None] < n_cmp) & (np.arange(LANES)[None, :] < n_blocks))
    onehot = (np.arange(seq)[:, None] // SLC_BLOCK) == np.arange(LANES)[None, :]
    return pl.pallas_call(
        functools.partial(_nsa_kernel, n_blocks=n_blocks),
        grid=(batch, nq),
        in_specs=[
            pl.BlockSpec((tq, NSA_HEADS * LANES), lambda b, t: (b * nq + t, 0)),
            pl.BlockSpec((1, LANES, NSA_KV_W), lambda b, t: (b, 0, 0)),
            pl.BlockSpec((1, LANES, NSA_KV_W), lambda b, t: (b, 0, 0)),
            pl.BlockSpec((seq, 2 * NSA_KV_W), lambda b, t: (b, 0)),
            pl.BlockSpec((seq, 2 * NSA_KV_W), lambda b, t: (b, 0)),
            pl.BlockSpec((tq, LANES), lambda b, t: (b * nq + t, 0)),
            pl.BlockSpec((N_WIN_TILES, NSA_KV_GROUPS, m, LANES), lambda b, t: (0, 0, 0, 0)),
            pl.BlockSpec((1, NSA_KV_GROUPS, m, LANES), lambda b, t: (N_WIN_TILES + t, 0, 0, 0)),
            _const_spec((LANES, LANES)),
            _const_spec((seq, LANES)),
        ],
        out_specs=pl.BlockSpec((tq, NSA_W), lambda b, t: (b * nq + t, 0)),
        out_shape=jax.ShapeDtypeStruct((batch * seq, NSA_W), BF16),
        compiler_params=_params(2),
        name="nsa",
    )(qa, kcmp, vcmp, ksw, vsw, small, tables, tables,
      jnp.asarray(overlap.astype(np.float32), BF16), jnp.asarray(onehot.astype(np.float32), BF16))


def _fox_kernel(q_ref, k_ref, v_ref, cum_ref, o_ref):
    qt = pl.program_id(2)
    tq = FOX_TQ
    q = jnp.concatenate([q_ref[:, 0:LANES], q_ref[:, LANES:2 * LANES]], axis=0)
    q0 = pl.multiple_of(qt * tq, tq)
    base = cum_ref[0, 0, :, pl.ds(q0, LANES)][:, 0:1]

    def decay_bias(row0):
        c = base - cum_ref[0, 0, :, pl.ds(row0, tq)]
        return jnp.concatenate([jnp.broadcast_to(c[0:1], (tq, tq)), jnp.broadcast_to(c[1:2], (tq, tq))],
                               axis=0)

    row = lax.broadcasted_iota(jnp.int32, (2 * tq, tq), 0) & (tq - 1)
    col = lax.broadcasted_iota(jnp.int32, (2 * tq, tq), 1)
    causal = col <= row
    state = _flash_step(q, k_ref[pl.ds(q0, tq), :], v_ref[pl.ds(q0, tq), :],
                        jnp.where(causal, decay_bias(q0), NEG_INF), _flash_init(2 * tq))

    def step(kt, state):
        row0 = pl.multiple_of(kt * tq, tq)
        return _flash_step(q, k_ref[pl.ds(row0, tq), :], v_ref[pl.ds(row0, tq), :], decay_bias(row0), state)

    m, l, acc = lax.fori_loop(0, qt, step, state)
    o = acc * (1.0 / l)
    lane = lax.broadcasted_iota(jnp.int32, (tq, LANES), 1)
    o_ref[...] = jnp.where(lane < HEAD_DIM, o[0:tq], o[tq:2 * tq]).astype(BF16)


def _fox(qb, kb, vb, cum, batch, seq):
    tq = FOX_TQ
    nq = seq // tq
    pairs = FOX_HEADS // 2
    return pl.pallas_call(
        _fox_kernel,
        grid=(batch, pairs, nq),
        in_specs=[
            pl.BlockSpec((tq, 2 * LANES), lambda b, p, t: (b * nq + t, p)),
            pl.BlockSpec((seq, LANES), lambda b, p, t: (b, p)),
            pl.BlockSpec((seq, LANES), lambda b, p, t: (b, p)),
            pl.BlockSpec((1, 1, 2, seq), lambda b, p, t: (b, p, 0, 0)),
        ],
        out_specs=pl.BlockSpec((tq, LANES), lambda b, p, t: (b * nq + t, p)),
        out_shape=jax.ShapeDtypeStruct((batch * seq, FOX_W), BF16),
        compiler_params=_params(3),
        name="fox",
    )(qb, kb, vb, cum.reshape(batch, pairs, 2, seq))


def _merge_ffn_kernel(x_ref, on_ref, of_ref, gmix_ref, wgm_ref, won_ref, wof_ref, wout_ref,
                      g2_ref, wup_ref, wdn_ref, o_ref):
    x = x_ref[...]
    u = _rms_rows(x, gmix_ref[...]).astype(BF16)
    gate = jax.nn.sigmoid(_dot(u, wgm_ref[...]))
    merged = (gate[:, :D_MODEL] * _dot(on_ref[...], won_ref[...])
              + gate[:, D_MODEL:] * _dot(of_ref[...], wof_ref[...]))
    x2 = x + _dot(merged.astype(BF16), wout_ref[...])
    o_ref[...] = _swiglu_residual(x2, g2_ref[...], wup_ref, wdn_ref)


def _merge_ffn(x2d, o_nsa, o_fox, mix_norm, w_gm, w_o_nsa, w_o_fox, w_out, gain2, w_up, w_down):
    n = x2d.shape[0]
    tm = TOKEN_TILE // 2

    def row(width):
        return pl.BlockSpec((tm, width), lambda i: (i, 0))

    return pl.pallas_call(
        _merge_ffn_kernel,
        grid=(n // tm,),
        in_specs=[row(D_MODEL), row(NSA_W), row(FOX_W), _const_spec((1, D_MODEL)),
                  _const_spec((D_MODEL, 2 * D_MODEL)), _const_spec((NSA_W, D_MODEL)),
                  _const_spec((FOX_W, D_MODEL)), _const_spec((D_MODEL, D_MODEL)),
                  _const_spec((1, D_MODEL)), _const_spec((D_MODEL, 2 * D_FF)),
                  _const_spec((D_FF, D_MODEL))],
        out_specs=row(D_MODEL),
        out_shape=jax.ShapeDtypeStruct((n, D_MODEL), F32),
        compiler_params=_params(1),
        name="merge_ffn",
    )(x2d, o_nsa, o_fox, mix_norm.reshape(1, D_MODEL), w_gm, w_o_nsa.astype(BF16),
      w_o_fox.astype(BF16), w_out.astype(BF16), gain2.reshape(1, D_MODEL), w_up.astype(BF16),
      w_down.astype(BF16))


def _layer(x, ffn1_norm, ffn1_w_up, ffn1_w_down, mix_norm, w_in, b_forget, nsa_q_gain, nsa_k_gain,
           fox_q_gain, fox_k_gain, cmp_pos_k, cmp_pos_v, cmp_k_w1, cmp_k_w2, cmp_v_w1, cmp_v_w2,
           w_o_nsa, w_o_fox, w_out, ffn2_norm, ffn2_w_up, ffn2_w_down, rel_bias_table):
    batch, seq, d = x.shape
    assert d == D_MODEL and seq % FOX_TQ == 0 and (batch * seq) % TOKEN_TILE == 0
    assert seq // SLC_BLOCK <= LANES and (seq - CMP_BLOCK) // CMP_STRIDE + 1 < LANES + 1
    assert seq // (CMP_BLOCK // 2) == LANES
    x2d = x.reshape(batch * seq, D_MODEL)

    x1 = _ffn(x2d, ffn1_norm, ffn1_w_up, ffn1_w_down)

    w_packed, gain_row, small_bias, w_gm = _pack_in_proj(w_in, b_forget, nsa_q_gain, nsa_k_gain,
                                                         fox_q_gain, fox_k_gain)
    qa, qb, kb, ksw, vb, vsw, kc, vc, small = _in_proj(x1, mix_norm, w_packed, gain_row, small_bias)

    kcmp, vcmp = _compress(kc, vc, batch, seq, _pack_compress(cmp_pos_k, cmp_k_w1, cmp_k_w2),
                           _pack_compress(cmp_pos_v, cmp_v_w1, cmp_v_w2), nsa_k_gain[0])
    tables = _bias_tables(rel_bias_table, seq // NSA_TQ)
    o_nsa = _nsa(qa, kcmp, vcmp, ksw, vsw, small, tables, batch, seq)

    n_gate = 3 * NSA_HEADS
    logf_t = small[:, n_gate:n_gate + FOX_HEADS].reshape(batch, seq, FOX_HEADS).transpose(0, 2, 1)
    o_fox = _fox(qb, kb, vb, _cumsum(logf_t), batch, seq)

    w_o_nsa_p = w_o_nsa.reshape(NSA_KV_GROUPS, NSA_Q_PER_GROUP, HEAD_DIM, D_MODEL).transpose(1, 0, 2, 3)
    out = _merge_ffn(x1, o_nsa, o_fox, mix_norm, w_gm, w_o_nsa_p.reshape(NSA_W, D_MODEL), w_o_fox,
                     w_out, ffn2_norm, ffn2_w_up, ffn2_w_down)
    return out.reshape(batch, seq, D_MODEL)


def kernel(x, ffn1_norm, ffn1_w_up, ffn1_w_down, mix_norm, w_in, b_forget, nsa_q_gain, nsa_k_gain,
           fox_q_gain, fox_k_gain, cmp_pos_k, cmp_pos_v, cmp_k_w1, cmp_k_w2, cmp_v_w1, cmp_v_w2,
           w_o_nsa, w_o_fox, w_out, ffn2_norm, ffn2_w_up, ffn2_w_down, rel_bias_table):
    for layer in range(ffn1_norm.shape[0]):
        x = _layer(x, ffn1_norm[layer], ffn1_w_up[layer], ffn1_w_down[layer], mix_norm[layer],
                   w_in[layer], b_forget[layer], nsa_q_gain[layer], nsa_k_gain[layer],
                   fox_q_gain[layer], fox_k_gain[layer], cmp_pos_k[layer], cmp_pos_v[layer],
                   cmp_k_w1[layer], cmp_k_w2[layer], cmp_v_w1[layer], cmp_v_w2[layer],
                   w_o_nsa[layer], w_o_fox[layer], w_out[layer], ffn2_norm[layer],
                   ffn2_w_up[layer], ffn2_w_down[layer], rel_bias_table)
    return x
```

```python
import functools
import math

import numpy as np
import jax
import jax.numpy as jnp
from jax import lax
from jax.experimental import pallas as pl
from jax.experimental.pallas import tpu as pltpu

F32 = jnp.float32
BF16 = jnp.bfloat16

D_MODEL = 1024
HEAD_DIM = 64
NSA_HEADS = 8
NSA_KV_GROUPS = 2
NSA_Q_PER_GROUP = NSA_HEADS // NSA_KV_GROUPS
CMP_BLOCK = 32
CMP_STRIDE = 16
CMP_HIDDEN = 128
SLC_BLOCK = 64
SLC_TOPK = 8
WINDOW = 512
FOX_HEADS = 8
D_FF = 2816
N_BUCKETS = 32
MAX_DISTANCE = 128
RMS_EPS = 1e-6
NEG_INF = -1.0e30
FORCE_BONUS = 1.0e4

NSA_W = NSA_HEADS * HEAD_DIM
NSA_KV_W = NSA_KV_GROUPS * HEAD_DIM
FOX_W = FOX_HEADS * HEAD_DIM
IN_SPLITS = (NSA_W, NSA_KV_W, NSA_KV_W, NSA_KV_W, NSA_KV_W, NSA_KV_W, NSA_KV_W, 3 * NSA_HEADS,
             FOX_W, FOX_W, FOX_W, FOX_HEADS, 2 * D_MODEL)

LANES = 128
TOKEN_TILE = 512
FFN_CHUNK = D_FF // 2
ATT_TILE = 256
N_WIN_TILES = WINDOW // ATT_TILE + 1
VMEM_LIMIT = 56 * 1024 * 1024
LOG2E = 1.4426950408889634

PQ_A = 0
PQ_B = PQ_A + NSA_HEADS * LANES
PK_B = PQ_B + FOX_HEADS * LANES
PK_SW = PK_B + FOX_W
P_NORM_END = PK_SW + 2 * NSA_KV_W
PV_B = P_NORM_END
PV_SW = PV_B + FOX_W
PKV_C = PV_SW + 2 * NSA_KV_W
P_SMALL = PKV_C + 2 * NSA_KV_W
P_END = P_SMALL + LANES
NORM_CHUNK = 256


def _dot(a, b):
    return jnp.dot(a, b, preferred_element_type=F32)


def _dot_nt(a, b):
    return lax.dot_general(a, b, (((1,), (1,)), ((), ())), preferred_element_type=F32)


def _split_dot(x, w):
    hi = x.astype(BF16)
    lo = (x - hi.astype(F32)).astype(BF16)
    return _dot(hi, w) + _dot(lo, w)


def _rms_rows(x, gain_row):
    ms = jnp.mean(x * x, axis=-1, keepdims=True)
    return x * lax.rsqrt(ms + RMS_EPS) * gain_row


def _const_spec(shape):
    nd = len(shape)
    return pl.BlockSpec(shape, lambda *_: (0,) * nd, pipeline_mode=pl.Buffered(1))


def _params(n_axes):
    return pltpu.CompilerParams(dimension_semantics=("arbitrary",) * n_axes,
                                vmem_limit_bytes=VMEM_LIMIT)


def _swiglu_residual(x, gain_row, wup_ref, wdn_ref):
    xn = _rms_rows(x, gain_row).astype(BF16)
    acc = jnp.zeros(x.shape, F32)
    for c in range(D_FF // FFN_CHUNK):
        lo = c * FFN_CHUNK
        gate = _dot(xn, wup_ref[:, lo:lo + FFN_CHUNK])
        up = _dot(xn, wup_ref[:, D_FF + lo:D_FF + lo + FFN_CHUNK])
        h = (gate * jax.nn.sigmoid(gate) * up).astype(BF16)
        acc = acc + _dot(h, wdn_ref[lo:lo + FFN_CHUNK, :])
    return x + 0.5 * acc


def _ffn_kernel(x_ref, g_ref, wup_ref, wdn_ref, o_ref):
    o_ref[...] = _swiglu_residual(x_ref[...], g_ref[...], wup_ref, wdn_ref)


def _ffn(x2d, gain, w_up, w_down):
    n = x2d.shape[0]
    row = pl.BlockSpec((TOKEN_TILE, D_MODEL), lambda i: (i, 0))
    return pl.pallas_call(
        _ffn_kernel,
        grid=(n // TOKEN_TILE,),
        in_specs=[row, _const_spec((1, D_MODEL)), _const_spec((D_MODEL, 2 * D_FF)),
                  _const_spec((D_FF, D_MODEL))],
        out_specs=row,
        out_shape=jax.ShapeDtypeStruct((n, D_MODEL), F32),
        compiler_params=_params(1),
        name="ffn",
    )(x2d, gain.reshape(1, D_MODEL), w_up.astype(BF16), w_down.astype(BF16))


def _block_ones(width, size=NORM_CHUNK):
    idx = np.arange(size) // width
    return jnp.asarray((idx[:, None] == idx[None, :]).astype(np.float32), BF16)


def _pack_in_proj(w_in, b_forget, nsa_q_gain, nsa_k_gain, fox_q_gain, fox_k_gain):
    scale = LOG2E / math.sqrt(HEAD_DIM)
    pts = np.cumsum(np.array(IN_SPLITS))[:-1].tolist()
    qa, kc, vc, ks, vs, kw, vw, ga, qb, kb, vb, fb, gm = jnp.split(w_in, pts, axis=1)
    z_w = jnp.zeros((D_MODEL, HEAD_DIM), F32)
    z_g = jnp.zeros((HEAD_DIM,), F32)
    cols, gains = [], []
    for h in range(NSA_HEADS):
        wh = qa[:, h * HEAD_DIM:(h + 1) * HEAD_DIM]
        first = (h // NSA_Q_PER_GROUP) == 0
        cols += [wh, z_w] if first else [z_w, wh]
        gains += [nsa_q_gain * scale, z_g] if first else [z_g, nsa_q_gain * scale]
    for h in range(FOX_HEADS):
        wh = qb[:, h * HEAD_DIM:(h + 1) * HEAD_DIM]
        first = (h % 2) == 0
        cols += [wh, z_w] if first else [z_w, wh]
        gains += [fox_q_gain * scale, z_g] if first else [z_g, fox_q_gain * scale]
    cols += [kb, ks, kw]
    gains += [jnp.tile(fox_k_gain, FOX_HEADS), jnp.tile(nsa_k_gain[1], NSA_KV_GROUPS),
              jnp.tile(nsa_k_gain[2], NSA_KV_GROUPS)]
    n_small_pad = LANES - ga.shape[1] - fb.shape[1]
    cols += [vb, vs, vw, kc, vc, ga, fb, jnp.zeros((D_MODEL, n_small_pad), F32)]
    w_packed = jnp.concatenate(cols, axis=1).astype(BF16)
    gain_row = jnp.concatenate(gains).reshape(1, P_NORM_END)
    small_bias = jnp.concatenate([jnp.zeros((ga.shape[1],), F32), b_forget,
                                  jnp.zeros((n_small_pad,), F32)]).reshape(1, LANES)
    return w_packed, gain_row, small_bias, gm.astype(BF16)


def _in_proj_kernel(x_ref, g_ref, w_ref, gain_ref, sbias_ref, bd128_ref, bd64_ref,
                    qa_ref, qb_ref, kb_ref, ksw_ref, vb_ref, vsw_ref, kc_ref, vc_ref, small_ref):
    u = _rms_rows(x_ref[...], g_ref[...]).astype(BF16)

    def normed(lo, width, bd_ref):
        parts = []
        for c in range(lo, lo + width, NORM_CHUNK):
            y = _dot(u, w_ref[:, c:c + NORM_CHUNK])
            ss = _split_dot(y * y, bd_ref[...])
            parts.append(y * lax.rsqrt(ss * (1.0 / HEAD_DIM) + RMS_EPS) * gain_ref[:, c:c + NORM_CHUNK])
        return parts

    for i, y in enumerate(normed(PQ_A, NSA_HEADS * LANES, bd128_ref)):
        qa_ref[:, i * NORM_CHUNK:(i + 1) * NORM_CHUNK] = y.astype(BF16)
    for i, y in enumerate(normed(PQ_B, FOX_HEADS * LANES, bd128_ref)):
        qb_ref[:, i * NORM_CHUNK:(i + 1) * NORM_CHUNK] = y.astype(BF16)
    for i, y in enumerate(normed(PK_B, FOX_W, bd64_ref)):
        kb_ref[:, i * NORM_CHUNK:(i + 1) * NORM_CHUNK] = y.astype(BF16)
    ksw_ref[...] = normed(PK_SW, 2 * NSA_KV_W, bd64_ref)[0].astype(BF16)
    low = lax.broadcasted_iota(jnp.int32, (u.shape[0], LANES), 1) < HEAD_DIM

    def store_with_ones(v, o_ref):
        for i in range(v.shape[1] // LANES):
            pair = v[:, i * LANES:(i + 1) * LANES]
            o_ref[:, 2 * i * LANES:(2 * i + 1) * LANES] = jnp.where(low, pair, 1.0).astype(BF16)
            o_ref[:, (2 * i + 1) * LANES:(2 * i + 2) * LANES] = jnp.where(low, 1.0, pair).astype(BF16)

    store_with_ones(_dot(u, w_ref[:, PV_B:PV_B + FOX_W]), vb_ref)
    store_with_ones(_dot(u, w_ref[:, PV_SW:PV_SW + 2 * NSA_KV_W]), vsw_ref)
    kvc = _dot(u, w_ref[:, PKV_C:PKV_C + 2 * NSA_KV_W])
    kc_ref[...] = kvc[:, :NSA_KV_W]
    vc_ref[...] = kvc[:, NSA_KV_W:]
    z = _dot(u, w_ref[:, P_SMALL:P_END]) + sbias_ref[...]
    lane = lax.broadcasted_iota(jnp.int32, z.shape, 1)
    log_sig = jnp.minimum(z, 0.0) - jnp.log1p(jnp.exp(-jnp.abs(z)))
    small_ref[...] = jnp.where(lane < 3 * NSA_HEADS, jax.nn.sigmoid(z), log_sig)


def _in_proj(x2d, mix_norm, w_packed, gain_row, small_bias):
    n = x2d.shape[0]

    def row(width):
        return pl.BlockSpec((TOKEN_TILE, width), lambda i: (i, 0))

    widths = (NSA_HEADS * LANES, FOX_HEADS * LANES, FOX_W, 2 * NSA_KV_W, 2 * FOX_W, 4 * NSA_KV_W,
              NSA_KV_W, NSA_KV_W, LANES)
    dtypes = (BF16, BF16, BF16, BF16, BF16, BF16, F32, F32, F32)
    return pl.pallas_call(
        _in_proj_kernel,
        grid=(n // TOKEN_TILE,),
        in_specs=[row(D_MODEL), _const_spec((1, D_MODEL)), _const_spec((D_MODEL, P_END)),
                  _const_spec((1, P_NORM_END)), _const_spec((1, LANES)),
                  _const_spec((NORM_CHUNK, NORM_CHUNK)), _const_spec((NORM_CHUNK, NORM_CHUNK))],
        out_specs=[row(w) for w in widths],
        out_shape=[jax.ShapeDtypeStruct((n, w), d) for w, d in zip(widths, dtypes)],
        compiler_params=_params(1),
        name="in_proj",
    )(x2d, mix_norm.reshape(1, D_MODEL), w_packed, gain_row, small_bias,
      _block_ones(LANES), _block_ones(HEAD_DIM))


def _pack_compress(pos, w1, w2):
    half = CMP_BLOCK // 2
    eye = jnp.eye(NSA_KV_GROUPS, dtype=F32)
    w1r = w1.reshape(CMP_BLOCK, HEAD_DIM, CMP_HIDDEN)

    def big(w):
        return jnp.einsum('ldh,pg->lpdgh', w, eye).reshape(half * NSA_KV_W, NSA_KV_GROUPS * CMP_HIDDEN)

    def posrow(p):
        return jnp.broadcast_to(p[:, None, :], (half, NSA_KV_GROUPS, HEAD_DIM)).reshape(1, half * NSA_KV_W)

    w2big = jnp.einsum('hd,pg->phgd', w2, eye).reshape(NSA_KV_GROUPS * CMP_HIDDEN, NSA_KV_W)
    return (posrow(pos[:half]), posrow(pos[half:]), big(w1r[:half]).astype(BF16),
            big(w1r[half:]).astype(BF16), w2big.astype(BF16))


def _compress_kernel(kc_ref, vc_ref, kp_lo, kp_hi, kw_lo, kw_hi, kw2, vp_lo, vp_hi, vw_lo, vw_hi, vw2,
                     kgain_ref, bd64_ref, ko_ref, vo_ref):
    def mlp(r, p_lo, p_hi, w_lo, w_hi, w2):
        first = _dot((r + p_lo[...]).astype(BF16), w_lo[...])
        second = _dot((r + p_hi[...]).astype(BF16), w_hi[...])
        h = first + pltpu.roll(second, second.shape[0] - 1, axis=0)
        return _dot((h * jax.nn.sigmoid(h)).astype(BF16), w2[...])

    k = mlp(kc_ref[0], kp_lo, kp_hi, kw_lo, kw_hi, kw2)
    ss = _split_dot(k * k, bd64_ref[...])
    ko_ref[0] = (k * lax.rsqrt(ss * (1.0 / HEAD_DIM) + RMS_EPS) * kgain_ref[...]).astype(BF16)
    vo_ref[0] = mlp(vc_ref[0], vp_lo, vp_hi, vw_lo, vw_hi, vw2).astype(BF16)


def _compress(kc, vc, batch, seq, k_pack, v_pack, k_gain):
    rows = seq // (CMP_BLOCK // 2)
    width = (CMP_BLOCK // 2) * NSA_KV_W
    blk = pl.BlockSpec((1, rows, width), lambda b: (b, 0, 0))
    out = pl.BlockSpec((1, rows, NSA_KV_W), lambda b: (b, 0, 0))
    pack_specs = [_const_spec((1, width)), _const_spec((1, width)),
                  _const_spec((width, NSA_KV_GROUPS * CMP_HIDDEN)),
                  _const_spec((width, NSA_KV_GROUPS * CMP_HIDDEN)),
                  _const_spec((NSA_KV_GROUPS * CMP_HIDDEN, NSA_KV_W))]
    bd = _block_ones(HEAD_DIM, NSA_KV_W)
    return pl.pallas_call(
        _compress_kernel,
        grid=(batch,),
        in_specs=[blk, blk] + pack_specs + pack_specs + [_const_spec((1, NSA_KV_W)),
                                                         _const_spec((NSA_KV_W, NSA_KV_W))],
        out_specs=[out, out],
        out_shape=[jax.ShapeDtypeStruct((batch, rows, NSA_KV_W), BF16)] * 2,
        compiler_params=_params(1),
        name="compress",
    )(kc.reshape(batch, rows, width), vc.reshape(batch, rows, width), *k_pack, *v_pack,
      jnp.tile(k_gain, NSA_KV_GROUPS).reshape(1, NSA_KV_W), bd)


def _cumsum_kernel(x_ref, o_ref):
    x = x_ref[0]
    lane = lax.broadcasted_iota(jnp.int32, x.shape, 1)
    k = 1
    while k < x.shape[1]:
        x = x + jnp.where(lane >= k, pltpu.roll(x, k, axis=1), 0.0)
        k *= 2
    o_ref[0] = x


def _cumsum(logf_t):
    b, h, s = logf_t.shape
    spec = pl.BlockSpec((1, h, s), lambda i: (i, 0, 0))
    return pl.pallas_call(
        _cumsum_kernel, grid=(b,), in_specs=[spec], out_specs=spec,
        out_shape=jax.ShapeDtypeStruct((b, h, s), F32), compiler_params=_params(1), name="cumsum",
    )(logf_t)


def _write_bias_tile(d, valid, tab_ref, o_ref):
    max_exact = N_BUCKETS // 2
    n = jnp.maximum(d, 0)
    nf = jnp.maximum(n, 1).astype(F32)
    large = max_exact + (jnp.log(nf / max_exact) / math.log(MAX_DISTANCE / max_exact)
                         * (N_BUCKETS - max_exact)).astype(jnp.int32)
    bucket = jnp.where(n < max_exact, n, jnp.minimum(large, N_BUCKETS - 1))
    for h in range(NSA_HEADS):
        acc = jnp.zeros(d.shape, F32)
        for b in range(N_BUCKETS):
            acc = jnp.where(bucket == b, tab_ref[b, h], acc)
        g, r = divmod(h, NSA_Q_PER_GROUP)
        o_ref[0, g, r] = jnp.where(valid, acc * LOG2E, NEG_INF)


def _toeplitz_bias_kernel(tab_ref, o_ref):
    s = pl.program_id(0)
    i = lax.broadcasted_iota(jnp.int32, (ATT_TILE, ATT_TILE), 0)
    j = lax.broadcasted_iota(jnp.int32, (ATT_TILE, ATT_TILE), 1)
    d = jnp.minimum(s, N_WIN_TILES - 1) * ATT_TILE + i - j
    d_hi = jnp.where(s < N_WIN_TILES, WINDOW, jnp.int32(1 << 30))
    _write_bias_tile(d, (d >= 0) & (d < d_hi), tab_ref, o_ref)


def _cmp_bias_kernel(tab_ref, o_ref):
    t = pl.program_id(0)
    i = lax.broadcasted_iota(jnp.int32, (ATT_TILE, LANES), 0)
    j = lax.broadcasted_iota(jnp.int32, (ATT_TILE, LANES), 1)
    d = t * ATT_TILE + i - CMP_STRIDE * j - (CMP_BLOCK - 1)
    _write_bias_tile(d, (d >= 0) & (j < LANES - 1), tab_ref, o_ref)


def _bias_tables(rel_bias_table, n_q_tiles):
    def call(body, steps, width, name):
        shape = (steps, NSA_KV_GROUPS, NSA_Q_PER_GROUP, ATT_TILE, width)
        return pl.pallas_call(
            body, grid=(steps,),
            in_specs=[pl.BlockSpec(memory_space=pltpu.SMEM)],
            out_specs=pl.BlockSpec((1,) + shape[1:], lambda s: (s, 0, 0, 0, 0)),
            out_shape=jax.ShapeDtypeStruct(shape, F32),
            compiler_params=_params(1), name=name,
        )(rel_bias_table)

    return (call(_toeplitz_bias_kernel, N_WIN_TILES + 1, ATT_TILE, "toeplitz_bias"),
            call(_cmp_bias_kernel, n_q_tiles, LANES, "cmp_bias"))


def _two_pass_attention(n_far, score_of, pv_of):
    mx = lax.fori_loop(1, n_far + 1, lambda dl, mx: jnp.maximum(mx, score_of(dl)), score_of(0))
    m_b = jnp.broadcast_to(jnp.max(mx, axis=-1, keepdims=True), mx.shape)

    def weights(dl):
        return jnp.exp2(score_of(dl) - m_b).astype(BF16)

    return lax.fori_loop(1, n_far + 1, lambda dl, acc: acc + pv_of(weights(dl), dl), pv_of(weights(0), 0))


def _select_blocks(imp, t0):
    tq = imp.shape[0]
    lane = lax.broadcasted_iota(jnp.int32, (tq, LANES), 1)
    cur = (t0 + lax.broadcasted_iota(jnp.int32, (tq, LANES), 0)) // SLC_BLOCK
    forced = (lane == 0) | (lane == cur) | (lane == cur - 1)
    score = jnp.where(lane <= cur, imp + jnp.where(forced, FORCE_BONUS, 0.0), NEG_INF)
    lane_f = lane.astype(F32)
    dead = -3.0e38
    picked = jnp.zeros((tq, LANES), jnp.bool_)
    for _ in range(SLC_TOPK):
        best = jnp.max(score, axis=-1, keepdims=True)
        first = jnp.min(jnp.where(score == best, lane_f, float(LANES)), axis=-1, keepdims=True)
        hit = lane_f == first
        picked = picked | hit
        score = jnp.where(hit, dead, score)
    return jnp.where(picked, 1.0, 0.0)


def _nsa_kernel(qa_ref, kcmp_ref, vcmp_ref, ksw_ref, vsw_ref, small_ref, tb_ref, bc_ref, ov_ref, e_ref,
                o_ref, amask_ref):
    qt = pl.program_id(1)
    tq = ATT_TILE
    rq = NSA_Q_PER_GROUP
    gates = small_ref[...]
    lane = lax.broadcasted_iota(jnp.int32, (tq, LANES), 1)

    def stacked_gate(branch, g):
        c0 = branch * NSA_HEADS + g * rq
        return jnp.concatenate([gates[:, c0 + r:c0 + r + 1] for r in range(rq)], axis=0)

    def key_rows(dl):
        return pl.ds(pl.multiple_of((qt - dl) * tq, tq), tq)

    outs = []
    for g in range(NSA_KV_GROUPS):
        sum_lane = HEAD_DIM if g == 0 else 0
        q = jnp.concatenate([qa_ref[:, (g * rq + r) * LANES:(g * rq + r + 1) * LANES] for r in range(rq)],
                            axis=0)

        s = _dot_nt(q, kcmp_ref[0]) + bc_ref[0, g].reshape(rq * tq, LANES)
        m = jnp.max(s, axis=-1, keepdims=True)
        p = jnp.where(s > 0.5 * NEG_INF, jnp.exp2(s - m), 0.0)
        l = jnp.sum(p, axis=-1, keepdims=True)
        p_c = p * (1.0 / jnp.where(l > 0.0, l, 1.0))
        o_cmp = _dot(p_c.astype(BF16), vcmp_ref[0])

        p_sum = p_c[0:tq] + p_c[tq:2 * tq] + p_c[2 * tq:3 * tq] + p_c[3 * tq:4 * tq]
        sel = _select_blocks(_split_dot(p_sum, ov_ref[...]), qt * tq).astype(BF16)

        def mask_step(kt, carry):
            cols = pl.ds(pl.multiple_of(kt * tq, tq), tq)
            amask_ref[:, cols] = (_dot(sel, e_ref[:, cols]) - 1.0) * (-NEG_INF)
            return carry

        lax.fori_loop(0, qt + 1, mask_step, 0)

        def slc_score(dl):
            tile = dl if isinstance(dl, int) else jnp.where(dl < N_WIN_TILES - 1, dl, N_WIN_TILES)
            rows = key_rows(dl)
            s = _dot_nt(q, ksw_ref[rows, 0:LANES]).reshape(rq, tq, tq)
            return (s + tb_ref[tile, g] + amask_ref[:, rows][None]).reshape(rq * tq, tq)

        def slc_pv(p, dl):
            return _dot(p, vsw_ref[key_rows(dl), g * LANES:(g + 1) * LANES])

        acc_s = _two_pass_attention(qt, slc_score, slc_pv)

        def win_score(dl):
            s = _dot_nt(q, ksw_ref[key_rows(dl), LANES:2 * LANES]).reshape(rq, tq, tq)
            return (s + tb_ref[dl, g]).reshape(rq * tq, tq)

        def win_pv(p, dl):
            return _dot(p, vsw_ref[key_rows(dl), (NSA_KV_GROUPS + g) * LANES:(NSA_KV_GROUPS + g + 1) * LANES])

        acc_w = _two_pass_attention(jnp.minimum(qt, N_WIN_TILES - 1), win_score, win_pv)

        o_slc = acc_s * (1.0 / acc_s[:, sum_lane:sum_lane + 1])
        o_win = acc_w * (1.0 / acc_w[:, sum_lane:sum_lane + 1])
        outs.append(stacked_gate(0, g) * o_cmp + stacked_gate(1, g) * o_slc + stacked_gate(2, g) * o_win)

    for r in range(rq):
        pair = jnp.where(lane < HEAD_DIM, outs[0][r * tq:(r + 1) * tq], outs[1][r * tq:(r + 1) * tq])
        o_ref[:, r * LANES:(r + 1) * LANES] = pair.astype(BF16)


def _nsa(qa, kcmp, vcmp, ksw, vsw, small, toeplitz, cmp_bias, batch, seq):
    tq = ATT_TILE
    nq = seq // tq
    rq = NSA_Q_PER_GROUP
    n_blocks = seq // SLC_BLOCK
    n_cmp = (seq - CMP_BLOCK) // CMP_STRIDE + 1
    ci = np.arange(LANES)[:, None] * CMP_STRIDE
    sj = np.arange(LANES)[None, :] * SLC_BLOCK
    overlap = ((ci <= sj + SLC_BLOCK - 1) & (ci + CMP_BLOCK - 1 >= sj)
               & (np.arange(LANES)[:, None] < n_cmp) & (np.arange(LANES)[None, :] < n_blocks))
    expand = np.arange(LANES)[:, None] == (np.arange(seq)[None, :] // SLC_BLOCK)
    return pl.pallas_call(
        _nsa_kernel,
        grid=(batch, nq),
        in_specs=[
            pl.BlockSpec((tq, NSA_HEADS * LANES), lambda b, t: (b * nq + t, 0)),
            pl.BlockSpec((1, LANES, NSA_KV_W), lambda b, t: (b, 0, 0)),
            pl.BlockSpec((1, LANES, NSA_KV_W), lambda b, t: (b, 0, 0)),
            pl.BlockSpec((seq, 2 * NSA_KV_W), lambda b, t: (b, 0)),
            pl.BlockSpec((seq, 2 * NSA_KV_GROUPS * LANES), lambda b, t: (b, 0)),
            pl.BlockSpec((tq, LANES), lambda b, t: (b * nq + t, 0)),
            _const_spec((N_WIN_TILES + 1, NSA_KV_GROUPS, rq, tq, tq)),
            pl.BlockSpec((1, NSA_KV_GROUPS, rq, tq, LANES), lambda b, t: (t, 0, 0, 0, 0)),
            _const_spec((LANES, LANES)),
            _const_spec((LANES, seq)),
        ],
        out_specs=pl.BlockSpec((tq, NSA_W), lambda b, t: (b * nq + t, 0)),
        out_shape=jax.ShapeDtypeStruct((batch * seq, NSA_W), BF16),
        scratch_shapes=[pltpu.VMEM((tq, seq), F32)],
        compiler_params=_params(2),
        name="nsa",
    )(qa, kcmp, vcmp, ksw, vsw, small, toeplitz, cmp_bias,
      jnp.asarray(overlap.astype(np.float32), BF16), jnp.asarray(expand.astype(np.float32), BF16))


def _fox_kernel(q_ref, k_ref, v_ref, cum_ref, o_ref):
    qt = pl.program_id(2)
    tq = ATT_TILE
    q = jnp.concatenate([q_ref[:, 0:LANES], q_ref[:, LANES:2 * LANES]], axis=0)
    base = cum_ref[0, 0, :, pl.ds(pl.multiple_of(qt * tq, tq), LANES)][:, 0:1]
    row = lax.broadcasted_iota(jnp.int32, (2, tq, tq), 1)
    col = lax.broadcasted_iota(jnp.int32, (2, tq, tq), 2)
    lane = lax.broadcasted_iota(jnp.int32, (tq, LANES), 1)

    def key_rows(dl):
        return pl.ds(pl.multiple_of((qt - dl) * tq, tq), tq)

    def score(dl):
        rows = key_rows(dl)
        decay = (base - cum_ref[0, 0, :, rows]) * LOG2E
        s = _dot_nt(q, k_ref[rows, :]).reshape(2, tq, tq) + decay[:, None, :]
        if isinstance(dl, int):
            s = jnp.where(col <= row, s, NEG_INF)
        return s.reshape(2 * tq, tq)

    def pv(p, dl):
        rows = key_rows(dl)
        return jnp.concatenate([_dot(p[0:tq], v_ref[rows, 0:LANES]),
                                _dot(p[tq:2 * tq], v_ref[rows, LANES:2 * LANES])], axis=0)

    acc = _two_pass_attention(qt, score, pv)
    first, second = acc[0:tq], acc[tq:2 * tq]
    o_ref[...] = jnp.where(lane < HEAD_DIM, first * (1.0 / first[:, HEAD_DIM:HEAD_DIM + 1]),
                           second * (1.0 / second[:, 0:1])).astype(BF16)


def _fox(qb, kb, vb, cum, batch, seq):
    tq = ATT_TILE
    nq = seq // tq
    pairs = FOX_HEADS // 2
    return pl.pallas_call(
        _fox_kernel,
        grid=(batch, pairs, nq),
        in_specs=[
            pl.BlockSpec((tq, 2 * LANES), lambda b, p, t: (b * nq + t, p)),
            pl.BlockSpec((seq, LANES), lambda b, p, t: (b, p)),
            pl.BlockSpec((seq, 2 * LANES), lambda b, p, t: (b, p)),
            pl.BlockSpec((1, 1, 2, seq), lambda b, p, t: (b, p, 0, 0)),
        ],
        out_specs=pl.BlockSpec((tq, LANES), lambda b, p, t: (b * nq + t, p)),
        out_shape=jax.ShapeDtypeStruct((batch * seq, FOX_W), BF16),
        compiler_params=_params(3),
        name="fox",
    )(qb, kb, vb, cum.reshape(batch, pairs, 2, seq))


def _merge_ffn_kernel(x_ref, on_ref, of_ref, gmix_ref, wgm_ref, won_ref, wof_ref, wout_ref,
                      g2_ref, wup_ref, wdn_ref, o_ref):
    x = x_ref[...]
    u = _rms_rows(x, gmix_ref[...]).astype(BF16)
    gate = jax.nn.sigmoid(_dot(u, wgm_ref[...]))
    merged = (gate[:, :D_MODEL] * _dot(on_ref[...], won_ref[...])
              + gate[:, D_MODEL:] * _dot(of_ref[...], wof_ref[...]))
    x2 = x + _dot(merged.astype(BF16), wout_ref[...])
    o_ref[...] = _swiglu_residual(x2, g2_ref[...], wup_ref, wdn_ref)


def _merge_ffn(x2d, o_nsa, o_fox, mix_norm, w_gm, w_o_nsa, w_o_fox, w_out, gain2, w_up, w_down):
    n = x2d.shape[0]
    tm = TOKEN_TILE // 2

    def row(width):
        return pl.BlockSpec((tm, width), lambda i: (i, 0))

    return pl.pallas_call(
        _merge_ffn_kernel,
        grid=(n // tm,),
        in_specs=[row(D_MODEL), row(NSA_W), row(FOX_W), _const_spec((1, D_MODEL)),
                  _const_spec((D_MODEL, 2 * D_MODEL)), _const_spec((NSA_W, D_MODEL)),
                  _const_spec((FOX_W, D_MODEL)), _const_spec((D_MODEL, D_MODEL)),
                  _const_spec((1, D_MODEL)), _const_spec((D_MODEL, 2 * D_FF)),
                  _const_spec((D_FF, D_MODEL))],
        out_specs=row(D_MODEL),
        out_shape=jax.ShapeDtypeStruct((n, D_MODEL), F32),
        compiler_params=_params(1),
        name="merge_ffn",
    )(x2d, o_nsa, o_fox, mix_norm.reshape(1, D_MODEL), w_gm, w_o_nsa.astype(BF16),
      w_o_fox.astype(BF16), w_out.astype(BF16), gain2.reshape(1, D_MODEL), w_up.astype(BF16),
      w_down.astype(BF16))


def _layer(x, ffn1_norm, ffn1_w_up, ffn1_w_down, mix_norm, w_in, b_forget, nsa_q_gain, nsa_k_gain,
           fox_q_gain, fox_k_gain, cmp_pos_k, cmp_pos_v, cmp_k_w1, cmp_k_w2, cmp_v_w1, cmp_v_w2,
           w_o_nsa, w_o_fox, w_out, ffn2_norm, ffn2_w_up, ffn2_w_down, rel_bias_table):
    batch, seq, d = x.shape
    assert d == D_MODEL and seq % ATT_TILE == 0 and (batch * seq) % TOKEN_TILE == 0
    assert seq // SLC_BLOCK <= LANES and (seq - CMP_BLOCK) // CMP_STRIDE + 1 == LANES - 1
    assert seq // (CMP_BLOCK // 2) == LANES
    x2d = x.reshape(batch * seq, D_MODEL)

    x1 = _ffn(x2d, ffn1_norm, ffn1_w_up, ffn1_w_down)

    w_packed, gain_row, small_bias, w_gm = _pack_in_proj(w_in, b_forget, nsa_q_gain, nsa_k_gain,
                                                         fox_q_gain, fox_k_gain)
    qa, qb, kb, ksw, vb, vsw, kc, vc, small = _in_proj(x1, mix_norm, w_packed, gain_row, small_bias)

    kcmp, vcmp = _compress(kc, vc, batch, seq, _pack_compress(cmp_pos_k, cmp_k_w1, cmp_k_w2),
                           _pack_compress(cmp_pos_v, cmp_v_w1, cmp_v_w2), nsa_k_gain[0])
    toeplitz, cmp_bias = _bias_tables(rel_bias_table, seq // ATT_TILE)
    o_nsa = _nsa(qa, kcmp, vcmp, ksw, vsw, small, toeplitz, cmp_bias, batch, seq)

    n_gate = 3 * NSA_HEADS
    logf_t = small[:, n_gate:n_gate + FOX_HEADS].reshape(batch, seq, FOX_HEADS).transpose(0, 2, 1)
    o_fox = _fox(qb, kb, vb, _cumsum(logf_t), batch, seq)

    w_o_nsa_p = w_o_nsa.reshape(NSA_KV_GROUPS, NSA_Q_PER_GROUP, HEAD_DIM, D_MODEL).transpose(1, 0, 2, 3)
    out = _merge_ffn(x1, o_nsa, o_fox, mix_norm, w_gm, w_o_nsa_p.reshape(NSA_W, D_MODEL), w_o_fox,
                     w_out, ffn2_norm, ffn2_w_up, ffn2_w_down)
    return out.reshape(batch, seq, D_MODEL)


def kernel(x, ffn1_norm, ffn1_w_up, ffn1_w_down, mix_norm, w_in, b_forget, nsa_q_gain, nsa_k_gain,
           fox_q_gain, fox_k_gain, cmp_pos_k, cmp_pos_v, cmp_k_w1, cmp_k_w2, cmp_v_w1, cmp_v_w2,
           w_o_nsa, w_o_fox, w_out, ffn2_norm, ffn2_w_up, ffn2_w_down, rel_bias_table):
    for layer in range(ffn1_norm.shape[0]):
        x = _layer(x, ffn1_norm[layer], ffn1_w_up[layer], ffn1_w_down[layer], mix_norm[layer],
                   w_in[layer], b_forget[layer], nsa_q_gain[layer], nsa_k_gain[layer],
                   fox_q_gain[layer], fox_k_gain[layer], cmp_pos_k[layer], cmp_pos_v[layer],
                   cmp_k_w1[layer], cmp_k_w2[layer], cmp_v_w1[layer], cmp_v_w2[layer],
                   w_o_nsa[layer], w_o_fox[layer], w_out[layer], ffn2_norm[layer],
                   ffn2_w_up[layer], ffn2_w_down[layer], rel_bias_table)
    return x
```

```python
import functools
import math

import numpy as np
import jax
import jax.numpy as jnp
from jax import lax
from jax.experimental import pallas as pl
from jax.experimental.pallas import tpu as pltpu

F32 = jnp.float32
BF16 = jnp.bfloat16

D_MODEL = 1024
HEAD_DIM = 64
NSA_HEADS = 8
NSA_KV_GROUPS = 2
NSA_Q_PER_GROUP = NSA_HEADS // NSA_KV_GROUPS
CMP_BLOCK = 32
CMP_STRIDE = 16
CMP_HIDDEN = 128
SLC_BLOCK = 64
SLC_TOPK = 8
WINDOW = 512
FOX_HEADS = 8
D_FF = 2816
N_BUCKETS = 32
MAX_DISTANCE = 128
RMS_EPS = 1e-6
NEG_INF = -1.0e30
FORCE_BONUS = 1.0e4

NSA_W = NSA_HEADS * HEAD_DIM
NSA_KV_W = NSA_KV_GROUPS * HEAD_DIM
FOX_W = FOX_HEADS * HEAD_DIM
IN_SPLITS = (NSA_W, NSA_KV_W, NSA_KV_W, NSA_KV_W, NSA_KV_W, NSA_KV_W, NSA_KV_W, 3 * NSA_HEADS,
             FOX_W, FOX_W, FOX_W, FOX_HEADS, 2 * D_MODEL)

LANES = 128
TOKEN_TILE = 512
FFN_CHUNK = D_FF // 2
ATT_TILE = 256
N_WIN_TILES = WINDOW // ATT_TILE + 1
VMEM_LIMIT = 56 * 1024 * 1024
LOG2E = 1.4426950408889634

PQ_A = 0
PQ_B = PQ_A + NSA_HEADS * LANES
PK_B = PQ_B + FOX_HEADS * LANES
PK_SW = PK_B + FOX_W
P_NORM_END = PK_SW + 2 * NSA_KV_W
PV_B = P_NORM_END
PV_SW = PV_B + FOX_W
PKV_C = PV_SW + 2 * NSA_KV_W
P_SMALL = PKV_C + 2 * NSA_KV_W
P_END = P_SMALL + LANES
NORM_CHUNK = 256


def _dot(a, b):
    return jnp.dot(a, b, preferred_element_type=F32)


def _dot_nt(a, b):
    return lax.dot_general(a, b, (((1,), (1,)), ((), ())), preferred_element_type=F32)


def _split_dot(x, w):
    hi = x.astype(BF16)
    lo = (x - hi.astype(F32)).astype(BF16)
    return _dot(hi, w) + _dot(lo, w)


def _rms_rows(x, gain_row):
    ms = jnp.mean(x * x, axis=-1, keepdims=True)
    return x * lax.rsqrt(ms + RMS_EPS) * gain_row


def _const_spec(shape):
    nd = len(shape)
    return pl.BlockSpec(shape, lambda *_: (0,) * nd, pipeline_mode=pl.Buffered(1))


def _params(n_axes):
    return pltpu.CompilerParams(dimension_semantics=("arbitrary",) * n_axes,
                                vmem_limit_bytes=VMEM_LIMIT)


def _swiglu_residual(x, gain_row, wup_ref, wdn_ref):
    xn = _rms_rows(x, gain_row).astype(BF16)
    acc = jnp.zeros(x.shape, F32)
    for c in range(D_FF // FFN_CHUNK):
        lo = c * FFN_CHUNK
        gate = _dot(xn, wup_ref[:, lo:lo + FFN_CHUNK])
        up = _dot(xn, wup_ref[:, D_FF + lo:D_FF + lo + FFN_CHUNK])
        h = (gate * jax.nn.sigmoid(gate) * up).astype(BF16)
        acc = acc + _dot(h, wdn_ref[lo:lo + FFN_CHUNK, :])
    return x + 0.5 * acc


def _ffn_kernel(x_ref, g_ref, wup_ref, wdn_ref, o_ref):
    o_ref[...] = _swiglu_residual(x_ref[...], g_ref[...], wup_ref, wdn_ref)


def _ffn(x2d, gain, w_up, w_down):
    n = x2d.shape[0]
    row = pl.BlockSpec((TOKEN_TILE, D_MODEL), lambda i: (i, 0))
    return pl.pallas_call(
        _ffn_kernel,
        grid=(n // TOKEN_TILE,),
        in_specs=[row, _const_spec((1, D_MODEL)), _const_spec((D_MODEL, 2 * D_FF)),
                  _const_spec((D_FF, D_MODEL))],
        out_specs=row,
        out_shape=jax.ShapeDtypeStruct((n, D_MODEL), F32),
        compiler_params=_params(1),
        name="ffn",
    )(x2d, gain.reshape(1, D_MODEL), w_up.astype(BF16), w_down.astype(BF16))


def _block_ones(width, size=NORM_CHUNK):
    idx = np.arange(size) // width
    return jnp.asarray((idx[:, None] == idx[None, :]).astype(np.float32), BF16)


def _pack_in_proj(w_in, b_forget, nsa_q_gain, nsa_k_gain, fox_q_gain, fox_k_gain):
    scale = LOG2E / math.sqrt(HEAD_DIM)
    pts = np.cumsum(np.array(IN_SPLITS))[:-1].tolist()
    qa, kc, vc, ks, vs, kw, vw, ga, qb, kb, vb, fb, gm = jnp.split(w_in, pts, axis=1)
    z_w = jnp.zeros((D_MODEL, HEAD_DIM), F32)
    z_g = jnp.zeros((HEAD_DIM,), F32)
    cols, gains = [], []
    for h in range(NSA_HEADS):
        wh = qa[:, h * HEAD_DIM:(h + 1) * HEAD_DIM]
        first = (h // NSA_Q_PER_GROUP) == 0
        cols += [wh, z_w] if first else [z_w, wh]
        gains += [nsa_q_gain * scale, z_g] if first else [z_g, nsa_q_gain * scale]
    for h in range(FOX_HEADS):
        wh = qb[:, h * HEAD_DIM:(h + 1) * HEAD_DIM]
        first = (h % 2) == 0
        cols += [wh, z_w] if first else [z_w, wh]
        gains += [fox_q_gain * scale, z_g] if first else [z_g, fox_q_gain * scale]
    cols += [kb, ks, kw]
    gains += [jnp.tile(fox_k_gain, FOX_HEADS), jnp.tile(nsa_k_gain[1], NSA_KV_GROUPS),
              jnp.tile(nsa_k_gain[2], NSA_KV_GROUPS)]
    n_small_pad = LANES - ga.shape[1] - fb.shape[1]
    cols += [vb, vs, vw, kc, vc, ga, fb, jnp.zeros((D_MODEL, n_small_pad), F32)]
    w_packed = jnp.concatenate(cols, axis=1).astype(BF16)
    gain_row = jnp.concatenate(gains).reshape(1, P_NORM_END)
    small_bias = jnp.concatenate([jnp.zeros((ga.shape[1],), F32), b_forget,
                                  jnp.zeros((n_small_pad,), F32)]).reshape(1, LANES)
    return w_packed, gain_row, small_bias, gm.astype(BF16)


def _in_proj_kernel(x_ref, g_ref, w_ref, gain_ref, sbias_ref, bd128_ref, bd64_ref,
                    qa_ref, qb_ref, kb_ref, ksw_ref, vb_ref, vsw_ref, kc_ref, vc_ref, small_ref):
    u = _rms_rows(x_ref[...], g_ref[...]).astype(BF16)

    def normed(lo, width, bd_ref):
        parts = []
        for c in range(lo, lo + width, NORM_CHUNK):
            y = _dot(u, w_ref[:, c:c + NORM_CHUNK])
            ss = _split_dot(y * y, bd_ref[...])
            parts.append(y * lax.rsqrt(ss * (1.0 / HEAD_DIM) + RMS_EPS) * gain_ref[:, c:c + NORM_CHUNK])
        return parts

    for i, y in enumerate(normed(PQ_A, NSA_HEADS * LANES, bd128_ref)):
        qa_ref[:, i * NORM_CHUNK:(i + 1) * NORM_CHUNK] = y.astype(BF16)
    for i, y in enumerate(normed(PQ_B, FOX_HEADS * LANES, bd128_ref)):
        qb_ref[:, i * NORM_CHUNK:(i + 1) * NORM_CHUNK] = y.astype(BF16)
    for i, y in enumerate(normed(PK_B, FOX_W, bd64_ref)):
        kb_ref[:, i * NORM_CHUNK:(i + 1) * NORM_CHUNK] = y.astype(BF16)
    ksw_ref[...] = normed(PK_SW, 2 * NSA_KV_W, bd64_ref)[0].astype(BF16)
    low = lax.broadcasted_iota(jnp.int32, (u.shape[0], LANES), 1) < HEAD_DIM

    def store_with_ones(v, o_ref):
        for i in range(v.shape[1] // LANES):
            pair = v[:, i * LANES:(i + 1) * LANES]
            o_ref[:, 2 * i * LANES:(2 * i + 1) * LANES] = jnp.where(low, pair, 1.0).astype(BF16)
            o_ref[:, (2 * i + 1) * LANES:(2 * i + 2) * LANES] = jnp.where(low, 1.0, pair).astype(BF16)

    store_with_ones(_dot(u, w_ref[:, PV_B:PV_B + FOX_W]), vb_ref)
    store_with_ones(_dot(u, w_ref[:, PV_SW:PV_SW + 2 * NSA_KV_W]), vsw_ref)
    kvc = _dot(u, w_ref[:, PKV_C:PKV_C + 2 * NSA_KV_W])
    kc_ref[...] = kvc[:, :NSA_KV_W]
    vc_ref[...] = kvc[:, NSA_KV_W:]
    z = _dot(u, w_ref[:, P_SMALL:P_END]) + sbias_ref[...]
    lane = lax.broadcasted_iota(jnp.int32, z.shape, 1)
    log_sig = jnp.minimum(z, 0.0) - jnp.log1p(jnp.exp(-jnp.abs(z)))
    small_ref[...] = jnp.where(lane < 3 * NSA_HEADS, jax.nn.sigmoid(z), log_sig)


def _in_proj(x2d, mix_norm, w_packed, gain_row, small_bias):
    n = x2d.shape[0]

    def row(width):
        return pl.BlockSpec((TOKEN_TILE, width), lambda i: (i, 0))

    widths = (NSA_HEADS * LANES, FOX_HEADS * LANES, FOX_W, 2 * NSA_KV_W, 2 * FOX_W, 4 * NSA_KV_W,
              NSA_KV_W, NSA_KV_W, LANES)
    dtypes = (BF16, BF16, BF16, BF16, BF16, BF16, F32, F32, F32)
    return pl.pallas_call(
        _in_proj_kernel,
        grid=(n // TOKEN_TILE,),
        in_specs=[row(D_MODEL), _const_spec((1, D_MODEL)), _const_spec((D_MODEL, P_END)),
                  _const_spec((1, P_NORM_END)), _const_spec((1, LANES)),
                  _const_spec((NORM_CHUNK, NORM_CHUNK)), _const_spec((NORM_CHUNK, NORM_CHUNK))],
        out_specs=[row(w) for w in widths],
        out_shape=[jax.ShapeDtypeStruct((n, w), d) for w, d in zip(widths, dtypes)],
        compiler_params=_params(1),
        name="in_proj",
    )(x2d, mix_norm.reshape(1, D_MODEL), w_packed, gain_row, small_bias,
      _block_ones(LANES), _block_ones(HEAD_DIM))


def _pack_compress(pos, w1, w2):
    half = CMP_BLOCK // 2
    eye = jnp.eye(NSA_KV_GROUPS, dtype=F32)
    w1r = w1.reshape(CMP_BLOCK, HEAD_DIM, CMP_HIDDEN)

    def big(w):
        return jnp.einsum('ldh,pg->lpdgh', w, eye).reshape(half * NSA_KV_W, NSA_KV_GROUPS * CMP_HIDDEN)

    def posrow(p):
        return jnp.broadcast_to(p[:, None, :], (half, NSA_KV_GROUPS, HEAD_DIM)).reshape(1, half * NSA_KV_W)

    w2big = jnp.einsum('hd,pg->phgd', w2, eye).reshape(NSA_KV_GROUPS * CMP_HIDDEN, NSA_KV_W)
    return (posrow(pos[:half]), posrow(pos[half:]), big(w1r[:half]).astype(BF16),
            big(w1r[half:]).astype(BF16), w2big.astype(BF16))


def _compress_kernel(kc_ref, vc_ref, kp_lo, kp_hi, kw_lo, kw_hi, kw2, vp_lo, vp_hi, vw_lo, vw_hi, vw2,
                     kgain_ref, bd64_ref, ko_ref, vo_ref):
    def mlp(r, p_lo, p_hi, w_lo, w_hi, w2):
        first = _dot((r + p_lo[...]).astype(BF16), w_lo[...])
        second = _dot((r + p_hi[...]).astype(BF16), w_hi[...])
        h = first + pltpu.roll(second, second.shape[0] - 1, axis=0)
        return _dot((h * jax.nn.sigmoid(h)).astype(BF16), w2[...])

    k = mlp(kc_ref[0], kp_lo, kp_hi, kw_lo, kw_hi, kw2)
    ss = _split_dot(k * k, bd64_ref[...])
    ko_ref[0] = (k * lax.rsqrt(ss * (1.0 / HEAD_DIM) + RMS_EPS) * kgain_ref[...]).astype(BF16)
    vo_ref[0] = mlp(vc_ref[0], vp_lo, vp_hi, vw_lo, vw_hi, vw2).astype(BF16)


def _compress(kc, vc, batch, seq, k_pack, v_pack, k_gain):
    rows = seq // (CMP_BLOCK // 2)
    width = (CMP_BLOCK // 2) * NSA_KV_W
    blk = pl.BlockSpec((1, rows, width), lambda b: (b, 0, 0))
    out = pl.BlockSpec((1, rows, NSA_KV_W), lambda b: (b, 0, 0))
    pack_specs = [_const_spec((1, width)), _const_spec((1, width)),
                  _const_spec((width, NSA_KV_GROUPS * CMP_HIDDEN)),
                  _const_spec((width, NSA_KV_GROUPS * CMP_HIDDEN)),
                  _const_spec((NSA_KV_GROUPS * CMP_HIDDEN, NSA_KV_W))]
    bd = _block_ones(HEAD_DIM, NSA_KV_W)
    return pl.pallas_call(
        _compress_kernel,
        grid=(batch,),
        in_specs=[blk, blk] + pack_specs + pack_specs + [_const_spec((1, NSA_KV_W)),
                                                         _const_spec((NSA_KV_W, NSA_KV_W))],
        out_specs=[out, out],
        out_shape=[jax.ShapeDtypeStruct((batch, rows, NSA_KV_W), BF16)] * 2,
        compiler_params=_params(1),
        name="compress",
    )(kc.reshape(batch, rows, width), vc.reshape(batch, rows, width), *k_pack, *v_pack,
      jnp.tile(k_gain, NSA_KV_GROUPS).reshape(1, NSA_KV_W), bd)


def _cumsum_kernel(x_ref, o_ref):
    x = x_ref[0]
    lane = lax.broadcasted_iota(jnp.int32, x.shape, 1)
    k = 1
    while k < x.shape[1]:
        x = x + jnp.where(lane >= k, pltpu.roll(x, k, axis=1), 0.0)
        k *= 2
    o_ref[0] = x


def _cumsum(logf_t):
    b, h, s = logf_t.shape
    spec = pl.BlockSpec((1, h, s), lambda i: (i, 0, 0))
    return pl.pallas_call(
        _cumsum_kernel, grid=(b,), in_specs=[spec], out_specs=spec,
        out_shape=jax.ShapeDtypeStruct((b, h, s), F32), compiler_params=_params(1), name="cumsum",
    )(logf_t)


def _write_bias_tile(d, valid, tab_ref, o_ref):
    max_exact = N_BUCKETS // 2
    n = jnp.maximum(d, 0)
    nf = jnp.maximum(n, 1).astype(F32)
    large = max_exact + (jnp.log(nf / max_exact) / math.log(MAX_DISTANCE / max_exact)
                         * (N_BUCKETS - max_exact)).astype(jnp.int32)
    bucket = jnp.where(n < max_exact, n, jnp.minimum(large, N_BUCKETS - 1))
    for h in range(NSA_HEADS):
        acc = jnp.zeros(d.shape, F32)
        for b in range(N_BUCKETS):
            acc = jnp.where(bucket == b, tab_ref[b, h], acc)
        g, r = divmod(h, NSA_Q_PER_GROUP)
        o_ref[0, g, r] = jnp.where(valid, acc * LOG2E, NEG_INF)


def _toeplitz_bias_kernel(tab_ref, o_ref):
    s = pl.program_id(0)
    i = lax.broadcasted_iota(jnp.int32, (ATT_TILE, ATT_TILE), 0)
    j = lax.broadcasted_iota(jnp.int32, (ATT_TILE, ATT_TILE), 1)
    d = jnp.minimum(s, N_WIN_TILES - 1) * ATT_TILE + i - j
    d_hi = jnp.where(s < N_WIN_TILES, WINDOW, jnp.int32(1 << 30))
    _write_bias_tile(d, (d >= 0) & (d < d_hi), tab_ref, o_ref)


def _cmp_bias_kernel(tab_ref, o_ref):
    t = pl.program_id(0)
    c = lax.broadcasted_iota(jnp.int32, (LANES, ATT_TILE), 0)
    i = lax.broadcasted_iota(jnp.int32, (LANES, ATT_TILE), 1)
    d = t * ATT_TILE + i - CMP_STRIDE * c - (CMP_BLOCK - 1)
    _write_bias_tile(d, (d >= 0) & (c < LANES - 1), tab_ref, o_ref)


def _bias_tables(rel_bias_table, n_q_tiles):
    def call(body, steps, rows, name):
        shape = (steps, NSA_KV_GROUPS, NSA_Q_PER_GROUP, rows, ATT_TILE)
        return pl.pallas_call(
            body, grid=(steps,),
            in_specs=[pl.BlockSpec(memory_space=pltpu.SMEM)],
            out_specs=pl.BlockSpec((1,) + shape[1:], lambda s: (s, 0, 0, 0, 0)),
            out_shape=jax.ShapeDtypeStruct(shape, F32),
            compiler_params=_params(1), name=name,
        )(rel_bias_table)

    return (call(_toeplitz_bias_kernel, N_WIN_TILES + 1, ATT_TILE, "toeplitz_bias"),
            call(_cmp_bias_kernel, n_q_tiles, LANES, "cmp_bias"))


def _two_pass_attention(chains, segments):
    n = len(chains)

    def half_max(i, dl):
        s = chains[i][0](dl)
        return jnp.maximum(s[:, :LANES], s[:, LANES:])

    mx = [half_max(i, 0) for i in range(n)]
    for lo, hi, members in segments:
        def pass1(dl, carry, members=members):
            return tuple(jnp.maximum(c, half_max(i, dl)) for c, i in zip(carry, members))

        for i, r in zip(members, lax.fori_loop(lo, hi, pass1, tuple(mx[i] for i in members))):
            mx[i] = r
    m_b = [jnp.broadcast_to(jnp.max(m, axis=-1, keepdims=True), (m.shape[0], ATT_TILE)) for m in mx]

    def weights(i, dl):
        return jnp.exp2(chains[i][0](dl) - m_b[i]).astype(BF16)

    acc = [chains[i][1](weights(i, 0), 0) for i in range(n)]
    for lo, hi, members in segments:
        def pass2(dl, carry, members=members):
            return tuple(c + chains[i][1](weights(i, dl), dl) for c, i in zip(carry, members))

        for i, r in zip(members, lax.fori_loop(lo, hi, pass2, tuple(acc[i] for i in members))):
            acc[i] = r
    return acc


def _select_blocks(imp_t, t0):
    blk = lax.broadcasted_iota(jnp.int32, imp_t.shape, 0)
    cur = (t0 + lax.broadcasted_iota(jnp.int32, imp_t.shape, 1)) // SLC_BLOCK
    forced = (blk == 0) | (blk == cur) | (blk == cur - 1)
    score = jnp.where(blk <= cur, imp_t + jnp.where(forced, FORCE_BONUS, 0.0), NEG_INF)
    blk_f = blk.astype(F32)
    dead = -3.0e38
    picked = jnp.zeros(imp_t.shape, F32)
    for _ in range(SLC_TOPK):
        best = jnp.max(score, axis=0, keepdims=True)
        first = jnp.min(jnp.where(score == best, blk_f, float(LANES)), axis=0, keepdims=True)
        hit = blk_f == first
        picked = jnp.where(hit, 1.0, picked)
        score = jnp.where(hit, dead, score)
    return picked


def _nsa_kernel(qa_ref, kcmp_ref, vcmp_ref, ksw_ref, vsw_ref, small_ref, tb_ref, bc_ref, ov_ref, e_ref,
                gx_ref, o_ref, amask_ref, *, n_blocks):
    qt = pl.program_id(1)
    tq = ATT_TILE
    rq = NSA_Q_PER_GROUP
    lane = lax.broadcasted_iota(jnp.int32, (tq, LANES), 1)

    def key_rows(dl):
        return pl.ds(pl.multiple_of((qt - dl) * tq, tq), tq)

    qs, o_cmps = [], []
    for g in range(NSA_KV_GROUPS):
        q = jnp.concatenate([qa_ref[:, (g * rq + r) * LANES:(g * rq + r + 1) * LANES] for r in range(rq)],
                            axis=0)
        qs.append(q)

        s = _dot_nt(kcmp_ref[0], q) + jnp.concatenate([bc_ref[0, g, r] for r in range(rq)], axis=1)
        m = jnp.max(s, axis=0, keepdims=True)
        p = jnp.where(s > 0.5 * NEG_INF, jnp.exp2(s - m), 0.0)
        l = jnp.sum(p, axis=0, keepdims=True)
        p_c = p * (1.0 / jnp.where(l > 0.0, l, 1.0))
        o_cmps.append(_dot(p_c.T.astype(BF16), vcmp_ref[0]))

        p_sum = p_c[:, 0:tq] + p_c[:, tq:2 * tq] + p_c[:, 2 * tq:3 * tq] + p_c[:, 3 * tq:4 * tq]
        hi = p_sum.astype(BF16)
        lo = (p_sum - hi.astype(F32)).astype(BF16)
        imp_t = _dot(ov_ref[...], hi) + _dot(ov_ref[...], lo)
        sel_t = _select_blocks(imp_t[0:n_blocks], qt * tq)
        sel = jnp.concatenate([sel_t, jnp.zeros((LANES - n_blocks, tq), F32)], axis=0).T.astype(BF16)

        def mask_step(kt, carry, g=g, sel=sel):
            cols = pl.ds(pl.multiple_of(kt * tq, tq), tq)
            amask_ref[g, :, cols] = (_dot(sel, e_ref[:, cols]) - 1.0) * (-NEG_INF)
            return carry

        lax.fori_loop(0, qt + 1, mask_step, 0)

    def slc_chain(g):
        def score(dl):
            tile = dl if isinstance(dl, int) else jnp.where(dl < N_WIN_TILES - 1, dl, N_WIN_TILES)
            rows = key_rows(dl)
            s = _dot_nt(qs[g], ksw_ref[rows, 0:LANES]).reshape(rq, tq, tq)
            return (s + tb_ref[tile, g] + amask_ref[g, :, rows][None]).reshape(rq * tq, tq)

        def pv(p, dl):
            return _dot(p, vsw_ref[key_rows(dl), g * LANES:(g + 1) * LANES])

        return score, pv

    def win_chain(g):
        def score(dl):
            s = _dot_nt(qs[g], ksw_ref[key_rows(dl), LANES:2 * LANES]).reshape(rq, tq, tq)
            return (s + tb_ref[dl, g]).reshape(rq * tq, tq)

        def pv(p, dl):
            c0 = (NSA_KV_GROUPS + g) * LANES
            return _dot(p, vsw_ref[key_rows(dl), c0:c0 + LANES])

        return score, pv

    groups = range(NSA_KV_GROUPS)
    n_win = jnp.minimum(qt, N_WIN_TILES - 1)
    slc_ids = tuple(groups)
    all_ids = tuple(range(2 * NSA_KV_GROUPS))
    accs = _two_pass_attention([slc_chain(g) for g in groups] + [win_chain(g) for g in groups],
                               [(1, n_win + 1, all_ids), (n_win + 1, qt + 1, slc_ids)])

    def normalised(acc, g):
        sum_lane = HEAD_DIM if g == 0 else 0
        return acc * (1.0 / acc[:, sum_lane:sum_lane + 1])

    branches = (o_cmps, [normalised(accs[g], g) for g in groups],
                [normalised(accs[NSA_KV_GROUPS + g], g) for g in groups])
    gates = small_ref[...]
    g_hi = gates.astype(BF16)
    g_lo = (gates - g_hi.astype(F32)).astype(BF16)
    out = [jnp.zeros((tq, LANES), F32) for _ in range(rq)]
    for b, per_group in enumerate(branches):
        gate = _dot(g_hi, gx_ref[b]) + _dot(g_lo, gx_ref[b])
        for r in range(rq):
            pair = jnp.where(lane < HEAD_DIM, per_group[0][r * tq:(r + 1) * tq],
                             per_group[1][r * tq:(r + 1) * tq])
            out[r] = out[r] + gate[:, r * LANES:(r + 1) * LANES] * pair
    for r in range(rq):
        o_ref[:, r * LANES:(r + 1) * LANES] = out[r].astype(BF16)


def _nsa(qa, kcmp, vcmp, ksw, vsw, small, toeplitz, cmp_bias, batch, seq):
    tq = ATT_TILE
    nq = seq // tq
    rq = NSA_Q_PER_GROUP
    n_blocks = seq // SLC_BLOCK
    n_cmp = (seq - CMP_BLOCK) // CMP_STRIDE + 1
    ci = np.arange(LANES)[:, None] * CMP_STRIDE
    sj = np.arange(LANES)[None, :] * SLC_BLOCK
    overlap = ((ci <= sj + SLC_BLOCK - 1) & (ci + CMP_BLOCK - 1 >= sj)
               & (np.arange(LANES)[:, None] < n_cmp) & (np.arange(LANES)[None, :] < n_blocks))
    expand = np.arange(LANES)[:, None] == (np.arange(seq)[None, :] // SLC_BLOCK)
    gate_expand = np.zeros((3, LANES, NSA_W), np.float32)
    for b in range(3):
        for g in range(NSA_KV_GROUPS):
            for r in range(rq):
                c0 = r * LANES + g * HEAD_DIM
                gate_expand[b, b * NSA_HEADS + g * rq + r, c0:c0 + HEAD_DIM] = 1.0
    return pl.pallas_call(
        functools.partial(_nsa_kernel, n_blocks=n_blocks),
        grid=(batch, nq),
        in_specs=[
            pl.BlockSpec((tq, NSA_HEADS * LANES), lambda b, t: (b * nq + t, 0)),
            pl.BlockSpec((1, LANES, NSA_KV_W), lambda b, t: (b, 0, 0)),
            pl.BlockSpec((1, LANES, NSA_KV_W), lambda b, t: (b, 0, 0)),
            pl.BlockSpec((seq, 2 * NSA_KV_W), lambda b, t: (b, 0)),
            pl.BlockSpec((seq, 2 * NSA_KV_GROUPS * LANES), lambda b, t: (b, 0)),
            pl.BlockSpec((tq, LANES), lambda b, t: (b * nq + t, 0)),
            _const_spec((N_WIN_TILES + 1, NSA_KV_GROUPS, rq, tq, tq)),
            pl.BlockSpec((1, NSA_KV_GROUPS, rq, LANES, tq), lambda b, t: (t, 0, 0, 0, 0)),
            _const_spec((LANES, LANES)),
            _const_spec((LANES, seq)),
            _const_spec((3, LANES, NSA_W)),
        ],
        out_specs=pl.BlockSpec((tq, NSA_W), lambda b, t: (b * nq + t, 0)),
        out_shape=jax.ShapeDtypeStruct((batch * seq, NSA_W), BF16),
        scratch_shapes=[pltpu.VMEM((NSA_KV_GROUPS, tq, seq), F32)],
        compiler_params=_params(2),
        name="nsa",
    )(qa, kcmp, vcmp, ksw, vsw, small, toeplitz, cmp_bias,
      jnp.asarray(overlap.T.astype(np.float32), BF16), jnp.asarray(expand.astype(np.float32), BF16),
      jnp.asarray(gate_expand, BF16))


def _fox_kernel(q_ref, k_ref, v_ref, cum_ref, o_ref):
    qt = pl.program_id(1)
    tq = ATT_TILE
    pairs = FOX_HEADS // 2
    row = lax.broadcasted_iota(jnp.int32, (2, tq, tq), 1)
    col = lax.broadcasted_iota(jnp.int32, (2, tq, tq), 2)
    lane = lax.broadcasted_iota(jnp.int32, (tq, LANES), 1)

    def key_rows(dl):
        return pl.ds(pl.multiple_of((qt - dl) * tq, tq), tq)

    def pair_chain(p):
        q = jnp.concatenate([q_ref[:, 2 * p * LANES:(2 * p + 1) * LANES],
                             q_ref[:, (2 * p + 1) * LANES:(2 * p + 2) * LANES]], axis=0)
        base = cum_ref[0, 2 * p:2 * p + 2, pl.ds(pl.multiple_of(qt * tq, tq), LANES)][:, 0:1]

        def score(dl):
            rows = key_rows(dl)
            decay = (base - cum_ref[0, 2 * p:2 * p + 2, rows]) * LOG2E
            s = _dot_nt(q, k_ref[rows, p * LANES:(p + 1) * LANES]).reshape(2, tq, tq) + decay[:, None, :]
            if isinstance(dl, int):
                s = jnp.where(col <= row, s, NEG_INF)
            return s.reshape(2 * tq, tq)

        def pv(w, dl):
            rows = key_rows(dl)
            return jnp.concatenate(
                [_dot(w[0:tq], v_ref[rows, 2 * p * LANES:(2 * p + 1) * LANES]),
                 _dot(w[tq:2 * tq], v_ref[rows, (2 * p + 1) * LANES:(2 * p + 2) * LANES])], axis=0)

        return score, pv

    accs = _two_pass_attention([pair_chain(p) for p in range(pairs)], [(1, qt + 1, tuple(range(pairs)))])
    for p, acc in enumerate(accs):
        first, second = acc[0:tq], acc[tq:2 * tq]
        o_ref[:, p * LANES:(p + 1) * LANES] = jnp.where(
            lane < HEAD_DIM, first * (1.0 / first[:, HEAD_DIM:HEAD_DIM + 1]),
            second * (1.0 / second[:, 0:1])).astype(BF16)


def _fox(qb, kb, vb, cum, batch, seq):
    tq = ATT_TILE
    nq = seq // tq
    return pl.pallas_call(
        _fox_kernel,
        grid=(batch, nq),
        in_specs=[
            pl.BlockSpec((tq, FOX_HEADS * LANES), lambda b, t: (b * nq + t, 0)),
            pl.BlockSpec((seq, FOX_W), lambda b, t: (b, 0)),
            pl.BlockSpec((seq, FOX_HEADS * LANES), lambda b, t: (b, 0)),
            pl.BlockSpec((1, FOX_HEADS, seq), lambda b, t: (b, 0, 0)),
        ],
        out_specs=pl.BlockSpec((tq, FOX_W), lambda b, t: (b * nq + t, 0)),
        out_shape=jax.ShapeDtypeStruct((batch * seq, FOX_W), BF16),
        compiler_params=_params(2),
        name="fox",
    )(qb, kb, vb, cum)


def _merge_ffn_kernel(x_ref, on_ref, of_ref, gmix_ref, wgm_ref, won_ref, wof_ref, wout_ref,
                      g2_ref, wup_ref, wdn_ref, o_ref):
    x = x_ref[...]
    u = _rms_rows(x, gmix_ref[...]).astype(BF16)
    gate = jax.nn.sigmoid(_dot(u, wgm_ref[...]))
    merged = (gate[:, :D_MODEL] * _dot(on_ref[...], won_ref[...])
              + gate[:, D_MODEL:] * _dot(of_ref[...], wof_ref[...]))
    x2 = x + _dot(merged.astype(BF16), wout_ref[...])
    o_ref[...] = _swiglu_residual(x2, g2_ref[...], wup_ref, wdn_ref)


def _merge_ffn(x2d, o_nsa, o_fox, mix_norm, w_gm, w_o_nsa, w_o_fox, w_out, gain2, w_up, w_down):
    n = x2d.shape[0]
    tm = TOKEN_TILE // 2

    def row(width):
        return pl.BlockSpec((tm, width), lambda i: (i, 0))

    return pl.pallas_call(
        _merge_ffn_kernel,
        grid=(n // tm,),
        in_specs=[row(D_MODEL), row(NSA_W), row(FOX_W), _const_spec((1, D_MODEL)),
                  _const_spec((D_MODEL, 2 * D_MODEL)), _const_spec((NSA_W, D_MODEL)),
                  _const_spec((FOX_W, D_MODEL)), _const_spec((D_MODEL, D_MODEL)),
                  _const_spec((1, D_MODEL)), _const_spec((D_MODEL, 2 * D_FF)),
                  _const_spec((D_FF, D_MODEL))],
        out_specs=row(D_MODEL),
        out_shape=jax.ShapeDtypeStruct((n, D_MODEL), F32),
        compiler_params=_params(1),
        name="merge_ffn",
    )(x2d, o_nsa, o_fox, mix_norm.reshape(1, D_MODEL), w_gm, w_o_nsa.astype(BF16),
      w_o_fox.astype(BF16), w_out.astype(BF16), gain2.reshape(1, D_MODEL), w_up.astype(BF16),
      w_down.astype(BF16))


def _layer(x, ffn1_norm, ffn1_w_up, ffn1_w_down, mix_norm, w_in, b_forget, nsa_q_gain, nsa_k_gain,
           fox_q_gain, fox_k_gain, cmp_pos_k, cmp_pos_v, cmp_k_w1, cmp_k_w2, cmp_v_w1, cmp_v_w2,
           w_o_nsa, w_o_fox, w_out, ffn2_norm, ffn2_w_up, ffn2_w_down, rel_bias_table):
    batch, seq, d = x.shape
    assert d == D_MODEL and seq % ATT_TILE == 0 and (batch * seq) % TOKEN_TILE == 0
    assert seq // SLC_BLOCK <= LANES and (seq - CMP_BLOCK) // CMP_STRIDE + 1 == LANES - 1
    assert seq // (CMP_BLOCK // 2) == LANES
    x2d = x.reshape(batch * seq, D_MODEL)

    x1 = _ffn(x2d, ffn1_norm, ffn1_w_up, ffn1_w_down)

    w_packed, gain_row, small_bias, w_gm = _pack_in_proj(w_in, b_forget, nsa_q_gain, nsa_k_gain,
                                                         fox_q_gain, fox_k_gain)
    qa, qb, kb, ksw, vb, vsw, kc, vc, small = _in_proj(x1, mix_norm, w_packed, gain_row, small_bias)

    kcmp, vcmp = _compress(kc, vc, batch, seq, _pack_compress(cmp_pos_k, cmp_k_w1, cmp_k_w2),
                           _pack_compress(cmp_pos_v, cmp_v_w1, cmp_v_w2), nsa_k_gain[0])
    toeplitz, cmp_bias = _bias_tables(rel_bias_table, seq // ATT_TILE)
    o_nsa = _nsa(qa, kcmp, vcmp, ksw, vsw, small, toeplitz, cmp_bias, batch, seq)

    n_gate = 3 * NSA_HEADS
    logf_t = small[:, n_gate:n_gate + FOX_HEADS].reshape(batch, seq, FOX_HEADS).transpose(0, 2, 1)
    o_fox = _fox(qb, kb, vb, _cumsum(logf_t), batch, seq)

    w_o_nsa_p = w_o_nsa.reshape(NSA_KV_GROUPS, NSA_Q_PER_GROUP, HEAD_DIM, D_MODEL).transpose(1, 0, 2, 3)
    out = _merge_ffn(x1, o_nsa, o_fox, mix_norm, w_gm, w_o_nsa_p.reshape(NSA_W, D_MODEL), w_o_fox,
                     w_out, ffn2_norm, ffn2_w_up, ffn2_w_down)
    return out.reshape(batch, seq, D_MODEL)


def kernel(x, ffn1_norm, ffn1_w_up, ffn1_w_down, mix_norm, w_in, b_forget, nsa_q_gain, nsa_k_gain,
           fox_q_gain, fox_k_gain, cmp_pos_k, cmp_pos_v, cmp_k_w1, cmp_k_w2, cmp_v_w1, cmp_v_w2,
           w_o_nsa, w_o_fox, w_out, ffn2_norm, ffn2_w_up, ffn2_w_down, rel_bias_table):
    for layer in range(ffn1_norm.shape[0]):
        x = _layer(x, ffn1_norm[layer], ffn1_w_up[layer], ffn1_w_down[layer], mix_norm[layer],
                   w_in[layer], b_forget[layer], nsa_q_gain[layer], nsa_k_gain[layer],
                   fox_q_gain[layer], fox_k_gain[layer], cmp_pos_k[layer], cmp_pos_v[layer],
                   cmp_k_w1[layer], cmp_k_w2[layer], cmp_v_w1[layer], cmp_v_w2[layer],
                   w_o_nsa[layer], w_o_fox[layer], w_out[layer], ffn2_norm[layer],
                   ffn2_w_up[layer], ffn2_w_down[layer], rel_bias_table)
    return x
```

```python
import functools
import math

import numpy as np
import jax
import jax.numpy as jnp
from jax import lax
from jax.experimental import pallas as pl
from jax.experimental.pallas import tpu as pltpu

F32 = jnp.float32
BF16 = jnp.bfloat16

D_MODEL = 1024
HEAD_DIM = 64
NSA_HEADS = 8
NSA_KV_GROUPS = 2
NSA_Q_PER_GROUP = NSA_HEADS // NSA_KV_GROUPS
CMP_BLOCK = 32
CMP_STRIDE = 16
CMP_HIDDEN = 128
SLC_BLOCK = 64
SLC_TOPK = 8
WINDOW = 512
FOX_HEADS = 8
D_FF = 2816
N_BUCKETS = 32
MAX_DISTANCE = 128
RMS_EPS = 1e-6
NEG_INF = -1.0e30
FORCE_BONUS = 1.0e4

NSA_W = NSA_HEADS * HEAD_DIM
NSA_KV_W = NSA_KV_GROUPS * HEAD_DIM
FOX_W = FOX_HEADS * HEAD_DIM
IN_SPLITS = (NSA_W, NSA_KV_W, NSA_KV_W, NSA_KV_W, NSA_KV_W, NSA_KV_W, NSA_KV_W, 3 * NSA_HEADS,
             FOX_W, FOX_W, FOX_W, FOX_HEADS, 2 * D_MODEL)

LANES = 128
TOKEN_TILE = 512
FFN_CHUNK = D_FF // 2
ATT_TILE = 256
N_WIN_TILES = WINDOW // ATT_TILE + 1
VMEM_LIMIT = 56 * 1024 * 1024
LOG2E = 1.4426950408889634
EXP2_RANGE = 100.0

PQ_A = 0
PQ_B = PQ_A + NSA_W
PK_B = PQ_B + FOX_W
PK_SW = PK_B + FOX_W
P_NORM_END = PK_SW + 2 * NSA_KV_W
PV_B = P_NORM_END
PV_SW = PV_B + FOX_W
PKV_C = PV_SW + 2 * NSA_KV_W
P_SMALL = PKV_C + 2 * NSA_KV_W
P_END = P_SMALL + LANES
NORM_CHUNK = 256


def _dot(a, b):
    return jnp.dot(a, b, preferred_element_type=F32)


def _dot_nt(a, b):
    return lax.dot_general(a, b, (((1,), (1,)), ((), ())), preferred_element_type=F32)


def _split_dot(x, w):
    hi = x.astype(BF16)
    lo = (x - hi.astype(F32)).astype(BF16)
    return _dot(hi, w) + _dot(lo, w)


def _rms_rows(x, gain_row):
    ms = jnp.mean(x * x, axis=-1, keepdims=True)
    return x * lax.rsqrt(ms + RMS_EPS) * gain_row


def _const_spec(shape):
    nd = len(shape)
    return pl.BlockSpec(shape, lambda *_: (0,) * nd, pipeline_mode=pl.Buffered(1))


def _params(n_axes):
    return pltpu.CompilerParams(dimension_semantics=("arbitrary",) * n_axes,
                                vmem_limit_bytes=VMEM_LIMIT)


def _swiglu_residual(x, gain_row, wup_ref, wdn_ref):
    xn = _rms_rows(x, gain_row).astype(BF16)
    acc = jnp.zeros(x.shape, F32)
    for c in range(D_FF // FFN_CHUNK):
        lo = c * FFN_CHUNK
        gate = _dot(xn, wup_ref[:, lo:lo + FFN_CHUNK])
        up = _dot(xn, wup_ref[:, D_FF + lo:D_FF + lo + FFN_CHUNK])
        h = (gate * jax.nn.sigmoid(gate) * up).astype(BF16)
        acc = acc + _dot(h, wdn_ref[lo:lo + FFN_CHUNK, :])
    return x + 0.5 * acc


def _ffn_kernel(x_ref, g_ref, wup_ref, wdn_ref, o_ref):
    o_ref[...] = _swiglu_residual(x_ref[...], g_ref[...], wup_ref, wdn_ref)


def _ffn(x2d, gain, w_up, w_down):
    n = x2d.shape[0]
    row = pl.BlockSpec((TOKEN_TILE, D_MODEL), lambda i: (i, 0))
    return pl.pallas_call(
        _ffn_kernel,
        grid=(n // TOKEN_TILE,),
        in_specs=[row, _const_spec((1, D_MODEL)), _const_spec((D_MODEL, 2 * D_FF)),
                  _const_spec((D_FF, D_MODEL))],
        out_specs=row,
        out_shape=jax.ShapeDtypeStruct((n, D_MODEL), F32),
        compiler_params=_params(1),
        name="ffn",
    )(x2d, gain.reshape(1, D_MODEL), w_up.astype(BF16), w_down.astype(BF16))


def _block_ones(width, size=NORM_CHUNK):
    idx = np.arange(size) // width
    return jnp.asarray((idx[:, None] == idx[None, :]).astype(np.float32), BF16)


def _pack_in_proj(w_in, b_forget, nsa_q_gain, nsa_k_gain, fox_q_gain, fox_k_gain):
    scale = LOG2E / math.sqrt(HEAD_DIM)
    pts = np.cumsum(np.array(IN_SPLITS))[:-1].tolist()
    qa, kc, vc, ks, vs, kw, vw, ga, qb, kb, vb, fb, gm = jnp.split(w_in, pts, axis=1)
    cols = []
    for r in range(NSA_Q_PER_GROUP):
        for g in range(NSA_KV_GROUPS):
            h = g * NSA_Q_PER_GROUP + r
            cols.append(qa[:, h * HEAD_DIM:(h + 1) * HEAD_DIM])
    cols += [qb, kb, ks, kw]
    gains = [jnp.tile(nsa_q_gain * scale, NSA_HEADS), jnp.tile(fox_q_gain * scale, FOX_HEADS),
             jnp.tile(fox_k_gain, FOX_HEADS), jnp.tile(nsa_k_gain[1], NSA_KV_GROUPS),
             jnp.tile(nsa_k_gain[2], NSA_KV_GROUPS)]
    n_small_pad = LANES - ga.shape[1] - fb.shape[1]
    cols += [vb, vs, vw, kc, vc, ga, fb, jnp.zeros((D_MODEL, n_small_pad), F32)]
    w_packed = jnp.concatenate(cols, axis=1).astype(BF16)
    gain_row = jnp.concatenate(gains).reshape(1, P_NORM_END)
    small_bias = jnp.concatenate([jnp.zeros((ga.shape[1],), F32), b_forget,
                                  jnp.zeros((n_small_pad,), F32)]).reshape(1, LANES)
    return w_packed, gain_row, small_bias, gm.astype(BF16)


def _in_proj_kernel(x_ref, g_ref, w_ref, gain_ref, sbias_ref, bd64_ref,
                    qa_ref, qb_ref, kb_ref, ksw_ref, vb_ref, vsw_ref, kc_ref, vc_ref, small_ref):
    u = _rms_rows(x_ref[...], g_ref[...]).astype(BF16)

    def store_normed(lo, o_ref):
        for i in range(o_ref.shape[1] // NORM_CHUNK):
            c = lo + i * NORM_CHUNK
            y = _dot(u, w_ref[:, c:c + NORM_CHUNK])
            ss = _dot((y * y).astype(BF16), bd64_ref[...])
            y = y * lax.rsqrt(ss * (1.0 / HEAD_DIM) + RMS_EPS) * gain_ref[:, c:c + NORM_CHUNK]
            o_ref[:, i * NORM_CHUNK:(i + 1) * NORM_CHUNK] = y.astype(BF16)

    store_normed(PQ_A, qa_ref)
    store_normed(PQ_B, qb_ref)
    store_normed(PK_B, kb_ref)
    store_normed(PK_SW, ksw_ref)
    low = lax.broadcasted_iota(jnp.int32, (u.shape[0], LANES), 1) < HEAD_DIM

    def store_with_ones(v, o_ref):
        for i in range(v.shape[1] // LANES):
            pair = v[:, i * LANES:(i + 1) * LANES]
            o_ref[:, 2 * i * LANES:(2 * i + 1) * LANES] = jnp.where(low, pair, 1.0).astype(BF16)
            o_ref[:, (2 * i + 1) * LANES:(2 * i + 2) * LANES] = jnp.where(low, 1.0, pair).astype(BF16)

    store_with_ones(_dot(u, w_ref[:, PV_B:PV_B + FOX_W]), vb_ref)
    store_with_ones(_dot(u, w_ref[:, PV_SW:PV_SW + 2 * NSA_KV_W]), vsw_ref)
    kvc = _dot(u, w_ref[:, PKV_C:PKV_C + 2 * NSA_KV_W])
    kc_ref[...] = kvc[:, :NSA_KV_W]
    vc_ref[...] = kvc[:, NSA_KV_W:]
    z = _dot(u, w_ref[:, P_SMALL:P_END]) + sbias_ref[...]
    lane = lax.broadcasted_iota(jnp.int32, z.shape, 1)
    log_sig = jnp.minimum(z, 0.0) - jnp.log1p(jnp.exp(-jnp.abs(z)))
    small_ref[...] = jnp.where(lane < 3 * NSA_HEADS, jax.nn.sigmoid(z), log_sig)


def _in_proj(x2d, mix_norm, w_packed, gain_row, small_bias):
    n = x2d.shape[0]

    def row(width):
        return pl.BlockSpec((TOKEN_TILE, width), lambda i: (i, 0))

    widths = (NSA_W, FOX_W, FOX_W, 2 * NSA_KV_W, 2 * FOX_W, 4 * NSA_KV_W, NSA_KV_W, NSA_KV_W, LANES)
    dtypes = (BF16, BF16, BF16, BF16, BF16, BF16, F32, F32, F32)
    return pl.pallas_call(
        _in_proj_kernel,
        grid=(n // TOKEN_TILE,),
        in_specs=[row(D_MODEL), _const_spec((1, D_MODEL)), _const_spec((D_MODEL, P_END)),
                  _const_spec((1, P_NORM_END)), _const_spec((1, LANES)),
                  _const_spec((NORM_CHUNK, NORM_CHUNK))],
        out_specs=[row(w) for w in widths],
        out_shape=[jax.ShapeDtypeStruct((n, w), d) for w, d in zip(widths, dtypes)],
        compiler_params=_params(1),
        name="in_proj",
    )(x2d, mix_norm.reshape(1, D_MODEL), w_packed, gain_row, small_bias, _block_ones(HEAD_DIM))


def _pack_compress(pos, w1, w2):
    half = CMP_BLOCK // 2
    eye = jnp.eye(NSA_KV_GROUPS, dtype=F32)
    w1r = w1.reshape(CMP_BLOCK, HEAD_DIM, CMP_HIDDEN)

    def big(w):
        return jnp.einsum('ldh,pg->lpdgh', w, eye).reshape(half * NSA_KV_W, NSA_KV_GROUPS * CMP_HIDDEN)

    def posrow(p):
        return jnp.broadcast_to(p[:, None, :], (half, NSA_KV_GROUPS, HEAD_DIM)).reshape(1, half * NSA_KV_W)

    w2big = jnp.einsum('hd,pg->phgd', w2, eye).reshape(NSA_KV_GROUPS * CMP_HIDDEN, NSA_KV_W)
    return (posrow(pos[:half]), posrow(pos[half:]), big(w1r[:half]).astype(BF16),
            big(w1r[half:]).astype(BF16), w2big.astype(BF16))


def _compress_kernel(kc_ref, vc_ref, kp_lo, kp_hi, kw_lo, kw_hi, kw2, vp_lo, vp_hi, vw_lo, vw_hi, vw2,
                     kgain_ref, bd64_ref, ko_ref, vo_ref):
    def mlp(r, p_lo, p_hi, w_lo, w_hi, w2):
        first = _dot((r + p_lo[...]).astype(BF16), w_lo[...])
        second = _dot((r + p_hi[...]).astype(BF16), w_hi[...])
        h = first + pltpu.roll(second, second.shape[0] - 1, axis=0)
        return _dot((h * jax.nn.sigmoid(h)).astype(BF16), w2[...])

    k = mlp(kc_ref[0], kp_lo, kp_hi, kw_lo, kw_hi, kw2)
    ss = _split_dot(k * k, bd64_ref[...])
    ko_ref[0] = (k * lax.rsqrt(ss * (1.0 / HEAD_DIM) + RMS_EPS) * kgain_ref[...]).astype(BF16)
    vo_ref[0] = mlp(vc_ref[0], vp_lo, vp_hi, vw_lo, vw_hi, vw2).astype(BF16)


def _compress(kc, vc, batch, seq, k_pack, v_pack, k_gain):
    rows = seq // (CMP_BLOCK // 2)
    width = (CMP_BLOCK // 2) * NSA_KV_W
    blk = pl.BlockSpec((1, rows, width), lambda b: (b, 0, 0))
    out = pl.BlockSpec((1, rows, NSA_KV_W), lambda b: (b, 0, 0))
    pack_specs = [_const_spec((1, width)), _const_spec((1, width)),
                  _const_spec((width, NSA_KV_GROUPS * CMP_HIDDEN)),
                  _const_spec((width, NSA_KV_GROUPS * CMP_HIDDEN)),
                  _const_spec((NSA_KV_GROUPS * CMP_HIDDEN, NSA_KV_W))]
    bd = _block_ones(HEAD_DIM, NSA_KV_W)
    return pl.pallas_call(
        _compress_kernel,
        grid=(batch,),
        in_specs=[blk, blk] + pack_specs + pack_specs + [_const_spec((1, NSA_KV_W)),
                                                         _const_spec((NSA_KV_W, NSA_KV_W))],
        out_specs=[out, out],
        out_shape=[jax.ShapeDtypeStruct((batch, rows, NSA_KV_W), BF16)] * 2,
        compiler_params=_params(1),
        name="compress",
    )(kc.reshape(batch, rows, width), vc.reshape(batch, rows, width), *k_pack, *v_pack,
      jnp.tile(k_gain, NSA_KV_GROUPS).reshape(1, NSA_KV_W), bd)


def _cumsum_kernel(x_ref, o_ref):
    x = x_ref[0]
    lane = lax.broadcasted_iota(jnp.int32, x.shape, 1)
    k = 1
    while k < x.shape[1]:
        x = x + jnp.where(lane >= k, pltpu.roll(x, k, axis=1), 0.0)
        k *= 2
    o_ref[0] = x


def _cumsum(logf_t):
    b, h, s = logf_t.shape
    spec = pl.BlockSpec((1, h, s), lambda i: (i, 0, 0))
    return pl.pallas_call(
        _cumsum_kernel, grid=(b,), in_specs=[spec], out_specs=spec,
        out_shape=jax.ShapeDtypeStruct((b, h, s), F32), compiler_params=_params(1), name="cumsum",
    )(logf_t)


def _write_bias_tile(d, valid, tab_ref, o_ref):
    max_exact = N_BUCKETS // 2
    n = jnp.maximum(d, 0)
    nf = jnp.maximum(n, 1).astype(F32)
    large = max_exact + (jnp.log(nf / max_exact) / math.log(MAX_DISTANCE / max_exact)
                         * (N_BUCKETS - max_exact)).astype(jnp.int32)
    bucket = jnp.where(n < max_exact, n, jnp.minimum(large, N_BUCKETS - 1))
    for h in range(NSA_HEADS):
        acc = jnp.zeros(d.shape, F32)
        for b in range(N_BUCKETS):
            acc = jnp.where(bucket == b, tab_ref[b, h], acc)
        g, r = divmod(h, NSA_Q_PER_GROUP)
        o_ref[0, g, r] = jnp.where(valid, acc * LOG2E, NEG_INF)


def _toeplitz_bias_kernel(tab_ref, o_ref):
    s = pl.program_id(0)
    i = lax.broadcasted_iota(jnp.int32, (ATT_TILE, ATT_TILE), 0)
    j = lax.broadcasted_iota(jnp.int32, (ATT_TILE, ATT_TILE), 1)
    d = jnp.minimum(s, N_WIN_TILES - 1) * ATT_TILE + i - j
    d_hi = jnp.where(s < N_WIN_TILES, WINDOW, jnp.int32(1 << 30))
    _write_bias_tile(d, (d >= 0) & (d < d_hi), tab_ref, o_ref)


def _cmp_bias_kernel(tab_ref, o_ref):
    t = pl.program_id(0)
    c = lax.broadcasted_iota(jnp.int32, (LANES, ATT_TILE), 0)
    i = lax.broadcasted_iota(jnp.int32, (LANES, ATT_TILE), 1)
    d = t * ATT_TILE + i - CMP_STRIDE * c - (CMP_BLOCK - 1)
    _write_bias_tile(d, (d >= 0) & (c < LANES - 1), tab_ref, o_ref)


def _bias_tables(rel_bias_table, n_q_tiles):
    def call(body, steps, rows, name):
        shape = (steps, NSA_KV_GROUPS, NSA_Q_PER_GROUP, rows, ATT_TILE)
        return pl.pallas_call(
            body, grid=(steps,),
            in_specs=[pl.BlockSpec(memory_space=pltpu.SMEM)],
            out_specs=pl.BlockSpec((1,) + shape[1:], lambda s: (s, 0, 0, 0, 0)),
            out_shape=jax.ShapeDtypeStruct(shape, F32),
            compiler_params=_params(1), name=name,
        )(rel_bias_table)

    return (call(_toeplitz_bias_kernel, N_WIN_TILES + 1, ATT_TILE, "toeplitz_bias"),
            call(_cmp_bias_kernel, n_q_tiles, LANES, "cmp_bias"))


def _score_bound(q_gain, k_gain):
    return 1.02 * LOG2E * math.sqrt(HEAD_DIM) * jnp.max(jnp.abs(q_gain)) * jnp.max(jnp.abs(k_gain))


def _attention(chains, groups, bounds, bounds_ok, acc_ref):
    n = len(chains)

    def later(k):
        return tuple(i for _, ids in groups[k:] for i in ids)

    def exact_max():
        def half_max(i, dl):
            s = chains[i][0](dl)
            return jnp.maximum(s[:, :LANES], s[:, LANES:])

        mx = [half_max(i, 0) for i in range(n)]
        lo = 1
        for k, (last, _) in enumerate(groups):
            members = later(k)

            def body(dl, carry, members=members):
                return tuple(jnp.maximum(c, half_max(i, dl)) for c, i in zip(carry, members))

            for i, r in zip(members, lax.fori_loop(lo, last + 1, body, tuple(mx[i] for i in members))):
                mx[i] = r
            lo = last + 1
        return tuple(jnp.broadcast_to(jnp.max(m, axis=-1, keepdims=True), (m.shape[0], ATT_TILE))
                     for m in mx)

    shift = lax.cond(bounds_ok, lambda: tuple(bounds), exact_max)

    def weights(i, dl):
        return jnp.exp2(chains[i][0](dl) - shift[i]).astype(BF16)

    for i in range(n):
        acc_ref[i] = chains[i][1](weights(i, 0), 0)
    lo = 1
    for k, (last, _) in enumerate(groups):
        def body(dl, carry, members=later(k)):
            for i in members:
                acc_ref[i] += chains[i][1](weights(i, dl), dl)
            return carry

        lax.fori_loop(lo, last + 1, body, 0)
        lo = last + 1
    return [acc_ref[i] for i in range(n)]


def _select_blocks(imp_t, t0):
    blk = lax.broadcasted_iota(jnp.int32, imp_t.shape, 0)
    cur = (t0 + lax.broadcasted_iota(jnp.int32, imp_t.shape, 1)) // SLC_BLOCK
    forced = (blk == 0) | (blk == cur) | (blk == cur - 1)
    score = jnp.where(blk <= cur, imp_t + jnp.where(forced, FORCE_BONUS, 0.0), NEG_INF)
    blk_f = blk.astype(F32)
    dead = -3.0e38
    picked = jnp.zeros(imp_t.shape, F32)
    for _ in range(SLC_TOPK):
        best = jnp.max(score, axis=0, keepdims=True)
        first = jnp.min(jnp.where(score == best, blk_f, float(LANES)), axis=0, keepdims=True)
        hit = blk_f == first
        picked = jnp.where(hit, 1.0, picked)
        score = jnp.where(hit, dead, score)
    return picked


def _nsa_kernel(bnd_ref, qa_ref, kcmp_ref, vcmp_ref, ksw_ref, vsw_ref, small_ref, tb_ref, bc_ref, ov_ref,
                e_ref, gx_ref, o_ref, amask_ref, acc_ref, *, n_blocks):
    qt = pl.program_id(1)
    tq = ATT_TILE
    rq = NSA_Q_PER_GROUP
    lane = lax.broadcasted_iota(jnp.int32, (tq, LANES), 1)
    own_lanes = [jnp.where((lane // HEAD_DIM) == g, 1.0, 0.0).astype(BF16) for g in range(NSA_KV_GROUPS)]

    def key_rows(dl):
        return pl.ds(pl.multiple_of((qt - dl) * tq, tq), tq)

    qs, o_cmps = [], []
    for g in range(NSA_KV_GROUPS):
        q = jnp.concatenate([qa_ref[:, r * LANES:(r + 1) * LANES] * own_lanes[g] for r in range(rq)],
                            axis=0)
        qs.append(q)

        s = _dot_nt(kcmp_ref[0], q) + jnp.concatenate([bc_ref[0, g, r] for r in range(rq)], axis=1)
        m = jnp.max(s, axis=0, keepdims=True)
        p = jnp.where(s > 0.5 * NEG_INF, jnp.exp2(s - m), 0.0)
        l = jnp.sum(p, axis=0, keepdims=True)
        p_c = p * (1.0 / jnp.where(l > 0.0, l, 1.0))
        o_cmps.append(_dot(p_c.T.astype(BF16), vcmp_ref[0]))

        p_sum = p_c[:, 0:tq] + p_c[:, tq:2 * tq] + p_c[:, 2 * tq:3 * tq] + p_c[:, 3 * tq:4 * tq]
        hi = p_sum.astype(BF16)
        lo = (p_sum - hi.astype(F32)).astype(BF16)
        imp_t = _dot(ov_ref[...], hi) + _dot(ov_ref[...], lo)
        sel_t = _select_blocks(imp_t[0:n_blocks], qt * tq)
        sel = jnp.concatenate([sel_t, jnp.zeros((LANES - n_blocks, tq), F32)], axis=0).T.astype(BF16)

        def mask_step(kt, carry, g=g, sel=sel):
            cols = pl.ds(pl.multiple_of(kt * tq, tq), tq)
            amask_ref[g, :, cols] = (_dot(sel, e_ref[:, cols]) - 1.0) * (-NEG_INF)
            return carry

        lax.fori_loop(0, qt + 1, mask_step, 0)

    def slc_chain(g):
        def score(dl):
            tile = dl if isinstance(dl, int) else jnp.where(dl < N_WIN_TILES - 1, dl, N_WIN_TILES)
            rows = key_rows(dl)
            s = _dot_nt(qs[g], ksw_ref[rows, 0:LANES]).reshape(rq, tq, tq)
            return (s + tb_ref[tile, g] + amask_ref[g, :, rows][None]).reshape(rq * tq, tq)

        def pv(p, dl):
            return _dot(p, vsw_ref[key_rows(dl), g * LANES:(g + 1) * LANES])

        return score, pv

    def win_chain(g):
        def score(dl):
            s = _dot_nt(qs[g], ksw_ref[key_rows(dl), LANES:2 * LANES]).reshape(rq, tq, tq)
            return (s + tb_ref[dl, g]).reshape(rq * tq, tq)

        def pv(p, dl):
            c0 = (NSA_KV_GROUPS + g) * LANES
            return _dot(p, vsw_ref[key_rows(dl), c0:c0 + LANES])

        return score, pv

    groups = range(NSA_KV_GROUPS)
    slc_ids = tuple(groups)
    win_ids = tuple(NSA_KV_GROUPS + g for g in groups)
    bounds = ([jnp.full((rq * tq, tq), bnd_ref[1], F32) for _ in groups]
              + [jnp.full((rq * tq, tq), bnd_ref[2], F32) for _ in groups])
    accs = _attention([slc_chain(g) for g in groups] + [win_chain(g) for g in groups],
                      [(jnp.minimum(qt, N_WIN_TILES - 1), win_ids), (qt, slc_ids)],
                      bounds, bnd_ref[0] > 0.5, acc_ref)

    def normalised(acc, g):
        sum_lane = HEAD_DIM if g == 0 else 0
        return acc * (1.0 / acc[:, sum_lane:sum_lane + 1])

    branches = (o_cmps, [normalised(accs[g], g) for g in groups],
                [normalised(accs[NSA_KV_GROUPS + g], g) for g in groups])
    gates = small_ref[...]
    g_hi = gates.astype(BF16)
    g_lo = (gates - g_hi.astype(F32)).astype(BF16)
    out = [jnp.zeros((tq, LANES), F32) for _ in range(rq)]
    for b, per_group in enumerate(branches):
        gate = _dot(g_hi, gx_ref[b]) + _dot(g_lo, gx_ref[b])
        for r in range(rq):
            pair = jnp.where(lane < HEAD_DIM, per_group[0][r * tq:(r + 1) * tq],
                             per_group[1][r * tq:(r + 1) * tq])
            out[r] = out[r] + gate[:, r * LANES:(r + 1) * LANES] * pair
    for r in range(rq):
        o_ref[:, r * LANES:(r + 1) * LANES] = out[r].astype(BF16)


def _nsa(bounds, qa, kcmp, vcmp, ksw, vsw, small, toeplitz, cmp_bias, batch, seq):
    tq = ATT_TILE
    nq = seq // tq
    rq = NSA_Q_PER_GROUP
    n_blocks = seq // SLC_BLOCK
    n_cmp = (seq - CMP_BLOCK) // CMP_STRIDE + 1
    ci = np.arange(LANES)[:, None] * CMP_STRIDE
    sj = np.arange(LANES)[None, :] * SLC_BLOCK
    overlap = ((ci <= sj + SLC_BLOCK - 1) & (ci + CMP_BLOCK - 1 >= sj)
               & (np.arange(LANES)[:, None] < n_cmp) & (np.arange(LANES)[None, :] < n_blocks))
    expand = np.arange(LANES)[:, None] == (np.arange(seq)[None, :] // SLC_BLOCK)
    gate_expand = np.zeros((3, LANES, NSA_W), np.float32)
    for b in range(3):
        for g in range(NSA_KV_GROUPS):
            for r in range(rq):
                c0 = r * LANES + g * HEAD_DIM
                gate_expand[b, b * NSA_HEADS + g * rq + r, c0:c0 + HEAD_DIM] = 1.0
    return pl.pallas_call(
        functools.partial(_nsa_kernel, n_blocks=n_blocks),
        grid=(batch, nq),
        in_specs=[
            pl.BlockSpec(memory_space=pltpu.SMEM),
            pl.BlockSpec((tq, NSA_W), lambda b, t: (b * nq + t, 0)),
            pl.BlockSpec((1, LANES, NSA_KV_W), lambda b, t: (b, 0, 0)),
            pl.BlockSpec((1, LANES, NSA_KV_W), lambda b, t: (b, 0, 0)),
            pl.BlockSpec((seq, 2 * NSA_KV_W), lambda b, t: (b, 0)),
            pl.BlockSpec((seq, 2 * NSA_KV_GROUPS * LANES), lambda b, t: (b, 0)),
            pl.BlockSpec((tq, LANES), lambda b, t: (b * nq + t, 0)),
            _const_spec((N_WIN_TILES + 1, NSA_KV_GROUPS, rq, tq, tq)),
            pl.BlockSpec((1, NSA_KV_GROUPS, rq, LANES, tq), lambda b, t: (t, 0, 0, 0, 0)),
            _const_spec((LANES, LANES)),
            _const_spec((LANES, seq)),
            _const_spec((3, LANES, NSA_W)),
        ],
        out_specs=pl.BlockSpec((tq, NSA_W), lambda b, t: (b * nq + t, 0)),
        out_shape=jax.ShapeDtypeStruct((batch * seq, NSA_W), BF16),
        scratch_shapes=[pltpu.VMEM((NSA_KV_GROUPS, tq, seq), F32),
                        pltpu.VMEM((2 * NSA_KV_GROUPS, rq * tq, LANES), F32)],
        compiler_params=_params(2),
        name="nsa",
    )(bounds, qa, kcmp, vcmp, ksw, vsw, small, toeplitz, cmp_bias,
      jnp.asarray(overlap.T.astype(np.float32), BF16), jnp.asarray(expand.astype(np.float32), BF16),
      jnp.asarray(gate_expand, BF16))


def _fox_kernel(bnd_ref, q_ref, k_ref, v_ref, cum_ref, cum_t_ref, o_ref, acc_ref):
    qt = pl.program_id(1)
    tq = ATT_TILE
    pairs = FOX_HEADS // 2
    row = lax.broadcasted_iota(jnp.int32, (2, tq, tq), 1)
    col = lax.broadcasted_iota(jnp.int32, (2, tq, tq), 2)
    lane = lax.broadcasted_iota(jnp.int32, (tq, LANES), 1)
    low = jnp.where(lane < HEAD_DIM, 1.0, 0.0).astype(BF16)
    high = jnp.where(lane < HEAD_DIM, 0.0, 1.0).astype(BF16)
    cum_rows = cum_t_ref[...]

    def key_rows(dl):
        return pl.ds(pl.multiple_of((qt - dl) * tq, tq), tq)

    bounds = []

    def pair_chain(p):
        pair = q_ref[:, p * LANES:(p + 1) * LANES]
        q = jnp.concatenate([pair * low, pair * high], axis=0)
        base = cum_ref[0, 2 * p:2 * p + 2, pl.ds(pl.multiple_of(qt * tq, tq), LANES)][:, 0:1]
        own = jnp.concatenate([base[h:h + 1, :] - cum_rows[:, 2 * p + h:2 * p + h + 1] for h in range(2)],
                              axis=0)
        bounds.append(jnp.broadcast_to(own * LOG2E + bnd_ref[1], (2 * tq, tq)))

        def score(dl):
            rows = key_rows(dl)
            decay = (base - cum_ref[0, 2 * p:2 * p + 2, rows]) * LOG2E
            s = _dot_nt(q, k_ref[rows, p * LANES:(p + 1) * LANES]).reshape(2, tq, tq) + decay[:, None, :]
            if isinstance(dl, int):
                s = jnp.where(col <= row, s, NEG_INF)
            return s.reshape(2 * tq, tq)

        def pv(w, dl):
            rows = key_rows(dl)
            return jnp.concatenate(
                [_dot(w[0:tq], v_ref[rows, 2 * p * LANES:(2 * p + 1) * LANES]),
                 _dot(w[tq:2 * tq], v_ref[rows, (2 * p + 1) * LANES:(2 * p + 2) * LANES])], axis=0)

        return score, pv

    chains = [pair_chain(p) for p in range(pairs)]
    accs = _attention(chains, [(qt, tuple(range(pairs)))], bounds, bnd_ref[0] > 0.5, acc_ref)
    for p, acc in enumerate(accs):
        first, second = acc[0:tq], acc[tq:2 * tq]
        o_ref[:, p * LANES:(p + 1) * LANES] = jnp.where(
            lane < HEAD_DIM, first * (1.0 / first[:, HEAD_DIM:HEAD_DIM + 1]),
            second * (1.0 / second[:, 0:1])).astype(BF16)


def _fox(bounds, qb, kb, vb, cum, batch, seq):
    tq = ATT_TILE
    nq = seq // tq
    cum_t = cum.transpose(0, 2, 1).reshape(batch * seq, FOX_HEADS)
    return pl.pallas_call(
        _fox_kernel,
        grid=(batch, nq),
        in_specs=[
            pl.BlockSpec(memory_space=pltpu.SMEM),
            pl.BlockSpec((tq, FOX_W), lambda b, t: (b * nq + t, 0)),
            pl.BlockSpec((seq, FOX_W), lambda b, t: (b, 0)),
            pl.BlockSpec((seq, FOX_HEADS * LANES), lambda b, t: (b, 0)),
            pl.BlockSpec((1, FOX_HEADS, seq), lambda b, t: (b, 0, 0)),
            pl.BlockSpec((tq, FOX_HEADS), lambda b, t: (b * nq + t, 0)),
        ],
        out_specs=pl.BlockSpec((tq, FOX_W), lambda b, t: (b * nq + t, 0)),
        out_shape=jax.ShapeDtypeStruct((batch * seq, FOX_W), BF16),
        scratch_shapes=[pltpu.VMEM((FOX_HEADS // 2, 2 * tq, LANES), F32)],
        compiler_params=_params(2),
        name="fox",
    )(bounds, qb, kb, vb, cum, cum_t)


def _merge_ffn_kernel(x_ref, on_ref, of_ref, gmix_ref, wgm_ref, won_ref, wof_ref, wout_ref,
                      g2_ref, wup_ref, wdn_ref, o_ref):
    x = x_ref[...]
    u = _rms_rows(x, gmix_ref[...]).astype(BF16)
    gate = jax.nn.sigmoid(_dot(u, wgm_ref[...]))
    merged = (gate[:, :D_MODEL] * _dot(on_ref[...], won_ref[...])
              + gate[:, D_MODEL:] * _dot(of_ref[...], wof_ref[...]))
    x2 = x + _dot(merged.astype(BF16), wout_ref[...])
    o_ref[...] = _swiglu_residual(x2, g2_ref[...], wup_ref, wdn_ref)


def _merge_ffn(x2d, o_nsa, o_fox, mix_norm, w_gm, w_o_nsa, w_o_fox, w_out, gain2, w_up, w_down):
    n = x2d.shape[0]
    tm = TOKEN_TILE // 2

    def row(width):
        return pl.BlockSpec((tm, width), lambda i: (i, 0))

    return pl.pallas_call(
        _merge_ffn_kernel,
        grid=(n // tm,),
        in_specs=[row(D_MODEL), row(NSA_W), row(FOX_W), _const_spec((1, D_MODEL)),
                  _const_spec((D_MODEL, 2 * D_MODEL)), _const_spec((NSA_W, D_MODEL)),
                  _const_spec((FOX_W, D_MODEL)), _const_spec((D_MODEL, D_MODEL)),
                  _const_spec((1, D_MODEL)), _const_spec((D_MODEL, 2 * D_FF)),
                  _const_spec((D_FF, D_MODEL))],
        out_specs=row(D_MODEL),
        out_shape=jax.ShapeDtypeStruct((n, D_MODEL), F32),
        compiler_params=_params(1),
        name="merge_ffn",
    )(x2d, o_nsa, o_fox, mix_norm.reshape(1, D_MODEL), w_gm, w_o_nsa.astype(BF16),
      w_o_fox.astype(BF16), w_out.astype(BF16), gain2.reshape(1, D_MODEL), w_up.astype(BF16),
      w_down.astype(BF16))


def _layer(x, ffn1_norm, ffn1_w_up, ffn1_w_down, mix_norm, w_in, b_forget, nsa_q_gain, nsa_k_gain,
           fox_q_gain, fox_k_gain, cmp_pos_k, cmp_pos_v, cmp_k_w1, cmp_k_w2, cmp_v_w1, cmp_v_w2,
           w_o_nsa, w_o_fox, w_out, ffn2_norm, ffn2_w_up, ffn2_w_down, rel_bias_table):
    batch, seq, d = x.shape
    assert d == D_MODEL and seq % ATT_TILE == 0 and (batch * seq) % TOKEN_TILE == 0
    assert seq // SLC_BLOCK <= LANES and (seq - CMP_BLOCK) // CMP_STRIDE + 1 == LANES - 1
    assert seq // (CMP_BLOCK // 2) == LANES
    x2d = x.reshape(batch * seq, D_MODEL)

    x1 = _ffn(x2d, ffn1_norm, ffn1_w_up, ffn1_w_down)

    w_packed, gain_row, small_bias, w_gm = _pack_in_proj(w_in, b_forget, nsa_q_gain, nsa_k_gain,
                                                         fox_q_gain, fox_k_gain)
    qa, qb, kb, ksw, vb, vsw, kc, vc, small = _in_proj(x1, mix_norm, w_packed, gain_row, small_bias)

    kcmp, vcmp = _compress(kc, vc, batch, seq, _pack_compress(cmp_pos_k, cmp_k_w1, cmp_k_w2),
                           _pack_compress(cmp_pos_v, cmp_v_w1, cmp_v_w2), nsa_k_gain[0])
    toeplitz, cmp_bias = _bias_tables(rel_bias_table, seq // ATT_TILE)
    b_slc = _score_bound(nsa_q_gain, nsa_k_gain[1])
    b_win = _score_bound(nsa_q_gain, nsa_k_gain[2])
    t_hi, t_lo = jnp.max(rel_bias_table) * LOG2E, jnp.min(rel_bias_table) * LOG2E
    nsa_ok = 2.0 * jnp.maximum(b_slc, b_win) + (t_hi - t_lo) < EXP2_RANGE
    nsa_bounds = jnp.stack([nsa_ok.astype(F32), b_slc + t_hi, b_win + t_hi])
    o_nsa = _nsa(nsa_bounds, qa, kcmp, vcmp, ksw, vsw, small, toeplitz, cmp_bias, batch, seq)

    n_gate = 3 * NSA_HEADS
    logf_t = small[:, n_gate:n_gate + FOX_HEADS].reshape(batch, seq, FOX_HEADS).transpose(0, 2, 1)
    b_fox = _score_bound(fox_q_gain, fox_k_gain)
    fox_bounds = jnp.stack([(2.0 * b_fox < EXP2_RANGE).astype(F32), b_fox])
    o_fox = _fox(fox_bounds, qb, kb, vb, _cumsum(logf_t), batch, seq)

    w_o_nsa_p = w_o_nsa.reshape(NSA_KV_GROUPS, NSA_Q_PER_GROUP, HEAD_DIM, D_MODEL).transpose(1, 0, 2, 3)
    out = _merge_ffn(x1, o_nsa, o_fox, mix_norm, w_gm, w_o_nsa_p.reshape(NSA_W, D_MODEL), w_o_fox,
                     w_out, ffn2_norm, ffn2_w_up, ffn2_w_down)
    return out.reshape(batch, seq, D_MODEL)


def kernel(x, ffn1_norm, ffn1_w_up, ffn1_w_down, mix_norm, w_in, b_forget, nsa_q_gain, nsa_k_gain,
           fox_q_gain, fox_k_gain, cmp_pos_k, cmp_pos_v, cmp_k_w1, cmp_k_w2, cmp_v_w1, cmp_v_w2,
           w_o_nsa, w_o_fox, w_out, ffn2_norm, ffn2_w_up, ffn2_w_down, rel_bias_table):
    for layer in range(ffn1_norm.shape[0]):
        x = _layer(x, ffn1_norm[layer], ffn1_w_up[layer], ffn1_w_down[layer], mix_norm[layer],
                   w_in[layer], b_forget[layer], nsa_q_gain[layer], nsa_k_gain[layer],
                   fox_q_gain[layer], fox_k_gain[layer], cmp_pos_k[layer], cmp_pos_v[layer],
                   cmp_k_w1[layer], cmp_k_w2[layer], cmp_v_w1[layer], cmp_v_w2[layer],
                   w_o_nsa[layer], w_o_fox[layer], w_out[layer], ffn2_norm[layer],
                   ffn2_w_up[layer], ffn2_w_down[layer], rel_bias_table)
    return x
```

```python
import functools
import math

import numpy as np
import jax
import jax.numpy as jnp
from jax import lax
from jax.experimental import pallas as pl
from jax.experimental.pallas import tpu as pltpu

F32 = jnp.float32
BF16 = jnp.bfloat16

D_MODEL = 1024
HEAD_DIM = 64
NSA_HEADS = 8
NSA_KV_GROUPS = 2
NSA_Q_PER_GROUP = NSA_HEADS // NSA_KV_GROUPS
CMP_BLOCK = 32
CMP_STRIDE = 16
CMP_HIDDEN = 128
SLC_BLOCK = 64
SLC_TOPK = 8
WINDOW = 512
FOX_HEADS = 8
D_FF = 2816
N_BUCKETS = 32
MAX_DISTANCE = 128
RMS_EPS = 1e-6
NEG_INF = -1.0e30
FORCE_BONUS = 1.0e4

NSA_W = NSA_HEADS * HEAD_DIM
NSA_KV_W = NSA_KV_GROUPS * HEAD_DIM
FOX_W = FOX_HEADS * HEAD_DIM
IN_SPLITS = (NSA_W, NSA_KV_W, NSA_KV_W, NSA_KV_W, NSA_KV_W, NSA_KV_W, NSA_KV_W, 3 * NSA_HEADS,
             FOX_W, FOX_W, FOX_W, FOX_HEADS, 2 * D_MODEL)

LANES = 128
TOKEN_TILE = 512
FFN_CHUNK = D_FF // 2
ATT_TILE = 256
N_WIN_TILES = WINDOW // ATT_TILE + 1
VMEM_LIMIT = 56 * 1024 * 1024
LOG2E = 1.4426950408889634
EXP2_RANGE = 100.0

PQ_A = 0
PQ_B = PQ_A + NSA_W
PK_B = PQ_B + FOX_W
PK_SW = PK_B + FOX_W
P_NORM_END = PK_SW + 2 * NSA_KV_W
PV_B = P_NORM_END
PV_SW = PV_B + FOX_W
PKV_C = PV_SW + 2 * NSA_KV_W
P_SMALL = PKV_C + 2 * NSA_KV_W
P_END = P_SMALL + LANES
NORM_CHUNK = 256


def _dot(a, b):
    return jnp.dot(a, b, preferred_element_type=F32)


def _dot_nt(a, b):
    return lax.dot_general(a, b, (((1,), (1,)), ((), ())), preferred_element_type=F32)


def _split_dot(x, w):
    hi = x.astype(BF16)
    lo = (x - hi.astype(F32)).astype(BF16)
    return _dot(hi, w) + _dot(lo, w)


def _rms_rows(x, gain_row):
    ms = jnp.mean(x * x, axis=-1, keepdims=True)
    return x * lax.rsqrt(ms + RMS_EPS) * gain_row


def _const_spec(shape):
    nd = len(shape)
    return pl.BlockSpec(shape, lambda *_: (0,) * nd, pipeline_mode=pl.Buffered(1))


def _params(n_axes):
    return pltpu.CompilerParams(dimension_semantics=("arbitrary",) * n_axes,
                                vmem_limit_bytes=VMEM_LIMIT)


def _swiglu_residual(x, gain_row, wup_ref, wdn_ref):
    xn = _rms_rows(x, gain_row).astype(BF16)
    acc = jnp.zeros(x.shape, F32)
    for c in range(D_FF // FFN_CHUNK):
        lo = c * FFN_CHUNK
        gate = _dot(xn, wup_ref[:, lo:lo + FFN_CHUNK])
        up = _dot(xn, wup_ref[:, D_FF + lo:D_FF + lo + FFN_CHUNK])
        h = (gate * jax.nn.sigmoid(gate) * up).astype(BF16)
        acc = acc + _dot(h, wdn_ref[lo:lo + FFN_CHUNK, :])
    return x + 0.5 * acc


def _ffn_kernel(x_ref, g_ref, wup_ref, wdn_ref, o_ref):
    o_ref[...] = _swiglu_residual(x_ref[...], g_ref[...], wup_ref, wdn_ref)


def _ffn(x2d, gain, w_up, w_down):
    n = x2d.shape[0]
    row = pl.BlockSpec((TOKEN_TILE, D_MODEL), lambda i: (i, 0))
    return pl.pallas_call(
        _ffn_kernel,
        grid=(n // TOKEN_TILE,),
        in_specs=[row, _const_spec((1, D_MODEL)), _const_spec((D_MODEL, 2 * D_FF)),
                  _const_spec((D_FF, D_MODEL))],
        out_specs=row,
        out_shape=jax.ShapeDtypeStruct((n, D_MODEL), F32),
        compiler_params=_params(1),
        name="ffn",
    )(x2d, gain.reshape(1, D_MODEL), w_up.astype(BF16), w_down.astype(BF16))


def _block_ones(width, size=NORM_CHUNK):
    idx = np.arange(size) // width
    return jnp.asarray((idx[:, None] == idx[None, :]).astype(np.float32), BF16)


def _pack_in_proj(w_in, b_forget, nsa_q_gain, nsa_k_gain, fox_q_gain, fox_k_gain):
    scale = LOG2E / math.sqrt(HEAD_DIM)
    pts = np.cumsum(np.array(IN_SPLITS))[:-1].tolist()
    qa, kc, vc, ks, vs, kw, vw, ga, qb, kb, vb, fb, gm = jnp.split(w_in, pts, axis=1)
    cols = []
    for r in range(NSA_Q_PER_GROUP):
        for g in range(NSA_KV_GROUPS):
            h = g * NSA_Q_PER_GROUP + r
            cols.append(qa[:, h * HEAD_DIM:(h + 1) * HEAD_DIM])
    cols += [qb, kb, ks, kw]
    gains = [jnp.tile(nsa_q_gain * scale, NSA_HEADS), jnp.tile(fox_q_gain * scale, FOX_HEADS),
             jnp.tile(fox_k_gain, FOX_HEADS), jnp.tile(nsa_k_gain[1], NSA_KV_GROUPS),
             jnp.tile(nsa_k_gain[2], NSA_KV_GROUPS)]
    n_small_pad = LANES - ga.shape[1] - fb.shape[1]
    cols += [vb, vs, vw, kc, vc, ga, fb, jnp.zeros((D_MODEL, n_small_pad), F32)]
    w_packed = jnp.concatenate(cols, axis=1).astype(BF16)
    gain_row = jnp.concatenate(gains).reshape(1, P_NORM_END)
    small_bias = jnp.concatenate([jnp.zeros((ga.shape[1],), F32), b_forget,
                                  jnp.zeros((n_small_pad,), F32)]).reshape(1, LANES)
    return w_packed, gain_row, small_bias, gm.astype(BF16)


def _in_proj_kernel(x_ref, g_ref, w_ref, gain_ref, sbias_ref, bd64_ref,
                    qa_ref, qb_ref, kb_ref, ksw_ref, vb_ref, vsw_ref, kc_ref, vc_ref, small_ref):
    u = _rms_rows(x_ref[...], g_ref[...]).astype(BF16)
    tm = u.shape[0]

    y = _dot(u, w_ref[:, 0:P_NORM_END])
    n_chunks = P_NORM_END // NORM_CHUNK
    squares = jnp.concatenate([jnp.square(y[:, c * NORM_CHUNK:(c + 1) * NORM_CHUNK]).astype(BF16)
                               for c in range(n_chunks)], axis=0)
    ss = _dot(squares, bd64_ref[...])
    chunk = 0
    for o_ref in (qa_ref, qb_ref, kb_ref, ksw_ref):
        for i in range(o_ref.shape[1] // NORM_CHUNK):
            cols = slice(chunk * NORM_CHUNK, (chunk + 1) * NORM_CHUNK)
            inv = lax.rsqrt(ss[chunk * tm:(chunk + 1) * tm] * (1.0 / HEAD_DIM) + RMS_EPS)
            o_ref[:, i * NORM_CHUNK:(i + 1) * NORM_CHUNK] = (y[:, cols] * inv * gain_ref[:, cols]).astype(BF16)
            chunk += 1
    low = lax.broadcasted_iota(jnp.int32, (u.shape[0], LANES), 1) < HEAD_DIM

    def store_with_ones(v, o_ref):
        for i in range(v.shape[1] // LANES):
            pair = v[:, i * LANES:(i + 1) * LANES]
            o_ref[:, 2 * i * LANES:(2 * i + 1) * LANES] = jnp.where(low, pair, 1.0).astype(BF16)
            o_ref[:, (2 * i + 1) * LANES:(2 * i + 2) * LANES] = jnp.where(low, 1.0, pair).astype(BF16)

    rest = _dot(u, w_ref[:, P_NORM_END:P_END])
    store_with_ones(rest[:, PV_B - P_NORM_END:PV_SW - P_NORM_END], vb_ref)
    store_with_ones(rest[:, PV_SW - P_NORM_END:PKV_C - P_NORM_END], vsw_ref)
    kc_ref[...] = rest[:, PKV_C - P_NORM_END:PKV_C - P_NORM_END + NSA_KV_W]
    vc_ref[...] = rest[:, PKV_C - P_NORM_END + NSA_KV_W:P_SMALL - P_NORM_END]
    z = rest[:, P_SMALL - P_NORM_END:] + sbias_ref[...]
    lane = lax.broadcasted_iota(jnp.int32, z.shape, 1)
    log_sig = jnp.minimum(z, 0.0) - jnp.log1p(jnp.exp(-jnp.abs(z)))
    small_ref[...] = jnp.where(lane < 3 * NSA_HEADS, jax.nn.sigmoid(z), log_sig)


def _in_proj(x2d, mix_norm, w_packed, gain_row, small_bias):
    n = x2d.shape[0]

    def row(width):
        return pl.BlockSpec((TOKEN_TILE, width), lambda i: (i, 0))

    widths = (NSA_W, FOX_W, FOX_W, 2 * NSA_KV_W, 2 * FOX_W, 4 * NSA_KV_W, NSA_KV_W, NSA_KV_W, LANES)
    dtypes = (BF16, BF16, BF16, BF16, BF16, BF16, F32, F32, F32)
    return pl.pallas_call(
        _in_proj_kernel,
        grid=(n // TOKEN_TILE,),
        in_specs=[row(D_MODEL), _const_spec((1, D_MODEL)), _const_spec((D_MODEL, P_END)),
                  _const_spec((1, P_NORM_END)), _const_spec((1, LANES)),
                  _const_spec((NORM_CHUNK, NORM_CHUNK))],
        out_specs=[row(w) for w in widths],
        out_shape=[jax.ShapeDtypeStruct((n, w), d) for w, d in zip(widths, dtypes)],
        compiler_params=_params(1),
        name="in_proj",
    )(x2d, mix_norm.reshape(1, D_MODEL), w_packed, gain_row, small_bias, _block_ones(HEAD_DIM))


def _pack_compress(pos, w1, w2):
    half = CMP_BLOCK // 2
    eye = jnp.eye(NSA_KV_GROUPS, dtype=F32)
    w1r = w1.reshape(CMP_BLOCK, HEAD_DIM, CMP_HIDDEN)

    def big(w):
        return jnp.einsum('ldh,pg->lpdgh', w, eye).reshape(half * NSA_KV_W, NSA_KV_GROUPS * CMP_HIDDEN)

    def posrow(p):
        return jnp.broadcast_to(p[:, None, :], (half, NSA_KV_GROUPS, HEAD_DIM)).reshape(1, half * NSA_KV_W)

    w2big = jnp.einsum('hd,pg->phgd', w2, eye).reshape(NSA_KV_GROUPS * CMP_HIDDEN, NSA_KV_W)
    return (posrow(pos[:half]), posrow(pos[half:]), big(w1r[:half]).astype(BF16),
            big(w1r[half:]).astype(BF16), w2big.astype(BF16))


def _compress_kernel(kc_ref, vc_ref, kp_lo, kp_hi, kw_lo, kw_hi, kw2, vp_lo, vp_hi, vw_lo, vw_hi, vw2,
                     kgain_ref, bd64_ref, ko_ref, vo_ref):
    def mlp(r, p_lo, p_hi, w_lo, w_hi, w2):
        first = _dot((r + p_lo[...]).astype(BF16), w_lo[...])
        second = _dot((r + p_hi[...]).astype(BF16), w_hi[...])
        h = first + pltpu.roll(second, second.shape[0] - 1, axis=0)
        return _dot((h * jax.nn.sigmoid(h)).astype(BF16), w2[...])

    k = mlp(kc_ref[0], kp_lo, kp_hi, kw_lo, kw_hi, kw2)
    ss = _split_dot(k * k, bd64_ref[...])
    ko_ref[0] = (k * lax.rsqrt(ss * (1.0 / HEAD_DIM) + RMS_EPS) * kgain_ref[...]).astype(BF16)
    vo_ref[0] = mlp(vc_ref[0], vp_lo, vp_hi, vw_lo, vw_hi, vw2).astype(BF16)


def _compress(kc, vc, batch, seq, k_pack, v_pack, k_gain):
    rows = seq // (CMP_BLOCK // 2)
    width = (CMP_BLOCK // 2) * NSA_KV_W
    blk = pl.BlockSpec((1, rows, width), lambda b: (b, 0, 0))
    out = pl.BlockSpec((1, rows, NSA_KV_W), lambda b: (b, 0, 0))
    pack_specs = [_const_spec((1, width)), _const_spec((1, width)),
                  _const_spec((width, NSA_KV_GROUPS * CMP_HIDDEN)),
                  _const_spec((width, NSA_KV_GROUPS * CMP_HIDDEN)),
                  _const_spec((NSA_KV_GROUPS * CMP_HIDDEN, NSA_KV_W))]
    bd = _block_ones(HEAD_DIM, NSA_KV_W)
    return pl.pallas_call(
        _compress_kernel,
        grid=(batch,),
        in_specs=[blk, blk] + pack_specs + pack_specs + [_const_spec((1, NSA_KV_W)),
                                                         _const_spec((NSA_KV_W, NSA_KV_W))],
        out_specs=[out, out],
        out_shape=[jax.ShapeDtypeStruct((batch, rows, NSA_KV_W), BF16)] * 2,
        compiler_params=_params(1),
        name="compress",
    )(kc.reshape(batch, rows, width), vc.reshape(batch, rows, width), *k_pack, *v_pack,
      jnp.tile(k_gain, NSA_KV_GROUPS).reshape(1, NSA_KV_W), bd)


def _cumsum_kernel(x_ref, o_ref):
    x = x_ref[0]
    lane = lax.broadcasted_iota(jnp.int32, x.shape, 1)
    k = 1
    while k < x.shape[1]:
        x = x + jnp.where(lane >= k, pltpu.roll(x, k, axis=1), 0.0)
        k *= 2
    o_ref[0] = x


def _cumsum(logf_t):
    b, h, s = logf_t.shape
    spec = pl.BlockSpec((1, h, s), lambda i: (i, 0, 0))
    return pl.pallas_call(
        _cumsum_kernel, grid=(b,), in_specs=[spec], out_specs=spec,
        out_shape=jax.ShapeDtypeStruct((b, h, s), F32), compiler_params=_params(1), name="cumsum",
    )(logf_t)


def _write_bias_tile(d, valid, tab_ref, o_ref):
    max_exact = N_BUCKETS // 2
    n = jnp.maximum(d, 0)
    nf = jnp.maximum(n, 1).astype(F32)
    large = max_exact + (jnp.log(nf / max_exact) / math.log(MAX_DISTANCE / max_exact)
                         * (N_BUCKETS - max_exact)).astype(jnp.int32)
    bucket = jnp.where(n < max_exact, n, jnp.minimum(large, N_BUCKETS - 1))
    for h in range(NSA_HEADS):
        acc = jnp.zeros(d.shape, F32)
        for b in range(N_BUCKETS):
            acc = jnp.where(bucket == b, tab_ref[b, h], acc)
        g, r = divmod(h, NSA_Q_PER_GROUP)
        o_ref[0, g, r] = jnp.where(valid, acc * LOG2E, NEG_INF)


def _toeplitz_bias_kernel(tab_ref, o_ref):
    s = pl.program_id(0)
    i = lax.broadcasted_iota(jnp.int32, (ATT_TILE, ATT_TILE), 0)
    j = lax.broadcasted_iota(jnp.int32, (ATT_TILE, ATT_TILE), 1)
    d = jnp.minimum(s, N_WIN_TILES - 1) * ATT_TILE + i - j
    d_hi = jnp.where(s < N_WIN_TILES, WINDOW, jnp.int32(1 << 30))
    _write_bias_tile(d, (d >= 0) & (d < d_hi), tab_ref, o_ref)


def _cmp_bias_kernel(tab_ref, o_ref):
    t = pl.program_id(0)
    c = lax.broadcasted_iota(jnp.int32, (LANES, ATT_TILE), 0)
    i = lax.broadcasted_iota(jnp.int32, (LANES, ATT_TILE), 1)
    d = t * ATT_TILE + i - CMP_STRIDE * c - (CMP_BLOCK - 1)
    _write_bias_tile(d, (d >= 0) & (c < LANES - 1), tab_ref, o_ref)


def _bias_tables(rel_bias_table, n_q_tiles):
    def call(body, steps, rows, name):
        shape = (steps, NSA_KV_GROUPS, NSA_Q_PER_GROUP, rows, ATT_TILE)
        return pl.pallas_call(
            body, grid=(steps,),
            in_specs=[pl.BlockSpec(memory_space=pltpu.SMEM)],
            out_specs=pl.BlockSpec((1,) + shape[1:], lambda s: (s, 0, 0, 0, 0)),
            out_shape=jax.ShapeDtypeStruct(shape, F32),
            compiler_params=_params(1), name=name,
        )(rel_bias_table)

    return (call(_toeplitz_bias_kernel, N_WIN_TILES + 1, ATT_TILE, "toeplitz_bias"),
            call(_cmp_bias_kernel, n_q_tiles, LANES, "cmp_bias"))


def _score_bound(q_gain, k_gain):
    return 1.02 * LOG2E * math.sqrt(HEAD_DIM) * jnp.max(jnp.abs(q_gain)) * jnp.max(jnp.abs(k_gain))


def _attention(chains, groups, bounds, bounds_ok, acc_ref):
    n = len(chains)

    def later(k):
        return tuple(i for _, ids in groups[k:] for i in ids)

    def exact_max():
        def half_max(i, dl):
            s = chains[i][0](dl)
            return jnp.maximum(s[:, :LANES], s[:, LANES:])

        mx = [half_max(i, 0) for i in range(n)]
        lo = 1
        for k, (last, _) in enumerate(groups):
            members = later(k)

            def body(dl, carry, members=members):
                return tuple(jnp.maximum(c, half_max(i, dl)) for c, i in zip(carry, members))

            for i, r in zip(members, lax.fori_loop(lo, last + 1, body, tuple(mx[i] for i in members))):
                mx[i] = r
            lo = last + 1
        return tuple(jnp.broadcast_to(jnp.max(m, axis=-1, keepdims=True), (m.shape[0], ATT_TILE))
                     for m in mx)

    shift = lax.cond(bounds_ok, lambda: tuple(bounds), exact_max)

    def weights(i, dl):
        return jnp.exp2(chains[i][0](dl) - shift[i]).astype(BF16)

    for i in range(n):
        acc_ref[i] = chains[i][1](weights(i, 0), 0)
    lo = 1
    for k, (last, _) in enumerate(groups):
        def body(dl, carry, members=later(k)):
            for i in members:
                acc_ref[i] += chains[i][1](weights(i, dl), dl)
            return carry

        lax.fori_loop(lo, last + 1, body, 0)
        lo = last + 1
    return [acc_ref[i] for i in range(n)]


def _pair_lanes(first, second, normalise):
    low = lax.broadcasted_iota(jnp.int32, first.shape, 1) < HEAD_DIM
    pair = jnp.where(low, first, second)
    if not normalise:
        return pair
    sums = pltpu.roll(jnp.where(low, second, first), HEAD_DIM, axis=1)
    return pair * (1.0 / sums)


def _select_blocks(imp_t, t0):
    blk = lax.broadcasted_iota(jnp.int32, imp_t.shape, 0)
    cur = (t0 + lax.broadcasted_iota(jnp.int32, imp_t.shape, 1)) // SLC_BLOCK
    forced = (blk == 0) | (blk == cur) | (blk == cur - 1)
    score = jnp.where(blk <= cur, imp_t + jnp.where(forced, FORCE_BONUS, 0.0), NEG_INF)
    blk_f = blk.astype(F32)
    dead = -3.0e38
    picked = jnp.zeros(imp_t.shape, F32)
    for _ in range(SLC_TOPK):
        best = jnp.max(score, axis=0, keepdims=True)
        first = jnp.min(jnp.where(score == best, blk_f, float(LANES)), axis=0, keepdims=True)
        hit = blk_f == first
        picked = jnp.where(hit, 1.0, picked)
        score = jnp.where(hit, dead, score)
    return picked


def _nsa_kernel(bnd_ref, qa_ref, kcmp_ref, vcmp_ref, ksw_ref, vsw_ref, small_ref, tb_ref, bc_ref, ov_ref,
                e_ref, gx_ref, o_ref, amask_ref, acc_ref, *, n_blocks):
    qt = pl.program_id(1)
    tq = ATT_TILE
    rq = NSA_Q_PER_GROUP
    lane = lax.broadcasted_iota(jnp.int32, (tq, LANES), 1)
    own_lanes = [jnp.where((lane // HEAD_DIM) == g, 1.0, 0.0).astype(BF16) for g in range(NSA_KV_GROUPS)]

    def key_rows(dl):
        return pl.ds(pl.multiple_of((qt - dl) * tq, tq), tq)

    qs, o_cmps = [], []
    for g in range(NSA_KV_GROUPS):
        q = jnp.concatenate([qa_ref[:, r * LANES:(r + 1) * LANES] * own_lanes[g] for r in range(rq)],
                            axis=0)
        qs.append(q)

        s = _dot_nt(kcmp_ref[0], q) + jnp.concatenate([bc_ref[0, g, r] for r in range(rq)], axis=1)
        m = jnp.max(s, axis=0, keepdims=True)
        p = jnp.where(s > 0.5 * NEG_INF, jnp.exp2(s - m), 0.0)
        l = jnp.sum(p, axis=0, keepdims=True)
        p_c = p * (1.0 / jnp.where(l > 0.0, l, 1.0))
        o_cmps.append(_dot(p_c.T.astype(BF16), vcmp_ref[0]))

        p_sum = p_c[:, 0:tq] + p_c[:, tq:2 * tq] + p_c[:, 2 * tq:3 * tq] + p_c[:, 3 * tq:4 * tq]
        hi = p_sum.astype(BF16)
        lo = (p_sum - hi.astype(F32)).astype(BF16)
        imp_t = _dot(ov_ref[...], hi) + _dot(ov_ref[...], lo)
        sel_t = _select_blocks(imp_t[0:n_blocks], qt * tq)
        block_bias = jnp.concatenate([jnp.where(sel_t > 0.0, 0.0, NEG_INF),
                                      jnp.zeros((LANES - n_blocks, tq), F32)], axis=0).T.astype(BF16)
        amask_ref[g] = _dot(block_bias, e_ref[...])

    def slc_chain(g):
        def score(dl):
            tile = dl if isinstance(dl, int) else jnp.where(dl < N_WIN_TILES - 1, dl, N_WIN_TILES)
            rows = key_rows(dl)
            s = _dot_nt(qs[g], ksw_ref[rows, 0:LANES]).reshape(rq, tq, tq)
            return (s + tb_ref[tile, g] + amask_ref[g, :, rows][None]).reshape(rq * tq, tq)

        def pv(p, dl):
            return _dot(p, vsw_ref[key_rows(dl), g * LANES:(g + 1) * LANES])

        return score, pv

    def win_chain(g):
        def score(dl):
            s = _dot_nt(qs[g], ksw_ref[key_rows(dl), LANES:2 * LANES]).reshape(rq, tq, tq)
            return (s + tb_ref[dl, g]).reshape(rq * tq, tq)

        def pv(p, dl):
            c0 = (NSA_KV_GROUPS + g) * LANES
            return _dot(p, vsw_ref[key_rows(dl), c0:c0 + LANES])

        return score, pv

    groups = range(NSA_KV_GROUPS)
    slc_ids = tuple(groups)
    win_ids = tuple(NSA_KV_GROUPS + g for g in groups)
    bounds = ([jnp.full((rq * tq, tq), bnd_ref[1], F32) for _ in groups]
              + [jnp.full((rq * tq, tq), bnd_ref[2], F32) for _ in groups])
    accs = _attention([slc_chain(g) for g in groups] + [win_chain(g) for g in groups],
                      [(jnp.minimum(qt, N_WIN_TILES - 1), win_ids), (qt, slc_ids)],
                      bounds, bnd_ref[0] > 0.5, acc_ref)

    gates = small_ref[...]
    g_hi = gates.astype(BF16)
    g_lo = (gates - g_hi.astype(F32)).astype(BF16)
    out = [jnp.zeros((tq, LANES), F32) for _ in range(rq)]
    for b, per_group in enumerate((o_cmps, accs[:NSA_KV_GROUPS], accs[NSA_KV_GROUPS:])):
        gate = _dot(g_hi, gx_ref[b]) + _dot(g_lo, gx_ref[b])
        for r in range(rq):
            first, second = (a[r * tq:(r + 1) * tq] for a in per_group)
            pair = _pair_lanes(first, second, normalise=b > 0)
            out[r] = out[r] + gate[:, r * LANES:(r + 1) * LANES] * pair
    for r in range(rq):
        o_ref[:, r * LANES:(r + 1) * LANES] = out[r].astype(BF16)


def _nsa(bounds, qa, kcmp, vcmp, ksw, vsw, small, toeplitz, cmp_bias, batch, seq):
    tq = ATT_TILE
    nq = seq // tq
    rq = NSA_Q_PER_GROUP
    n_blocks = seq // SLC_BLOCK
    n_cmp = (seq - CMP_BLOCK) // CMP_STRIDE + 1
    ci = np.arange(LANES)[:, None] * CMP_STRIDE
    sj = np.arange(LANES)[None, :] * SLC_BLOCK
    overlap = ((ci <= sj + SLC_BLOCK - 1) & (ci + CMP_BLOCK - 1 >= sj)
               & (np.arange(LANES)[:, None] < n_cmp) & (np.arange(LANES)[None, :] < n_blocks))
    expand = np.arange(LANES)[:, None] == (np.arange(seq)[None, :] // SLC_BLOCK)
    gate_expand = np.zeros((3, LANES, NSA_W), np.float32)
    for b in range(3):
        for g in range(NSA_KV_GROUPS):
            for r in range(rq):
                c0 = r * LANES + g * HEAD_DIM
                gate_expand[b, b * NSA_HEADS + g * rq + r, c0:c0 + HEAD_DIM] = 1.0
    return pl.pallas_call(
        functools.partial(_nsa_kernel, n_blocks=n_blocks),
        grid=(batch, nq),
        in_specs=[
            pl.BlockSpec(memory_space=pltpu.SMEM),
            pl.BlockSpec((tq, NSA_W), lambda b, t: (b * nq + t, 0)),
            pl.BlockSpec((1, LANES, NSA_KV_W), lambda b, t: (b, 0, 0)),
            pl.BlockSpec((1, LANES, NSA_KV_W), lambda b, t: (b, 0, 0)),
            pl.BlockSpec((seq, 2 * NSA_KV_W), lambda b, t: (b, 0)),
            pl.BlockSpec((seq, 2 * NSA_KV_GROUPS * LANES), lambda b, t: (b, 0)),
            pl.BlockSpec((tq, LANES), lambda b, t: (b * nq + t, 0)),
            _const_spec((N_WIN_TILES + 1, NSA_KV_GROUPS, rq, tq, tq)),
            pl.BlockSpec((1, NSA_KV_GROUPS, rq, LANES, tq), lambda b, t: (t, 0, 0, 0, 0)),
            _const_spec((LANES, LANES)),
            _const_spec((LANES, seq)),
            _const_spec((3, LANES, NSA_W)),
        ],
        out_specs=pl.BlockSpec((tq, NSA_W), lambda b, t: (b * nq + t, 0)),
        out_shape=jax.ShapeDtypeStruct((batch * seq, NSA_W), BF16),
        scratch_shapes=[pltpu.VMEM((NSA_KV_GROUPS, tq, seq), F32),
                        pltpu.VMEM((2 * NSA_KV_GROUPS, rq * tq, LANES), F32)],
        compiler_params=_params(2),
        name="nsa",
    )(bounds, qa, kcmp, vcmp, ksw, vsw, small, toeplitz, cmp_bias,
      jnp.asarray(overlap.T.astype(np.float32), BF16), jnp.asarray(expand.astype(np.float32), BF16),
      jnp.asarray(gate_expand, BF16))


def _fox_kernel(bnd_ref, q_ref, k_ref, v_ref, cum_ref, cum_t_ref, o_ref, acc_ref):
    qt = pl.program_id(1)
    tq = ATT_TILE
    pairs = FOX_HEADS // 2
    row = lax.broadcasted_iota(jnp.int32, (2, tq, tq), 1)
    col = lax.broadcasted_iota(jnp.int32, (2, tq, tq), 2)
    lane = lax.broadcasted_iota(jnp.int32, (tq, LANES), 1)
    low = jnp.where(lane < HEAD_DIM, 1.0, 0.0).astype(BF16)
    high = jnp.where(lane < HEAD_DIM, 0.0, 1.0).astype(BF16)
    n_seq = q_ref.shape[0]

    def key_rows(dl):
        return pl.ds(pl.multiple_of((qt - dl) * tq, tq), tq)

    bounds = []

    def pair_chain(b, p):
        pair = q_ref[b, :, p * LANES:(p + 1) * LANES]
        q = jnp.concatenate([pair * low, pair * high], axis=0)
        base = cum_ref[b, 2 * p:2 * p + 2, pl.ds(pl.multiple_of(qt * tq, tq), LANES)][:, 0:1]
        cum_rows = cum_t_ref[b]
        own = jnp.concatenate([base[h:h + 1, :] - cum_rows[:, 2 * p + h:2 * p + h + 1] for h in range(2)],
                              axis=0)
        bounds.append(jnp.broadcast_to(own * LOG2E + bnd_ref[1], (2 * tq, tq)))

        def score(dl):
            rows = key_rows(dl)
            decay = (base - cum_ref[b, 2 * p:2 * p + 2, rows]) * LOG2E
            s = _dot_nt(q, k_ref[b, rows, p * LANES:(p + 1) * LANES]).reshape(2, tq, tq) + decay[:, None, :]
            if isinstance(dl, int):
                s = jnp.where(col <= row, s, NEG_INF)
            return s.reshape(2 * tq, tq)

        def pv(w, dl):
            rows = key_rows(dl)
            return jnp.concatenate(
                [_dot(w[0:tq], v_ref[b, rows, 2 * p * LANES:(2 * p + 1) * LANES]),
                 _dot(w[tq:2 * tq], v_ref[b, rows, (2 * p + 1) * LANES:(2 * p + 2) * LANES])], axis=0)

        return score, pv

    chains = [pair_chain(b, p) for b in range(n_seq) for p in range(pairs)]
    accs = _attention(chains, [(qt, tuple(range(len(chains))))], bounds, bnd_ref[0] > 0.5, acc_ref)
    for i, acc in enumerate(accs):
        b, p = divmod(i, pairs)
        o_ref[b, :, p * LANES:(p + 1) * LANES] = _pair_lanes(acc[0:tq], acc[tq:2 * tq], True).astype(BF16)


def _fox(bounds, qb, kb, vb, cum, batch, seq):
    tq = ATT_TILE
    nq = seq // tq
    per_step = 2 if batch % 2 == 0 else 1
    out = pl.pallas_call(
        _fox_kernel,
        grid=(batch // per_step, nq),
        in_specs=[
            pl.BlockSpec(memory_space=pltpu.SMEM),
            pl.BlockSpec((per_step, tq, FOX_W), lambda b, t: (b, t, 0)),
            pl.BlockSpec((per_step, seq, FOX_W), lambda b, t: (b, 0, 0)),
            pl.BlockSpec((per_step, seq, FOX_HEADS * LANES), lambda b, t: (b, 0, 0)),
            pl.BlockSpec((per_step, FOX_HEADS, seq), lambda b, t: (b, 0, 0)),
            pl.BlockSpec((per_step, tq, FOX_HEADS), lambda b, t: (b, t, 0)),
        ],
        out_specs=pl.BlockSpec((per_step, tq, FOX_W), lambda b, t: (b, t, 0)),
        out_shape=jax.ShapeDtypeStruct((batch, seq, FOX_W), BF16),
        scratch_shapes=[pltpu.VMEM((per_step * FOX_HEADS // 2, 2 * tq, LANES), F32)],
        compiler_params=_params(2),
        name="fox",
    )(bounds, qb.reshape(batch, seq, FOX_W), kb.reshape(batch, seq, FOX_W),
      vb.reshape(batch, seq, FOX_HEADS * LANES), cum, cum.transpose(0, 2, 1))
    return out.reshape(batch * seq, FOX_W)


def _merge_ffn_kernel(x_ref, on_ref, of_ref, gmix_ref, wgm_ref, won_ref, wof_ref, wout_ref,
                      g2_ref, wup_ref, wdn_ref, o_ref):
    x = x_ref[...]
    u = _rms_rows(x, gmix_ref[...]).astype(BF16)
    gate = jax.nn.sigmoid(_dot(u, wgm_ref[...]))
    merged = (gate[:, :D_MODEL] * _dot(on_ref[...], won_ref[...])
              + gate[:, D_MODEL:] * _dot(of_ref[...], wof_ref[...]))
    x2 = x + _dot(merged.astype(BF16), wout_ref[...])
    o_ref[...] = _swiglu_residual(x2, g2_ref[...], wup_ref, wdn_ref)


def _merge_ffn(x2d, o_nsa, o_fox, mix_norm, w_gm, w_o_nsa, w_o_fox, w_out, gain2, w_up, w_down):
    n = x2d.shape[0]
    tm = TOKEN_TILE // 2

    def row(width):
        return pl.BlockSpec((tm, width), lambda i: (i, 0))

    return pl.pallas_call(
        _merge_ffn_kernel,
        grid=(n // tm,),
        in_specs=[row(D_MODEL), row(NSA_W), row(FOX_W), _const_spec((1, D_MODEL)),
                  _const_spec((D_MODEL, 2 * D_MODEL)), _const_spec((NSA_W, D_MODEL)),
                  _const_spec((FOX_W, D_MODEL)), _const_spec((D_MODEL, D_MODEL)),
                  _const_spec((1, D_MODEL)), _const_spec((D_MODEL, 2 * D_FF)),
                  _const_spec((D_FF, D_MODEL))],
        out_specs=row(D_MODEL),
        out_shape=jax.ShapeDtypeStruct((n, D_MODEL), F32),
        compiler_params=_params(1),
        name="merge_ffn",
    )(x2d, o_nsa, o_fox, mix_norm.reshape(1, D_MODEL), w_gm, w_o_nsa.astype(BF16),
      w_o_fox.astype(BF16), w_out.astype(BF16), gain2.reshape(1, D_MODEL), w_up.astype(BF16),
      w_down.astype(BF16))


def _layer(x, ffn1_norm, ffn1_w_up, ffn1_w_down, mix_norm, w_in, b_forget, nsa_q_gain, nsa_k_gain,
           fox_q_gain, fox_k_gain, cmp_pos_k, cmp_pos_v, cmp_k_w1, cmp_k_w2, cmp_v_w1, cmp_v_w2,
           w_o_nsa, w_o_fox, w_out, ffn2_norm, ffn2_w_up, ffn2_w_down, rel_bias_table):
    batch, seq, d = x.shape
    assert d == D_MODEL and seq % ATT_TILE == 0 and (batch * seq) % TOKEN_TILE == 0
    assert seq // SLC_BLOCK <= LANES and (seq - CMP_BLOCK) // CMP_STRIDE + 1 == LANES - 1
    assert seq // (CMP_BLOCK // 2) == LANES
    x2d = x.reshape(batch * seq, D_MODEL)

    x1 = _ffn(x2d, ffn1_norm, ffn1_w_up, ffn1_w_down)

    w_packed, gain_row, small_bias, w_gm = _pack_in_proj(w_in, b_forget, nsa_q_gain, nsa_k_gain,
                                                         fox_q_gain, fox_k_gain)
    qa, qb, kb, ksw, vb, vsw, kc, vc, small = _in_proj(x1, mix_norm, w_packed, gain_row, small_bias)

    kcmp, vcmp = _compress(kc, vc, batch, seq, _pack_compress(cmp_pos_k, cmp_k_w1, cmp_k_w2),
                           _pack_compress(cmp_pos_v, cmp_v_w1, cmp_v_w2), nsa_k_gain[0])
    toeplitz, cmp_bias = _bias_tables(rel_bias_table, seq // ATT_TILE)
    b_slc = _score_bound(nsa_q_gain, nsa_k_gain[1])
    b_win = _score_bound(nsa_q_gain, nsa_k_gain[2])
    t_hi, t_lo = jnp.max(rel_bias_table) * LOG2E, jnp.min(rel_bias_table) * LOG2E
    nsa_ok = 2.0 * jnp.maximum(b_slc, b_win) + (t_hi - t_lo) < EXP2_RANGE
    nsa_bounds = jnp.stack([nsa_ok.astype(F32), b_slc + t_hi, b_win + t_hi])
    o_nsa = _nsa(nsa_bounds, qa, kcmp, vcmp, ksw, vsw, small, toeplitz, cmp_bias, batch, seq)

    n_gate = 3 * NSA_HEADS
    logf_t = small[:, n_gate:n_gate + FOX_HEADS].reshape(batch, seq, FOX_HEADS).transpose(0, 2, 1)
    b_fox = _score_bound(fox_q_gain, fox_k_gain)
    fox_bounds = jnp.stack([(2.0 * b_fox < EXP2_RANGE).astype(F32), b_fox])
    o_fox = _fox(fox_bounds, qb, kb, vb, _cumsum(logf_t), batch, seq)

    w_o_nsa_p = w_o_nsa.reshape(NSA_KV_GROUPS, NSA_Q_PER_GROUP, HEAD_DIM, D_MODEL).transpose(1, 0, 2, 3)
    out = _merge_ffn(x1, o_nsa, o_fox, mix_norm, w_gm, w_o_nsa_p.reshape(NSA_W, D_MODEL), w_o_fox,
                     w_out, ffn2_norm, ffn2_w_up, ffn2_w_down)
    return out.reshape(batch, seq, D_MODEL)


def kernel(x, ffn1_norm, ffn1_w_up, ffn1_w_down, mix_norm, w_in, b_forget, nsa_q_gain, nsa_k_gain,
           fox_q_gain, fox_k_gain, cmp_pos_k, cmp_pos_v, cmp_k_w1, cmp_k_w2, cmp_v_w1, cmp_v_w2,
           w_o_nsa, w_o_fox, w_out, ffn2_norm, ffn2_w_up, ffn2_w_down, rel_bias_table):
    for layer in range(ffn1_norm.shape[0]):
        x = _layer(x, ffn1_norm[layer], ffn1_w_up[layer], ffn1_w_down[layer], mix_norm[layer],
                   w_in[layer], b_forget[layer], nsa_q_gain[layer], nsa_k_gain[layer],
                   fox_q_gain[layer], fox_k_gain[layer], cmp_pos_k[layer], cmp_pos_v[layer],
                   cmp_k_w1[layer], cmp_k_w2[layer], cmp_v_w1[layer], cmp_v_w2[layer],
                   w_o_nsa[layer], w_o_fox[layer], w_out[layer], ffn2_norm[layer],
                   ffn2_w_up[layer], ffn2_w_down[layer], rel_bias_table)
    return x
```

```python
import functools
import math

import numpy as np
import jax
import jax.numpy as jnp
from jax import lax
from jax.experimental import pallas as pl
from jax.experimental.pallas import tpu as pltpu

F32 = jnp.float32
BF16 = jnp.bfloat16

D_MODEL = 1024
HEAD_DIM = 64
NSA_HEADS = 8
NSA_KV_GROUPS = 2
NSA_Q_PER_GROUP = NSA_HEADS // NSA_KV_GROUPS
CMP_BLOCK = 32
CMP_STRIDE = 16
CMP_HIDDEN = 128
SLC_BLOCK = 64
SLC_TOPK = 8
WINDOW = 512
FOX_HEADS = 8
D_FF = 2816
N_BUCKETS = 32
MAX_DISTANCE = 128
RMS_EPS = 1e-6
NEG_INF = -1.0e30
FORCE_BONUS = 1.0e4

NSA_W = NSA_HEADS * HEAD_DIM
NSA_KV_W = NSA_KV_GROUPS * HEAD_DIM
FOX_W = FOX_HEADS * HEAD_DIM
IN_SPLITS = (NSA_W, NSA_KV_W, NSA_KV_W, NSA_KV_W, NSA_KV_W, NSA_KV_W, NSA_KV_W, 3 * NSA_HEADS,
             FOX_W, FOX_W, FOX_W, FOX_HEADS, 2 * D_MODEL)

LANES = 128
TOKEN_TILE = 512
FFN_CHUNK = D_FF // 2
ATT_TILE = 256
N_WIN_TILES = WINDOW // ATT_TILE + 1
VMEM_LIMIT = 56 * 1024 * 1024
LOG2E = 1.4426950408889634
EXP2_RANGE = 100.0

PQ_A = 0
PQ_B = PQ_A + NSA_W
PK_B = PQ_B + FOX_W
PK_SW = PK_B + FOX_W
P_NORM_END = PK_SW + 2 * NSA_KV_W
PV_B = P_NORM_END
PV_SW = PV_B + FOX_W
PKV_C = PV_SW + 2 * NSA_KV_W
P_SMALL = PKV_C + 2 * NSA_KV_W
P_END = P_SMALL + LANES
NORM_CHUNK = 256


def _dot(a, b):
    return jnp.dot(a, b, preferred_element_type=F32)


def _dot_nt(a, b):
    return lax.dot_general(a, b, (((1,), (1,)), ((), ())), preferred_element_type=F32)


def _split_dot(x, w):
    hi = x.astype(BF16)
    lo = (x - hi.astype(F32)).astype(BF16)
    return _dot(hi, w) + _dot(lo, w)


def _rms_rows(x, gain_row):
    ms = jnp.mean(x * x, axis=-1, keepdims=True)
    return x * lax.rsqrt(ms + RMS_EPS) * gain_row


def _const_spec(shape):
    nd = len(shape)
    return pl.BlockSpec(shape, lambda *_: (0,) * nd, pipeline_mode=pl.Buffered(1))


def _params(n_axes):
    return pltpu.CompilerParams(dimension_semantics=("arbitrary",) * n_axes,
                                vmem_limit_bytes=VMEM_LIMIT)


def _swiglu_residual(x, gain_row, wup_ref, wdn_ref):
    xn = _rms_rows(x, gain_row).astype(BF16)
    acc = jnp.zeros(x.shape, F32)
    for c in range(D_FF // FFN_CHUNK):
        lo = c * FFN_CHUNK
        gate = _dot(xn, wup_ref[:, lo:lo + FFN_CHUNK])
        up = _dot(xn, wup_ref[:, D_FF + lo:D_FF + lo + FFN_CHUNK])
        h = (gate * jax.nn.sigmoid(gate) * up).astype(BF16)
        acc = acc + _dot(h, wdn_ref[lo:lo + FFN_CHUNK, :])
    return x + 0.5 * acc


def _ffn_kernel(x_ref, g_ref, wup_ref, wdn_ref, o_ref):
    o_ref[...] = _swiglu_residual(x_ref[...], g_ref[...], wup_ref, wdn_ref)


def _ffn(x2d, gain, w_up, w_down):
    n = x2d.shape[0]
    row = pl.BlockSpec((TOKEN_TILE, D_MODEL), lambda i: (i, 0))
    return pl.pallas_call(
        _ffn_kernel,
        grid=(n // TOKEN_TILE,),
        in_specs=[row, _const_spec((1, D_MODEL)), _const_spec((D_MODEL, 2 * D_FF)),
                  _const_spec((D_FF, D_MODEL))],
        out_specs=row,
        out_shape=jax.ShapeDtypeStruct((n, D_MODEL), F32),
        compiler_params=_params(1),
        name="ffn",
    )(x2d, gain.reshape(1, D_MODEL), w_up.astype(BF16), w_down.astype(BF16))


def _block_ones(width, size=NORM_CHUNK):
    idx = np.arange(size) // width
    return jnp.asarray((idx[:, None] == idx[None, :]).astype(np.float32), BF16)


def _pack_in_proj(w_in, b_forget, nsa_q_gain, nsa_k_gain, fox_q_gain, fox_k_gain):
    scale = LOG2E / math.sqrt(HEAD_DIM)
    pts = np.cumsum(np.array(IN_SPLITS))[:-1].tolist()
    qa, kc, vc, ks, vs, kw, vw, ga, qb, kb, vb, fb, gm = jnp.split(w_in, pts, axis=1)
    cols = []
    for r in range(NSA_Q_PER_GROUP):
        for g in range(NSA_KV_GROUPS):
            h = g * NSA_Q_PER_GROUP + r
            cols.append(qa[:, h * HEAD_DIM:(h + 1) * HEAD_DIM])
    cols += [qb, kb, ks, kw]
    gains = [jnp.tile(nsa_q_gain * scale, NSA_HEADS), jnp.tile(fox_q_gain * scale, FOX_HEADS),
             jnp.tile(fox_k_gain, FOX_HEADS), jnp.tile(nsa_k_gain[1], NSA_KV_GROUPS),
             jnp.tile(nsa_k_gain[2], NSA_KV_GROUPS)]
    n_small_pad = LANES - ga.shape[1] - fb.shape[1]
    cols += [vb, vs, vw, kc, vc, ga, fb, jnp.zeros((D_MODEL, n_small_pad), F32)]
    w_packed = jnp.concatenate(cols, axis=1).astype(BF16)
    gain_row = jnp.concatenate(gains).reshape(1, P_NORM_END)
    small_bias = jnp.concatenate([jnp.zeros((ga.shape[1],), F32), b_forget,
                                  jnp.zeros((n_small_pad,), F32)]).reshape(1, LANES)
    return w_packed, gain_row, small_bias, gm.astype(BF16)


def _in_proj_kernel(x_ref, g_ref, w_ref, gain_ref, sbias_ref, bd64_ref,
                    qa_ref, qb_ref, kb_ref, ksw_ref, vb_ref, vsw_ref, kc_ref, vc_ref, small_ref):
    u = _rms_rows(x_ref[...], g_ref[...]).astype(BF16)
    tm = u.shape[0]

    y = _dot(u, w_ref[:, 0:P_NORM_END])
    n_chunks = P_NORM_END // NORM_CHUNK
    squares = jnp.concatenate([jnp.square(y[:, c * NORM_CHUNK:(c + 1) * NORM_CHUNK]).astype(BF16)
                               for c in range(n_chunks)], axis=0)
    ss = _dot(squares, bd64_ref[...])
    chunk = 0
    for o_ref in (qa_ref, qb_ref, kb_ref, ksw_ref):
        for i in range(o_ref.shape[1] // NORM_CHUNK):
            cols = slice(chunk * NORM_CHUNK, (chunk + 1) * NORM_CHUNK)
            inv = lax.rsqrt(ss[chunk * tm:(chunk + 1) * tm] * (1.0 / HEAD_DIM) + RMS_EPS)
            o_ref[:, i * NORM_CHUNK:(i + 1) * NORM_CHUNK] = (y[:, cols] * inv * gain_ref[:, cols]).astype(BF16)
            chunk += 1
    low = lax.broadcasted_iota(jnp.int32, (u.shape[0], LANES), 1) < HEAD_DIM

    def store_with_ones(v, o_ref):
        for i in range(v.shape[1] // LANES):
            pair = v[:, i * LANES:(i + 1) * LANES]
            o_ref[:, 2 * i * LANES:(2 * i + 1) * LANES] = jnp.where(low, pair, 1.0).astype(BF16)
            o_ref[:, (2 * i + 1) * LANES:(2 * i + 2) * LANES] = jnp.where(low, 1.0, pair).astype(BF16)

    rest = _dot(u, w_ref[:, P_NORM_END:P_END])
    store_with_ones(rest[:, PV_B - P_NORM_END:PV_SW - P_NORM_END], vb_ref)
    store_with_ones(rest[:, PV_SW - P_NORM_END:PKV_C - P_NORM_END], vsw_ref)
    kc_ref[...] = rest[:, PKV_C - P_NORM_END:PKV_C - P_NORM_END + NSA_KV_W]
    vc_ref[...] = rest[:, PKV_C - P_NORM_END + NSA_KV_W:P_SMALL - P_NORM_END]
    z = rest[:, P_SMALL - P_NORM_END:] + sbias_ref[...]
    lane = lax.broadcasted_iota(jnp.int32, z.shape, 1)
    log_sig = jnp.minimum(z, 0.0) - jnp.log1p(jnp.exp(-jnp.abs(z)))
    small_ref[...] = jnp.where(lane < 3 * NSA_HEADS, jax.nn.sigmoid(z), log_sig)


def _in_proj(x2d, mix_norm, w_packed, gain_row, small_bias):
    n = x2d.shape[0]

    def row(width):
        return pl.BlockSpec((TOKEN_TILE, width), lambda i: (i, 0))

    widths = (NSA_W, FOX_W, FOX_W, 2 * NSA_KV_W, 2 * FOX_W, 4 * NSA_KV_W, NSA_KV_W, NSA_KV_W, LANES)
    dtypes = (BF16, BF16, BF16, BF16, BF16, BF16, F32, F32, F32)
    return pl.pallas_call(
        _in_proj_kernel,
        grid=(n // TOKEN_TILE,),
        in_specs=[row(D_MODEL), _const_spec((1, D_MODEL)), _const_spec((D_MODEL, P_END)),
                  _const_spec((1, P_NORM_END)), _const_spec((1, LANES)),
                  _const_spec((NORM_CHUNK, NORM_CHUNK))],
        out_specs=[row(w) for w in widths],
        out_shape=[jax.ShapeDtypeStruct((n, w), d) for w, d in zip(widths, dtypes)],
        compiler_params=_params(1),
        name="in_proj",
    )(x2d, mix_norm.reshape(1, D_MODEL), w_packed, gain_row, small_bias, _block_ones(HEAD_DIM))


def _pack_compress(pos, w1, w2):
    half = CMP_BLOCK // 2
    eye = jnp.eye(NSA_KV_GROUPS, dtype=F32)
    w1r = w1.reshape(CMP_BLOCK, HEAD_DIM, CMP_HIDDEN)

    def big(w):
        return jnp.einsum('ldh,pg->lpdgh', w, eye).reshape(half * NSA_KV_W, NSA_KV_GROUPS * CMP_HIDDEN)

    def posrow(p):
        return jnp.broadcast_to(p[:, None, :], (half, NSA_KV_GROUPS, HEAD_DIM)).reshape(1, half * NSA_KV_W)

    w2big = jnp.einsum('hd,pg->phgd', w2, eye).reshape(NSA_KV_GROUPS * CMP_HIDDEN, NSA_KV_W)
    return (posrow(pos[:half]), posrow(pos[half:]), big(w1r[:half]).astype(BF16),
            big(w1r[half:]).astype(BF16), w2big.astype(BF16))


def _compress_kernel(kc_ref, vc_ref, kp_lo, kp_hi, kw_lo, kw_hi, kw2, vp_lo, vp_hi, vw_lo, vw_hi, vw2,
                     kgain_ref, bd64_ref, ko_ref, vo_ref):
    def mlp(r, p_lo, p_hi, w_lo, w_hi, w2):
        first = _dot((r + p_lo[...]).astype(BF16), w_lo[...])
        second = _dot((r + p_hi[...]).astype(BF16), w_hi[...])
        h = first + pltpu.roll(second, second.shape[0] - 1, axis=0)
        return _dot((h * jax.nn.sigmoid(h)).astype(BF16), w2[...])

    k = mlp(kc_ref[0], kp_lo, kp_hi, kw_lo, kw_hi, kw2)
    ss = _split_dot(k * k, bd64_ref[...])
    ko_ref[0] = (k * lax.rsqrt(ss * (1.0 / HEAD_DIM) + RMS_EPS) * kgain_ref[...]).astype(BF16)
    vo_ref[0] = mlp(vc_ref[0], vp_lo, vp_hi, vw_lo, vw_hi, vw2).astype(BF16)


def _compress(kc, vc, batch, seq, k_pack, v_pack, k_gain):
    rows = seq // (CMP_BLOCK // 2)
    width = (CMP_BLOCK // 2) * NSA_KV_W
    blk = pl.BlockSpec((1, rows, width), lambda b: (b, 0, 0))
    out = pl.BlockSpec((1, rows, NSA_KV_W), lambda b: (b, 0, 0))
    pack_specs = [_const_spec((1, width)), _const_spec((1, width)),
                  _const_spec((width, NSA_KV_GROUPS * CMP_HIDDEN)),
                  _const_spec((width, NSA_KV_GROUPS * CMP_HIDDEN)),
                  _const_spec((NSA_KV_GROUPS * CMP_HIDDEN, NSA_KV_W))]
    bd = _block_ones(HEAD_DIM, NSA_KV_W)
    return pl.pallas_call(
        _compress_kernel,
        grid=(batch,),
        in_specs=[blk, blk] + pack_specs + pack_specs + [_const_spec((1, NSA_KV_W)),
                                                         _const_spec((NSA_KV_W, NSA_KV_W))],
        out_specs=[out, out],
        out_shape=[jax.ShapeDtypeStruct((batch, rows, NSA_KV_W), BF16)] * 2,
        compiler_params=_params(1),
        name="compress",
    )(kc.reshape(batch, rows, width), vc.reshape(batch, rows, width), *k_pack, *v_pack,
      jnp.tile(k_gain, NSA_KV_GROUPS).reshape(1, NSA_KV_W), bd)


def _cumsum_kernel(x_ref, o_ref):
    x = x_ref[0]
    lane = lax.broadcasted_iota(jnp.int32, x.shape, 1)
    k = 1
    while k < x.shape[1]:
        x = x + jnp.where(lane >= k, pltpu.roll(x, k, axis=1), 0.0)
        k *= 2
    o_ref[0] = x


def _cumsum(logf_t):
    b, h, s = logf_t.shape
    spec = pl.BlockSpec((1, h, s), lambda i: (i, 0, 0))
    return pl.pallas_call(
        _cumsum_kernel, grid=(b,), in_specs=[spec], out_specs=spec,
        out_shape=jax.ShapeDtypeStruct((b, h, s), F32), compiler_params=_params(1), name="cumsum",
    )(logf_t)


def _write_bias_tile(d, valid, tab_ref, o_ref):
    max_exact = N_BUCKETS // 2
    n = jnp.maximum(d, 0)
    nf = jnp.maximum(n, 1).astype(F32)
    large = max_exact + (jnp.log(nf / max_exact) / math.log(MAX_DISTANCE / max_exact)
                         * (N_BUCKETS - max_exact)).astype(jnp.int32)
    bucket = jnp.where(n < max_exact, n, jnp.minimum(large, N_BUCKETS - 1))
    for h in range(NSA_HEADS):
        acc = jnp.zeros(d.shape, F32)
        for b in range(N_BUCKETS):
            acc = jnp.where(bucket == b, tab_ref[b, h], acc)
        g, r = divmod(h, NSA_Q_PER_GROUP)
        o_ref[0, g, r] = jnp.where(valid, acc * LOG2E, NEG_INF)


def _toeplitz_bias_kernel(tab_ref, o_ref):
    s = pl.program_id(0)
    i = lax.broadcasted_iota(jnp.int32, (ATT_TILE, ATT_TILE), 0)
    j = lax.broadcasted_iota(jnp.int32, (ATT_TILE, ATT_TILE), 1)
    d = jnp.minimum(s, N_WIN_TILES - 1) * ATT_TILE + i - j
    d_hi = jnp.where(s < N_WIN_TILES, WINDOW, jnp.int32(1 << 30))
    _write_bias_tile(d, (d >= 0) & (d < d_hi), tab_ref, o_ref)


def _cmp_bias_kernel(tab_ref, o_ref):
    t = pl.program_id(0)
    c = lax.broadcasted_iota(jnp.int32, (LANES, ATT_TILE), 0)
    i = lax.broadcasted_iota(jnp.int32, (LANES, ATT_TILE), 1)
    d = t * ATT_TILE + i - CMP_STRIDE * c - (CMP_BLOCK - 1)
    _write_bias_tile(d, (d >= 0) & (c < LANES - 1), tab_ref, o_ref)


def _bias_tables(rel_bias_table, n_q_tiles):
    def call(body, steps, rows, name):
        shape = (steps, NSA_KV_GROUPS, NSA_Q_PER_GROUP, rows, ATT_TILE)
        return pl.pallas_call(
            body, grid=(steps,),
            in_specs=[pl.BlockSpec(memory_space=pltpu.SMEM)],
            out_specs=pl.BlockSpec((1,) + shape[1:], lambda s: (s, 0, 0, 0, 0)),
            out_shape=jax.ShapeDtypeStruct(shape, F32),
            compiler_params=_params(1), name=name,
        )(rel_bias_table)

    return (call(_toeplitz_bias_kernel, N_WIN_TILES + 1, ATT_TILE, "toeplitz_bias"),
            call(_cmp_bias_kernel, n_q_tiles, LANES, "cmp_bias"))


def _score_bound(q_gain, k_gain):
    return 1.02 * LOG2E * math.sqrt(HEAD_DIM) * jnp.max(jnp.abs(q_gain)) * jnp.max(jnp.abs(k_gain))


def _attention(chains, groups, bounds, acc_ref):
    n = len(chains)

    def later(k):
        return tuple(i for _, ids in groups[k:] for i in ids)

    def exact_max():
        def half_max(i, dl):
            s = chains[i][0](dl)
            return jnp.maximum(s[:, :LANES], s[:, LANES:])

        mx = [half_max(i, 0) for i in range(n)]
        lo = 1
        for k, (last, _) in enumerate(groups):
            members = later(k)

            def body(dl, carry, members=members):
                return tuple(jnp.maximum(c, half_max(i, dl)) for c, i in zip(carry, members))

            for i, r in zip(members, lax.fori_loop(lo, last + 1, body, tuple(mx[i] for i in members))):
                mx[i] = r
            lo = last + 1
        return tuple(jnp.broadcast_to(jnp.max(m, axis=-1, keepdims=True), (m.shape[0], ATT_TILE))
                     for m in mx)

    shift = exact_max() if bounds is None else [b() for b in bounds]

    def weights(i, dl):
        return jnp.exp2(chains[i][0](dl) - shift[i]).astype(BF16)

    def product(i, dl):
        return chains[i][1](weights(i, dl), dl)

    for i in range(n):
        acc_ref[i] = product(i, 0)
    lo = 1
    for k, (last, _) in enumerate(groups):
        members = later(k)
        count = last + 1 - lo

        def body(j, carry, members=members, lo=lo):
            dl = lo + 2 * j
            for i in members:
                acc_ref[i] += product(i, dl) + product(i, dl + 1)
            return carry

        lax.fori_loop(0, count // 2, body, 0)

        @pl.when(count % 2 == 1)
        def _(members=members, last=last):
            for i in members:
                acc_ref[i] += product(i, last)

        lo = last + 1
    return [acc_ref[i] for i in range(n)]


def _pair_lanes(first, second, normalise):
    low = lax.broadcasted_iota(jnp.int32, first.shape, 1) < HEAD_DIM
    pair = jnp.where(low, first, second)
    if not normalise:
        return pair
    sums = pltpu.roll(jnp.where(low, second, first), HEAD_DIM, axis=1)
    return pair * (1.0 / sums)


def _select_blocks(imp_t, t0):
    blk = lax.broadcasted_iota(jnp.int32, imp_t.shape, 0)
    cur = (t0 + lax.broadcasted_iota(jnp.int32, imp_t.shape, 1)) // SLC_BLOCK
    forced = (blk == 0) | (blk == cur) | (blk == cur - 1)
    score = jnp.where(blk <= cur, imp_t + jnp.where(forced, FORCE_BONUS, 0.0), NEG_INF)
    blk_f = blk.astype(F32)
    dead = -3.0e38
    picked = jnp.zeros(imp_t.shape, F32)
    for _ in range(SLC_TOPK):
        best = jnp.max(score, axis=0, keepdims=True)
        first = jnp.min(jnp.where(score == best, blk_f, float(LANES)), axis=0, keepdims=True)
        hit = blk_f == first
        picked = jnp.where(hit, 1.0, picked)
        score = jnp.where(hit, dead, score)
    return picked


def _nsa_kernel(bnd_ref, qa_ref, kcmp_ref, vcmp_ref, ksw_ref, vsw_ref, small_ref, tb_ref, bc_ref, ov_ref,
                e_ref, gx_ref, o_ref, amask_ref, acc_ref, *, n_blocks, exact):
    qt = pl.program_id(1)
    tq = ATT_TILE
    rq = NSA_Q_PER_GROUP
    lane = lax.broadcasted_iota(jnp.int32, (tq, LANES), 1)
    own_lanes = [jnp.where((lane // HEAD_DIM) == g, 1.0, 0.0).astype(BF16) for g in range(NSA_KV_GROUPS)]

    def key_rows(dl):
        return pl.ds(pl.multiple_of((qt - dl) * tq, tq), tq)

    qs, o_cmps = [], []
    for g in range(NSA_KV_GROUPS):
        q = jnp.concatenate([qa_ref[:, r * LANES:(r + 1) * LANES] * own_lanes[g] for r in range(rq)],
                            axis=0)
        qs.append(q)

        s = _dot_nt(kcmp_ref[0], q) + jnp.concatenate([bc_ref[0, g, r] for r in range(rq)], axis=1)
        m = jnp.max(s, axis=0, keepdims=True)
        p = jnp.where(s > 0.5 * NEG_INF, jnp.exp2(s - m), 0.0)
        l = jnp.sum(p, axis=0, keepdims=True)
        p_c = p * (1.0 / jnp.where(l > 0.0, l, 1.0))
        o_cmps.append(_dot(p_c.T.astype(BF16), vcmp_ref[0]))

        p_sum = p_c[:, 0:tq] + p_c[:, tq:2 * tq] + p_c[:, 2 * tq:3 * tq] + p_c[:, 3 * tq:4 * tq]
        hi = p_sum.astype(BF16)
        lo = (p_sum - hi.astype(F32)).astype(BF16)
        imp_t = _dot(ov_ref[...], hi) + _dot(ov_ref[...], lo)
        sel_t = _select_blocks(imp_t[0:n_blocks], qt * tq)
        block_bias = jnp.concatenate([jnp.where(sel_t > 0.0, 0.0, NEG_INF),
                                      jnp.zeros((LANES - n_blocks, tq), F32)], axis=0).T.astype(BF16)
        amask_ref[g] = _dot(block_bias, e_ref[...])

    def slc_chain(g):
        def score(dl):
            tile = dl if isinstance(dl, int) else jnp.where(dl < N_WIN_TILES - 1, dl, N_WIN_TILES)
            rows = key_rows(dl)
            s = _dot_nt(qs[g], ksw_ref[rows, 0:LANES]).reshape(rq, tq, tq)
            return (s + tb_ref[tile, g] + amask_ref[g, :, rows][None]).reshape(rq * tq, tq)

        def pv(p, dl):
            return _dot(p, vsw_ref[key_rows(dl), g * LANES:(g + 1) * LANES])

        return score, pv

    def win_chain(g):
        def score(dl):
            s = _dot_nt(qs[g], ksw_ref[key_rows(dl), LANES:2 * LANES]).reshape(rq, tq, tq)
            return (s + tb_ref[dl, g]).reshape(rq * tq, tq)

        def pv(p, dl):
            c0 = (NSA_KV_GROUPS + g) * LANES
            return _dot(p, vsw_ref[key_rows(dl), c0:c0 + LANES])

        return score, pv

    groups = range(NSA_KV_GROUPS)
    slc_ids = tuple(groups)
    win_ids = tuple(NSA_KV_GROUPS + g for g in groups)
    bounds = None if exact else [lambda i=i: jnp.full((rq * tq, tq), bnd_ref[i // NSA_KV_GROUPS], F32)
                                 for i in range(2 * NSA_KV_GROUPS)]
    accs = _attention([slc_chain(g) for g in groups] + [win_chain(g) for g in groups],
                      [(jnp.minimum(qt, N_WIN_TILES - 1), win_ids), (qt, slc_ids)], bounds, acc_ref)

    gates = small_ref[...]
    g_hi = gates.astype(BF16)
    g_lo = (gates - g_hi.astype(F32)).astype(BF16)
    out = [jnp.zeros((tq, LANES), F32) for _ in range(rq)]
    for b, per_group in enumerate((o_cmps, accs[:NSA_KV_GROUPS], accs[NSA_KV_GROUPS:])):
        gate = _dot(g_hi, gx_ref[b]) + _dot(g_lo, gx_ref[b])
        for r in range(rq):
            first, second = (a[r * tq:(r + 1) * tq] for a in per_group)
            pair = _pair_lanes(first, second, normalise=b > 0)
            out[r] = out[r] + gate[:, r * LANES:(r + 1) * LANES] * pair
    for r in range(rq):
        o_ref[:, r * LANES:(r + 1) * LANES] = out[r].astype(BF16)


def _nsa(bounds, qa, kcmp, vcmp, ksw, vsw, small, toeplitz, cmp_bias, batch, seq, exact):
    tq = ATT_TILE
    nq = seq // tq
    rq = NSA_Q_PER_GROUP
    n_blocks = seq // SLC_BLOCK
    n_cmp = (seq - CMP_BLOCK) // CMP_STRIDE + 1
    ci = np.arange(LANES)[:, None] * CMP_STRIDE
    sj = np.arange(LANES)[None, :] * SLC_BLOCK
    overlap = ((ci <= sj + SLC_BLOCK - 1) & (ci + CMP_BLOCK - 1 >= sj)
               & (np.arange(LANES)[:, None] < n_cmp) & (np.arange(LANES)[None, :] < n_blocks))
    expand = np.arange(LANES)[:, None] == (np.arange(seq)[None, :] // SLC_BLOCK)
    gate_expand = np.zeros((3, LANES, NSA_W), np.float32)
    for b in range(3):
        for g in range(NSA_KV_GROUPS):
            for r in range(rq):
                c0 = r * LANES + g * HEAD_DIM
                gate_expand[b, b * NSA_HEADS + g * rq + r, c0:c0 + HEAD_DIM] = 1.0
    return pl.pallas_call(
        functools.partial(_nsa_kernel, n_blocks=n_blocks, exact=exact),
        grid=(batch, nq),
        in_specs=[
            pl.BlockSpec(memory_space=pltpu.SMEM),
            pl.BlockSpec((tq, NSA_W), lambda b, t: (b * nq + t, 0)),
            pl.BlockSpec((1, LANES, NSA_KV_W), lambda b, t: (b, 0, 0)),
            pl.BlockSpec((1, LANES, NSA_KV_W), lambda b, t: (b, 0, 0)),
            pl.BlockSpec((seq, 2 * NSA_KV_W), lambda b, t: (b, 0)),
            pl.BlockSpec((seq, 2 * NSA_KV_GROUPS * LANES), lambda b, t: (b, 0)),
            pl.BlockSpec((tq, LANES), lambda b, t: (b * nq + t, 0)),
            _const_spec((N_WIN_TILES + 1, NSA_KV_GROUPS, rq, tq, tq)),
            pl.BlockSpec((1, NSA_KV_GROUPS, rq, LANES, tq), lambda b, t: (t, 0, 0, 0, 0)),
            _const_spec((LANES, LANES)),
            _const_spec((LANES, seq)),
            _const_spec((3, LANES, NSA_W)),
        ],
        out_specs=pl.BlockSpec((tq, NSA_W), lambda b, t: (b * nq + t, 0)),
        out_shape=jax.ShapeDtypeStruct((batch * seq, NSA_W), BF16),
        scratch_shapes=[pltpu.VMEM((NSA_KV_GROUPS, tq, seq), F32),
                        pltpu.VMEM((2 * NSA_KV_GROUPS, rq * tq, LANES), F32)],
        compiler_params=_params(2),
        name="nsa",
    )(bounds, qa, kcmp, vcmp, ksw, vsw, small, toeplitz, cmp_bias,
      jnp.asarray(overlap.T.astype(np.float32), BF16), jnp.asarray(expand.astype(np.float32), BF16),
      jnp.asarray(gate_expand, BF16))


def _fox_kernel(bnd_ref, q_ref, k_ref, v_ref, cum_ref, cum_t_ref, o_ref, acc_ref, *, exact):
    qt = pl.program_id(1)
    tq = ATT_TILE
    pairs = FOX_HEADS // 2
    row = lax.broadcasted_iota(jnp.int32, (2, tq, tq), 1)
    col = lax.broadcasted_iota(jnp.int32, (2, tq, tq), 2)
    lane = lax.broadcasted_iota(jnp.int32, (tq, LANES), 1)
    low = jnp.where(lane < HEAD_DIM, 1.0, 0.0).astype(BF16)
    high = jnp.where(lane < HEAD_DIM, 0.0, 1.0).astype(BF16)
    n_seq = q_ref.shape[0]

    def key_rows(dl):
        return pl.ds(pl.multiple_of((qt - dl) * tq, tq), tq)

    bounds = []

    def pair_chain(b, p):
        pair = q_ref[b, :, p * LANES:(p + 1) * LANES]
        q = jnp.concatenate([pair * low, pair * high], axis=0)
        base = cum_ref[b, 2 * p:2 * p + 2, pl.ds(pl.multiple_of(qt * tq, tq), LANES)][:, 0:1]

        def decay_of(rows):
            return (base - cum_ref[b, 2 * p:2 * p + 2, rows]) * LOG2E

        def bound():
            cum_rows = cum_t_ref[b]
            own = jnp.concatenate([base[h:h + 1, :] - cum_rows[:, 2 * p + h:2 * p + h + 1]
                                   for h in range(2)], axis=0)
            return jnp.broadcast_to(own * LOG2E + bnd_ref[0], (2 * tq, tq))

        bounds.append(bound)

        def score(dl):
            rows = key_rows(dl)
            decay = decay_of(rows)
            s = _dot_nt(q, k_ref[b, rows, p * LANES:(p + 1) * LANES]).reshape(2, tq, tq) + decay[:, None, :]
            if isinstance(dl, int):
                s = jnp.where(col <= row, s, NEG_INF)
            return s.reshape(2 * tq, tq)

        def pv(w, dl):
            rows = key_rows(dl)
            return jnp.concatenate(
                [_dot(w[0:tq], v_ref[b, rows, 2 * p * LANES:(2 * p + 1) * LANES]),
                 _dot(w[tq:2 * tq], v_ref[b, rows, (2 * p + 1) * LANES:(2 * p + 2) * LANES])], axis=0)

        return score, pv

    chains = [pair_chain(b, p) for b in range(n_seq) for p in range(pairs)]
    accs = _attention(chains, [(qt, tuple(range(len(chains))))], None if exact else bounds, acc_ref)
    for i, acc in enumerate(accs):
        b, p = divmod(i, pairs)
        o_ref[b, :, p * LANES:(p + 1) * LANES] = _pair_lanes(acc[0:tq], acc[tq:2 * tq], True).astype(BF16)


def _fox(bounds, qb, kb, vb, cum, batch, seq, exact):
    tq = ATT_TILE
    nq = seq // tq
    per_step = 2 if batch % 2 == 0 else 1
    out = pl.pallas_call(
        functools.partial(_fox_kernel, exact=exact),
        grid=(batch // per_step, nq),
        in_specs=[
            pl.BlockSpec(memory_space=pltpu.SMEM),
            pl.BlockSpec((per_step, tq, FOX_W), lambda b, t: (b, t, 0)),
            pl.BlockSpec((per_step, seq, FOX_W), lambda b, t: (b, 0, 0)),
            pl.BlockSpec((per_step, seq, FOX_HEADS * LANES), lambda b, t: (b, 0, 0)),
            pl.BlockSpec((per_step, FOX_HEADS, seq), lambda b, t: (b, 0, 0)),
            pl.BlockSpec((per_step, tq, FOX_HEADS), lambda b, t: (b, t, 0)),
        ],
        out_specs=pl.BlockSpec((per_step, tq, FOX_W), lambda b, t: (b, t, 0)),
        out_shape=jax.ShapeDtypeStruct((batch, seq, FOX_W), BF16),
        scratch_shapes=[pltpu.VMEM((per_step * FOX_HEADS // 2, 2 * tq, LANES), F32)],
        compiler_params=_params(2),
        name="fox",
    )(bounds, qb.reshape(batch, seq, FOX_W), kb.reshape(batch, seq, FOX_W),
      vb.reshape(batch, seq, FOX_HEADS * LANES), cum, cum.transpose(0, 2, 1))
    return out.reshape(batch * seq, FOX_W)


def _merge_ffn_kernel(x_ref, on_ref, of_ref, gmix_ref, wgm_ref, won_ref, wof_ref, wout_ref,
                      g2_ref, wup_ref, wdn_ref, o_ref):
    x = x_ref[...]
    u = _rms_rows(x, gmix_ref[...]).astype(BF16)
    gate = jax.nn.sigmoid(_dot(u, wgm_ref[...]))
    merged = (gate[:, :D_MODEL] * _dot(on_ref[...], won_ref[...])
              + gate[:, D_MODEL:] * _dot(of_ref[...], wof_ref[...]))
    x2 = x + _dot(merged.astype(BF16), wout_ref[...])
    o_ref[...] = _swiglu_residual(x2, g2_ref[...], wup_ref, wdn_ref)


def _merge_ffn(x2d, o_nsa, o_fox, mix_norm, w_gm, w_o_nsa, w_o_fox, w_out, gain2, w_up, w_down):
    n = x2d.shape[0]
    tm = TOKEN_TILE // 2

    def row(width):
        return pl.BlockSpec((tm, width), lambda i: (i, 0))

    return pl.pallas_call(
        _merge_ffn_kernel,
        grid=(n // tm,),
        in_specs=[row(D_MODEL), row(NSA_W), row(FOX_W), _const_spec((1, D_MODEL)),
                  _const_spec((D_MODEL, 2 * D_MODEL)), _const_spec((NSA_W, D_MODEL)),
                  _const_spec((FOX_W, D_MODEL)), _const_spec((D_MODEL, D_MODEL)),
                  _const_spec((1, D_MODEL)), _const_spec((D_MODEL, 2 * D_FF)),
                  _const_spec((D_FF, D_MODEL))],
        out_specs=row(D_MODEL),
        out_shape=jax.ShapeDtypeStruct((n, D_MODEL), F32),
        compiler_params=_params(1),
        name="merge_ffn",
    )(x2d, o_nsa, o_fox, mix_norm.reshape(1, D_MODEL), w_gm, w_o_nsa.astype(BF16),
      w_o_fox.astype(BF16), w_out.astype(BF16), gain2.reshape(1, D_MODEL), w_up.astype(BF16),
      w_down.astype(BF16))


def _layer(x, ffn1_norm, ffn1_w_up, ffn1_w_down, mix_norm, w_in, b_forget, nsa_q_gain, nsa_k_gain,
           fox_q_gain, fox_k_gain, cmp_pos_k, cmp_pos_v, cmp_k_w1, cmp_k_w2, cmp_v_w1, cmp_v_w2,
           w_o_nsa, w_o_fox, w_out, ffn2_norm, ffn2_w_up, ffn2_w_down, rel_bias_table):
    batch, seq, d = x.shape
    assert d == D_MODEL and seq % ATT_TILE == 0 and (batch * seq) % TOKEN_TILE == 0
    assert seq // SLC_BLOCK <= LANES and (seq - CMP_BLOCK) // CMP_STRIDE + 1 == LANES - 1
    assert seq // (CMP_BLOCK // 2) == LANES
    x2d = x.reshape(batch * seq, D_MODEL)

    x1 = _ffn(x2d, ffn1_norm, ffn1_w_up, ffn1_w_down)

    w_packed, gain_row, small_bias, w_gm = _pack_in_proj(w_in, b_forget, nsa_q_gain, nsa_k_gain,
                                                         fox_q_gain, fox_k_gain)
    qa, qb, kb, ksw, vb, vsw, kc, vc, small = _in_proj(x1, mix_norm, w_packed, gain_row, small_bias)

    kcmp, vcmp = _compress(kc, vc, batch, seq, _pack_compress(cmp_pos_k, cmp_k_w1, cmp_k_w2),
                           _pack_compress(cmp_pos_v, cmp_v_w1, cmp_v_w2), nsa_k_gain[0])
    toeplitz, cmp_bias = _bias_tables(rel_bias_table, seq // ATT_TILE)
    b_slc = _score_bound(nsa_q_gain, nsa_k_gain[1])
    b_win = _score_bound(nsa_q_gain, nsa_k_gain[2])
    t_hi, t_lo = jnp.max(rel_bias_table) * LOG2E, jnp.min(rel_bias_table) * LOG2E
    nsa_ok = 2.0 * jnp.maximum(b_slc, b_win) + (t_hi - t_lo) < EXP2_RANGE
    nsa_bounds = jnp.stack([b_slc + t_hi, b_win + t_hi])
    o_nsa = lax.cond(
        nsa_ok,
        lambda: _nsa(nsa_bounds, qa, kcmp, vcmp, ksw, vsw, small, toeplitz, cmp_bias, batch, seq, False),
        lambda: _nsa(nsa_bounds, qa, kcmp, vcmp, ksw, vsw, small, toeplitz, cmp_bias, batch, seq, True))

    n_gate = 3 * NSA_HEADS
    logf_t = small[:, n_gate:n_gate + FOX_HEADS].reshape(batch, seq, FOX_HEADS).transpose(0, 2, 1)
    cum = _cumsum(logf_t)
    b_fox = _score_bound(fox_q_gain, fox_k_gain)
    fox_bounds = jnp.stack([b_fox])
    o_fox = lax.cond(2.0 * b_fox < EXP2_RANGE,
                     lambda: _fox(fox_bounds, qb, kb, vb, cum, batch, seq, False),
                     lambda: _fox(fox_bounds, qb, kb, vb, cum, batch, seq, True))

    w_o_nsa_p = w_o_nsa.reshape(NSA_KV_GROUPS, NSA_Q_PER_GROUP, HEAD_DIM, D_MODEL).transpose(1, 0, 2, 3)
    out = _merge_ffn(x1, o_nsa, o_fox, mix_norm, w_gm, w_o_nsa_p.reshape(NSA_W, D_MODEL), w_o_fox,
                     w_out, ffn2_norm, ffn2_w_up, ffn2_w_down)
    return out.reshape(batch, seq, D_MODEL)


def kernel(x, ffn1_norm, ffn1_w_up, ffn1_w_down, mix_norm, w_in, b_forget, nsa_q_gain, nsa_k_gain,
           fox_q_gain, fox_k_gain, cmp_pos_k, cmp_pos_v, cmp_k_w1, cmp_k_w2, cmp_v_w1, cmp_v_w2,
           w_o_nsa, w_o_fox, w_out, ffn2_norm, ffn2_w_up, ffn2_w_down, rel_bias_table):
    for layer in range(ffn1_norm.shape[0]):
        x = _layer(x, ffn1_norm[layer], ffn1_w_up[layer], ffn1_w_down[layer], mix_norm[layer],
                   w_in[layer], b_forget[layer], nsa_q_gain[layer], nsa_k_gain[layer],
                   fox_q_gain[layer], fox_k_gain[layer], cmp_pos_k[layer], cmp_pos_v[layer],
                   cmp_k_w1[layer], cmp_k_w2[layer], cmp_v_w1[layer], cmp_v_w2[layer],
                   w_o_nsa[layer], w_o_fox[layer], w_out[layer], ffn2_norm[layer],
                   ffn2_w_up[layer], ffn2_w_down[layer], rel_bias_table)
    return x
```

```python
import functools
import math

import numpy as np
import jax
import jax.numpy as jnp
from jax import lax
from jax.experimental import pallas as pl
from jax.experimental.pallas import tpu as pltpu

F32 = jnp.float32
BF16 = jnp.bfloat16

D_MODEL = 1024
HEAD_DIM = 64
NSA_HEADS = 8
NSA_KV_GROUPS = 2
NSA_Q_PER_GROUP = NSA_HEADS // NSA_KV_GROUPS
CMP_BLOCK = 32
CMP_STRIDE = 16
CMP_HIDDEN = 128
SLC_BLOCK = 64
SLC_TOPK = 8
WINDOW = 512
FOX_HEADS = 8
D_FF = 2816
N_BUCKETS = 32
MAX_DISTANCE = 128
RMS_EPS = 1e-6
NEG_INF = -1.0e30
FORCE_BONUS = 1.0e4

NSA_W = NSA_HEADS * HEAD_DIM
NSA_KV_W = NSA_KV_GROUPS * HEAD_DIM
FOX_W = FOX_HEADS * HEAD_DIM
IN_SPLITS = (NSA_W, NSA_KV_W, NSA_KV_W, NSA_KV_W, NSA_KV_W, NSA_KV_W, NSA_KV_W, 3 * NSA_HEADS,
             FOX_W, FOX_W, FOX_W, FOX_HEADS, 2 * D_MODEL)

LANES = 128
TOKEN_TILE = 512
FFN_CHUNK = D_FF // 2
ATT_TILE = 256
N_WIN_TILES = WINDOW // ATT_TILE + 1
VMEM_LIMIT = 56 * 1024 * 1024
LOG2E = 1.4426950408889634
EXP2_RANGE = 100.0

PQ_A = 0
PQ_B = PQ_A + NSA_W
PK_B = PQ_B + FOX_W
PK_SW = PK_B + FOX_W
P_NORM_END = PK_SW + 2 * NSA_KV_W
PV_B = P_NORM_END
PV_SW = PV_B + FOX_W
PKV_C = PV_SW + 2 * NSA_KV_W
P_SMALL = PKV_C + 2 * NSA_KV_W
P_END = P_SMALL + LANES
NORM_CHUNK = 256


def _dot(a, b):
    return jnp.dot(a, b, preferred_element_type=F32)


def _dot_nt(a, b):
    return lax.dot_general(a, b, (((1,), (1,)), ((), ())), preferred_element_type=F32)


def _split_dot(x, w):
    hi = x.astype(BF16)
    lo = (x - hi.astype(F32)).astype(BF16)
    return _dot(hi, w) + _dot(lo, w)


def _rms_rows(x, gain_row):
    ms = jnp.mean(x * x, axis=-1, keepdims=True)
    return x * lax.rsqrt(ms + RMS_EPS) * gain_row


def _const_spec(shape):
    nd = len(shape)
    return pl.BlockSpec(shape, lambda *_: (0,) * nd, pipeline_mode=pl.Buffered(1))


def _params(n_axes):
    return pltpu.CompilerParams(dimension_semantics=("arbitrary",) * n_axes,
                                vmem_limit_bytes=VMEM_LIMIT)


def _swiglu_residual(x, gain_row, wup_ref, wdn_ref):
    xn = _rms_rows(x, gain_row).astype(BF16)
    acc = jnp.zeros(x.shape, F32)
    for c in range(D_FF // FFN_CHUNK):
        lo = c * FFN_CHUNK
        gate = _dot(xn, wup_ref[:, lo:lo + FFN_CHUNK])
        up = _dot(xn, wup_ref[:, D_FF + lo:D_FF + lo + FFN_CHUNK])
        h = (gate * jax.nn.sigmoid(gate) * up).astype(BF16)
        acc = acc + _dot(h, wdn_ref[lo:lo + FFN_CHUNK, :])
    return x + 0.5 * acc


def _ffn_kernel(x_ref, g_ref, wup_ref, wdn_ref, o_ref):
    o_ref[...] = _swiglu_residual(x_ref[...], g_ref[...], wup_ref, wdn_ref)


def _ffn(x2d, gain, w_up, w_down):
    n = x2d.shape[0]
    row = pl.BlockSpec((TOKEN_TILE, D_MODEL), lambda i: (i, 0))
    return pl.pallas_call(
        _ffn_kernel,
        grid=(n // TOKEN_TILE,),
        in_specs=[row, _const_spec((1, D_MODEL)), _const_spec((D_MODEL, 2 * D_FF)),
                  _const_spec((D_FF, D_MODEL))],
        out_specs=row,
        out_shape=jax.ShapeDtypeStruct((n, D_MODEL), F32),
        compiler_params=_params(1),
        name="ffn",
    )(x2d, gain.reshape(1, D_MODEL), w_up.astype(BF16), w_down.astype(BF16))


def _block_ones(width, size=NORM_CHUNK):
    idx = np.arange(size) // width
    return jnp.asarray((idx[:, None] == idx[None, :]).astype(np.float32), BF16)


def _pack_in_proj(w_in, b_forget, nsa_q_gain, nsa_k_gain, fox_q_gain, fox_k_gain):
    scale = LOG2E / math.sqrt(HEAD_DIM)
    pts = np.cumsum(np.array(IN_SPLITS))[:-1].tolist()
    qa, kc, vc, ks, vs, kw, vw, ga, qb, kb, vb, fb, gm = jnp.split(w_in, pts, axis=1)
    cols = []
    for r in range(NSA_Q_PER_GROUP):
        for g in range(NSA_KV_GROUPS):
            h = g * NSA_Q_PER_GROUP + r
            cols.append(qa[:, h * HEAD_DIM:(h + 1) * HEAD_DIM])
    cols += [qb, kb, ks, kw]
    gains = [jnp.tile(nsa_q_gain * scale, NSA_HEADS), jnp.tile(fox_q_gain * scale, FOX_HEADS),
             jnp.tile(fox_k_gain, FOX_HEADS), jnp.tile(nsa_k_gain[1], NSA_KV_GROUPS),
             jnp.tile(nsa_k_gain[2], NSA_KV_GROUPS)]
    n_small_pad = LANES - ga.shape[1] - fb.shape[1]
    cols += [vb, vs, vw, kc, vc, ga, fb, jnp.zeros((D_MODEL, n_small_pad), F32)]
    w_packed = jnp.concatenate(cols, axis=1).astype(BF16)
    gain_row = jnp.concatenate(gains).reshape(1, P_NORM_END)
    small_bias = jnp.concatenate([jnp.zeros((ga.shape[1],), F32), b_forget,
                                  jnp.zeros((n_small_pad,), F32)]).reshape(1, LANES)
    return w_packed, gain_row, small_bias, gm.astype(BF16)


def _in_proj_kernel(x_ref, g_ref, w_ref, gain_ref, sbias_ref, bd64_ref,
                    qa_ref, qb_ref, kb_ref, ksw_ref, vb_ref, vsw_ref, kc_ref, vc_ref, small_ref, kv_scr):
    u = _rms_rows(x_ref[...], g_ref[...]).astype(BF16)
    tm = u.shape[0]

    y = _dot(u, w_ref[:, 0:P_NORM_END])
    n_chunks = P_NORM_END // NORM_CHUNK
    squares = jnp.concatenate([jnp.square(y[:, c * NORM_CHUNK:(c + 1) * NORM_CHUNK]).astype(BF16)
                               for c in range(n_chunks)], axis=0)
    ss = _dot(squares, bd64_ref[...])
    chunk = 0
    for o_ref in (qa_ref, qb_ref, kb_ref, ksw_ref):
        for i in range(o_ref.shape[1] // NORM_CHUNK):
            cols = slice(chunk * NORM_CHUNK, (chunk + 1) * NORM_CHUNK)
            inv = lax.rsqrt(ss[chunk * tm:(chunk + 1) * tm] * (1.0 / HEAD_DIM) + RMS_EPS)
            o_ref[:, i * NORM_CHUNK:(i + 1) * NORM_CHUNK] = (y[:, cols] * inv * gain_ref[:, cols]).astype(BF16)
            chunk += 1
    low = lax.broadcasted_iota(jnp.int32, (u.shape[0], LANES), 1) < HEAD_DIM

    def store_with_ones(v, o_ref):
        for i in range(v.shape[1] // LANES):
            pair = v[:, i * LANES:(i + 1) * LANES]
            o_ref[:, 2 * i * LANES:(2 * i + 1) * LANES] = jnp.where(low, pair, 1.0).astype(BF16)
            o_ref[:, (2 * i + 1) * LANES:(2 * i + 2) * LANES] = jnp.where(low, 1.0, pair).astype(BF16)

    rest = _dot(u, w_ref[:, P_NORM_END:P_END])
    store_with_ones(rest[:, PV_B - P_NORM_END:PV_SW - P_NORM_END], vb_ref)
    store_with_ones(rest[:, PV_SW - P_NORM_END:PKV_C - P_NORM_END], vsw_ref)
    for j, o_ref in enumerate((kc_ref, vc_ref)):
        c0 = PKV_C - P_NORM_END + j * NSA_KV_W
        kv_scr[j] = rest[:, c0:c0 + NSA_KV_W]
        for l in range(CMP_STRIDE):
            o_ref[0, :, l * NSA_KV_W:(l + 1) * NSA_KV_W] = kv_scr[j, pl.ds(l, tm // CMP_STRIDE,
                                                                           stride=CMP_STRIDE), :]
    z = rest[:, P_SMALL - P_NORM_END:] + sbias_ref[...]
    lane = lax.broadcasted_iota(jnp.int32, z.shape, 1)
    log_sig = jnp.minimum(z, 0.0) - jnp.log1p(jnp.exp(-jnp.abs(z)))
    small_ref[...] = jnp.where(lane < 3 * NSA_HEADS, jax.nn.sigmoid(z), log_sig)


def _in_proj(x2d, mix_norm, w_packed, gain_row, small_bias, batch, seq):
    n = x2d.shape[0]
    steps_per_seq = seq // TOKEN_TILE

    def row(width):
        return pl.BlockSpec((TOKEN_TILE, width), lambda i: (i, 0))

    def out(width, dtype):
        return row(width), jax.ShapeDtypeStruct((n, width), dtype)

    grouped = (pl.BlockSpec((1, TOKEN_TILE // CMP_STRIDE, CMP_STRIDE * NSA_KV_W),
                            lambda i: (i // steps_per_seq, i % steps_per_seq, 0)),
               jax.ShapeDtypeStruct((batch, seq // CMP_STRIDE, CMP_STRIDE * NSA_KV_W), F32))
    outs = [out(NSA_W, BF16), out(FOX_W, BF16), out(FOX_W, BF16), out(2 * NSA_KV_W, BF16),
            out(2 * FOX_W, BF16), out(4 * NSA_KV_W, BF16), grouped, grouped, out(LANES, F32)]
    return pl.pallas_call(
        _in_proj_kernel,
        grid=(n // TOKEN_TILE,),
        in_specs=[row(D_MODEL), _const_spec((1, D_MODEL)), _const_spec((D_MODEL, P_END)),
                  _const_spec((1, P_NORM_END)), _const_spec((1, LANES)),
                  _const_spec((NORM_CHUNK, NORM_CHUNK))],
        out_specs=[spec for spec, _ in outs],
        out_shape=[shape for _, shape in outs],
        scratch_shapes=[pltpu.VMEM((2, TOKEN_TILE, NSA_KV_W), F32)],
        compiler_params=_params(1),
        name="in_proj",
    )(x2d, mix_norm.reshape(1, D_MODEL), w_packed, gain_row, small_bias, _block_ones(HEAD_DIM))


def _pack_compress(pos, w1, w2):
    half = CMP_BLOCK // 2
    eye = jnp.eye(NSA_KV_GROUPS, dtype=F32)
    w1r = w1.reshape(CMP_BLOCK, HEAD_DIM, CMP_HIDDEN)

    def big(w):
        return jnp.einsum('ldh,pg->lpdgh', w, eye).reshape(half * NSA_KV_W, NSA_KV_GROUPS * CMP_HIDDEN)

    def posrow(p):
        return jnp.broadcast_to(p[:, None, :], (half, NSA_KV_GROUPS, HEAD_DIM)).reshape(1, half * NSA_KV_W)

    w2big = jnp.einsum('hd,pg->phgd', w2, eye).reshape(NSA_KV_GROUPS * CMP_HIDDEN, NSA_KV_W)
    return (posrow(pos[:half]), posrow(pos[half:]), big(w1r[:half]).astype(BF16),
            big(w1r[half:]).astype(BF16), w2big.astype(BF16))


def _compress_kernel(kc_ref, vc_ref, kp_lo, kp_hi, kw_lo, kw_hi, kw2, vp_lo, vp_hi, vw_lo, vw_hi, vw2,
                     kgain_ref, bd64_ref, ko_ref, vo_ref):
    def mlp(r, p_lo, p_hi, w_lo, w_hi, w2):
        first = _dot((r + p_lo[...]).astype(BF16), w_lo[...])
        second = _dot((r + p_hi[...]).astype(BF16), w_hi[...])
        h = first + pltpu.roll(second, second.shape[0] - 1, axis=0)
        return _dot((h * jax.nn.sigmoid(h)).astype(BF16), w2[...])

    k = mlp(kc_ref[0], kp_lo, kp_hi, kw_lo, kw_hi, kw2)
    ss = _split_dot(k * k, bd64_ref[...])
    ko_ref[0] = (k * lax.rsqrt(ss * (1.0 / HEAD_DIM) + RMS_EPS) * kgain_ref[...]).astype(BF16)
    vo_ref[0] = mlp(vc_ref[0], vp_lo, vp_hi, vw_lo, vw_hi, vw2).astype(BF16)


def _compress(kc, vc, batch, seq, k_pack, v_pack, k_gain):
    rows = seq // (CMP_BLOCK // 2)
    width = (CMP_BLOCK // 2) * NSA_KV_W
    blk = pl.BlockSpec((1, rows, width), lambda b: (b, 0, 0))
    out = pl.BlockSpec((1, rows, NSA_KV_W), lambda b: (b, 0, 0))
    pack_specs = [_const_spec((1, width)), _const_spec((1, width)),
                  _const_spec((width, NSA_KV_GROUPS * CMP_HIDDEN)),
                  _const_spec((width, NSA_KV_GROUPS * CMP_HIDDEN)),
                  _const_spec((NSA_KV_GROUPS * CMP_HIDDEN, NSA_KV_W))]
    bd = _block_ones(HEAD_DIM, NSA_KV_W)
    return pl.pallas_call(
        _compress_kernel,
        grid=(batch,),
        in_specs=[blk, blk] + pack_specs + pack_specs + [_const_spec((1, NSA_KV_W)),
                                                         _const_spec((NSA_KV_W, NSA_KV_W))],
        out_specs=[out, out],
        out_shape=[jax.ShapeDtypeStruct((batch, rows, NSA_KV_W), BF16)] * 2,
        compiler_params=_params(1),
        name="compress",
    )(kc.reshape(batch, rows, width), vc.reshape(batch, rows, width), *k_pack, *v_pack,
      jnp.tile(k_gain, NSA_KV_GROUPS).reshape(1, NSA_KV_W), bd)


FORGET_LANE = 3 * NSA_HEADS


def _cumsum_kernel(x_ref, cols_ref, rows_ref):
    x = x_ref[...]
    row = lax.broadcasted_iota(jnp.int32, x.shape, 0)
    k = 1
    while k < x.shape[0]:
        x = x + jnp.where(row >= k, pltpu.roll(x, k, axis=0), 0.0)
        k *= 2
    cols_ref[...] = x
    rows_ref[0] = x.T[FORGET_LANE:FORGET_LANE + FOX_HEADS]


def _cumsum(small, batch, seq):
    spec = pl.BlockSpec((seq, LANES), lambda b: (b, 0))
    return pl.pallas_call(
        _cumsum_kernel, grid=(batch,), in_specs=[spec],
        out_specs=[spec, pl.BlockSpec((1, FOX_HEADS, seq), lambda b: (b, 0, 0))],
        out_shape=[jax.ShapeDtypeStruct((batch * seq, LANES), F32),
                   jax.ShapeDtypeStruct((batch, FOX_HEADS, seq), F32)],
        compiler_params=_params(1), name="cumsum",
    )(small)


def _write_bias_tile(d, valid, tab_ref, o_ref):
    max_exact = N_BUCKETS // 2
    n = jnp.maximum(d, 0)
    nf = jnp.maximum(n, 1).astype(F32)
    large = max_exact + (jnp.log(nf / max_exact) / math.log(MAX_DISTANCE / max_exact)
                         * (N_BUCKETS - max_exact)).astype(jnp.int32)
    bucket = jnp.where(n < max_exact, n, jnp.minimum(large, N_BUCKETS - 1))
    for h in range(NSA_HEADS):
        acc = jnp.zeros(d.shape, F32)
        for b in range(N_BUCKETS):
            acc = jnp.where(bucket == b, tab_ref[b, h], acc)
        g, r = divmod(h, NSA_Q_PER_GROUP)
        o_ref[0, g, r] = jnp.where(valid, acc * LOG2E, NEG_INF)


def _toeplitz_bias_kernel(tab_ref, o_ref):
    s = pl.program_id(0)
    i = lax.broadcasted_iota(jnp.int32, (ATT_TILE, ATT_TILE), 0)
    j = lax.broadcasted_iota(jnp.int32, (ATT_TILE, ATT_TILE), 1)
    d = jnp.minimum(s, N_WIN_TILES - 1) * ATT_TILE + i - j
    d_hi = jnp.where(s < N_WIN_TILES, WINDOW, jnp.int32(1 << 30))
    _write_bias_tile(d, (d >= 0) & (d < d_hi), tab_ref, o_ref)


def _cmp_bias_kernel(tab_ref, o_ref):
    t = pl.program_id(0)
    c = lax.broadcasted_iota(jnp.int32, (LANES, ATT_TILE), 0)
    i = lax.broadcasted_iota(jnp.int32, (LANES, ATT_TILE), 1)
    d = t * ATT_TILE + i - CMP_STRIDE * c - (CMP_BLOCK - 1)
    _write_bias_tile(d, (d >= 0) & (c < LANES - 1), tab_ref, o_ref)


def _bias_tables(rel_bias_table, n_q_tiles):
    def call(body, steps, rows, name):
        shape = (steps, NSA_KV_GROUPS, NSA_Q_PER_GROUP, rows, ATT_TILE)
        return pl.pallas_call(
            body, grid=(steps,),
            in_specs=[pl.BlockSpec(memory_space=pltpu.SMEM)],
            out_specs=pl.BlockSpec((1,) + shape[1:], lambda s: (s, 0, 0, 0, 0)),
            out_shape=jax.ShapeDtypeStruct(shape, F32),
            compiler_params=_params(1), name=name,
        )(rel_bias_table)

    return (call(_toeplitz_bias_kernel, N_WIN_TILES + 1, ATT_TILE, "toeplitz_bias"),
            call(_cmp_bias_kernel, n_q_tiles, LANES, "cmp_bias"))


def _score_bound(q_gain, k_gain):
    return 1.02 * LOG2E * math.sqrt(HEAD_DIM) * jnp.max(jnp.abs(q_gain)) * jnp.max(jnp.abs(k_gain))


def _attention(chains, groups, bounds, acc_ref):
    n = len(chains)

    def later(k):
        return tuple(i for _, ids in groups[k:] for i in ids)

    def exact_max():
        def half_max(i, dl):
            s = chains[i][0](dl)
            return jnp.maximum(s[:, :LANES], s[:, LANES:])

        mx = [half_max(i, 0) for i in range(n)]
        lo = 1
        for k, (last, _) in enumerate(groups):
            members = later(k)

            def body(dl, carry, members=members):
                return tuple(jnp.maximum(c, half_max(i, dl)) for c, i in zip(carry, members))

            for i, r in zip(members, lax.fori_loop(lo, last + 1, body, tuple(mx[i] for i in members))):
                mx[i] = r
            lo = last + 1
        return tuple(jnp.broadcast_to(jnp.max(m, axis=-1, keepdims=True), (m.shape[0], ATT_TILE))
                     for m in mx)

    shift = exact_max() if bounds is None else [b() for b in bounds]

    def weights(i, dl):
        return jnp.exp2(chains[i][0](dl) - shift[i]).astype(BF16)

    def product(i, dl):
        return chains[i][1](weights(i, dl), dl)

    for i in range(n):
        acc_ref[i] = product(i, 0)
    lo = 1
    for k, (last, _) in enumerate(groups):
        members = later(k)
        count = last + 1 - lo

        def body(j, carry, members=members, lo=lo):
            dl = lo + 2 * j
            for i in members:
                acc_ref[i] += product(i, dl) + product(i, dl + 1)
            return carry

        lax.fori_loop(0, count // 2, body, 0)

        @pl.when(count % 2 == 1)
        def _(members=members, last=last):
            for i in members:
                acc_ref[i] += product(i, last)

        lo = last + 1
    return [acc_ref[i] for i in range(n)]


def _pair_lanes(first, second, normalise):
    low = lax.broadcasted_iota(jnp.int32, first.shape, 1) < HEAD_DIM
    pair = jnp.where(low, first, second)
    if not normalise:
        return pair
    sums = pltpu.roll(jnp.where(low, second, first), HEAD_DIM, axis=1)
    return pair * (1.0 / sums)


def _select_blocks(imp_t, t0):
    blk = lax.broadcasted_iota(jnp.int32, imp_t.shape, 0)
    cur = (t0 + lax.broadcasted_iota(jnp.int32, imp_t.shape, 1)) // SLC_BLOCK
    forced = (blk == 0) | (blk == cur) | (blk == cur - 1)
    score = jnp.where(blk <= cur, imp_t + jnp.where(forced, FORCE_BONUS, 0.0), NEG_INF)
    blk_f = blk.astype(F32)
    dead = -3.0e38
    picked = jnp.zeros(imp_t.shape, F32)
    for _ in range(SLC_TOPK):
        best = jnp.max(score, axis=0, keepdims=True)
        first = jnp.min(jnp.where(score == best, blk_f, float(LANES)), axis=0, keepdims=True)
        hit = blk_f == first
        picked = jnp.where(hit, 1.0, picked)
        score = jnp.where(hit, dead, score)
    return picked


def _nsa_kernel(bnd_ref, qa_ref, kcmp_ref, vcmp_ref, ksw_ref, vsw_ref, small_ref, tb_ref, bc_ref, ov_ref,
                e_ref, gx_ref, o_ref, amask_ref, acc_ref, *, n_blocks, exact):
    qt = pl.program_id(1)
    tq = ATT_TILE
    rq = NSA_Q_PER_GROUP
    lane = lax.broadcasted_iota(jnp.int32, (tq, LANES), 1)
    own_lanes = [jnp.where((lane // HEAD_DIM) == g, 1.0, 0.0).astype(BF16) for g in range(NSA_KV_GROUPS)]

    def key_rows(dl):
        return pl.ds(pl.multiple_of((qt - dl) * tq, tq), tq)

    qs, o_cmps = [], []
    for g in range(NSA_KV_GROUPS):
        q = jnp.concatenate([qa_ref[:, r * LANES:(r + 1) * LANES] * own_lanes[g] for r in range(rq)],
                            axis=0)
        qs.append(q)

        s = _dot_nt(kcmp_ref[0], q) + jnp.concatenate([bc_ref[0, g, r] for r in range(rq)], axis=1)
        m = jnp.max(s, axis=0, keepdims=True)
        p = jnp.where(s > 0.5 * NEG_INF, jnp.exp2(s - m), 0.0)
        l = jnp.sum(p, axis=0, keepdims=True)
        p_c = p * (1.0 / jnp.where(l > 0.0, l, 1.0))
        o_cmps.append(_dot(p_c.T.astype(BF16), vcmp_ref[0]))

        p_sum = p_c[:, 0:tq] + p_c[:, tq:2 * tq] + p_c[:, 2 * tq:3 * tq] + p_c[:, 3 * tq:4 * tq]
        hi = p_sum.astype(BF16)
        lo = (p_sum - hi.astype(F32)).astype(BF16)
        imp_t = _dot(ov_ref[...], hi) + _dot(ov_ref[...], lo)
        sel_t = _select_blocks(imp_t[0:n_blocks], qt * tq)
        block_bias = jnp.concatenate([jnp.where(sel_t > 0.0, 0.0, NEG_INF),
                                      jnp.zeros((LANES - n_blocks, tq), F32)], axis=0).T.astype(BF16)
        amask_ref[g] = _dot(block_bias, e_ref[...])

    def slc_chain(g):
        def score(dl):
            tile = dl if isinstance(dl, int) else jnp.where(dl < N_WIN_TILES - 1, dl, N_WIN_TILES)
            rows = key_rows(dl)
            s = _dot_nt(qs[g], ksw_ref[rows, 0:LANES]).reshape(rq, tq, tq)
            return (s + tb_ref[tile, g] + amask_ref[g, :, rows][None]).reshape(rq * tq, tq)

        def pv(p, dl):
            return _dot(p, vsw_ref[key_rows(dl), g * LANES:(g + 1) * LANES])

        return score, pv

    def win_chain(g):
        def score(dl):
            s = _dot_nt(qs[g], ksw_ref[key_rows(dl), LANES:2 * LANES]).reshape(rq, tq, tq)
            return (s + tb_ref[dl, g]).reshape(rq * tq, tq)

        def pv(p, dl):
            c0 = (NSA_KV_GROUPS + g) * LANES
            return _dot(p, vsw_ref[key_rows(dl), c0:c0 + LANES])

        return score, pv

    groups = range(NSA_KV_GROUPS)
    slc_ids = tuple(groups)
    win_ids = tuple(NSA_KV_GROUPS + g for g in groups)
    bounds = None if exact else [lambda i=i: jnp.full((rq * tq, tq), bnd_ref[i // NSA_KV_GROUPS], F32)
                                 for i in range(2 * NSA_KV_GROUPS)]
    accs = _attention([slc_chain(g) for g in groups] + [win_chain(g) for g in groups],
                      [(jnp.minimum(qt, N_WIN_TILES - 1), win_ids), (qt, slc_ids)], bounds, acc_ref)

    gates = small_ref[...]
    g_hi = gates.astype(BF16)
    g_lo = (gates - g_hi.astype(F32)).astype(BF16)
    out = [jnp.zeros((tq, LANES), F32) for _ in range(rq)]
    for b, per_group in enumerate((o_cmps, accs[:NSA_KV_GROUPS], accs[NSA_KV_GROUPS:])):
        gate = _dot(g_hi, gx_ref[b]) + _dot(g_lo, gx_ref[b])
        for r in range(rq):
            first, second = (a[r * tq:(r + 1) * tq] for a in per_group)
            pair = _pair_lanes(first, second, normalise=b > 0)
            out[r] = out[r] + gate[:, r * LANES:(r + 1) * LANES] * pair
    for r in range(rq):
        o_ref[:, r * LANES:(r + 1) * LANES] = out[r].astype(BF16)


def _nsa(bounds, qa, kcmp, vcmp, ksw, vsw, small, toeplitz, cmp_bias, batch, seq, exact):
    tq = ATT_TILE
    nq = seq // tq
    rq = NSA_Q_PER_GROUP
    n_blocks = seq // SLC_BLOCK
    n_cmp = (seq - CMP_BLOCK) // CMP_STRIDE + 1
    ci = np.arange(LANES)[:, None] * CMP_STRIDE
    sj = np.arange(LANES)[None, :] * SLC_BLOCK
    overlap = ((ci <= sj + SLC_BLOCK - 1) & (ci + CMP_BLOCK - 1 >= sj)
               & (np.arange(LANES)[:, None] < n_cmp) & (np.arange(LANES)[None, :] < n_blocks))
    expand = np.arange(LANES)[:, None] == (np.arange(seq)[None, :] // SLC_BLOCK)
    gate_expand = np.zeros((3, LANES, NSA_W), np.float32)
    for b in range(3):
        for g in range(NSA_KV_GROUPS):
            for r in range(rq):
                c0 = r * LANES + g * HEAD_DIM
                gate_expand[b, b * NSA_HEADS + g * rq + r, c0:c0 + HEAD_DIM] = 1.0
    return pl.pallas_call(
        functools.partial(_nsa_kernel, n_blocks=n_blocks, exact=exact),
        grid=(batch, nq),
        in_specs=[
            pl.BlockSpec(memory_space=pltpu.SMEM),
            pl.BlockSpec((tq, NSA_W), lambda b, t: (b * nq + t, 0)),
            pl.BlockSpec((1, LANES, NSA_KV_W), lambda b, t: (b, 0, 0)),
            pl.BlockSpec((1, LANES, NSA_KV_W), lambda b, t: (b, 0, 0)),
            pl.BlockSpec((seq, 2 * NSA_KV_W), lambda b, t: (b, 0)),
            pl.BlockSpec((seq, 2 * NSA_KV_GROUPS * LANES), lambda b, t: (b, 0)),
            pl.BlockSpec((tq, LANES), lambda b, t: (b * nq + t, 0)),
            _const_spec((N_WIN_TILES + 1, NSA_KV_GROUPS, rq, tq, tq)),
            pl.BlockSpec((1, NSA_KV_GROUPS, rq, LANES, tq), lambda b, t: (t, 0, 0, 0, 0)),
            _const_spec((LANES, LANES)),
            _const_spec((LANES, seq)),
            _const_spec((3, LANES, NSA_W)),
        ],
        out_specs=pl.BlockSpec((tq, NSA_W), lambda b, t: (b * nq + t, 0)),
        out_shape=jax.ShapeDtypeStruct((batch * seq, NSA_W), BF16),
        scratch_shapes=[pltpu.VMEM((NSA_KV_GROUPS, tq, seq), F32),
                        pltpu.VMEM((2 * NSA_KV_GROUPS, rq * tq, LANES), F32)],
        compiler_params=_params(2),
        name="nsa",
    )(bounds, qa, kcmp, vcmp, ksw, vsw, small, toeplitz, cmp_bias,
      jnp.asarray(overlap.T.astype(np.float32), BF16), jnp.asarray(expand.astype(np.float32), BF16),
      jnp.asarray(gate_expand, BF16))


def _fox_kernel(bnd_ref, q_ref, k_ref, v_ref, cum_ref, cum_t_ref, o_ref, acc_ref, *, exact):
    qt = pl.program_id(1)
    tq = ATT_TILE
    pairs = FOX_HEADS // 2
    row = lax.broadcasted_iota(jnp.int32, (2, tq, tq), 1)
    col = lax.broadcasted_iota(jnp.int32, (2, tq, tq), 2)
    lane = lax.broadcasted_iota(jnp.int32, (tq, LANES), 1)
    low = jnp.where(lane < HEAD_DIM, 1.0, 0.0).astype(BF16)
    high = jnp.where(lane < HEAD_DIM, 0.0, 1.0).astype(BF16)
    n_seq = q_ref.shape[0]

    def key_rows(dl):
        return pl.ds(pl.multiple_of((qt - dl) * tq, tq), tq)

    bounds = []

    def pair_chain(b, p):
        pair = q_ref[b, :, p * LANES:(p + 1) * LANES]
        q = jnp.concatenate([pair * low, pair * high], axis=0)
        base = cum_ref[b, 2 * p:2 * p + 2, pl.ds(pl.multiple_of(qt * tq, tq), LANES)][:, 0:1]

        def decay_of(rows):
            return (base - cum_ref[b, 2 * p:2 * p + 2, rows]) * LOG2E

        def bound():
            cum_rows = cum_t_ref[b]
            c0 = FORGET_LANE + 2 * p
            own = jnp.concatenate([base[h:h + 1, :] - cum_rows[:, c0 + h:c0 + h + 1] for h in range(2)],
                                  axis=0)
            return jnp.broadcast_to(own * LOG2E + bnd_ref[0], (2 * tq, tq))

        bounds.append(bound)

        def score(dl):
            rows = key_rows(dl)
            decay = decay_of(rows)
            s = _dot_nt(q, k_ref[b, rows, p * LANES:(p + 1) * LANES]).reshape(2, tq, tq) + decay[:, None, :]
            if isinstance(dl, int):
                s = jnp.where(col <= row, s, NEG_INF)
            return s.reshape(2 * tq, tq)

        def pv(w, dl):
            rows = key_rows(dl)
            return jnp.concatenate(
                [_dot(w[0:tq], v_ref[b, rows, 2 * p * LANES:(2 * p + 1) * LANES]),
                 _dot(w[tq:2 * tq], v_ref[b, rows, (2 * p + 1) * LANES:(2 * p + 2) * LANES])], axis=0)

        return score, pv

    chains = [pair_chain(b, p) for b in range(n_seq) for p in range(pairs)]
    accs = _attention(chains, [(qt, tuple(range(len(chains))))], None if exact else bounds, acc_ref)
    for i, acc in enumerate(accs):
        b, p = divmod(i, pairs)
        o_ref[b, :, p * LANES:(p + 1) * LANES] = _pair_lanes(acc[0:tq], acc[tq:2 * tq], True).astype(BF16)


def _fox(bounds, qb, kb, vb, cum, cum_cols, batch, seq, exact):
    tq = ATT_TILE
    nq = seq // tq
    per_step = 2 if batch % 2 == 0 else 1
    out = pl.pallas_call(
        functools.partial(_fox_kernel, exact=exact),
        grid=(batch // per_step, nq),
        in_specs=[
            pl.BlockSpec(memory_space=pltpu.SMEM),
            pl.BlockSpec((per_step, tq, FOX_W), lambda b, t: (b, t, 0)),
            pl.BlockSpec((per_step, seq, FOX_W), lambda b, t: (b, 0, 0)),
            pl.BlockSpec((per_step, seq, FOX_HEADS * LANES), lambda b, t: (b, 0, 0)),
            pl.BlockSpec((per_step, FOX_HEADS, seq), lambda b, t: (b, 0, 0)),
            pl.BlockSpec((per_step, tq, LANES), lambda b, t: (b, t, 0)),
        ],
        out_specs=pl.BlockSpec((per_step, tq, FOX_W), lambda b, t: (b, t, 0)),
        out_shape=jax.ShapeDtypeStruct((batch, seq, FOX_W), BF16),
        scratch_shapes=[pltpu.VMEM((per_step * FOX_HEADS // 2, 2 * tq, LANES), F32)],
        compiler_params=_params(2),
        name="fox",
    )(bounds, qb.reshape(batch, seq, FOX_W), kb.reshape(batch, seq, FOX_W),
      vb.reshape(batch, seq, FOX_HEADS * LANES), cum, cum_cols.reshape(batch, seq, LANES))
    return out.reshape(batch * seq, FOX_W)


def _merge_ffn_kernel(x_ref, on_ref, of_ref, gmix_ref, wgm_ref, won_ref, wof_ref, wout_ref,
                      g2_ref, wup_ref, wdn_ref, o_ref):
    x = x_ref[...]
    u = _rms_rows(x, gmix_ref[...]).astype(BF16)
    gate = jax.nn.sigmoid(_dot(u, wgm_ref[...]))
    merged = (gate[:, :D_MODEL] * _dot(on_ref[...], won_ref[...])
              + gate[:, D_MODEL:] * _dot(of_ref[...], wof_ref[...]))
    x2 = x + _dot(merged.astype(BF16), wout_ref[...])
    o_ref[...] = _swiglu_residual(x2, g2_ref[...], wup_ref, wdn_ref)


def _merge_ffn(x2d, o_nsa, o_fox, mix_norm, w_gm, w_o_nsa, w_o_fox, w_out, gain2, w_up, w_down):
    n = x2d.shape[0]
    tm = TOKEN_TILE // 2

    def row(width):
        return pl.BlockSpec((tm, width), lambda i: (i, 0))

    return pl.pallas_call(
        _merge_ffn_kernel,
        grid=(n // tm,),
        in_specs=[row(D_MODEL), row(NSA_W), row(FOX_W), _const_spec((1, D_MODEL)),
                  _const_spec((D_MODEL, 2 * D_MODEL)), _const_spec((NSA_W, D_MODEL)),
                  _const_spec((FOX_W, D_MODEL)), _const_spec((D_MODEL, D_MODEL)),
                  _const_spec((1, D_MODEL)), _const_spec((D_MODEL, 2 * D_FF)),
                  _const_spec((D_FF, D_MODEL))],
        out_specs=row(D_MODEL),
        out_shape=jax.ShapeDtypeStruct((n, D_MODEL), F32),
        compiler_params=_params(1),
        name="merge_ffn",
    )(x2d, o_nsa, o_fox, mix_norm.reshape(1, D_MODEL), w_gm, w_o_nsa.astype(BF16),
      w_o_fox.astype(BF16), w_out.astype(BF16), gain2.reshape(1, D_MODEL), w_up.astype(BF16),
      w_down.astype(BF16))


def _layer(x, ffn1_norm, ffn1_w_up, ffn1_w_down, mix_norm, w_in, b_forget, nsa_q_gain, nsa_k_gain,
           fox_q_gain, fox_k_gain, cmp_pos_k, cmp_pos_v, cmp_k_w1, cmp_k_w2, cmp_v_w1, cmp_v_w2,
           w_o_nsa, w_o_fox, w_out, ffn2_norm, ffn2_w_up, ffn2_w_down, rel_bias_table):
    batch, seq, d = x.shape
    assert d == D_MODEL and seq % ATT_TILE == 0 and (batch * seq) % TOKEN_TILE == 0
    assert seq // SLC_BLOCK <= LANES and (seq - CMP_BLOCK) // CMP_STRIDE + 1 == LANES - 1
    assert seq // (CMP_BLOCK // 2) == LANES
    x2d = x.reshape(batch * seq, D_MODEL)

    x1 = _ffn(x2d, ffn1_norm, ffn1_w_up, ffn1_w_down)

    w_packed, gain_row, small_bias, w_gm = _pack_in_proj(w_in, b_forget, nsa_q_gain, nsa_k_gain,
                                                         fox_q_gain, fox_k_gain)
    qa, qb, kb, ksw, vb, vsw, kc, vc, small = _in_proj(x1, mix_norm, w_packed, gain_row, small_bias,
                                                       batch, seq)

    kcmp, vcmp = _compress(kc, vc, batch, seq, _pack_compress(cmp_pos_k, cmp_k_w1, cmp_k_w2),
                           _pack_compress(cmp_pos_v, cmp_v_w1, cmp_v_w2), nsa_k_gain[0])
    toeplitz, cmp_bias = _bias_tables(rel_bias_table, seq // ATT_TILE)
    b_slc = _score_bound(nsa_q_gain, nsa_k_gain[1])
    b_win = _score_bound(nsa_q_gain, nsa_k_gain[2])
    t_hi, t_lo = jnp.max(rel_bias_table) * LOG2E, jnp.min(rel_bias_table) * LOG2E
    nsa_ok = 2.0 * jnp.maximum(b_slc, b_win) + (t_hi - t_lo) < EXP2_RANGE
    nsa_bounds = jnp.stack([b_slc + t_hi, b_win + t_hi])
    o_nsa = lax.cond(
        nsa_ok,
        lambda: _nsa(nsa_bounds, qa, kcmp, vcmp, ksw, vsw, small, toeplitz, cmp_bias, batch, seq, False),
        lambda: _nsa(nsa_bounds, qa, kcmp, vcmp, ksw, vsw, small, toeplitz, cmp_bias, batch, seq, True))

    cum_cols, cum = _cumsum(small, batch, seq)
    b_fox = _score_bound(fox_q_gain, fox_k_gain)
    fox_bounds = jnp.stack([b_fox])
    o_fox = lax.cond(2.0 * b_fox < EXP2_RANGE,
                     lambda: _fox(fox_bounds, qb, kb, vb, cum, cum_cols, batch, seq, False),
                     lambda: _fox(fox_bounds, qb, kb, vb, cum, cum_cols, batch, seq, True))

    w_o_nsa_p = w_o_nsa.reshape(NSA_KV_GROUPS, NSA_Q_PER_GROUP, HEAD_DIM, D_MODEL).transpose(1, 0, 2, 3)
    out = _merge_ffn(x1, o_nsa, o_fox, mix_norm, w_gm, w_o_nsa_p.reshape(NSA_W, D_MODEL), w_o_fox,
                     w_out, ffn2_norm, ffn2_w_up, ffn2_w_down)
    return out.reshape(batch, seq, D_MODEL)


def kernel(x, ffn1_norm, ffn1_w_up, ffn1_w_down, mix_norm, w_in, b_forget, nsa_q_gain, nsa_k_gain,
           fox_q_gain, fox_k_gain, cmp_pos_k, cmp_pos_v, cmp_k_w1, cmp_k_w2, cmp_v_w1, cmp_v_w2,
           w_o_nsa, w_o_fox, w_out, ffn2_norm, ffn2_w_up, ffn2_w_down, rel_bias_table):
    for layer in range(ffn1_norm.shape[0]):
        x = _layer(x, ffn1_norm[layer], ffn1_w_up[layer], ffn1_w_down[layer], mix_norm[layer],
                   w_in[layer], b_forget[layer], nsa_q_gain[layer], nsa_k_gain[layer],
                   fox_q_gain[layer], fox_k_gain[layer], cmp_pos_k[layer], cmp_pos_v[layer],
                   cmp_k_w1[layer], cmp_k_w2[layer], cmp_v_w1[layer], cmp_v_w2[layer],
                   w_o_nsa[layer], w_o_fox[layer], w_out[layer], ffn2_norm[layer],
                   ffn2_w_up[layer], ffn2_w_down[layer], rel_bias_table)
    return x
```

```python
import functools
import math

import numpy as np
import jax
import jax.numpy as jnp
from jax import lax
from jax.experimental import pallas as pl
from jax.experimental.pallas import tpu as pltpu

F32 = jnp.float32
BF16 = jnp.bfloat16

D_MODEL = 1024
HEAD_DIM = 64
NSA_HEADS = 8
NSA_KV_GROUPS = 2
NSA_Q_PER_GROUP = NSA_HEADS // NSA_KV_GROUPS
CMP_BLOCK = 32
CMP_STRIDE = 16
CMP_HIDDEN = 128
SLC_BLOCK = 64
SLC_TOPK = 8
WINDOW = 512
FOX_HEADS = 8
D_FF = 2816
N_BUCKETS = 32
MAX_DISTANCE = 128
RMS_EPS = 1e-6
NEG_INF = -1.0e30
FORCE_BONUS = 1.0e4

NSA_W = NSA_HEADS * HEAD_DIM
NSA_KV_W = NSA_KV_GROUPS * HEAD_DIM
FOX_W = FOX_HEADS * HEAD_DIM
IN_SPLITS = (NSA_W, NSA_KV_W, NSA_KV_W, NSA_KV_W, NSA_KV_W, NSA_KV_W, NSA_KV_W, 3 * NSA_HEADS,
             FOX_W, FOX_W, FOX_W, FOX_HEADS, 2 * D_MODEL)

LANES = 128
TOKEN_TILE = 512
FFN_CHUNK = D_FF // 2
ATT_TILE = 256
N_WIN_TILES = WINDOW // ATT_TILE + 1
VMEM_LIMIT = 56 * 1024 * 1024
LOG2E = 1.4426950408889634
EXP2_RANGE = 100.0

PQ_A = 0
PQ_B = PQ_A + NSA_W
PK_B = PQ_B + FOX_W
PK_SW = PK_B + FOX_W
P_NORM_END = PK_SW + 2 * NSA_KV_W
PV_B = P_NORM_END
PV_SW = PV_B + FOX_W
PKV_C = PV_SW + 2 * NSA_KV_W
P_SMALL = PKV_C + 2 * NSA_KV_W
P_END = P_SMALL + LANES
NORM_CHUNK = 256


def _dot(a, b):
    return jnp.dot(a, b, preferred_element_type=F32)


def _dot_nt(a, b):
    return lax.dot_general(a, b, (((1,), (1,)), ((), ())), preferred_element_type=F32)


def _split_dot(x, w):
    hi = x.astype(BF16)
    lo = (x - hi.astype(F32)).astype(BF16)
    return _dot(hi, w) + _dot(lo, w)


def _rms_rows(x, gain_row):
    ms = jnp.mean(x * x, axis=-1, keepdims=True)
    return x * lax.rsqrt(ms + RMS_EPS) * gain_row


def _const_spec(shape):
    nd = len(shape)
    return pl.BlockSpec(shape, lambda *_: (0,) * nd, pipeline_mode=pl.Buffered(1))


def _params(n_axes):
    return pltpu.CompilerParams(dimension_semantics=("arbitrary",) * n_axes,
                                vmem_limit_bytes=VMEM_LIMIT)


def _swiglu_residual(x, gain_row, wup_ref, wdn_ref):
    xn = _rms_rows(x, gain_row).astype(BF16)
    acc = jnp.zeros(x.shape, F32)
    for c in range(D_FF // FFN_CHUNK):
        lo = c * FFN_CHUNK
        gate = _dot(xn, wup_ref[:, lo:lo + FFN_CHUNK])
        up = _dot(xn, wup_ref[:, D_FF + lo:D_FF + lo + FFN_CHUNK])
        h = (gate * jax.nn.sigmoid(gate) * up).astype(BF16)
        acc = acc + _dot(h, wdn_ref[lo:lo + FFN_CHUNK, :])
    return x + 0.5 * acc


def _ffn_kernel(x_ref, g_ref, wup_ref, wdn_ref, o_ref):
    o_ref[...] = _swiglu_residual(x_ref[...], g_ref[...], wup_ref, wdn_ref)


def _ffn(x2d, gain, w_up, w_down):
    n = x2d.shape[0]
    row = pl.BlockSpec((TOKEN_TILE, D_MODEL), lambda i: (i, 0))
    return pl.pallas_call(
        _ffn_kernel,
        grid=(n // TOKEN_TILE,),
        in_specs=[row, _const_spec((1, D_MODEL)), _const_spec((D_MODEL, 2 * D_FF)),
                  _const_spec((D_FF, D_MODEL))],
        out_specs=row,
        out_shape=jax.ShapeDtypeStruct((n, D_MODEL), F32),
        compiler_params=_params(1),
        name="ffn",
    )(x2d, gain.reshape(1, D_MODEL), w_up.astype(BF16), w_down.astype(BF16))


def _block_ones(width, size=NORM_CHUNK):
    idx = np.arange(size) // width
    return jnp.asarray((idx[:, None] == idx[None, :]).astype(np.float32), BF16)


def _pack_in_proj(w_in, b_forget, nsa_q_gain, nsa_k_gain, fox_q_gain, fox_k_gain):
    scale = LOG2E / math.sqrt(HEAD_DIM)
    pts = np.cumsum(np.array(IN_SPLITS))[:-1].tolist()
    qa, kc, vc, ks, vs, kw, vw, ga, qb, kb, vb, fb, gm = jnp.split(w_in, pts, axis=1)
    cols = []
    for r in range(NSA_Q_PER_GROUP):
        for g in range(NSA_KV_GROUPS):
            h = g * NSA_Q_PER_GROUP + r
            cols.append(qa[:, h * HEAD_DIM:(h + 1) * HEAD_DIM])
    cols += [qb, kb, ks, kw]
    gains = [jnp.tile(nsa_q_gain * scale, NSA_HEADS), jnp.tile(fox_q_gain * scale, FOX_HEADS),
             jnp.tile(fox_k_gain, FOX_HEADS), jnp.tile(nsa_k_gain[1], NSA_KV_GROUPS),
             jnp.tile(nsa_k_gain[2], NSA_KV_GROUPS)]
    n_small_pad = LANES - ga.shape[1] - fb.shape[1]
    cols += [vb, vs, vw, kc, vc, ga, fb, jnp.zeros((D_MODEL, n_small_pad), F32)]
    w_packed = jnp.concatenate(cols, axis=1).astype(BF16)
    gain_row = jnp.concatenate(gains).reshape(1, P_NORM_END)
    small_bias = jnp.concatenate([jnp.zeros((ga.shape[1],), F32), b_forget,
                                  jnp.zeros((n_small_pad,), F32)]).reshape(1, LANES)
    return w_packed, gain_row, small_bias, gm.astype(BF16)


def _in_proj_kernel(x_ref, g_ref, w_ref, gain_ref, sbias_ref, bd64_ref,
                    qa_ref, qb_ref, kb_ref, ksw_ref, vb_ref, vsw_ref, kc_ref, vc_ref, small_ref, kv_scr):
    u = _rms_rows(x_ref[...], g_ref[...]).astype(BF16)
    tm = u.shape[0]

    y = _dot(u, w_ref[:, 0:P_NORM_END])
    n_chunks = P_NORM_END // NORM_CHUNK
    squares = jnp.concatenate([jnp.square(y[:, c * NORM_CHUNK:(c + 1) * NORM_CHUNK]).astype(BF16)
                               for c in range(n_chunks)], axis=0)
    ss = _dot(squares, bd64_ref[...])
    chunk = 0
    for o_ref in (qa_ref, qb_ref, kb_ref, ksw_ref):
        for i in range(o_ref.shape[1] // NORM_CHUNK):
            cols = slice(chunk * NORM_CHUNK, (chunk + 1) * NORM_CHUNK)
            inv = lax.rsqrt(ss[chunk * tm:(chunk + 1) * tm] * (1.0 / HEAD_DIM) + RMS_EPS)
            o_ref[:, i * NORM_CHUNK:(i + 1) * NORM_CHUNK] = (y[:, cols] * inv * gain_ref[:, cols]).astype(BF16)
            chunk += 1
    low = lax.broadcasted_iota(jnp.int32, (u.shape[0], LANES), 1) < HEAD_DIM

    def store_with_ones(v, o_ref):
        for i in range(v.shape[1] // LANES):
            pair = v[:, i * LANES:(i + 1) * LANES]
            o_ref[:, 2 * i * LANES:(2 * i + 1) * LANES] = jnp.where(low, pair, 1.0).astype(BF16)
            o_ref[:, (2 * i + 1) * LANES:(2 * i + 2) * LANES] = jnp.where(low, 1.0, pair).astype(BF16)

    rest = _dot(u, w_ref[:, P_NORM_END:P_END])
    store_with_ones(rest[:, PV_B - P_NORM_END:PV_SW - P_NORM_END], vb_ref)
    store_with_ones(rest[:, PV_SW - P_NORM_END:PKV_C - P_NORM_END], vsw_ref)
    for j, o_ref in enumerate((kc_ref, vc_ref)):
        c0 = PKV_C - P_NORM_END + j * NSA_KV_W
        kv_scr[j] = rest[:, c0:c0 + NSA_KV_W]
        for l in range(CMP_STRIDE):
            o_ref[0, :, l * NSA_KV_W:(l + 1) * NSA_KV_W] = kv_scr[j, pl.ds(l, tm // CMP_STRIDE,
                                                                           stride=CMP_STRIDE), :]
    z = rest[:, P_SMALL - P_NORM_END:] + sbias_ref[...]
    lane = lax.broadcasted_iota(jnp.int32, z.shape, 1)
    log_sig = jnp.minimum(z, 0.0) - jnp.log1p(jnp.exp(-jnp.abs(z)))
    small_ref[...] = jnp.where(lane < 3 * NSA_HEADS, jax.nn.sigmoid(z), log_sig)


def _in_proj(x2d, mix_norm, w_packed, gain_row, small_bias, batch, seq):
    n = x2d.shape[0]
    steps_per_seq = seq // TOKEN_TILE

    def row(width):
        return pl.BlockSpec((TOKEN_TILE, width), lambda i: (i, 0))

    def out(width, dtype):
        return row(width), jax.ShapeDtypeStruct((n, width), dtype)

    grouped = (pl.BlockSpec((1, TOKEN_TILE // CMP_STRIDE, CMP_STRIDE * NSA_KV_W),
                            lambda i: (i // steps_per_seq, i % steps_per_seq, 0)),
               jax.ShapeDtypeStruct((batch, seq // CMP_STRIDE, CMP_STRIDE * NSA_KV_W), F32))
    outs = [out(NSA_W, BF16), out(FOX_W, BF16), out(FOX_W, BF16), out(2 * NSA_KV_W, BF16),
            out(2 * FOX_W, BF16), out(4 * NSA_KV_W, BF16), grouped, grouped, out(LANES, F32)]
    return pl.pallas_call(
        _in_proj_kernel,
        grid=(n // TOKEN_TILE,),
        in_specs=[row(D_MODEL), _const_spec((1, D_MODEL)), _const_spec((D_MODEL, P_END)),
                  _const_spec((1, P_NORM_END)), _const_spec((1, LANES)),
                  _const_spec((NORM_CHUNK, NORM_CHUNK))],
        out_specs=[spec for spec, _ in outs],
        out_shape=[shape for _, shape in outs],
        scratch_shapes=[pltpu.VMEM((2, TOKEN_TILE, NSA_KV_W), F32)],
        compiler_params=_params(1),
        name="in_proj",
    )(x2d, mix_norm.reshape(1, D_MODEL), w_packed, gain_row, small_bias, _block_ones(HEAD_DIM))


def _pack_compress(pos, w1, w2):
    half = CMP_BLOCK // 2
    eye = jnp.eye(NSA_KV_GROUPS, dtype=F32)
    w1r = w1.reshape(CMP_BLOCK, HEAD_DIM, CMP_HIDDEN)

    def big(w):
        return jnp.einsum('ldh,pg->lpdgh', w, eye).reshape(half * NSA_KV_W, NSA_KV_GROUPS * CMP_HIDDEN)

    def posrow(p):
        return jnp.broadcast_to(p[:, None, :], (half, NSA_KV_GROUPS, HEAD_DIM)).reshape(1, half * NSA_KV_W)

    w2big = jnp.einsum('hd,pg->phgd', w2, eye).reshape(NSA_KV_GROUPS * CMP_HIDDEN, NSA_KV_W)
    return (posrow(pos[:half]), posrow(pos[half:]), big(w1r[:half]).astype(BF16),
            big(w1r[half:]).astype(BF16), w2big.astype(BF16))


def _compress_kernel(kc_ref, vc_ref, kp_lo, kp_hi, kw_lo, kw_hi, kw2, vp_lo, vp_hi, vw_lo, vw_hi, vw2,
                     kgain_ref, bd64_ref, ko_ref, vo_ref):
    def mlp(r, p_lo, p_hi, w_lo, w_hi, w2):
        first = _dot((r + p_lo[...]).astype(BF16), w_lo[...])
        second = _dot((r + p_hi[...]).astype(BF16), w_hi[...])
        h = first + pltpu.roll(second, second.shape[0] - 1, axis=0)
        return _dot((h * jax.nn.sigmoid(h)).astype(BF16), w2[...])

    k = mlp(kc_ref[0], kp_lo, kp_hi, kw_lo, kw_hi, kw2)
    ss = _split_dot(k * k, bd64_ref[...])
    ko_ref[0] = (k * lax.rsqrt(ss * (1.0 / HEAD_DIM) + RMS_EPS) * kgain_ref[...]).astype(BF16)
    vo_ref[0] = mlp(vc_ref[0], vp_lo, vp_hi, vw_lo, vw_hi, vw2).astype(BF16)


def _compress(kc, vc, batch, seq, k_pack, v_pack, k_gain):
    rows = seq // (CMP_BLOCK // 2)
    width = (CMP_BLOCK // 2) * NSA_KV_W
    blk = pl.BlockSpec((1, rows, width), lambda b: (b, 0, 0))
    out = pl.BlockSpec((1, rows, NSA_KV_W), lambda b: (b, 0, 0))
    pack_specs = [_const_spec((1, width)), _const_spec((1, width)),
                  _const_spec((width, NSA_KV_GROUPS * CMP_HIDDEN)),
                  _const_spec((width, NSA_KV_GROUPS * CMP_HIDDEN)),
                  _const_spec((NSA_KV_GROUPS * CMP_HIDDEN, NSA_KV_W))]
    bd = _block_ones(HEAD_DIM, NSA_KV_W)
    return pl.pallas_call(
        _compress_kernel,
        grid=(batch,),
        in_specs=[blk, blk] + pack_specs + pack_specs + [_const_spec((1, NSA_KV_W)),
                                                         _const_spec((NSA_KV_W, NSA_KV_W))],
        out_specs=[out, out],
        out_shape=[jax.ShapeDtypeStruct((batch, rows, NSA_KV_W), BF16)] * 2,
        compiler_params=_params(1),
        name="compress",
    )(kc.reshape(batch, rows, width), vc.reshape(batch, rows, width), *k_pack, *v_pack,
      jnp.tile(k_gain, NSA_KV_GROUPS).reshape(1, NSA_KV_W), bd)


FORGET_LANE = 3 * NSA_HEADS


def _cumsum_kernel(x_ref, cols_ref, rows_ref):
    x = x_ref[...]
    row = lax.broadcasted_iota(jnp.int32, x.shape, 0)
    k = 1
    while k < x.shape[0]:
        x = x + jnp.where(row >= k, pltpu.roll(x, k, axis=0), 0.0)
        k *= 2
    cols_ref[...] = x
    rows_ref[0] = x.T[FORGET_LANE:FORGET_LANE + FOX_HEADS]


def _cumsum(small, batch, seq):
    spec = pl.BlockSpec((seq, LANES), lambda b: (b, 0))
    return pl.pallas_call(
        _cumsum_kernel, grid=(batch,), in_specs=[spec],
        out_specs=[spec, pl.BlockSpec((1, FOX_HEADS, seq), lambda b: (b, 0, 0))],
        out_shape=[jax.ShapeDtypeStruct((batch * seq, LANES), F32),
                   jax.ShapeDtypeStruct((batch, FOX_HEADS, seq), F32)],
        compiler_params=_params(1), name="cumsum",
    )(small)


def _write_bias_tile(d, valid, tab_ref, o_ref):
    max_exact = N_BUCKETS // 2
    n = jnp.maximum(d, 0)
    nf = jnp.maximum(n, 1).astype(F32)
    large = max_exact + (jnp.log(nf / max_exact) / math.log(MAX_DISTANCE / max_exact)
                         * (N_BUCKETS - max_exact)).astype(jnp.int32)
    bucket = jnp.where(n < max_exact, n, jnp.minimum(large, N_BUCKETS - 1))
    for h in range(NSA_HEADS):
        acc = jnp.zeros(d.shape, F32)
        for b in range(N_BUCKETS):
            acc = jnp.where(bucket == b, tab_ref[b, h], acc)
        g, r = divmod(h, NSA_Q_PER_GROUP)
        o_ref[0, g, r] = jnp.where(valid, acc * LOG2E, NEG_INF)


def _toeplitz_bias_kernel(tab_ref, o_ref):
    s = pl.program_id(0)
    i = lax.broadcasted_iota(jnp.int32, (ATT_TILE, ATT_TILE), 0)
    j = lax.broadcasted_iota(jnp.int32, (ATT_TILE, ATT_TILE), 1)
    d = jnp.minimum(s, N_WIN_TILES - 1) * ATT_TILE + i - j
    d_hi = jnp.where(s < N_WIN_TILES, WINDOW, jnp.int32(1 << 30))
    _write_bias_tile(d, (d >= 0) & (d < d_hi), tab_ref, o_ref)


def _cmp_bias_kernel(tab_ref, o_ref):
    t = pl.program_id(0)
    c = lax.broadcasted_iota(jnp.int32, (LANES, ATT_TILE), 0)
    i = lax.broadcasted_iota(jnp.int32, (LANES, ATT_TILE), 1)
    d = t * ATT_TILE + i - CMP_STRIDE * c - (CMP_BLOCK - 1)
    _write_bias_tile(d, (d >= 0) & (c < LANES - 1), tab_ref, o_ref)


def _bias_tables(rel_bias_table, n_q_tiles):
    def call(body, steps, rows, name):
        shape = (steps, NSA_KV_GROUPS, NSA_Q_PER_GROUP, rows, ATT_TILE)
        return pl.pallas_call(
            body, grid=(steps,),
            in_specs=[pl.BlockSpec(memory_space=pltpu.SMEM)],
            out_specs=pl.BlockSpec((1,) + shape[1:], lambda s: (s, 0, 0, 0, 0)),
            out_shape=jax.ShapeDtypeStruct(shape, F32),
            compiler_params=_params(1), name=name,
        )(rel_bias_table)

    return (call(_toeplitz_bias_kernel, N_WIN_TILES + 1, ATT_TILE, "toeplitz_bias"),
            call(_cmp_bias_kernel, n_q_tiles, LANES, "cmp_bias"))


def _score_bound(q_gain, k_gain):
    return 1.02 * LOG2E * math.sqrt(HEAD_DIM) * jnp.max(jnp.abs(q_gain)) * jnp.max(jnp.abs(k_gain))


def _attention(chains, groups, bounds, acc_ref):
    n = len(chains)

    def later(k):
        return tuple(i for _, ids in groups[k:] for i in ids)

    def exact_max():
        def half_max(i, dl):
            s = chains[i][0](dl)
            return jnp.maximum(s[:, :LANES], s[:, LANES:])

        mx = [half_max(i, 0) for i in range(n)]
        lo = 1
        for k, (last, _) in enumerate(groups):
            members = later(k)

            def body(dl, carry, members=members):
                return tuple(jnp.maximum(c, half_max(i, dl)) for c, i in zip(carry, members))

            for i, r in zip(members, lax.fori_loop(lo, last + 1, body, tuple(mx[i] for i in members))):
                mx[i] = r
            lo = last + 1
        return tuple(jnp.broadcast_to(jnp.max(m, axis=-1, keepdims=True), (m.shape[0], ATT_TILE))
                     for m in mx)

    shift = exact_max() if bounds is None else [b() for b in bounds]

    def weights(i, dl):
        return jnp.exp2(chains[i][0](dl) - shift[i]).astype(BF16)

    def product(i, dl):
        return chains[i][1](weights(i, dl), dl)

    for i in range(n):
        acc_ref[i] = product(i, 0)
    lo = 1
    for k, (last, _) in enumerate(groups):
        members = later(k)
        count = last + 1 - lo

        def body(j, carry, members=members, lo=lo):
            dl = lo + 2 * j
            for i in members:
                acc_ref[i] += product(i, dl) + product(i, dl + 1)
            return carry

        lax.fori_loop(0, count // 2, body, 0)

        @pl.when(count % 2 == 1)
        def _(members=members, last=last):
            for i in members:
                acc_ref[i] += product(i, last)

        lo = last + 1
    return [acc_ref[i] for i in range(n)]


def _pair_lanes(first, second, normalise):
    low = lax.broadcasted_iota(jnp.int32, first.shape, 1) < HEAD_DIM
    pair = jnp.where(low, first, second)
    if not normalise:
        return pair
    sums = pltpu.roll(jnp.where(low, second, first), HEAD_DIM, axis=1)
    return pair * (1.0 / sums)


def _select_blocks(imp_t, t0, tq):
    blk = lax.broadcasted_iota(jnp.int32, imp_t.shape, 0)
    cur = (t0 + (lax.broadcasted_iota(jnp.int32, imp_t.shape, 1) & (tq - 1))) // SLC_BLOCK
    forced = (blk == 0) | (blk == cur) | (blk == cur - 1)
    score = jnp.where(blk <= cur, imp_t + jnp.where(forced, FORCE_BONUS, 0.0), NEG_INF)
    blk_f = blk.astype(F32)
    dead = -3.0e38
    picked = jnp.zeros(imp_t.shape, F32)
    for _ in range(SLC_TOPK):
        best = jnp.max(score, axis=0, keepdims=True)
        first = jnp.min(jnp.where(score == best, blk_f, float(LANES)), axis=0, keepdims=True)
        hit = blk_f == first
        picked = jnp.where(hit, 1.0, picked)
        score = jnp.where(hit, dead, score)
    return picked


def _nsa_kernel(bnd_ref, qa_ref, kcmp_ref, vcmp_ref, ksw_ref, vsw_ref, small_ref, tb_ref, bc_ref, ov_ref,
                e_ref, gx_ref, o_ref, amask_ref, acc_ref, *, n_blocks, exact):
    qt = pl.program_id(1)
    tq = ATT_TILE
    rq = NSA_Q_PER_GROUP
    lane = lax.broadcasted_iota(jnp.int32, (tq, LANES), 1)
    own_lanes = [jnp.where((lane // HEAD_DIM) == g, 1.0, 0.0).astype(BF16) for g in range(NSA_KV_GROUPS)]

    def key_rows(dl):
        return pl.ds(pl.multiple_of((qt - dl) * tq, tq), tq)

    groups = range(NSA_KV_GROUPS)
    slc_ids = tuple(groups)
    win_ids = tuple(NSA_KV_GROUPS + g for g in groups)
    qs = [jnp.concatenate([qa_ref[:, r * LANES:(r + 1) * LANES] * own_lanes[g] for r in range(rq)], axis=0)
          for g in groups]

    def win_chain(g):
        def score(dl):
            s = _dot_nt(qs[g], ksw_ref[key_rows(dl), LANES:2 * LANES]).reshape(rq, tq, tq)
            return (s + tb_ref[dl, g]).reshape(rq * tq, tq)

        def pv(p, dl):
            c0 = (NSA_KV_GROUPS + g) * LANES
            return _dot(p, vsw_ref[key_rows(dl), c0:c0 + LANES])

        return score, pv

    bounds = None if exact else [lambda i=i: jnp.full((rq * tq, tq), bnd_ref[i // NSA_KV_GROUPS], F32)
                                 for i in range(2 * NSA_KV_GROUPS)]

    s = (_dot_nt(kcmp_ref[0], jnp.concatenate(qs, axis=0))
         + jnp.concatenate([bc_ref[0, g, r] for g in groups for r in range(rq)], axis=1))
    m = jnp.max(s, axis=0, keepdims=True)
    p = jnp.where(s > 0.5 * NEG_INF, jnp.exp2(s - m), 0.0)
    l = jnp.sum(p, axis=0, keepdims=True)
    p_c = p * (1.0 / jnp.where(l > 0.0, l, 1.0))
    o_cmp = _dot(p_c.T.astype(BF16), vcmp_ref[0])
    o_cmps = [o_cmp[g * rq * tq:(g + 1) * rq * tq] for g in groups]

    p_sum = jnp.concatenate([sum(p_c[:, (g * rq + r) * tq:(g * rq + r + 1) * tq] for r in range(rq))
                             for g in groups], axis=1)
    hi = p_sum.astype(BF16)
    lo = (p_sum - hi.astype(F32)).astype(BF16)
    imp_t = _dot(ov_ref[...], hi) + _dot(ov_ref[...], lo)
    sel_t = _select_blocks(imp_t[0:n_blocks], qt * tq, tq)
    block_bias = jnp.concatenate([jnp.where(sel_t > 0.0, 0.0, NEG_INF),
                                  jnp.zeros((LANES - n_blocks, NSA_KV_GROUPS * tq), F32)], axis=0)
    amask_ref[...] = _dot(block_bias.T.astype(BF16), e_ref[...]).reshape(amask_ref.shape)

    def slc_chain(g):
        def score(dl):
            tile = dl if isinstance(dl, int) else jnp.where(dl < N_WIN_TILES - 1, dl, N_WIN_TILES)
            rows = key_rows(dl)
            s = _dot_nt(qs[g], ksw_ref[rows, 0:LANES]).reshape(rq, tq, tq)
            return (s + tb_ref[tile, g] + amask_ref[g, :, rows][None]).reshape(rq * tq, tq)

        def pv(p, dl):
            return _dot(p, vsw_ref[key_rows(dl), g * LANES:(g + 1) * LANES])

        return score, pv

    accs = _attention([slc_chain(g) for g in groups] + [win_chain(g) for g in groups],
                      [(jnp.minimum(qt, N_WIN_TILES - 1), win_ids), (qt, slc_ids)], bounds, acc_ref)

    gates = small_ref[...]
    g_hi = gates.astype(BF16)
    g_lo = (gates - g_hi.astype(F32)).astype(BF16)
    out = [jnp.zeros((tq, LANES), F32) for _ in range(rq)]
    for b, per_group in enumerate((o_cmps, accs[:NSA_KV_GROUPS], accs[NSA_KV_GROUPS:])):
        gate = _dot(g_hi, gx_ref[b]) + _dot(g_lo, gx_ref[b])
        for r in range(rq):
            first, second = (a[r * tq:(r + 1) * tq] for a in per_group)
            pair = _pair_lanes(first, second, normalise=b > 0)
            out[r] = out[r] + gate[:, r * LANES:(r + 1) * LANES] * pair
    for r in range(rq):
        o_ref[:, r * LANES:(r + 1) * LANES] = out[r].astype(BF16)


def _nsa(bounds, qa, kcmp, vcmp, ksw, vsw, small, toeplitz, cmp_bias, batch, seq, exact):
    tq = ATT_TILE
    nq = seq // tq
    rq = NSA_Q_PER_GROUP
    n_blocks = seq // SLC_BLOCK
    n_cmp = (seq - CMP_BLOCK) // CMP_STRIDE + 1
    ci = np.arange(LANES)[:, None] * CMP_STRIDE
    sj = np.arange(LANES)[None, :] * SLC_BLOCK
    overlap = ((ci <= sj + SLC_BLOCK - 1) & (ci + CMP_BLOCK - 1 >= sj)
               & (np.arange(LANES)[:, None] < n_cmp) & (np.arange(LANES)[None, :] < n_blocks))
    expand = np.arange(LANES)[:, None] == (np.arange(seq)[None, :] // SLC_BLOCK)
    gate_expand = np.zeros((3, LANES, NSA_W), np.float32)
    for b in range(3):
        for g in range(NSA_KV_GROUPS):
            for r in range(rq):
                c0 = r * LANES + g * HEAD_DIM
                gate_expand[b, b * NSA_HEADS + g * rq + r, c0:c0 + HEAD_DIM] = 1.0
    return pl.pallas_call(
        functools.partial(_nsa_kernel, n_blocks=n_blocks, exact=exact),
        grid=(batch, nq),
        in_specs=[
            pl.BlockSpec(memory_space=pltpu.SMEM),
            pl.BlockSpec((tq, NSA_W), lambda b, t: (b * nq + t, 0)),
            pl.BlockSpec((1, LANES, NSA_KV_W), lambda b, t: (b, 0, 0)),
            pl.BlockSpec((1, LANES, NSA_KV_W), lambda b, t: (b, 0, 0)),
            pl.BlockSpec((seq, 2 * NSA_KV_W), lambda b, t: (b, 0)),
            pl.BlockSpec((seq, 2 * NSA_KV_GROUPS * LANES), lambda b, t: (b, 0)),
            pl.BlockSpec((tq, LANES), lambda b, t: (b * nq + t, 0)),
            _const_spec((N_WIN_TILES + 1, NSA_KV_GROUPS, rq, tq, tq)),
            pl.BlockSpec((1, NSA_KV_GROUPS, rq, LANES, tq), lambda b, t: (t, 0, 0, 0, 0)),
            _const_spec((LANES, LANES)),
            _const_spec((LANES, seq)),
            _const_spec((3, LANES, NSA_W)),
        ],
        out_specs=pl.BlockSpec((tq, NSA_W), lambda b, t: (b * nq + t, 0)),
        out_shape=jax.ShapeDtypeStruct((batch * seq, NSA_W), BF16),
        scratch_shapes=[pltpu.VMEM((NSA_KV_GROUPS, tq, seq), F32),
                        pltpu.VMEM((2 * NSA_KV_GROUPS, rq * tq, LANES), F32)],
        compiler_params=_params(2),
        name="nsa",
    )(bounds, qa, kcmp, vcmp, ksw, vsw, small, toeplitz, cmp_bias,
      jnp.asarray(overlap.T.astype(np.float32), BF16), jnp.asarray(expand.astype(np.float32), BF16),
      jnp.asarray(gate_expand, BF16))


def _fox_kernel(bnd_ref, q_ref, k_ref, v_ref, cum_ref, cum_t_ref, o_ref, acc_ref, *, exact):
    qt = pl.program_id(1)
    tq = ATT_TILE
    pairs = FOX_HEADS // 2
    row = lax.broadcasted_iota(jnp.int32, (2, tq, tq), 1)
    col = lax.broadcasted_iota(jnp.int32, (2, tq, tq), 2)
    lane = lax.broadcasted_iota(jnp.int32, (tq, LANES), 1)
    low = jnp.where(lane < HEAD_DIM, 1.0, 0.0).astype(BF16)
    high = jnp.where(lane < HEAD_DIM, 0.0, 1.0).astype(BF16)
    n_seq = q_ref.shape[0]

    def key_rows(dl):
        return pl.ds(pl.multiple_of((qt - dl) * tq, tq), tq)

    bounds = []

    def pair_chain(b, p):
        pair = q_ref[b, :, p * LANES:(p + 1) * LANES]
        q = jnp.concatenate([pair * low, pair * high], axis=0)
        base = cum_ref[b, 2 * p:2 * p + 2, pl.ds(pl.multiple_of(qt * tq, tq), LANES)][:, 0:1]

        def decay_of(rows):
            return (base - cum_ref[b, 2 * p:2 * p + 2, rows]) * LOG2E

        def bound():
            cum_rows = cum_t_ref[b]
            c0 = FORGET_LANE + 2 * p
            own = jnp.concatenate([base[h:h + 1, :] - cum_rows[:, c0 + h:c0 + h + 1] for h in range(2)],
                                  axis=0)
            return jnp.broadcast_to(own * LOG2E + bnd_ref[0], (2 * tq, tq))

        bounds.append(bound)

        def score(dl):
            rows = key_rows(dl)
            decay = decay_of(rows)
            s = _dot_nt(q, k_ref[b, rows, p * LANES:(p + 1) * LANES]).reshape(2, tq, tq) + decay[:, None, :]
            if isinstance(dl, int):
                s = jnp.where(col <= row, s, NEG_INF)
            return s.reshape(2 * tq, tq)

        def pv(w, dl):
            rows = key_rows(dl)
            return jnp.concatenate(
                [_dot(w[0:tq], v_ref[b, rows, 2 * p * LANES:(2 * p + 1) * LANES]),
                 _dot(w[tq:2 * tq], v_ref[b, rows, (2 * p + 1) * LANES:(2 * p + 2) * LANES])], axis=0)

        return score, pv

    chains = [pair_chain(b, p) for b in range(n_seq) for p in range(pairs)]
    accs = _attention(chains, [(qt, tuple(range(len(chains))))], None if exact else bounds, acc_ref)
    for i, acc in enumerate(accs):
        b, p = divmod(i, pairs)
        o_ref[b, :, p * LANES:(p + 1) * LANES] = _pair_lanes(acc[0:tq], acc[tq:2 * tq], True).astype(BF16)


def _fox(bounds, qb, kb, vb, cum, cum_cols, batch, seq, exact):
    tq = ATT_TILE
    nq = seq // tq
    per_step = 2 if batch % 2 == 0 else 1
    out = pl.pallas_call(
        functools.partial(_fox_kernel, exact=exact),
        grid=(batch // per_step, nq),
        in_specs=[
            pl.BlockSpec(memory_space=pltpu.SMEM),
            pl.BlockSpec((per_step, tq, FOX_W), lambda b, t: (b, t, 0)),
            pl.BlockSpec((per_step, seq, FOX_W), lambda b, t: (b, 0, 0)),
            pl.BlockSpec((per_step, seq, FOX_HEADS * LANES), lambda b, t: (b, 0, 0)),
            pl.BlockSpec((per_step, FOX_HEADS, seq), lambda b, t: (b, 0, 0)),
            pl.BlockSpec((per_step, tq, LANES), lambda b, t: (b, t, 0)),
        ],
        out_specs=pl.BlockSpec((per_step, tq, FOX_W), lambda b, t: (b, t, 0)),
        out_shape=jax.ShapeDtypeStruct((batch, seq, FOX_W), BF16),
        scratch_shapes=[pltpu.VMEM((per_step * FOX_HEADS // 2, 2 * tq, LANES), F32)],
        compiler_params=_params(2),
        name="fox",
    )(bounds, qb.reshape(batch, seq, FOX_W), kb.reshape(batch, seq, FOX_W),
      vb.reshape(batch, seq, FOX_HEADS * LANES), cum, cum_cols.reshape(batch, seq, LANES))
    return out.reshape(batch * seq, FOX_W)


def _merge_ffn_kernel(x_ref, on_ref, of_ref, gmix_ref, wgm_ref, won_ref, wof_ref, wout_ref,
                      g2_ref, wup_ref, wdn_ref, o_ref):
    x = x_ref[...]
    u = _rms_rows(x, gmix_ref[...]).astype(BF16)
    gate = jax.nn.sigmoid(_dot(u, wgm_ref[...]))
    merged = (gate[:, :D_MODEL] * _dot(on_ref[...], won_ref[...])
              + gate[:, D_MODEL:] * _dot(of_ref[...], wof_ref[...]))
    x2 = x + _dot(merged.astype(BF16), wout_ref[...])
    o_ref[...] = _swiglu_residual(x2, g2_ref[...], wup_ref, wdn_ref)


def _merge_ffn(x2d, o_nsa, o_fox, mix_norm, w_gm, w_o_nsa, w_o_fox, w_out, gain2, w_up, w_down):
    n = x2d.shape[0]
    tm = TOKEN_TILE // 2

    def row(width):
        return pl.BlockSpec((tm, width), lambda i: (i, 0))

    return pl.pallas_call(
        _merge_ffn_kernel,
        grid=(n // tm,),
        in_specs=[row(D_MODEL), row(NSA_W), row(FOX_W), _const_spec((1, D_MODEL)),
                  _const_spec((D_MODEL, 2 * D_MODEL)), _const_spec((NSA_W, D_MODEL)),
                  _const_spec((FOX_W, D_MODEL)), _const_spec((D_MODEL, D_MODEL)),
                  _const_spec((1, D_MODEL)), _const_spec((D_MODEL, 2 * D_FF)),
                  _const_spec((D_FF, D_MODEL))],
        out_specs=row(D_MODEL),
        out_shape=jax.ShapeDtypeStruct((n, D_MODEL), F32),
        compiler_params=_params(1),
        name="merge_ffn",
    )(x2d, o_nsa, o_fox, mix_norm.reshape(1, D_MODEL), w_gm, w_o_nsa.astype(BF16),
      w_o_fox.astype(BF16), w_out.astype(BF16), gain2.reshape(1, D_MODEL), w_up.astype(BF16),
      w_down.astype(BF16))


def _layer(x, ffn1_norm, ffn1_w_up, ffn1_w_down, mix_norm, w_in, b_forget, nsa_q_gain, nsa_k_gain,
           fox_q_gain, fox_k_gain, cmp_pos_k, cmp_pos_v, cmp_k_w1, cmp_k_w2, cmp_v_w1, cmp_v_w2,
           w_o_nsa, w_o_fox, w_out, ffn2_norm, ffn2_w_up, ffn2_w_down, rel_bias_table):
    batch, seq, d = x.shape
    assert d == D_MODEL and seq % ATT_TILE == 0 and (batch * seq) % TOKEN_TILE == 0
    assert seq // SLC_BLOCK <= LANES and (seq - CMP_BLOCK) // CMP_STRIDE + 1 == LANES - 1
    assert seq // (CMP_BLOCK // 2) == LANES
    x2d = x.reshape(batch * seq, D_MODEL)

    x1 = _ffn(x2d, ffn1_norm, ffn1_w_up, ffn1_w_down)

    w_packed, gain_row, small_bias, w_gm = _pack_in_proj(w_in, b_forget, nsa_q_gain, nsa_k_gain,
                                                         fox_q_gain, fox_k_gain)
    qa, qb, kb, ksw, vb, vsw, kc, vc, small = _in_proj(x1, mix_norm, w_packed, gain_row, small_bias,
                                                       batch, seq)

    kcmp, vcmp = _compress(kc, vc, batch, seq, _pack_compress(cmp_pos_k, cmp_k_w1, cmp_k_w2),
                           _pack_compress(cmp_pos_v, cmp_v_w1, cmp_v_w2), nsa_k_gain[0])
    toeplitz, cmp_bias = _bias_tables(rel_bias_table, seq // ATT_TILE)
    b_slc = _score_bound(nsa_q_gain, nsa_k_gain[1])
    b_win = _score_bound(nsa_q_gain, nsa_k_gain[2])
    t_hi, t_lo = jnp.max(rel_bias_table) * LOG2E, jnp.min(rel_bias_table) * LOG2E
    nsa_ok = 2.0 * jnp.maximum(b_slc, b_win) + (t_hi - t_lo) < EXP2_RANGE
    nsa_bounds = jnp.stack([b_slc + t_hi, b_win + t_hi])
    o_nsa = lax.cond(
        nsa_ok,
        lambda: _nsa(nsa_bounds, qa, kcmp, vcmp, ksw, vsw, small, toeplitz, cmp_bias, batch, seq, False),
        lambda: _nsa(nsa_bounds, qa, kcmp, vcmp, ksw, vsw, small, toeplitz, cmp_bias, batch, seq, True))

    cum_cols, cum = _cumsum(small, batch, seq)
    b_fox = _score_bound(fox_q_gain, fox_k_gain)
    fox_bounds = jnp.stack([b_fox])
    o_fox = lax.cond(2.0 * b_fox < EXP2_RANGE,
                     lambda: _fox(fox_bounds, qb, kb, vb, cum, cum_cols, batch, seq, False),
                     lambda: _fox(fox_bounds, qb, kb, vb, cum, cum_cols, batch, seq, True))

    w_o_nsa_p = w_o_nsa.reshape(NSA_KV_GROUPS, NSA_Q_PER_GROUP, HEAD_DIM, D_MODEL).transpose(1, 0, 2, 3)
    out = _merge_ffn(x1, o_nsa, o_fox, mix_norm, w_gm, w_o_nsa_p.reshape(NSA_W, D_MODEL), w_o_fox,
                     w_out, ffn2_norm, ffn2_w_up, ffn2_w_down)
    return out.reshape(batch, seq, D_MODEL)


def kernel(x, ffn1_norm, ffn1_w_up, ffn1_w_down, mix_norm, w_in, b_forget, nsa_q_gain, nsa_k_gain,
           fox_q_gain, fox_k_gain, cmp_pos_k, cmp_pos_v, cmp_k_w1, cmp_k_w2, cmp_v_w1, cmp_v_w2,
           w_o_nsa, w_o_fox, w_out, ffn2_norm, ffn2_w_up, ffn2_w_down, rel_bias_table):
    for layer in range(ffn1_norm.shape[0]):
        x = _layer(x, ffn1_norm[layer], ffn1_w_up[layer], ffn1_w_down[layer], mix_norm[layer],
                   w_in[layer], b_forget[layer], nsa_q_gain[layer], nsa_k_gain[layer],
                   fox_q_gain[layer], fox_k_gain[layer], cmp_pos_k[layer], cmp_pos_v[layer],
                   cmp_k_w1[layer], cmp_k_w2[layer], cmp_v_w1[layer], cmp_v_w2[layer],
                   w_o_nsa[layer], w_o_fox[layer], w_out[layer], ffn2_norm[layer],
                   ffn2_w_up[layer], ffn2_w_down[layer], rel_bias_table)
    return x
```

```python
import functools
import math

import numpy as np
import jax
import jax.numpy as jnp
from jax import lax
from jax.experimental import pallas as pl
from jax.experimental.pallas import tpu as pltpu

F32 = jnp.float32
BF16 = jnp.bfloat16

D_MODEL = 1024
HEAD_DIM = 64
NSA_HEADS = 8
NSA_KV_GROUPS = 2
NSA_Q_PER_GROUP = NSA_HEADS // NSA_KV_GROUPS
CMP_BLOCK = 32
CMP_STRIDE = 16
CMP_HIDDEN = 128
SLC_BLOCK = 64
SLC_TOPK = 8
WINDOW = 512
FOX_HEADS = 8
D_FF = 2816
N_BUCKETS = 32
MAX_DISTANCE = 128
RMS_EPS = 1e-6
NEG_INF = -1.0e30
FORCE_BONUS = 1.0e4

NSA_W = NSA_HEADS * HEAD_DIM
NSA_KV_W = NSA_KV_GROUPS * HEAD_DIM
FOX_W = FOX_HEADS * HEAD_DIM
IN_SPLITS = (NSA_W, NSA_KV_W, NSA_KV_W, NSA_KV_W, NSA_KV_W, NSA_KV_W, NSA_KV_W, 3 * NSA_HEADS,
             FOX_W, FOX_W, FOX_W, FOX_HEADS, 2 * D_MODEL)

LANES = 128
TOKEN_TILE = 512
FFN_CHUNK = D_FF // 2
ATT_TILE = 256
N_WIN_TILES = WINDOW // ATT_TILE + 1
VMEM_LIMIT = 56 * 1024 * 1024
LOG2E = 1.4426950408889634
EXP2_RANGE = 100.0

PQ_A = 0
PQ_B = PQ_A + NSA_W
PK_B = PQ_B + FOX_W
PK_SW = PK_B + FOX_W
P_NORM_END = PK_SW + 2 * NSA_KV_W
PV_B = P_NORM_END
PV_SW = PV_B + FOX_W
PKV_C = PV_SW + 2 * NSA_KV_W
P_SMALL = PKV_C + 2 * NSA_KV_W
P_END = P_SMALL + LANES
NORM_CHUNK = 256


def _dot(a, b):
    return jnp.dot(a, b, preferred_element_type=F32)


def _dot_nt(a, b):
    return lax.dot_general(a, b, (((1,), (1,)), ((), ())), preferred_element_type=F32)


def _split_dot(x, w):
    hi = x.astype(BF16)
    lo = (x - hi.astype(F32)).astype(BF16)
    return _dot(hi, w) + _dot(lo, w)


def _rms_rows(x, gain_row):
    ms = jnp.mean(x * x, axis=-1, keepdims=True)
    return x * lax.rsqrt(ms + RMS_EPS) * gain_row


def _const_spec(shape):
    nd = len(shape)
    return pl.BlockSpec(shape, lambda *_: (0,) * nd, pipeline_mode=pl.Buffered(1))


def _params(n_axes):
    return pltpu.CompilerParams(dimension_semantics=("arbitrary",) * n_axes,
                                vmem_limit_bytes=VMEM_LIMIT)


def _swiglu_residual(x, gain_row, wup_ref, wdn_ref):
    xn = _rms_rows(x, gain_row).astype(BF16)
    acc = jnp.zeros(x.shape, F32)
    for c in range(D_FF // FFN_CHUNK):
        lo = c * FFN_CHUNK
        gate = _dot(xn, wup_ref[:, lo:lo + FFN_CHUNK])
        up = _dot(xn, wup_ref[:, D_FF + lo:D_FF + lo + FFN_CHUNK])
        h = (gate * jax.nn.sigmoid(gate) * up).astype(BF16)
        acc = acc + _dot(h, wdn_ref[lo:lo + FFN_CHUNK, :])
    return x + 0.5 * acc


def _ffn_kernel(x_ref, g_ref, wup_ref, wdn_ref, o_ref):
    o_ref[...] = _swiglu_residual(x_ref[...], g_ref[...], wup_ref, wdn_ref)


def _ffn(x2d, gain, w_up, w_down):
    n = x2d.shape[0]
    row = pl.BlockSpec((TOKEN_TILE, D_MODEL), lambda i: (i, 0))
    return pl.pallas_call(
        _ffn_kernel,
        grid=(n // TOKEN_TILE,),
        in_specs=[row, _const_spec((1, D_MODEL)), _const_spec((D_MODEL, 2 * D_FF)),
                  _const_spec((D_FF, D_MODEL))],
        out_specs=row,
        out_shape=jax.ShapeDtypeStruct((n, D_MODEL), F32),
        compiler_params=_params(1),
        name="ffn",
    )(x2d, gain.reshape(1, D_MODEL), w_up.astype(BF16), w_down.astype(BF16))


def _block_ones(width, size=NORM_CHUNK):
    idx = np.arange(size) // width
    return jnp.asarray((idx[:, None] == idx[None, :]).astype(np.float32), BF16)


def _pack_in_proj(w_in, b_forget, nsa_q_gain, nsa_k_gain, fox_q_gain, fox_k_gain):
    scale = LOG2E / math.sqrt(HEAD_DIM)
    pts = np.cumsum(np.array(IN_SPLITS))[:-1].tolist()
    qa, kc, vc, ks, vs, kw, vw, ga, qb, kb, vb, fb, gm = jnp.split(w_in, pts, axis=1)
    cols = []
    for r in range(NSA_Q_PER_GROUP):
        for g in range(NSA_KV_GROUPS):
            h = g * NSA_Q_PER_GROUP + r
            cols.append(qa[:, h * HEAD_DIM:(h + 1) * HEAD_DIM])
    cols += [qb, kb, ks, kw]
    gains = [jnp.tile(nsa_q_gain * scale, NSA_HEADS), jnp.tile(fox_q_gain * scale, FOX_HEADS),
             jnp.tile(fox_k_gain, FOX_HEADS), jnp.tile(nsa_k_gain[1], NSA_KV_GROUPS),
             jnp.tile(nsa_k_gain[2], NSA_KV_GROUPS)]
    n_small_pad = LANES - ga.shape[1] - fb.shape[1]
    cols += [vb, vs, vw, kc, vc, ga, fb, jnp.zeros((D_MODEL, n_small_pad), F32)]
    w_packed = jnp.concatenate(cols, axis=1).astype(BF16)
    gain_row = jnp.concatenate(gains).reshape(1, P_NORM_END)
    small_bias = jnp.concatenate([jnp.zeros((ga.shape[1],), F32), b_forget,
                                  jnp.zeros((n_small_pad,), F32)]).reshape(1, LANES)
    return w_packed, gain_row, small_bias, gm.astype(BF16)


def _in_proj_kernel(x_ref, g_ref, w_ref, gain_ref, sbias_ref, bd64_ref,
                    qa_ref, qb_ref, kb_ref, ksw_ref, vb_ref, vsw_ref, kc_ref, vc_ref, small_ref, kv_scr):
    u = _rms_rows(x_ref[...], g_ref[...]).astype(BF16)
    tm = u.shape[0]

    y = _dot(u, w_ref[:, 0:P_NORM_END])
    n_chunks = P_NORM_END // NORM_CHUNK
    squares = jnp.concatenate([jnp.square(y[:, c * NORM_CHUNK:(c + 1) * NORM_CHUNK]).astype(BF16)
                               for c in range(n_chunks)], axis=0)
    ss = _dot(squares, bd64_ref[...])
    chunk = 0
    for o_ref in (qa_ref, qb_ref, kb_ref, ksw_ref):
        for i in range(o_ref.shape[1] // NORM_CHUNK):
            cols = slice(chunk * NORM_CHUNK, (chunk + 1) * NORM_CHUNK)
            inv = lax.rsqrt(ss[chunk * tm:(chunk + 1) * tm] * (1.0 / HEAD_DIM) + RMS_EPS)
            o_ref[:, i * NORM_CHUNK:(i + 1) * NORM_CHUNK] = (y[:, cols] * inv * gain_ref[:, cols]).astype(BF16)
            chunk += 1
    low = lax.broadcasted_iota(jnp.int32, (u.shape[0], LANES), 1) < HEAD_DIM

    def store_with_ones(v, o_ref):
        for i in range(v.shape[1] // LANES):
            pair = v[:, i * LANES:(i + 1) * LANES]
            o_ref[:, 2 * i * LANES:(2 * i + 1) * LANES] = jnp.where(low, pair, 1.0).astype(BF16)
            o_ref[:, (2 * i + 1) * LANES:(2 * i + 2) * LANES] = jnp.where(low, 1.0, pair).astype(BF16)

    rest = _dot(u, w_ref[:, P_NORM_END:P_END])
    store_with_ones(rest[:, PV_B - P_NORM_END:PV_SW - P_NORM_END], vb_ref)
    store_with_ones(rest[:, PV_SW - P_NORM_END:PKV_C - P_NORM_END], vsw_ref)
    for j, o_ref in enumerate((kc_ref, vc_ref)):
        c0 = PKV_C - P_NORM_END + j * NSA_KV_W
        kv_scr[j] = rest[:, c0:c0 + NSA_KV_W]
        for l in range(CMP_STRIDE):
            o_ref[0, :, l * NSA_KV_W:(l + 1) * NSA_KV_W] = kv_scr[j, pl.ds(l, tm // CMP_STRIDE,
                                                                           stride=CMP_STRIDE), :]
    z = rest[:, P_SMALL - P_NORM_END:] + sbias_ref[...]
    lane = lax.broadcasted_iota(jnp.int32, z.shape, 1)
    log_sig = jnp.minimum(z, 0.0) - jnp.log1p(jnp.exp(-jnp.abs(z)))
    small_ref[...] = jnp.where(lane < 3 * NSA_HEADS, jax.nn.sigmoid(z), log_sig)


def _in_proj(x2d, mix_norm, w_packed, gain_row, small_bias, batch, seq):
    n = x2d.shape[0]
    steps_per_seq = seq // TOKEN_TILE

    def row(width):
        return pl.BlockSpec((TOKEN_TILE, width), lambda i: (i, 0))

    def out(width, dtype):
        return row(width), jax.ShapeDtypeStruct((n, width), dtype)

    grouped = (pl.BlockSpec((1, TOKEN_TILE // CMP_STRIDE, CMP_STRIDE * NSA_KV_W),
                            lambda i: (i // steps_per_seq, i % steps_per_seq, 0)),
               jax.ShapeDtypeStruct((batch, seq // CMP_STRIDE, CMP_STRIDE * NSA_KV_W), F32))
    outs = [out(NSA_W, BF16), out(FOX_W, BF16), out(FOX_W, BF16), out(2 * NSA_KV_W, BF16),
            out(2 * FOX_W, BF16), out(4 * NSA_KV_W, BF16), grouped, grouped, out(LANES, F32)]
    return pl.pallas_call(
        _in_proj_kernel,
        grid=(n // TOKEN_TILE,),
        in_specs=[row(D_MODEL), _const_spec((1, D_MODEL)), _const_spec((D_MODEL, P_END)),
                  _const_spec((1, P_NORM_END)), _const_spec((1, LANES)),
                  _const_spec((NORM_CHUNK, NORM_CHUNK))],
        out_specs=[spec for spec, _ in outs],
        out_shape=[shape for _, shape in outs],
        scratch_shapes=[pltpu.VMEM((2, TOKEN_TILE, NSA_KV_W), F32)],
        compiler_params=_params(1),
        name="in_proj",
    )(x2d, mix_norm.reshape(1, D_MODEL), w_packed, gain_row, small_bias, _block_ones(HEAD_DIM))


def _pack_compress(pos, w1, w2):
    half = CMP_BLOCK // 2
    eye = jnp.eye(NSA_KV_GROUPS, dtype=F32)
    w1r = w1.reshape(CMP_BLOCK, HEAD_DIM, CMP_HIDDEN)

    def big(w):
        return jnp.einsum('ldh,pg->lpdgh', w, eye).reshape(half * NSA_KV_W, NSA_KV_GROUPS * CMP_HIDDEN)

    def posrow(p):
        return jnp.broadcast_to(p[:, None, :], (half, NSA_KV_GROUPS, HEAD_DIM)).reshape(1, half * NSA_KV_W)

    w2big = jnp.einsum('hd,pg->phgd', w2, eye).reshape(NSA_KV_GROUPS * CMP_HIDDEN, NSA_KV_W)
    return (posrow(pos[:half]), posrow(pos[half:]), big(w1r[:half]).astype(BF16),
            big(w1r[half:]).astype(BF16), w2big.astype(BF16))


def _compress_kernel(kc_ref, vc_ref, kp_lo, kp_hi, kw_lo, kw_hi, kw2, vp_lo, vp_hi, vw_lo, vw_hi, vw2,
                     kgain_ref, bd64_ref, ko_ref, vo_ref):
    def mlp(r, p_lo, p_hi, w_lo, w_hi, w2):
        first = _dot((r + p_lo[...]).astype(BF16), w_lo[...])
        second = _dot((r + p_hi[...]).astype(BF16), w_hi[...])
        h = first + pltpu.roll(second, second.shape[0] - 1, axis=0)
        return _dot((h * jax.nn.sigmoid(h)).astype(BF16), w2[...])

    k = mlp(kc_ref[0], kp_lo, kp_hi, kw_lo, kw_hi, kw2)
    ss = _split_dot(k * k, bd64_ref[...])
    ko_ref[0] = (k * lax.rsqrt(ss * (1.0 / HEAD_DIM) + RMS_EPS) * kgain_ref[...]).astype(BF16)
    vo_ref[0] = mlp(vc_ref[0], vp_lo, vp_hi, vw_lo, vw_hi, vw2).astype(BF16)


def _compress(kc, vc, batch, seq, k_pack, v_pack, k_gain):
    rows = seq // (CMP_BLOCK // 2)
    width = (CMP_BLOCK // 2) * NSA_KV_W
    blk = pl.BlockSpec((1, rows, width), lambda b: (b, 0, 0))
    out = pl.BlockSpec((1, rows, NSA_KV_W), lambda b: (b, 0, 0))
    pack_specs = [_const_spec((1, width)), _const_spec((1, width)),
                  _const_spec((width, NSA_KV_GROUPS * CMP_HIDDEN)),
                  _const_spec((width, NSA_KV_GROUPS * CMP_HIDDEN)),
                  _const_spec((NSA_KV_GROUPS * CMP_HIDDEN, NSA_KV_W))]
    bd = _block_ones(HEAD_DIM, NSA_KV_W)
    return pl.pallas_call(
        _compress_kernel,
        grid=(batch,),
        in_specs=[blk, blk] + pack_specs + pack_specs + [_const_spec((1, NSA_KV_W)),
                                                         _const_spec((NSA_KV_W, NSA_KV_W))],
        out_specs=[out, out],
        out_shape=[jax.ShapeDtypeStruct((batch, rows, NSA_KV_W), BF16)] * 2,
        compiler_params=_params(1),
        name="compress",
    )(kc.reshape(batch, rows, width), vc.reshape(batch, rows, width), *k_pack, *v_pack,
      jnp.tile(k_gain, NSA_KV_GROUPS).reshape(1, NSA_KV_W), bd)


FORGET_LANE = 3 * NSA_HEADS


def _cumsum_kernel(x_ref, cols_ref, rows_ref):
    x = x_ref[...]
    row = lax.broadcasted_iota(jnp.int32, x.shape, 0)
    k = 1
    while k < x.shape[0]:
        x = x + jnp.where(row >= k, pltpu.roll(x, k, axis=0), 0.0)
        k *= 2
    cols_ref[...] = x
    rows_ref[0] = x.T[FORGET_LANE:FORGET_LANE + FOX_HEADS]


def _cumsum(small, batch, seq):
    spec = pl.BlockSpec((seq, LANES), lambda b: (b, 0))
    return pl.pallas_call(
        _cumsum_kernel, grid=(batch,), in_specs=[spec],
        out_specs=[spec, pl.BlockSpec((1, FOX_HEADS, seq), lambda b: (b, 0, 0))],
        out_shape=[jax.ShapeDtypeStruct((batch * seq, LANES), F32),
                   jax.ShapeDtypeStruct((batch, FOX_HEADS, seq), F32)],
        compiler_params=_params(1), name="cumsum",
    )(small)


def _write_bias_tile(d, valid, tab_ref, o_ref):
    max_exact = N_BUCKETS // 2
    n = jnp.maximum(d, 0)
    nf = jnp.maximum(n, 1).astype(F32)
    large = max_exact + (jnp.log(nf / max_exact) / math.log(MAX_DISTANCE / max_exact)
                         * (N_BUCKETS - max_exact)).astype(jnp.int32)
    bucket = jnp.where(n < max_exact, n, jnp.minimum(large, N_BUCKETS - 1))
    for h in range(NSA_HEADS):
        acc = jnp.zeros(d.shape, F32)
        for b in range(N_BUCKETS):
            acc = jnp.where(bucket == b, tab_ref[b, h], acc)
        g, r = divmod(h, NSA_Q_PER_GROUP)
        o_ref[0, g, r] = jnp.where(valid, acc * LOG2E, NEG_INF)


def _toeplitz_bias_kernel(tab_ref, o_ref):
    s = pl.program_id(0)
    i = lax.broadcasted_iota(jnp.int32, (ATT_TILE, ATT_TILE), 0)
    j = lax.broadcasted_iota(jnp.int32, (ATT_TILE, ATT_TILE), 1)
    d = jnp.minimum(s, N_WIN_TILES - 1) * ATT_TILE + i - j
    d_hi = jnp.where(s < N_WIN_TILES, WINDOW, jnp.int32(1 << 30))
    _write_bias_tile(d, (d >= 0) & (d < d_hi), tab_ref, o_ref)


def _cmp_bias_kernel(tab_ref, o_ref):
    t = pl.program_id(0)
    c = lax.broadcasted_iota(jnp.int32, (LANES, ATT_TILE), 0)
    i = lax.broadcasted_iota(jnp.int32, (LANES, ATT_TILE), 1)
    d = t * ATT_TILE + i - CMP_STRIDE * c - (CMP_BLOCK - 1)
    _write_bias_tile(d, (d >= 0) & (c < LANES - 1), tab_ref, o_ref)


def _bias_tables(rel_bias_table, n_q_tiles):
    def call(body, steps, rows, name):
        shape = (steps, NSA_KV_GROUPS, NSA_Q_PER_GROUP, rows, ATT_TILE)
        return pl.pallas_call(
            body, grid=(steps,),
            in_specs=[pl.BlockSpec(memory_space=pltpu.SMEM)],
            out_specs=pl.BlockSpec((1,) + shape[1:], lambda s: (s, 0, 0, 0, 0)),
            out_shape=jax.ShapeDtypeStruct(shape, F32),
            compiler_params=_params(1), name=name,
        )(rel_bias_table)

    return (call(_toeplitz_bias_kernel, N_WIN_TILES + 1, ATT_TILE, "toeplitz_bias"),
            call(_cmp_bias_kernel, n_q_tiles, LANES, "cmp_bias"))


def _score_bound(q_gain, k_gain):
    return 1.02 * LOG2E * math.sqrt(HEAD_DIM) * jnp.max(jnp.abs(q_gain)) * jnp.max(jnp.abs(k_gain))


def _attention(chains, groups, bounds, acc_ref):
    n = len(chains)

    def later(k):
        return tuple(i for _, ids in groups[k:] for i in ids)

    def exact_max():
        def half_max(i, dl):
            s = chains[i][0](dl)
            return jnp.maximum(s[:, :LANES], s[:, LANES:])

        mx = [half_max(i, 0) for i in range(n)]
        lo = 1
        for k, (last, _) in enumerate(groups):
            members = later(k)

            def body(dl, carry, members=members):
                return tuple(jnp.maximum(c, half_max(i, dl)) for c, i in zip(carry, members))

            for i, r in zip(members, lax.fori_loop(lo, last + 1, body, tuple(mx[i] for i in members))):
                mx[i] = r
            lo = last + 1
        return tuple(jnp.broadcast_to(jnp.max(m, axis=-1, keepdims=True), (m.shape[0], ATT_TILE))
                     for m in mx)

    shift = exact_max() if bounds is None else [b() for b in bounds]

    def weights(i, dl):
        return jnp.exp2(chains[i][0](dl) - shift[i]).astype(BF16)

    def product(i, dl):
        return chains[i][1](weights(i, dl), dl)

    for i in range(n):
        acc_ref[i] = product(i, 0)
    lo = 1
    for k, (last, _) in enumerate(groups):
        members = later(k)
        count = last + 1 - lo

        def body(j, carry, members=members, lo=lo):
            dl = lo + 2 * j
            for i in members:
                acc_ref[i] += product(i, dl) + product(i, dl + 1)
            return carry

        lax.fori_loop(0, count // 2, body, 0)

        @pl.when(count % 2 == 1)
        def _(members=members, last=last):
            for i in members:
                acc_ref[i] += product(i, last)

        lo = last + 1
    return [acc_ref[i] for i in range(n)]


def _pair_lanes(first, second, normalise):
    low = lax.broadcasted_iota(jnp.int32, first.shape, 1) < HEAD_DIM
    pair = jnp.where(low, first, second)
    if not normalise:
        return pair
    sums = pltpu.roll(jnp.where(low, second, first), HEAD_DIM, axis=1)
    return pair * (1.0 / sums)


def _select_blocks(imp_t, t0, tq):
    blk = lax.broadcasted_iota(jnp.int32, imp_t.shape, 0)
    cur = (t0 + (lax.broadcasted_iota(jnp.int32, imp_t.shape, 1) & (tq - 1))) // SLC_BLOCK
    forced = (blk == 0) | (blk == cur) | (blk == cur - 1)
    score = jnp.where(blk <= cur, imp_t + jnp.where(forced, FORCE_BONUS, 0.0), NEG_INF)
    blk_f = blk.astype(F32)
    dead = -3.0e38
    picked = jnp.zeros(imp_t.shape, F32)
    for _ in range(SLC_TOPK):
        best = jnp.max(score, axis=0, keepdims=True)
        first = jnp.min(jnp.where(score == best, blk_f, float(LANES)), axis=0, keepdims=True)
        hit = blk_f == first
        picked = jnp.where(hit, 1.0, picked)
        score = jnp.where(hit, dead, score)
    return picked


def _nsa_kernel(bnd_ref, qa_ref, kcmp_ref, vcmp_ref, ksw_ref, vsw_ref, small_ref, tb_ref, bc_ref, ov_ref,
                e_ref, gx_ref, o_ref, amask_ref, acc_ref, *, n_blocks, exact):
    qt = pl.program_id(1)
    tq = ATT_TILE
    rq = NSA_Q_PER_GROUP
    lane = lax.broadcasted_iota(jnp.int32, (tq, LANES), 1)
    own_lanes = [jnp.where((lane // HEAD_DIM) == g, 1.0, 0.0).astype(BF16) for g in range(NSA_KV_GROUPS)]

    def key_rows(dl):
        return pl.ds(pl.multiple_of((qt - dl) * tq, tq), tq)

    groups = range(NSA_KV_GROUPS)
    n_seq = qa_ref.shape[0]
    n_grp = NSA_KV_GROUPS

    def win_chain(b, g, q):
        def score(dl):
            s = _dot_nt(q, ksw_ref[b, key_rows(dl), LANES:2 * LANES]).reshape(rq, tq, tq)
            return (s + tb_ref[dl, g]).reshape(rq * tq, tq)

        def pv(p, dl):
            c0 = (n_grp + g) * LANES
            return _dot(p, vsw_ref[b, key_rows(dl), c0:c0 + LANES])

        return score, pv

    def slc_chain(b, g, q):
        def score(dl):
            tile = dl if isinstance(dl, int) else jnp.where(dl < N_WIN_TILES - 1, dl, N_WIN_TILES)
            rows = key_rows(dl)
            s = _dot_nt(q, ksw_ref[b, rows, 0:LANES]).reshape(rq, tq, tq)
            return (s + tb_ref[tile, g] + amask_ref[b * n_grp + g, :, rows][None]).reshape(rq * tq, tq)

        def pv(p, dl):
            return _dot(p, vsw_ref[b, key_rows(dl), g * LANES:(g + 1) * LANES])

        return score, pv

    slc_chains, win_chains, o_cmps = [], [], []
    for b in range(n_seq):
        qs = [jnp.concatenate([qa_ref[b, :, r * LANES:(r + 1) * LANES] * own_lanes[g] for r in range(rq)],
                              axis=0) for g in groups]
        slc_chains += [slc_chain(b, g, qs[g]) for g in groups]
        win_chains += [win_chain(b, g, qs[g]) for g in groups]

        s = (_dot_nt(kcmp_ref[b], jnp.concatenate(qs, axis=0))
             + jnp.concatenate([bc_ref[0, g, r] for g in groups for r in range(rq)], axis=1))
        m = jnp.max(s, axis=0, keepdims=True)
        p = jnp.where(s > 0.5 * NEG_INF, jnp.exp2(s - m), 0.0)
        l = jnp.sum(p, axis=0, keepdims=True)
        p_c = p * (1.0 / jnp.where(l > 0.0, l, 1.0))
        o_cmp = _dot(p_c.T.astype(BF16), vcmp_ref[b])
        o_cmps.append([o_cmp[g * rq * tq:(g + 1) * rq * tq] for g in groups])

        p_sum = jnp.concatenate([sum(p_c[:, (g * rq + r) * tq:(g * rq + r + 1) * tq] for r in range(rq))
                                 for g in groups], axis=1)
        hi = p_sum.astype(BF16)
        lo = (p_sum - hi.astype(F32)).astype(BF16)
        imp_t = _dot(ov_ref[...], hi) + _dot(ov_ref[...], lo)
        sel_t = _select_blocks(imp_t[0:n_blocks], qt * tq, tq)
        block_bias = jnp.concatenate([jnp.where(sel_t > 0.0, 0.0, NEG_INF),
                                      jnp.zeros((LANES - n_blocks, n_grp * tq), F32)], axis=0)
        amask_ref[b * n_grp:(b + 1) * n_grp] = _dot(block_bias.T.astype(BF16), e_ref[...]).reshape(
            n_grp, tq, e_ref.shape[1])

    n_slc = len(slc_chains)
    slc_ids = tuple(range(n_slc))
    win_ids = tuple(range(n_slc, 2 * n_slc))
    bounds = None if exact else [lambda i=i: jnp.full((rq * tq, tq), bnd_ref[i // n_slc], F32)
                                 for i in range(2 * n_slc)]
    accs = _attention(slc_chains + win_chains,
                      [(jnp.minimum(qt, N_WIN_TILES - 1), win_ids), (qt, slc_ids)], bounds, acc_ref)

    for b in range(n_seq):
        gates = small_ref[b]
        g_hi = gates.astype(BF16)
        g_lo = (gates - g_hi.astype(F32)).astype(BF16)
        out = [jnp.zeros((tq, LANES), F32) for _ in range(rq)]
        lo_, hi_ = b * n_grp, (b + 1) * n_grp
        for br, per_group in enumerate((o_cmps[b], accs[lo_:hi_], accs[n_slc + lo_:n_slc + hi_])):
            gate = _dot(g_hi, gx_ref[br]) + _dot(g_lo, gx_ref[br])
            for r in range(rq):
                first, second = (a[r * tq:(r + 1) * tq] for a in per_group)
                pair = _pair_lanes(first, second, normalise=br > 0)
                out[r] = out[r] + gate[:, r * LANES:(r + 1) * LANES] * pair
        for r in range(rq):
            o_ref[b, :, r * LANES:(r + 1) * LANES] = out[r].astype(BF16)


def _nsa(bounds, qa, kcmp, vcmp, ksw, vsw, small, toeplitz, cmp_bias, batch, seq, exact):
    tq = ATT_TILE
    nq = seq // tq
    rq = NSA_Q_PER_GROUP
    n_blocks = seq // SLC_BLOCK
    n_cmp = (seq - CMP_BLOCK) // CMP_STRIDE + 1
    ci = np.arange(LANES)[:, None] * CMP_STRIDE
    sj = np.arange(LANES)[None, :] * SLC_BLOCK
    overlap = ((ci <= sj + SLC_BLOCK - 1) & (ci + CMP_BLOCK - 1 >= sj)
               & (np.arange(LANES)[:, None] < n_cmp) & (np.arange(LANES)[None, :] < n_blocks))
    expand = np.arange(LANES)[:, None] == (np.arange(seq)[None, :] // SLC_BLOCK)
    gate_expand = np.zeros((3, LANES, NSA_W), np.float32)
    for b in range(3):
        for g in range(NSA_KV_GROUPS):
            for r in range(rq):
                c0 = r * LANES + g * HEAD_DIM
                gate_expand[b, b * NSA_HEADS + g * rq + r, c0:c0 + HEAD_DIM] = 1.0
    per_step = 2 if batch % 2 == 0 else 1

    def rows(n_rows, width):
        return pl.BlockSpec((per_step, n_rows, width), lambda b, t: (b, 0, 0))

    def tile(width):
        return pl.BlockSpec((per_step, tq, width), lambda b, t: (b, t, 0))

    out = pl.pallas_call(
        functools.partial(_nsa_kernel, n_blocks=n_blocks, exact=exact),
        grid=(batch // per_step, nq),
        in_specs=[
            pl.BlockSpec(memory_space=pltpu.SMEM),
            tile(NSA_W),
            rows(LANES, NSA_KV_W),
            rows(LANES, NSA_KV_W),
            rows(seq, 2 * NSA_KV_W),
            rows(seq, 2 * NSA_KV_GROUPS * LANES),
            tile(LANES),
            _const_spec((N_WIN_TILES + 1, NSA_KV_GROUPS, rq, tq, tq)),
            pl.BlockSpec((1, NSA_KV_GROUPS, rq, LANES, tq), lambda b, t: (t, 0, 0, 0, 0)),
            _const_spec((LANES, LANES)),
            _const_spec((LANES, seq)),
            _const_spec((3, LANES, NSA_W)),
        ],
        out_specs=tile(NSA_W),
        out_shape=jax.ShapeDtypeStruct((batch, seq, NSA_W), BF16),
        scratch_shapes=[pltpu.VMEM((per_step * NSA_KV_GROUPS, tq, seq), F32),
                        pltpu.VMEM((per_step * 2 * NSA_KV_GROUPS, rq * tq, LANES), F32)],
        compiler_params=_params(2),
        name="nsa",
    )(bounds, qa.reshape(batch, seq, NSA_W), kcmp, vcmp, ksw.reshape(batch, seq, 2 * NSA_KV_W),
      vsw.reshape(batch, seq, 2 * NSA_KV_GROUPS * LANES), small.reshape(batch, seq, LANES),
      toeplitz, cmp_bias,
      jnp.asarray(overlap.T.astype(np.float32), BF16), jnp.asarray(expand.astype(np.float32), BF16),
      jnp.asarray(gate_expand, BF16))
    return out.reshape(batch * seq, NSA_W)


def _fox_kernel(bnd_ref, q_ref, k_ref, v_ref, cum_ref, cum_t_ref, o_ref, acc_ref, *, exact):
    qt = pl.program_id(1)
    tq = ATT_TILE
    pairs = FOX_HEADS // 2
    row = lax.broadcasted_iota(jnp.int32, (2, tq, tq), 1)
    col = lax.broadcasted_iota(jnp.int32, (2, tq, tq), 2)
    lane = lax.broadcasted_iota(jnp.int32, (tq, LANES), 1)
    low = jnp.where(lane < HEAD_DIM, 1.0, 0.0).astype(BF16)
    high = jnp.where(lane < HEAD_DIM, 0.0, 1.0).astype(BF16)
    n_seq = q_ref.shape[0]

    def key_rows(dl):
        return pl.ds(pl.multiple_of((qt - dl) * tq, tq), tq)

    bounds = []

    def pair_chain(b, p):
        pair = q_ref[b, :, p * LANES:(p + 1) * LANES]
        q = jnp.concatenate([pair * low, pair * high], axis=0)
        base = cum_ref[b, 2 * p:2 * p + 2, pl.ds(pl.multiple_of(qt * tq, tq), LANES)][:, 0:1]

        def decay_of(rows):
            return (base - cum_ref[b, 2 * p:2 * p + 2, rows]) * LOG2E

        def bound():
            cum_rows = cum_t_ref[b]
            c0 = FORGET_LANE + 2 * p
            own = jnp.concatenate([base[h:h + 1, :] - cum_rows[:, c0 + h:c0 + h + 1] for h in range(2)],
                                  axis=0)
            return jnp.broadcast_to(own * LOG2E + bnd_ref[0], (2 * tq, tq))

        bounds.append(bound)

        def score(dl):
            rows = key_rows(dl)
            decay = decay_of(rows)
            s = _dot_nt(q, k_ref[b, rows, p * LANES:(p + 1) * LANES]).reshape(2, tq, tq) + decay[:, None, :]
            if isinstance(dl, int):
                s = jnp.where(col <= row, s, NEG_INF)
            return s.reshape(2 * tq, tq)

        def pv(w, dl):
            rows = key_rows(dl)
            return jnp.concatenate(
                [_dot(w[0:tq], v_ref[b, rows, 2 * p * LANES:(2 * p + 1) * LANES]),
                 _dot(w[tq:2 * tq], v_ref[b, rows, (2 * p + 1) * LANES:(2 * p + 2) * LANES])], axis=0)

        return score, pv

    chains = [pair_chain(b, p) for b in range(n_seq) for p in range(pairs)]
    accs = _attention(chains, [(qt, tuple(range(len(chains))))], None if exact else bounds, acc_ref)
    for i, acc in enumerate(accs):
        b, p = divmod(i, pairs)
        o_ref[b, :, p * LANES:(p + 1) * LANES] = _pair_lanes(acc[0:tq], acc[tq:2 * tq], True).astype(BF16)


def _fox(bounds, qb, kb, vb, cum, cum_cols, batch, seq, exact):
    tq = ATT_TILE
    nq = seq // tq
    per_step = 2 if batch % 2 == 0 else 1
    out = pl.pallas_call(
        functools.partial(_fox_kernel, exact=exact),
        grid=(batch // per_step, nq),
        in_specs=[
            pl.BlockSpec(memory_space=pltpu.SMEM),
            pl.BlockSpec((per_step, tq, FOX_W), lambda b, t: (b, t, 0)),
            pl.BlockSpec((per_step, seq, FOX_W), lambda b, t: (b, 0, 0)),
            pl.BlockSpec((per_step, seq, FOX_HEADS * LANES), lambda b, t: (b, 0, 0)),
            pl.BlockSpec((per_step, FOX_HEADS, seq), lambda b, t: (b, 0, 0)),
            pl.BlockSpec((per_step, tq, LANES), lambda b, t: (b, t, 0)),
        ],
        out_specs=pl.BlockSpec((per_step, tq, FOX_W), lambda b, t: (b, t, 0)),
        out_shape=jax.ShapeDtypeStruct((batch, seq, FOX_W), BF16),
        scratch_shapes=[pltpu.VMEM((per_step * FOX_HEADS // 2, 2 * tq, LANES), F32)],
        compiler_params=_params(2),
        name="fox",
    )(bounds, qb.reshape(batch, seq, FOX_W), kb.reshape(batch, seq, FOX_W),
      vb.reshape(batch, seq, FOX_HEADS * LANES), cum, cum_cols.reshape(batch, seq, LANES))
    return out.reshape(batch * seq, FOX_W)


def _merge_ffn_kernel(x_ref, on_ref, of_ref, gmix_ref, wgm_ref, won_ref, wof_ref, wout_ref,
                      g2_ref, wup_ref, wdn_ref, o_ref):
    x = x_ref[...]
    u = _rms_rows(x, gmix_ref[...]).astype(BF16)
    gate = jax.nn.sigmoid(_dot(u, wgm_ref[...]))
    merged = (gate[:, :D_MODEL] * _dot(on_ref[...], won_ref[...])
              + gate[:, D_MODEL:] * _dot(of_ref[...], wof_ref[...]))
    x2 = x + _dot(merged.astype(BF16), wout_ref[...])
    o_ref[...] = _swiglu_residual(x2, g2_ref[...], wup_ref, wdn_ref)


def _merge_ffn(x2d, o_nsa, o_fox, mix_norm, w_gm, w_o_nsa, w_o_fox, w_out, gain2, w_up, w_down):
    n = x2d.shape[0]
    tm = TOKEN_TILE

    def row(width):
        return pl.BlockSpec((tm, width), lambda i: (i, 0))

    return pl.pallas_call(
        _merge_ffn_kernel,
        grid=(n // tm,),
        in_specs=[row(D_MODEL), row(NSA_W), row(FOX_W), _const_spec((1, D_MODEL)),
                  _const_spec((D_MODEL, 2 * D_MODEL)), _const_spec((NSA_W, D_MODEL)),
                  _const_spec((FOX_W, D_MODEL)), _const_spec((D_MODEL, D_MODEL)),
                  _const_spec((1, D_MODEL)), _const_spec((D_MODEL, 2 * D_FF)),
                  _const_spec((D_FF, D_MODEL))],
        out_specs=row(D_MODEL),
        out_shape=jax.ShapeDtypeStruct((n, D_MODEL), F32),
        compiler_params=_params(1),
        name="merge_ffn",
    )(x2d, o_nsa, o_fox, mix_norm.reshape(1, D_MODEL), w_gm, w_o_nsa.astype(BF16),
      w_o_fox.astype(BF16), w_out.astype(BF16), gain2.reshape(1, D_MODEL), w_up.astype(BF16),
      w_down.astype(BF16))


def _layer(x, ffn1_norm, ffn1_w_up, ffn1_w_down, mix_norm, w_in, b_forget, nsa_q_gain, nsa_k_gain,
           fox_q_gain, fox_k_gain, cmp_pos_k, cmp_pos_v, cmp_k_w1, cmp_k_w2, cmp_v_w1, cmp_v_w2,
           w_o_nsa, w_o_fox, w_out, ffn2_norm, ffn2_w_up, ffn2_w_down, rel_bias_table):
    batch, seq, d = x.shape
    assert d == D_MODEL and seq % ATT_TILE == 0 and (batch * seq) % TOKEN_TILE == 0
    assert seq // SLC_BLOCK <= LANES and (seq - CMP_BLOCK) // CMP_STRIDE + 1 == LANES - 1
    assert seq // (CMP_BLOCK // 2) == LANES
    x2d = x.reshape(batch * seq, D_MODEL)

    x1 = _ffn(x2d, ffn1_norm, ffn1_w_up, ffn1_w_down)

    w_packed, gain_row, small_bias, w_gm = _pack_in_proj(w_in, b_forget, nsa_q_gain, nsa_k_gain,
                                                         fox_q_gain, fox_k_gain)
    qa, qb, kb, ksw, vb, vsw, kc, vc, small = _in_proj(x1, mix_norm, w_packed, gain_row, small_bias,
                                                       batch, seq)

    kcmp, vcmp = _compress(kc, vc, batch, seq, _pack_compress(cmp_pos_k, cmp_k_w1, cmp_k_w2),
                           _pack_compress(cmp_pos_v, cmp_v_w1, cmp_v_w2), nsa_k_gain[0])
    toeplitz, cmp_bias = _bias_tables(rel_bias_table, seq // ATT_TILE)
    b_slc = _score_bound(nsa_q_gain, nsa_k_gain[1])
    b_win = _score_bound(nsa_q_gain, nsa_k_gain[2])
    t_hi, t_lo = jnp.max(rel_bias_table) * LOG2E, jnp.min(rel_bias_table) * LOG2E
    nsa_ok = 2.0 * jnp.maximum(b_slc, b_win) + (t_hi - t_lo) < EXP2_RANGE
    nsa_bounds = jnp.stack([b_slc + t_hi, b_win + t_hi])
    o_nsa = lax.cond(
        nsa_ok,
        lambda: _nsa(nsa_bounds, qa, kcmp, vcmp, ksw, vsw, small, toeplitz, cmp_bias, batch, seq, False),
        lambda: _nsa(nsa_bounds, qa, kcmp, vcmp, ksw, vsw, small, toeplitz, cmp_bias, batch, seq, True))

    cum_cols, cum = _cumsum(small, batch, seq)
    b_fox = _score_bound(fox_q_gain, fox_k_gain)
    fox_bounds = jnp.stack([b_fox])
    o_fox = lax.cond(2.0 * b_fox < EXP2_RANGE,
                     lambda: _fox(fox_bounds, qb, kb, vb, cum, cum_cols, batch, seq, False),
                     lambda: _fox(fox_bounds, qb, kb, vb, cum, cum_cols, batch, seq, True))

    w_o_nsa_p = w_o_nsa.reshape(NSA_KV_GROUPS, NSA_Q_PER_GROUP, HEAD_DIM, D_MODEL).transpose(1, 0, 2, 3)
    out = _merge_ffn(x1, o_nsa, o_fox, mix_norm, w_gm, w_o_nsa_p.reshape(NSA_W, D_MODEL), w_o_fox,
                     w_out, ffn2_norm, ffn2_w_up, ffn2_w_down)
    return out.reshape(batch, seq, D_MODEL)


def kernel(x, ffn1_norm, ffn1_w_up, ffn1_w_down, mix_norm, w_in, b_forget, nsa_q_gain, nsa_k_gain,
           fox_q_gain, fox_k_gain, cmp_pos_k, cmp_pos_v, cmp_k_w1, cmp_k_w2, cmp_v_w1, cmp_v_w2,
           w_o_nsa, w_o_fox, w_out, ffn2_norm, ffn2_w_up, ffn2_w_down, rel_bias_table):
    for layer in range(ffn1_norm.shape[0]):
        x = _layer(x, ffn1_norm[layer], ffn1_w_up[layer], ffn1_w_down[layer], mix_norm[layer],
                   w_in[layer], b_forget[layer], nsa_q_gain[layer], nsa_k_gain[layer],
                   fox_q_gain[layer], fox_k_gain[layer], cmp_pos_k[layer], cmp_pos_v[layer],
                   cmp_k_w1[layer], cmp_k_w2[layer], cmp_v_w1[layer], cmp_v_w2[layer],
                   w_o_nsa[layer], w_o_fox[layer], w_out[layer], ffn2_norm[layer],
                   ffn2_w_up[layer], ffn2_w_down[layer], rel_bias_table)
    return x
```

```python
import functools
import math

import numpy as np
import jax
import jax.numpy as jnp
from jax import lax
from jax.experimental import pallas as pl
from jax.experimental.pallas import tpu as pltpu

F32 = jnp.float32
BF16 = jnp.bfloat16

D_MODEL = 1024
HEAD_DIM = 64
NSA_HEADS = 8
NSA_KV_GROUPS = 2
NSA_Q_PER_GROUP = NSA_HEADS // NSA_KV_GROUPS
CMP_BLOCK = 32
CMP_STRIDE = 16
CMP_HIDDEN = 128
SLC_BLOCK = 64
SLC_TOPK = 8
WINDOW = 512
FOX_HEADS = 8
D_FF = 2816
N_BUCKETS = 32
MAX_DISTANCE = 128
RMS_EPS = 1e-6
NEG_INF = -1.0e30
FORCE_BONUS = 1.0e4

NSA_W = NSA_HEADS * HEAD_DIM
NSA_KV_W = NSA_KV_GROUPS * HEAD_DIM
FOX_W = FOX_HEADS * HEAD_DIM
IN_SPLITS = (NSA_W, NSA_KV_W, NSA_KV_W, NSA_KV_W, NSA_KV_W, NSA_KV_W, NSA_KV_W, 3 * NSA_HEADS,
             FOX_W, FOX_W, FOX_W, FOX_HEADS, 2 * D_MODEL)

LANES = 128
TOKEN_TILE = 512
FFN_CHUNK = D_FF // 2
ATT_TILE = 256
N_WIN_TILES = WINDOW // ATT_TILE + 1
VMEM_LIMIT = 56 * 1024 * 1024
LOG2E = 1.4426950408889634
EXP2_RANGE = 100.0

PQ_A = 0
PQ_B = PQ_A + NSA_W
PK_B = PQ_B + FOX_W
PK_SW = PK_B + FOX_W
P_NORM_END = PK_SW + 2 * NSA_KV_W
PV_B = P_NORM_END
PV_SW = PV_B + FOX_W
PKV_C = PV_SW + 2 * NSA_KV_W
P_SMALL = PKV_C + 2 * NSA_KV_W
P_END = P_SMALL + LANES
NORM_CHUNK = 256


def _dot(a, b):
    return jnp.dot(a, b, preferred_element_type=F32)


def _dot_nt(a, b):
    return lax.dot_general(a, b, (((1,), (1,)), ((), ())), preferred_element_type=F32)


def _split_dot(x, w):
    hi = x.astype(BF16)
    lo = (x - hi.astype(F32)).astype(BF16)
    return _dot(hi, w) + _dot(lo, w)


def _rms_rows(x, gain_row):
    ms = jnp.mean(x * x, axis=-1, keepdims=True)
    return x * lax.rsqrt(ms + RMS_EPS) * gain_row


def _const_spec(shape):
    nd = len(shape)
    return pl.BlockSpec(shape, lambda *_: (0,) * nd, pipeline_mode=pl.Buffered(1))


def _params(n_axes):
    return pltpu.CompilerParams(dimension_semantics=("arbitrary",) * n_axes,
                                vmem_limit_bytes=VMEM_LIMIT)


def _swiglu_residual(x, gain_row, wup_ref, wdn_ref):
    xn = _rms_rows(x, gain_row).astype(BF16)
    acc = jnp.zeros(x.shape, F32)
    for c in range(D_FF // FFN_CHUNK):
        lo = c * FFN_CHUNK
        gate = _dot(xn, wup_ref[:, lo:lo + FFN_CHUNK])
        up = _dot(xn, wup_ref[:, D_FF + lo:D_FF + lo + FFN_CHUNK])
        h = (gate * jax.nn.sigmoid(gate) * up).astype(BF16)
        acc = acc + _dot(h, wdn_ref[lo:lo + FFN_CHUNK, :])
    return x + 0.5 * acc


def _ffn_kernel(x_ref, g_ref, wup_ref, wdn_ref, o_ref):
    o_ref[...] = _swiglu_residual(x_ref[...], g_ref[...], wup_ref, wdn_ref)


def _ffn(x2d, gain, w_up, w_down):
    n = x2d.shape[0]
    row = pl.BlockSpec((TOKEN_TILE, D_MODEL), lambda i: (i, 0))
    return pl.pallas_call(
        _ffn_kernel,
        grid=(n // TOKEN_TILE,),
        in_specs=[row, _const_spec((1, D_MODEL)), _const_spec((D_MODEL, 2 * D_FF)),
                  _const_spec((D_FF, D_MODEL))],
        out_specs=row,
        out_shape=jax.ShapeDtypeStruct((n, D_MODEL), F32),
        compiler_params=_params(1),
        name="ffn",
    )(x2d, gain.reshape(1, D_MODEL), w_up.astype(BF16), w_down.astype(BF16))


def _block_ones(width, size=NORM_CHUNK):
    idx = np.arange(size) // width
    return jnp.asarray((idx[:, None] == idx[None, :]).astype(np.float32), BF16)


def _pack_in_proj(w_in, b_forget, nsa_q_gain, nsa_k_gain, fox_q_gain, fox_k_gain):
    scale = LOG2E / math.sqrt(HEAD_DIM)
    pts = np.cumsum(np.array(IN_SPLITS))[:-1].tolist()
    qa, kc, vc, ks, vs, kw, vw, ga, qb, kb, vb, fb, gm = jnp.split(w_in, pts, axis=1)
    cols = []
    for r in range(NSA_Q_PER_GROUP):
        for g in range(NSA_KV_GROUPS):
            h = g * NSA_Q_PER_GROUP + r
            cols.append(qa[:, h * HEAD_DIM:(h + 1) * HEAD_DIM])
    cols += [qb, kb, ks, kw]
    gains = [jnp.tile(nsa_q_gain * scale, NSA_HEADS), jnp.tile(fox_q_gain * scale, FOX_HEADS),
             jnp.tile(fox_k_gain, FOX_HEADS), jnp.tile(nsa_k_gain[1], NSA_KV_GROUPS),
             jnp.tile(nsa_k_gain[2], NSA_KV_GROUPS)]
    n_small_pad = LANES - ga.shape[1] - fb.shape[1]
    cols += [vb, vs, vw, kc, vc, ga, fb, jnp.zeros((D_MODEL, n_small_pad), F32)]
    w_packed = jnp.concatenate(cols, axis=1).astype(BF16)
    gain_row = jnp.concatenate(gains).reshape(1, P_NORM_END)
    small_bias = jnp.concatenate([jnp.zeros((ga.shape[1],), F32), b_forget,
                                  jnp.zeros((n_small_pad,), F32)]).reshape(1, LANES)
    return w_packed, gain_row, small_bias, gm.astype(BF16)


def _in_proj_kernel(x_ref, g_ref, w_ref, gain_ref, sbias_ref, bd64_ref,
                    qa_ref, qb_ref, kb_ref, ksw_ref, vb_ref, vsw_ref, kc_ref, vc_ref, small_ref, kv_scr):
    u = _rms_rows(x_ref[...], g_ref[...]).astype(BF16)
    tm = u.shape[0]

    y = _dot(u, w_ref[:, 0:P_NORM_END])
    n_chunks = P_NORM_END // NORM_CHUNK
    squares = jnp.concatenate([jnp.square(y[:, c * NORM_CHUNK:(c + 1) * NORM_CHUNK]).astype(BF16)
                               for c in range(n_chunks)], axis=0)
    ss = _dot(squares, bd64_ref[...])
    chunk = 0
    for o_ref in (qa_ref, qb_ref, kb_ref, ksw_ref):
        for i in range(o_ref.shape[1] // NORM_CHUNK):
            cols = slice(chunk * NORM_CHUNK, (chunk + 1) * NORM_CHUNK)
            inv = lax.rsqrt(ss[chunk * tm:(chunk + 1) * tm] * (1.0 / HEAD_DIM) + RMS_EPS)
            o_ref[:, i * NORM_CHUNK:(i + 1) * NORM_CHUNK] = (y[:, cols] * inv * gain_ref[:, cols]).astype(BF16)
            chunk += 1
    low = lax.broadcasted_iota(jnp.int32, (u.shape[0], LANES), 1) < HEAD_DIM

    def store_with_ones(v, o_ref):
        for i in range(v.shape[1] // LANES):
            pair = v[:, i * LANES:(i + 1) * LANES]
            o_ref[:, 2 * i * LANES:(2 * i + 1) * LANES] = jnp.where(low, pair, 1.0).astype(BF16)
            o_ref[:, (2 * i + 1) * LANES:(2 * i + 2) * LANES] = jnp.where(low, 1.0, pair).astype(BF16)

    rest = _dot(u, w_ref[:, P_NORM_END:P_END])
    store_with_ones(rest[:, PV_B - P_NORM_END:PV_SW - P_NORM_END], vb_ref)
    store_with_ones(rest[:, PV_SW - P_NORM_END:PKV_C - P_NORM_END], vsw_ref)
    for j, o_ref in enumerate((kc_ref, vc_ref)):
        c0 = PKV_C - P_NORM_END + j * NSA_KV_W
        kv_scr[j] = rest[:, c0:c0 + NSA_KV_W]
        for l in range(CMP_STRIDE):
            o_ref[0, :, l * NSA_KV_W:(l + 1) * NSA_KV_W] = kv_scr[j, pl.ds(l, tm // CMP_STRIDE,
                                                                           stride=CMP_STRIDE), :]
    z = rest[:, P_SMALL - P_NORM_END:] + sbias_ref[...]
    lane = lax.broadcasted_iota(jnp.int32, z.shape, 1)
    log_sig = jnp.minimum(z, 0.0) - jnp.log1p(jnp.exp(-jnp.abs(z)))
    small_ref[...] = jnp.where(lane < 3 * NSA_HEADS, jax.nn.sigmoid(z), log_sig)


def _in_proj(x2d, mix_norm, w_packed, gain_row, small_bias, batch, seq):
    n = x2d.shape[0]
    steps_per_seq = seq // TOKEN_TILE

    def row(width):
        return pl.BlockSpec((TOKEN_TILE, width), lambda i: (i, 0))

    def out(width, dtype):
        return row(width), jax.ShapeDtypeStruct((n, width), dtype)

    grouped = (pl.BlockSpec((1, TOKEN_TILE // CMP_STRIDE, CMP_STRIDE * NSA_KV_W),
                            lambda i: (i // steps_per_seq, i % steps_per_seq, 0)),
               jax.ShapeDtypeStruct((batch, seq // CMP_STRIDE, CMP_STRIDE * NSA_KV_W), F32))
    outs = [out(NSA_W, BF16), out(FOX_W, BF16), out(FOX_W, BF16), out(2 * NSA_KV_W, BF16),
            out(2 * FOX_W, BF16), out(4 * NSA_KV_W, BF16), grouped, grouped, out(LANES, F32)]
    return pl.pallas_call(
        _in_proj_kernel,
        grid=(n // TOKEN_TILE,),
        in_specs=[row(D_MODEL), _const_spec((1, D_MODEL)), _const_spec((D_MODEL, P_END)),
                  _const_spec((1, P_NORM_END)), _const_spec((1, LANES)),
                  _const_spec((NORM_CHUNK, NORM_CHUNK))],
        out_specs=[spec for spec, _ in outs],
        out_shape=[shape for _, shape in outs],
        scratch_shapes=[pltpu.VMEM((2, TOKEN_TILE, NSA_KV_W), F32)],
        compiler_params=_params(1),
        name="in_proj",
    )(x2d, mix_norm.reshape(1, D_MODEL), w_packed, gain_row, small_bias, _block_ones(HEAD_DIM))


def _pack_compress(pos, w1, w2):
    half = CMP_BLOCK // 2
    eye = jnp.eye(NSA_KV_GROUPS, dtype=F32)
    w1r = w1.reshape(CMP_BLOCK, HEAD_DIM, CMP_HIDDEN)

    def big(w):
        return jnp.einsum('ldh,pg->lpdgh', w, eye).reshape(half * NSA_KV_W, NSA_KV_GROUPS * CMP_HIDDEN)

    def posrow(p):
        return jnp.broadcast_to(p[:, None, :], (half, NSA_KV_GROUPS, HEAD_DIM)).reshape(1, half * NSA_KV_W)

    w2big = jnp.einsum('hd,pg->phgd', w2, eye).reshape(NSA_KV_GROUPS * CMP_HIDDEN, NSA_KV_W)
    return (posrow(pos[:half]), posrow(pos[half:]), big(w1r[:half]).astype(BF16),
            big(w1r[half:]).astype(BF16), w2big.astype(BF16))


def _compress_kernel(kc_ref, vc_ref, kp_lo, kp_hi, kw_lo, kw_hi, kw2, vp_lo, vp_hi, vw_lo, vw_hi, vw2,
                     kgain_ref, bd64_ref, ko_ref, vo_ref):
    def mlp(r, p_lo, p_hi, w_lo, w_hi, w2):
        first = _dot((r + p_lo[...]).astype(BF16), w_lo[...])
        second = _dot((r + p_hi[...]).astype(BF16), w_hi[...])
        h = first + pltpu.roll(second, second.shape[0] - 1, axis=0)
        return _dot((h * jax.nn.sigmoid(h)).astype(BF16), w2[...])

    k = mlp(kc_ref[0], kp_lo, kp_hi, kw_lo, kw_hi, kw2)
    ss = _split_dot(k * k, bd64_ref[...])
    ko_ref[0] = (k * lax.rsqrt(ss * (1.0 / HEAD_DIM) + RMS_EPS) * kgain_ref[...]).astype(BF16)
    vo_ref[0] = mlp(vc_ref[0], vp_lo, vp_hi, vw_lo, vw_hi, vw2).astype(BF16)


def _compress(kc, vc, batch, seq, k_pack, v_pack, k_gain):
    rows = seq // (CMP_BLOCK // 2)
    width = (CMP_BLOCK // 2) * NSA_KV_W
    blk = pl.BlockSpec((1, rows, width), lambda b: (b, 0, 0))
    out = pl.BlockSpec((1, rows, NSA_KV_W), lambda b: (b, 0, 0))
    pack_specs = [_const_spec((1, width)), _const_spec((1, width)),
                  _const_spec((width, NSA_KV_GROUPS * CMP_HIDDEN)),
                  _const_spec((width, NSA_KV_GROUPS * CMP_HIDDEN)),
                  _const_spec((NSA_KV_GROUPS * CMP_HIDDEN, NSA_KV_W))]
    bd = _block_ones(HEAD_DIM, NSA_KV_W)
    return pl.pallas_call(
        _compress_kernel,
        grid=(batch,),
        in_specs=[blk, blk] + pack_specs + pack_specs + [_const_spec((1, NSA_KV_W)),
                                                         _const_spec((NSA_KV_W, NSA_KV_W))],
        out_specs=[out, out],
        out_shape=[jax.ShapeDtypeStruct((batch, rows, NSA_KV_W), BF16)] * 2,
        compiler_params=_params(1),
        name="compress",
    )(kc.reshape(batch, rows, width), vc.reshape(batch, rows, width), *k_pack, *v_pack,
      jnp.tile(k_gain, NSA_KV_GROUPS).reshape(1, NSA_KV_W), bd)


FORGET_LANE = 3 * NSA_HEADS


def _cumsum_kernel(x_ref, cols_ref, rows_ref):
    x = x_ref[...]
    row = lax.broadcasted_iota(jnp.int32, x.shape, 0)
    k = 1
    while k < x.shape[0]:
        x = x + jnp.where(row >= k, pltpu.roll(x, k, axis=0), 0.0)
        k *= 2
    cols_ref[...] = x
    rows_ref[0] = x.T[FORGET_LANE:FORGET_LANE + FOX_HEADS]


def _cumsum(small, batch, seq):
    spec = pl.BlockSpec((seq, LANES), lambda b: (b, 0))
    return pl.pallas_call(
        _cumsum_kernel, grid=(batch,), in_specs=[spec],
        out_specs=[spec, pl.BlockSpec((1, FOX_HEADS, seq), lambda b: (b, 0, 0))],
        out_shape=[jax.ShapeDtypeStruct((batch * seq, LANES), F32),
                   jax.ShapeDtypeStruct((batch, FOX_HEADS, seq), F32)],
        compiler_params=_params(1), name="cumsum",
    )(small)


def _write_bias_rows(d, valid, tab_ref, o_ref, rows):
    max_exact = N_BUCKETS // 2
    n = jnp.maximum(d, 0)
    nf = jnp.maximum(n, 1).astype(F32)
    large = max_exact + (jnp.log(nf / max_exact) / math.log(MAX_DISTANCE / max_exact)
                         * (N_BUCKETS - max_exact)).astype(jnp.int32)
    bucket = jnp.where(n < max_exact, n, jnp.minimum(large, N_BUCKETS - 1))
    for h in range(NSA_HEADS):
        acc = jnp.zeros(d.shape, F32)
        for b in range(N_BUCKETS):
            acc = jnp.where(bucket == b, tab_ref[b, h], acc)
        g, r = divmod(h, NSA_Q_PER_GROUP)
        o_ref[0, g, r, rows, :] = jnp.where(valid, acc * LOG2E, NEG_INF)


def _write_far_tile(valid, tab_ref, o_ref):
    for h in range(NSA_HEADS):
        g, r = divmod(h, NSA_Q_PER_GROUP)
        far = jnp.where(valid, tab_ref[N_BUCKETS - 1, h], 0.0) * LOG2E
        o_ref[0, g, r] = jnp.where(valid, far, NEG_INF)


def _toeplitz_bias_kernel(tab_ref, o_ref):
    s = pl.program_id(0)
    tile = ATT_TILE

    def distances(n_rows):
        i = lax.broadcasted_iota(jnp.int32, (n_rows, tile), 0)
        j = lax.broadcasted_iota(jnp.int32, (n_rows, tile), 1)
        d = jnp.minimum(s, N_WIN_TILES - 1) * tile + i - j
        d_hi = jnp.where(s < N_WIN_TILES, WINDOW, jnp.int32(1 << 30))
        return d, (d >= 0) & (d < d_hi)

    _write_far_tile(distances(tile)[1], tab_ref, o_ref)
    for step in range(N_WIN_TILES + 1):
        near = min(tile, max(0, MAX_DISTANCE + tile - 1 - min(step, N_WIN_TILES - 1) * tile))
        near = -(-near // 8) * 8
        if near:
            @pl.when(s == step)
            def _(near=near):
                d, valid = distances(near)
                _write_bias_rows(d, valid, tab_ref, o_ref, slice(0, near))


def _cmp_bias_kernel(tab_ref, o_ref):
    t = pl.program_id(0)
    chunk = 8

    def distances(c0, n_rows):
        c = c0 + lax.broadcasted_iota(jnp.int32, (n_rows, ATT_TILE), 0)
        i = lax.broadcasted_iota(jnp.int32, (n_rows, ATT_TILE), 1)
        d = t * ATT_TILE + i - CMP_STRIDE * c - (CMP_BLOCK - 1)
        return d, (d >= 0) & (c < LANES - 1)

    _write_far_tile(distances(0, LANES)[1], tab_ref, o_ref)
    first = (t * ATT_TILE - (MAX_DISTANCE + CMP_BLOCK - 1)) // (chunk * CMP_STRIDE)
    n_chunks = (ATT_TILE + MAX_DISTANCE) // (chunk * CMP_STRIDE) + 2

    def refine(m, carry):
        c0 = pl.multiple_of(jnp.clip(first + m, 0, LANES // chunk - 1) * chunk, chunk)
        d, valid = distances(c0, chunk)
        _write_bias_rows(d, valid, tab_ref, o_ref, pl.ds(c0, chunk))
        return carry

    lax.fori_loop(0, n_chunks, refine, 0)


def _bias_tables(rel_bias_table, n_q_tiles):
    def call(body, steps, rows, name):
        shape = (steps, NSA_KV_GROUPS, NSA_Q_PER_GROUP, rows, ATT_TILE)
        return pl.pallas_call(
            body, grid=(steps,),
            in_specs=[pl.BlockSpec(memory_space=pltpu.SMEM)],
            out_specs=pl.BlockSpec((1,) + shape[1:], lambda s: (s, 0, 0, 0, 0)),
            out_shape=jax.ShapeDtypeStruct(shape, F32),
            compiler_params=_params(1), name=name,
        )(rel_bias_table)

    return (call(_toeplitz_bias_kernel, N_WIN_TILES + 1, ATT_TILE, "toeplitz_bias"),
            call(_cmp_bias_kernel, n_q_tiles, LANES, "cmp_bias"))


def _score_bound(q_gain, k_gain):
    return 1.02 * LOG2E * math.sqrt(HEAD_DIM) * jnp.max(jnp.abs(q_gain)) * jnp.max(jnp.abs(k_gain))


def _attention(chains, groups, bounds, acc_ref):
    n = len(chains)

    def later(k):
        return tuple(i for _, ids in groups[k:] for i in ids)

    def exact_max():
        def half_max(i, dl):
            s = chains[i][0](dl)
            return jnp.maximum(s[:, :LANES], s[:, LANES:])

        mx = [half_max(i, 0) for i in range(n)]
        lo = 1
        for k, (last, _) in enumerate(groups):
            members = later(k)

            def body(dl, carry, members=members):
                return tuple(jnp.maximum(c, half_max(i, dl)) for c, i in zip(carry, members))

            for i, r in zip(members, lax.fori_loop(lo, last + 1, body, tuple(mx[i] for i in members))):
                mx[i] = r
            lo = last + 1
        return tuple(jnp.broadcast_to(jnp.max(m, axis=-1, keepdims=True), (m.shape[0], ATT_TILE))
                     for m in mx)

    shift = exact_max() if bounds is None else [b() for b in bounds]

    def weights(i, dl):
        return jnp.exp2(chains[i][0](dl) - shift[i]).astype(BF16)

    def product(i, dl):
        return chains[i][1](weights(i, dl), dl)

    for i in range(n):
        acc_ref[i] = product(i, 0)
    lo = 1
    for k, (last, _) in enumerate(groups):
        members = later(k)
        count = last + 1 - lo

        def body(j, carry, members=members, lo=lo):
            dl = lo + 2 * j
            for i in members:
                acc_ref[i] += product(i, dl) + product(i, dl + 1)
            return carry

        lax.fori_loop(0, count // 2, body, 0)

        @pl.when(count % 2 == 1)
        def _(members=members, last=last):
            for i in members:
                acc_ref[i] += product(i, last)

        lo = last + 1
    return [acc_ref[i] for i in range(n)]


def _pair_lanes(first, second, normalise):
    low = lax.broadcasted_iota(jnp.int32, first.shape, 1) < HEAD_DIM
    pair = jnp.where(low, first, second)
    if not normalise:
        return pair
    sums = pltpu.roll(jnp.where(low, second, first), HEAD_DIM, axis=1)
    return pair * (1.0 / sums)


def _select_blocks(imp_t, t0, tq):
    blk = lax.broadcasted_iota(jnp.int32, imp_t.shape, 0)
    cur = (t0 + (lax.broadcasted_iota(jnp.int32, imp_t.shape, 1) & (tq - 1))) // SLC_BLOCK
    forced = (blk == 0) | (blk == cur) | (blk == cur - 1)
    score = jnp.where(blk <= cur, imp_t + jnp.where(forced, FORCE_BONUS, 0.0), NEG_INF)
    blk_f = blk.astype(F32)
    dead = -3.0e38
    picked = jnp.zeros(imp_t.shape, F32)
    for _ in range(SLC_TOPK):
        best = jnp.max(score, axis=0, keepdims=True)
        first = jnp.min(jnp.where(score == best, blk_f, float(LANES)), axis=0, keepdims=True)
        hit = blk_f == first
        picked = jnp.where(hit, 1.0, picked)
        score = jnp.where(hit, dead, score)
    return picked


def _nsa_kernel(bnd_ref, qa_ref, kcmp_ref, vcmp_ref, ksw_ref, vsw_ref, small_ref, tb_ref, bc_ref, ov_ref,
                e_ref, gx_ref, o_ref, amask_ref, acc_ref, *, n_blocks, exact):
    qt = pl.program_id(1)
    tq = ATT_TILE
    rq = NSA_Q_PER_GROUP
    lane = lax.broadcasted_iota(jnp.int32, (tq, LANES), 1)
    own_lanes = [jnp.where((lane // HEAD_DIM) == g, 1.0, 0.0).astype(BF16) for g in range(NSA_KV_GROUPS)]

    def key_rows(dl):
        return pl.ds(pl.multiple_of((qt - dl) * tq, tq), tq)

    groups = range(NSA_KV_GROUPS)
    n_seq = qa_ref.shape[0]
    n_grp = NSA_KV_GROUPS

    def win_chain(b, g, q):
        def score(dl):
            s = _dot_nt(q, ksw_ref[b, key_rows(dl), LANES:2 * LANES]).reshape(rq, tq, tq)
            return (s + tb_ref[dl, g]).reshape(rq * tq, tq)

        def pv(p, dl):
            c0 = (n_grp + g) * LANES
            return _dot(p, vsw_ref[b, key_rows(dl), c0:c0 + LANES])

        return score, pv

    def slc_chain(b, g, q):
        def score(dl):
            tile = dl if isinstance(dl, int) else jnp.where(dl < N_WIN_TILES - 1, dl, N_WIN_TILES)
            rows = key_rows(dl)
            s = _dot_nt(q, ksw_ref[b, rows, 0:LANES]).reshape(rq, tq, tq)
            return (s + tb_ref[tile, g] + amask_ref[b * n_grp + g, :, rows][None]).reshape(rq * tq, tq)

        def pv(p, dl):
            return _dot(p, vsw_ref[b, key_rows(dl), g * LANES:(g + 1) * LANES])

        return score, pv

    slc_chains, win_chains, o_cmps = [], [], []
    for b in range(n_seq):
        qs = [jnp.concatenate([qa_ref[b, :, r * LANES:(r + 1) * LANES] * own_lanes[g] for r in range(rq)],
                              axis=0) for g in groups]
        slc_chains += [slc_chain(b, g, qs[g]) for g in groups]
        win_chains += [win_chain(b, g, qs[g]) for g in groups]

        s = (_dot_nt(kcmp_ref[b], jnp.concatenate(qs, axis=0))
             + jnp.concatenate([bc_ref[0, g, r] for g in groups for r in range(rq)], axis=1))
        m = jnp.max(s, axis=0, keepdims=True)
        p = jnp.where(s > 0.5 * NEG_INF, jnp.exp2(s - m), 0.0)
        l = jnp.sum(p, axis=0, keepdims=True)
        p_c = p * (1.0 / jnp.where(l > 0.0, l, 1.0))
        o_cmp = _dot(p_c.T.astype(BF16), vcmp_ref[b])
        o_cmps.append([o_cmp[g * rq * tq:(g + 1) * rq * tq] for g in groups])

        p_sum = jnp.concatenate([sum(p_c[:, (g * rq + r) * tq:(g * rq + r + 1) * tq] for r in range(rq))
                                 for g in groups], axis=1)
        hi = p_sum.astype(BF16)
        lo = (p_sum - hi.astype(F32)).astype(BF16)
        imp_t = _dot(ov_ref[...], hi) + _dot(ov_ref[...], lo)
        sel_t = _select_blocks(imp_t[0:n_blocks], qt * tq, tq)
        block_bias = jnp.concatenate([jnp.where(sel_t > 0.0, 0.0, NEG_INF),
                                      jnp.zeros((LANES - n_blocks, n_grp * tq), F32)], axis=0)
        amask_ref[b * n_grp:(b + 1) * n_grp] = _dot(block_bias.T.astype(BF16), e_ref[...]).reshape(
            n_grp, tq, e_ref.shape[1])

    n_slc = len(slc_chains)
    slc_ids = tuple(range(n_slc))
    win_ids = tuple(range(n_slc, 2 * n_slc))
    bounds = None if exact else [lambda i=i: jnp.full((rq * tq, tq), bnd_ref[i // n_slc], F32)
                                 for i in range(2 * n_slc)]
    accs = _attention(slc_chains + win_chains,
                      [(jnp.minimum(qt, N_WIN_TILES - 1), win_ids), (qt, slc_ids)], bounds, acc_ref)

    for b in range(n_seq):
        gates = small_ref[b]
        g_hi = gates.astype(BF16)
        g_lo = (gates - g_hi.astype(F32)).astype(BF16)
        out = [jnp.zeros((tq, LANES), F32) for _ in range(rq)]
        lo_, hi_ = b * n_grp, (b + 1) * n_grp
        for br, per_group in enumerate((o_cmps[b], accs[lo_:hi_], accs[n_slc + lo_:n_slc + hi_])):
            gate = _dot(g_hi, gx_ref[br]) + _dot(g_lo, gx_ref[br])
            for r in range(rq):
                first, second = (a[r * tq:(r + 1) * tq] for a in per_group)
                pair = _pair_lanes(first, second, normalise=br > 0)
                out[r] = out[r] + gate[:, r * LANES:(r + 1) * LANES] * pair
        for r in range(rq):
            o_ref[b, :, r * LANES:(r + 1) * LANES] = out[r].astype(BF16)


def _nsa(bounds, qa, kcmp, vcmp, ksw, vsw, small, toeplitz, cmp_bias, batch, seq, exact):
    tq = ATT_TILE
    nq = seq // tq
    rq = NSA_Q_PER_GROUP
    n_blocks = seq // SLC_BLOCK
    n_cmp = (seq - CMP_BLOCK) // CMP_STRIDE + 1
    ci = np.arange(LANES)[:, None] * CMP_STRIDE
    sj = np.arange(LANES)[None, :] * SLC_BLOCK
    overlap = ((ci <= sj + SLC_BLOCK - 1) & (ci + CMP_BLOCK - 1 >= sj)
               & (np.arange(LANES)[:, None] < n_cmp) & (np.arange(LANES)[None, :] < n_blocks))
    expand = np.arange(LANES)[:, None] == (np.arange(seq)[None, :] // SLC_BLOCK)
    gate_expand = np.zeros((3, LANES, NSA_W), np.float32)
    for b in range(3):
        for g in range(NSA_KV_GROUPS):
            for r in range(rq):
                c0 = r * LANES + g * HEAD_DIM
                gate_expand[b, b * NSA_HEADS + g * rq + r, c0:c0 + HEAD_DIM] = 1.0
    per_step = 2 if batch % 2 == 0 else 1

    def rows(n_rows, width):
        return pl.BlockSpec((per_step, n_rows, width), lambda b, t: (b, 0, 0))

    def tile(width):
        return pl.BlockSpec((per_step, tq, width), lambda b, t: (b, t, 0))

    out = pl.pallas_call(
        functools.partial(_nsa_kernel, n_blocks=n_blocks, exact=exact),
        grid=(batch // per_step, nq),
        in_specs=[
            pl.BlockSpec(memory_space=pltpu.SMEM),
            tile(NSA_W),
            rows(LANES, NSA_KV_W),
            rows(LANES, NSA_KV_W),
            rows(seq, 2 * NSA_KV_W),
            rows(seq, 2 * NSA_KV_GROUPS * LANES),
            tile(LANES),
            _const_spec((N_WIN_TILES + 1, NSA_KV_GROUPS, rq, tq, tq)),
            pl.BlockSpec((1, NSA_KV_GROUPS, rq, LANES, tq), lambda b, t: (t, 0, 0, 0, 0)),
            _const_spec((LANES, LANES)),
            _const_spec((LANES, seq)),
            _const_spec((3, LANES, NSA_W)),
        ],
        out_specs=tile(NSA_W),
        out_shape=jax.ShapeDtypeStruct((batch, seq, NSA_W), BF16),
        scratch_shapes=[pltpu.VMEM((per_step * NSA_KV_GROUPS, tq, seq), F32),
                        pltpu.VMEM((per_step * 2 * NSA_KV_GROUPS, rq * tq, LANES), F32)],
        compiler_params=_params(2),
        name="nsa",
    )(bounds, qa.reshape(batch, seq, NSA_W), kcmp, vcmp, ksw.reshape(batch, seq, 2 * NSA_KV_W),
      vsw.reshape(batch, seq, 2 * NSA_KV_GROUPS * LANES), small.reshape(batch, seq, LANES),
      toeplitz, cmp_bias,
      jnp.asarray(overlap.T.astype(np.float32), BF16), jnp.asarray(expand.astype(np.float32), BF16),
      jnp.asarray(gate_expand, BF16))
    return out.reshape(batch * seq, NSA_W)


def _fox_kernel(bnd_ref, q_ref, k_ref, v_ref, cum_ref, cum_t_ref, o_ref, acc_ref, *, exact):
    qt = pl.program_id(1)
    tq = ATT_TILE
    pairs = FOX_HEADS // 2
    row = lax.broadcasted_iota(jnp.int32, (2, tq, tq), 1)
    col = lax.broadcasted_iota(jnp.int32, (2, tq, tq), 2)
    lane = lax.broadcasted_iota(jnp.int32, (tq, LANES), 1)
    low = jnp.where(lane < HEAD_DIM, 1.0, 0.0).astype(BF16)
    high = jnp.where(lane < HEAD_DIM, 0.0, 1.0).astype(BF16)
    n_seq = q_ref.shape[0]

    def key_rows(dl):
        return pl.ds(pl.multiple_of((qt - dl) * tq, tq), tq)

    bounds = []

    def pair_chain(b, p):
        pair = q_ref[b, :, p * LANES:(p + 1) * LANES]
        q = jnp.concatenate([pair * low, pair * high], axis=0)
        base = cum_ref[b, 2 * p:2 * p + 2, pl.ds(pl.multiple_of(qt * tq, tq), LANES)][:, 0:1]

        def decay_of(rows):
            return (base - cum_ref[b, 2 * p:2 * p + 2, rows]) * LOG2E

        def bound():
            cum_rows = cum_t_ref[b]
            c0 = FORGET_LANE + 2 * p
            own = jnp.concatenate([base[h:h + 1, :] - cum_rows[:, c0 + h:c0 + h + 1] for h in range(2)],
                                  axis=0)
            return jnp.broadcast_to(own * LOG2E + bnd_ref[0], (2 * tq, tq))

        bounds.append(bound)

        def score(dl):
            rows = key_rows(dl)
            decay = decay_of(rows)
            s = _dot_nt(q, k_ref[b, rows, p * LANES:(p + 1) * LANES]).reshape(2, tq, tq) + decay[:, None, :]
            if isinstance(dl, int):
                s = jnp.where(col <= row, s, NEG_INF)
            return s.reshape(2 * tq, tq)

        def pv(w, dl):
            rows = key_rows(dl)
            return jnp.concatenate(
                [_dot(w[0:tq], v_ref[b, rows, 2 * p * LANES:(2 * p + 1) * LANES]),
                 _dot(w[tq:2 * tq], v_ref[b, rows, (2 * p + 1) * LANES:(2 * p + 2) * LANES])], axis=0)

        return score, pv

    chains = [pair_chain(b, p) for b in range(n_seq) for p in range(pairs)]
    accs = _attention(chains, [(qt, tuple(range(len(chains))))], None if exact else bounds, acc_ref)
    for i, acc in enumerate(accs):
        b, p = divmod(i, pairs)
        o_ref[b, :, p * LANES:(p + 1) * LANES] = _pair_lanes(acc[0:tq], acc[tq:2 * tq], True).astype(BF16)


def _fox(bounds, qb, kb, vb, cum, cum_cols, batch, seq, exact):
    tq = ATT_TILE
    nq = seq // tq
    per_step = 2 if batch % 2 == 0 else 1
    out = pl.pallas_call(
        functools.partial(_fox_kernel, exact=exact),
        grid=(batch // per_step, nq),
        in_specs=[
            pl.BlockSpec(memory_space=pltpu.SMEM),
            pl.BlockSpec((per_step, tq, FOX_W), lambda b, t: (b, t, 0)),
            pl.BlockSpec((per_step, seq, FOX_W), lambda b, t: (b, 0, 0)),
            pl.BlockSpec((per_step, seq, FOX_HEADS * LANES), lambda b, t: (b, 0, 0)),
            pl.BlockSpec((per_step, FOX_HEADS, seq), lambda b, t: (b, 0, 0)),
            pl.BlockSpec((per_step, tq, LANES), lambda b, t: (b, t, 0)),
        ],
        out_specs=pl.BlockSpec((per_step, tq, FOX_W), lambda b, t: (b, t, 0)),
        out_shape=jax.ShapeDtypeStruct((batch, seq, FOX_W), BF16),
        scratch_shapes=[pltpu.VMEM((per_step * FOX_HEADS // 2, 2 * tq, LANES), F32)],
        compiler_params=_params(2),
        name="fox",
    )(bounds, qb.reshape(batch, seq, FOX_W), kb.reshape(batch, seq, FOX_W),
      vb.reshape(batch, seq, FOX_HEADS * LANES), cum, cum_cols.reshape(batch, seq, LANES))
    return out.reshape(batch * seq, FOX_W)


def _merge_ffn_kernel(x_ref, on_ref, of_ref, gmix_ref, wgm_ref, won_ref, wof_ref, wout_ref,
                      g2_ref, wup_ref, wdn_ref, o_ref):
    x = x_ref[...]
    u = _rms_rows(x, gmix_ref[...]).astype(BF16)
    gate = jax.nn.sigmoid(_dot(u, wgm_ref[...]))
    merged = (gate[:, :D_MODEL] * _dot(on_ref[...], won_ref[...])
              + gate[:, D_MODEL:] * _dot(of_ref[...], wof_ref[...]))
    x2 = x + _dot(merged.astype(BF16), wout_ref[...])
    o_ref[...] = _swiglu_residual(x2, g2_ref[...], wup_ref, wdn_ref)


def _merge_ffn(x2d, o_nsa, o_fox, mix_norm, w_gm, w_o_nsa, w_o_fox, w_out, gain2, w_up, w_down):
    n = x2d.shape[0]
    tm = TOKEN_TILE

    def row(width):
        return pl.BlockSpec((tm, width), lambda i: (i, 0))

    return pl.pallas_call(
        _merge_ffn_kernel,
        grid=(n // tm,),
        in_specs=[row(D_MODEL), row(NSA_W), row(FOX_W), _const_spec((1, D_MODEL)),
                  _const_spec((D_MODEL, 2 * D_MODEL)), _const_spec((NSA_W, D_MODEL)),
                  _const_spec((FOX_W, D_MODEL)), _const_spec((D_MODEL, D_MODEL)),
                  _const_spec((1, D_MODEL)), _const_spec((D_MODEL, 2 * D_FF)),
                  _const_spec((D_FF, D_MODEL))],
        out_specs=row(D_MODEL),
        out_shape=jax.ShapeDtypeStruct((n, D_MODEL), F32),
        compiler_params=_params(1),
        name="merge_ffn",
    )(x2d, o_nsa, o_fox, mix_norm.reshape(1, D_MODEL), w_gm, w_o_nsa.astype(BF16),
      w_o_fox.astype(BF16), w_out.astype(BF16), gain2.reshape(1, D_MODEL), w_up.astype(BF16),
      w_down.astype(BF16))


def _layer(x, ffn1_norm, ffn1_w_up, ffn1_w_down, mix_norm, w_in, b_forget, nsa_q_gain, nsa_k_gain,
           fox_q_gain, fox_k_gain, cmp_pos_k, cmp_pos_v, cmp_k_w1, cmp_k_w2, cmp_v_w1, cmp_v_w2,
           w_o_nsa, w_o_fox, w_out, ffn2_norm, ffn2_w_up, ffn2_w_down, rel_bias_table):
    batch, seq, d = x.shape
    assert d == D_MODEL and seq % ATT_TILE == 0 and (batch * seq) % TOKEN_TILE == 0
    assert seq // SLC_BLOCK <= LANES and (seq - CMP_BLOCK) // CMP_STRIDE + 1 == LANES - 1
    assert seq // (CMP_BLOCK // 2) == LANES
    x2d = x.reshape(batch * seq, D_MODEL)

    x1 = _ffn(x2d, ffn1_norm, ffn1_w_up, ffn1_w_down)

    w_packed, gain_row, small_bias, w_gm = _pack_in_proj(w_in, b_forget, nsa_q_gain, nsa_k_gain,
                                                         fox_q_gain, fox_k_gain)
    qa, qb, kb, ksw, vb, vsw, kc, vc, small = _in_proj(x1, mix_norm, w_packed, gain_row, small_bias,
                                                       batch, seq)

    kcmp, vcmp = _compress(kc, vc, batch, seq, _pack_compress(cmp_pos_k, cmp_k_w1, cmp_k_w2),
                           _pack_compress(cmp_pos_v, cmp_v_w1, cmp_v_w2), nsa_k_gain[0])
    toeplitz, cmp_bias = _bias_tables(rel_bias_table, seq // ATT_TILE)
    b_slc = _score_bound(nsa_q_gain, nsa_k_gain[1])
    b_win = _score_bound(nsa_q_gain, nsa_k_gain[2])
    t_hi, t_lo = jnp.max(rel_bias_table) * LOG2E, jnp.min(rel_bias_table) * LOG2E
    nsa_ok = 2.0 * jnp.maximum(b_slc, b_win) + (t_hi - t_lo) < EXP2_RANGE
    nsa_bounds = jnp.stack([b_slc + t_hi, b_win + t_hi])
    o_nsa = lax.cond(
        nsa_ok,
        lambda: _nsa(nsa_bounds, qa, kcmp, vcmp, ksw, vsw, small, toeplitz, cmp_bias, batch, seq, False),
        lambda: _nsa(nsa_bounds, qa, kcmp, vcmp, ksw, vsw, small, toeplitz, cmp_bias, batch, seq, True))

    cum_cols, cum = _cumsum(small, batch, seq)
    b_fox = _score_bound(fox_q_gain, fox_k_gain)
    fox_bounds = jnp.stack([b_fox])
    o_fox = lax.cond(2.0 * b_fox < EXP2_RANGE,
                     lambda: _fox(fox_bounds, qb, kb, vb, cum, cum_cols, batch, seq, False),
                     lambda: _fox(fox_bounds, qb, kb, vb, cum, cum_cols, batch, seq, True))

    w_o_nsa_p = w_o_nsa.reshape(NSA_KV_GROUPS, NSA_Q_PER_GROUP, HEAD_DIM, D_MODEL).transpose(1, 0, 2, 3)
    out = _merge_ffn(x1, o_nsa, o_fox, mix_norm, w_gm, w_o_nsa_p.reshape(NSA_W, D_MODEL), w_o_fox,
                     w_out, ffn2_norm, ffn2_w_up, ffn2_w_down)
    return out.reshape(batch, seq, D_MODEL)


def kernel(x, ffn1_norm, ffn1_w_up, ffn1_w_down, mix_norm, w_in, b_forget, nsa_q_gain, nsa_k_gain,
           fox_q_gain, fox_k_gain, cmp_pos_k, cmp_pos_v, cmp_k_w1, cmp_k_w2, cmp_v_w1, cmp_v_w2,
           w_o_nsa, w_o_fox, w_out, ffn2_norm, ffn2_w_up, ffn2_w_down, rel_bias_table):
    for layer in range(ffn1_norm.shape[0]):
        x = _layer(x, ffn1_norm[layer], ffn1_w_up[layer], ffn1_w_down[layer], mix_norm[layer],
                   w_in[layer], b_forget[layer], nsa_q_gain[layer], nsa_k_gain[layer],
                   fox_q_gain[layer], fox_k_gain[layer], cmp_pos_k[layer], cmp_pos_v[layer],
                   cmp_k_w1[layer], cmp_k_w2[layer], cmp_v_w1[layer], cmp_v_w2[layer],
                   w_o_nsa[layer], w_o_fox[layer], w_out[layer], ffn2_norm[layer],
                   ffn2_w_up[layer], ffn2_w_down[layer], rel_bias_table)
    return x
```

```python
import functools
import math

import numpy as np
import jax
import jax.numpy as jnp
from jax import lax
from jax.experimental import pallas as pl
from jax.experimental.pallas import tpu as pltpu

F32 = jnp.float32
BF16 = jnp.bfloat16

D_MODEL = 1024
HEAD_DIM = 64
NSA_HEADS = 8
NSA_KV_GROUPS = 2
NSA_Q_PER_GROUP = NSA_HEADS // NSA_KV_GROUPS
CMP_BLOCK = 32
CMP_STRIDE = 16
CMP_HIDDEN = 128
SLC_BLOCK = 64
SLC_TOPK = 8
WINDOW = 512
FOX_HEADS = 8
D_FF = 2816
N_BUCKETS = 32
MAX_DISTANCE = 128
RMS_EPS = 1e-6
NEG_INF = -1.0e30
FORCE_BONUS = 1.0e4

NSA_W = NSA_HEADS * HEAD_DIM
NSA_KV_W = NSA_KV_GROUPS * HEAD_DIM
FOX_W = FOX_HEADS * HEAD_DIM
IN_SPLITS = (NSA_W, NSA_KV_W, NSA_KV_W, NSA_KV_W, NSA_KV_W, NSA_KV_W, NSA_KV_W, 3 * NSA_HEADS,
             FOX_W, FOX_W, FOX_W, FOX_HEADS, 2 * D_MODEL)

LANES = 128
TOKEN_TILE = 512
FFN_CHUNK = D_FF // 2
ATT_TILE = 256
N_WIN_TILES = WINDOW // ATT_TILE + 1
VMEM_LIMIT = 56 * 1024 * 1024
LOG2E = 1.4426950408889634
EXP2_RANGE = 100.0

PQ_A = 0
PQ_B = PQ_A + NSA_W
PK_B = PQ_B + FOX_W
PK_SW = PK_B + FOX_W
P_NORM_END = PK_SW + 2 * NSA_KV_W
PV_B = P_NORM_END
PV_SW = PV_B + FOX_W
PKV_C = PV_SW + 2 * NSA_KV_W
P_SMALL = PKV_C + 2 * NSA_KV_W
P_END = P_SMALL + LANES
NORM_CHUNK = 256


def _dot(a, b):
    return jnp.dot(a, b, preferred_element_type=F32)


def _dot_nt(a, b):
    return lax.dot_general(a, b, (((1,), (1,)), ((), ())), preferred_element_type=F32)


def _split_dot(x, w):
    hi = x.astype(BF16)
    lo = (x - hi.astype(F32)).astype(BF16)
    return _dot(hi, w) + _dot(lo, w)


def _rms_rows(x, gain_row):
    ms = jnp.mean(x * x, axis=-1, keepdims=True)
    return x * lax.rsqrt(ms + RMS_EPS) * gain_row


def _const_spec(shape):
    nd = len(shape)
    return pl.BlockSpec(shape, lambda *_: (0,) * nd, pipeline_mode=pl.Buffered(1))


def _params(n_axes):
    return pltpu.CompilerParams(dimension_semantics=("arbitrary",) * n_axes,
                                vmem_limit_bytes=VMEM_LIMIT)


def _swiglu_residual(x, gain_row, wup_ref, wdn_ref):
    xn = _rms_rows(x, gain_row).astype(BF16)
    acc = jnp.zeros(x.shape, F32)
    for c in range(D_FF // FFN_CHUNK):
        lo = c * FFN_CHUNK
        gate = _dot(xn, wup_ref[:, lo:lo + FFN_CHUNK])
        up = _dot(xn, wup_ref[:, D_FF + lo:D_FF + lo + FFN_CHUNK])
        h = (gate * jax.nn.sigmoid(gate) * up).astype(BF16)
        acc = acc + _dot(h, wdn_ref[lo:lo + FFN_CHUNK, :])
    return x + 0.5 * acc


def _ffn_kernel(x_ref, g_ref, wup_ref, wdn_ref, o_ref):
    o_ref[...] = _swiglu_residual(x_ref[...], g_ref[...], wup_ref, wdn_ref)


def _ffn(x2d, gain, w_up, w_down):
    n = x2d.shape[0]
    tm = 2 * TOKEN_TILE if n % (2 * TOKEN_TILE) == 0 else TOKEN_TILE
    row = pl.BlockSpec((tm, D_MODEL), lambda i: (i, 0))
    return pl.pallas_call(
        _ffn_kernel,
        grid=(n // tm,),
        in_specs=[row, _const_spec((1, D_MODEL)), _const_spec((D_MODEL, 2 * D_FF)),
                  _const_spec((D_FF, D_MODEL))],
        out_specs=row,
        out_shape=jax.ShapeDtypeStruct((n, D_MODEL), F32),
        compiler_params=_params(1),
        name="ffn",
    )(x2d, gain.reshape(1, D_MODEL), w_up.astype(BF16), w_down.astype(BF16))


def _block_ones(width, size=NORM_CHUNK):
    idx = np.arange(size) // width
    return jnp.asarray((idx[:, None] == idx[None, :]).astype(np.float32), BF16)


def _pack_in_proj(w_in, b_forget, nsa_q_gain, nsa_k_gain, fox_q_gain, fox_k_gain):
    scale = LOG2E / math.sqrt(HEAD_DIM)
    pts = np.cumsum(np.array(IN_SPLITS))[:-1].tolist()
    qa, kc, vc, ks, vs, kw, vw, ga, qb, kb, vb, fb, gm = jnp.split(w_in, pts, axis=1)
    cols = []
    for r in range(NSA_Q_PER_GROUP):
        for g in range(NSA_KV_GROUPS):
            h = g * NSA_Q_PER_GROUP + r
            cols.append(qa[:, h * HEAD_DIM:(h + 1) * HEAD_DIM])
    cols += [qb, kb, ks, kw]
    gains = [jnp.tile(nsa_q_gain * scale, NSA_HEADS), jnp.tile(fox_q_gain * scale, FOX_HEADS),
             jnp.tile(fox_k_gain, FOX_HEADS), jnp.tile(nsa_k_gain[1], NSA_KV_GROUPS),
             jnp.tile(nsa_k_gain[2], NSA_KV_GROUPS)]
    n_small_pad = LANES - ga.shape[1] - fb.shape[1]
    cols += [vb, vs, vw, kc, vc, ga, fb, jnp.zeros((D_MODEL, n_small_pad), F32)]
    w_packed = jnp.concatenate(cols, axis=1).astype(BF16)
    gain_row = jnp.concatenate(gains).reshape(1, P_NORM_END)
    small_bias = jnp.concatenate([jnp.zeros((ga.shape[1],), F32), b_forget,
                                  jnp.zeros((n_small_pad,), F32)]).reshape(1, LANES)
    return w_packed, gain_row, small_bias, gm.astype(BF16)


def _in_proj_kernel(x_ref, g_ref, w_ref, gain_ref, sbias_ref, bd64_ref,
                    qa_ref, qb_ref, kb_ref, ksw_ref, vb_ref, vsw_ref, kc_ref, vc_ref, small_ref, kv_scr):
    u = _rms_rows(x_ref[...], g_ref[...]).astype(BF16)
    tm = u.shape[0]

    y = _dot(u, w_ref[:, 0:P_NORM_END])
    n_chunks = P_NORM_END // NORM_CHUNK
    squares = jnp.concatenate([jnp.square(y[:, c * NORM_CHUNK:(c + 1) * NORM_CHUNK]).astype(BF16)
                               for c in range(n_chunks)], axis=0)
    ss = _dot(squares, bd64_ref[...])
    chunk = 0
    for o_ref in (qa_ref, qb_ref, kb_ref, ksw_ref):
        for i in range(o_ref.shape[1] // NORM_CHUNK):
            cols = slice(chunk * NORM_CHUNK, (chunk + 1) * NORM_CHUNK)
            inv = lax.rsqrt(ss[chunk * tm:(chunk + 1) * tm] * (1.0 / HEAD_DIM) + RMS_EPS)
            o_ref[:, i * NORM_CHUNK:(i + 1) * NORM_CHUNK] = (y[:, cols] * inv * gain_ref[:, cols]).astype(BF16)
            chunk += 1
    low = lax.broadcasted_iota(jnp.int32, (u.shape[0], LANES), 1) < HEAD_DIM

    def store_with_ones(v, o_ref):
        for i in range(v.shape[1] // LANES):
            pair = v[:, i * LANES:(i + 1) * LANES]
            o_ref[:, 2 * i * LANES:(2 * i + 1) * LANES] = jnp.where(low, pair, 1.0).astype(BF16)
            o_ref[:, (2 * i + 1) * LANES:(2 * i + 2) * LANES] = jnp.where(low, 1.0, pair).astype(BF16)

    rest = _dot(u, w_ref[:, P_NORM_END:P_END])
    store_with_ones(rest[:, PV_B - P_NORM_END:PV_SW - P_NORM_END], vb_ref)
    store_with_ones(rest[:, PV_SW - P_NORM_END:PKV_C - P_NORM_END], vsw_ref)
    for j, o_ref in enumerate((kc_ref, vc_ref)):
        c0 = PKV_C - P_NORM_END + j * NSA_KV_W
        kv_scr[j] = rest[:, c0:c0 + NSA_KV_W]
        for l in range(CMP_STRIDE):
            o_ref[0, :, l * NSA_KV_W:(l + 1) * NSA_KV_W] = kv_scr[j, pl.ds(l, tm // CMP_STRIDE,
                                                                           stride=CMP_STRIDE), :]
    z = rest[:, P_SMALL - P_NORM_END:] + sbias_ref[...]
    lane = lax.broadcasted_iota(jnp.int32, z.shape, 1)
    log_sig = jnp.minimum(z, 0.0) - jnp.log1p(jnp.exp(-jnp.abs(z)))
    small_ref[...] = jnp.where(lane < 3 * NSA_HEADS, jax.nn.sigmoid(z), log_sig)


def _in_proj(x2d, mix_norm, w_packed, gain_row, small_bias, batch, seq):
    n = x2d.shape[0]
    steps_per_seq = seq // TOKEN_TILE

    def row(width):
        return pl.BlockSpec((TOKEN_TILE, width), lambda i: (i, 0))

    def out(width, dtype):
        return row(width), jax.ShapeDtypeStruct((n, width), dtype)

    grouped = (pl.BlockSpec((1, TOKEN_TILE // CMP_STRIDE, CMP_STRIDE * NSA_KV_W),
                            lambda i: (i // steps_per_seq, i % steps_per_seq, 0)),
               jax.ShapeDtypeStruct((batch, seq // CMP_STRIDE, CMP_STRIDE * NSA_KV_W), F32))
    outs = [out(NSA_W, BF16), out(FOX_W, BF16), out(FOX_W, BF16), out(2 * NSA_KV_W, BF16),
            out(2 * FOX_W, BF16), out(4 * NSA_KV_W, BF16), grouped, grouped, out(LANES, F32)]
    return pl.pallas_call(
        _in_proj_kernel,
        grid=(n // TOKEN_TILE,),
        in_specs=[row(D_MODEL), _const_spec((1, D_MODEL)), _const_spec((D_MODEL, P_END)),
                  _const_spec((1, P_NORM_END)), _const_spec((1, LANES)),
                  _const_spec((NORM_CHUNK, NORM_CHUNK))],
        out_specs=[spec for spec, _ in outs],
        out_shape=[shape for _, shape in outs],
        scratch_shapes=[pltpu.VMEM((2, TOKEN_TILE, NSA_KV_W), F32)],
        compiler_params=_params(1),
        name="in_proj",
    )(x2d, mix_norm.reshape(1, D_MODEL), w_packed, gain_row, small_bias, _block_ones(HEAD_DIM))


def _pack_compress(pos, w1, w2):
    half = CMP_BLOCK // 2
    eye = jnp.eye(NSA_KV_GROUPS, dtype=F32)
    w1r = w1.reshape(CMP_BLOCK, HEAD_DIM, CMP_HIDDEN)

    def big(w):
        return jnp.einsum('ldh,pg->lpdgh', w, eye).reshape(half * NSA_KV_W, NSA_KV_GROUPS * CMP_HIDDEN)

    def posrow(p):
        return jnp.broadcast_to(p[:, None, :], (half, NSA_KV_GROUPS, HEAD_DIM)).reshape(1, half * NSA_KV_W)

    w2big = jnp.einsum('hd,pg->phgd', w2, eye).reshape(NSA_KV_GROUPS * CMP_HIDDEN, NSA_KV_W)
    return (posrow(pos[:half]), posrow(pos[half:]), big(w1r[:half]).astype(BF16),
            big(w1r[half:]).astype(BF16), w2big.astype(BF16))


def _compress_kernel(kc_ref, vc_ref, kp_lo, kp_hi, kw_lo, kw_hi, kw2, vp_lo, vp_hi, vw_lo, vw_hi, vw2,
                     kgain_ref, bd64_ref, ko_ref, vo_ref):
    def mlp(r, p_lo, p_hi, w_lo, w_hi, w2):
        first = _dot((r + p_lo[...]).astype(BF16), w_lo[...])
        second = _dot((r + p_hi[...]).astype(BF16), w_hi[...])
        h = first + pltpu.roll(second, second.shape[0] - 1, axis=0)
        return _dot((h * jax.nn.sigmoid(h)).astype(BF16), w2[...])

    k = mlp(kc_ref[0], kp_lo, kp_hi, kw_lo, kw_hi, kw2)
    ss = _split_dot(k * k, bd64_ref[...])
    ko_ref[0] = (k * lax.rsqrt(ss * (1.0 / HEAD_DIM) + RMS_EPS) * kgain_ref[...]).astype(BF16)
    vo_ref[0] = mlp(vc_ref[0], vp_lo, vp_hi, vw_lo, vw_hi, vw2).astype(BF16)


def _compress(kc, vc, batch, seq, k_pack, v_pack, k_gain):
    rows = seq // (CMP_BLOCK // 2)
    width = (CMP_BLOCK // 2) * NSA_KV_W
    blk = pl.BlockSpec((1, rows, width), lambda b: (b, 0, 0))
    out = pl.BlockSpec((1, rows, NSA_KV_W), lambda b: (b, 0, 0))
    pack_specs = [_const_spec((1, width)), _const_spec((1, width)),
                  _const_spec((width, NSA_KV_GROUPS * CMP_HIDDEN)),
                  _const_spec((width, NSA_KV_GROUPS * CMP_HIDDEN)),
                  _const_spec((NSA_KV_GROUPS * CMP_HIDDEN, NSA_KV_W))]
    bd = _block_ones(HEAD_DIM, NSA_KV_W)
    return pl.pallas_call(
        _compress_kernel,
        grid=(batch,),
        in_specs=[blk, blk] + pack_specs + pack_specs + [_const_spec((1, NSA_KV_W)),
                                                         _const_spec((NSA_KV_W, NSA_KV_W))],
        out_specs=[out, out],
        out_shape=[jax.ShapeDtypeStruct((batch, rows, NSA_KV_W), BF16)] * 2,
        compiler_params=_params(1),
        name="compress",
    )(kc.reshape(batch, rows, width), vc.reshape(batch, rows, width), *k_pack, *v_pack,
      jnp.tile(k_gain, NSA_KV_GROUPS).reshape(1, NSA_KV_W), bd)


FORGET_LANE = 3 * NSA_HEADS


def _cumsum_kernel(x_ref, cols_ref, rows_ref):
    x = x_ref[...]
    row = lax.broadcasted_iota(jnp.int32, x.shape, 0)
    k = 1
    while k < x.shape[0]:
        x = x + jnp.where(row >= k, pltpu.roll(x, k, axis=0), 0.0)
        k *= 2
    cols_ref[...] = x
    rows_ref[0] = x.T[FORGET_LANE:FORGET_LANE + FOX_HEADS]


def _cumsum(small, batch, seq):
    spec = pl.BlockSpec((seq, LANES), lambda b: (b, 0))
    return pl.pallas_call(
        _cumsum_kernel, grid=(batch,), in_specs=[spec],
        out_specs=[spec, pl.BlockSpec((1, FOX_HEADS, seq), lambda b: (b, 0, 0))],
        out_shape=[jax.ShapeDtypeStruct((batch * seq, LANES), F32),
                   jax.ShapeDtypeStruct((batch, FOX_HEADS, seq), F32)],
        compiler_params=_params(1), name="cumsum",
    )(small)


def _write_bias_rows(d, valid, tab_ref, o_ref, rows):
    max_exact = N_BUCKETS // 2
    n = jnp.maximum(d, 0)
    nf = jnp.maximum(n, 1).astype(F32)
    large = max_exact + (jnp.log(nf / max_exact) / math.log(MAX_DISTANCE / max_exact)
                         * (N_BUCKETS - max_exact)).astype(jnp.int32)
    bucket = jnp.where(n < max_exact, n, jnp.minimum(large, N_BUCKETS - 1))
    for h in range(NSA_HEADS):
        acc = jnp.zeros(d.shape, F32)
        for b in range(N_BUCKETS):
            acc = jnp.where(bucket == b, tab_ref[b, h], acc)
        g, r = divmod(h, NSA_Q_PER_GROUP)
        o_ref[0, g, r, rows, :] = jnp.where(valid, acc * LOG2E, NEG_INF)


def _write_far_tile(valid, tab_ref, o_ref):
    for h in range(NSA_HEADS):
        g, r = divmod(h, NSA_Q_PER_GROUP)
        far = jnp.where(valid, tab_ref[N_BUCKETS - 1, h], 0.0) * LOG2E
        o_ref[0, g, r] = jnp.where(valid, far, NEG_INF)


def _toeplitz_bias_kernel(tab_ref, o_ref):
    s = pl.program_id(0)
    tile = ATT_TILE

    def distances(n_rows):
        i = lax.broadcasted_iota(jnp.int32, (n_rows, tile), 0)
        j = lax.broadcasted_iota(jnp.int32, (n_rows, tile), 1)
        d = jnp.minimum(s, N_WIN_TILES - 1) * tile + i - j
        d_hi = jnp.where(s < N_WIN_TILES, WINDOW, jnp.int32(1 << 30))
        return d, (d >= 0) & (d < d_hi)

    _write_far_tile(distances(tile)[1], tab_ref, o_ref)
    for step in range(N_WIN_TILES + 1):
        near = min(tile, max(0, MAX_DISTANCE + tile - 1 - min(step, N_WIN_TILES - 1) * tile))
        near = -(-near // 8) * 8
        if near:
            @pl.when(s == step)
            def _(near=near):
                d, valid = distances(near)
                _write_bias_rows(d, valid, tab_ref, o_ref, slice(0, near))


def _cmp_bias_kernel(tab_ref, o_ref):
    t = pl.program_id(0)
    chunk = 8

    def distances(c0, n_rows):
        c = c0 + lax.broadcasted_iota(jnp.int32, (n_rows, ATT_TILE), 0)
        i = lax.broadcasted_iota(jnp.int32, (n_rows, ATT_TILE), 1)
        d = t * ATT_TILE + i - CMP_STRIDE * c - (CMP_BLOCK - 1)
        return d, (d >= 0) & (c < LANES - 1)

    _write_far_tile(distances(0, LANES)[1], tab_ref, o_ref)
    first = (t * ATT_TILE - (MAX_DISTANCE + CMP_BLOCK - 1)) // (chunk * CMP_STRIDE)
    n_chunks = (ATT_TILE + MAX_DISTANCE) // (chunk * CMP_STRIDE) + 2

    def refine(m, carry):
        c0 = pl.multiple_of(jnp.clip(first + m, 0, LANES // chunk - 1) * chunk, chunk)
        d, valid = distances(c0, chunk)
        _write_bias_rows(d, valid, tab_ref, o_ref, pl.ds(c0, chunk))
        return carry

    lax.fori_loop(0, n_chunks, refine, 0)


def _bias_tables(rel_bias_table, n_q_tiles):
    def call(body, steps, rows, name):
        shape = (steps, NSA_KV_GROUPS, NSA_Q_PER_GROUP, rows, ATT_TILE)
        return pl.pallas_call(
            body, grid=(steps,),
            in_specs=[pl.BlockSpec(memory_space=pltpu.SMEM)],
            out_specs=pl.BlockSpec((1,) + shape[1:], lambda s: (s, 0, 0, 0, 0)),
            out_shape=jax.ShapeDtypeStruct(shape, F32),
            compiler_params=_params(1), name=name,
        )(rel_bias_table)

    return (call(_toeplitz_bias_kernel, N_WIN_TILES + 1, ATT_TILE, "toeplitz_bias"),
            call(_cmp_bias_kernel, n_q_tiles, LANES, "cmp_bias"))


def _score_bound(q_gain, k_gain):
    return 1.02 * LOG2E * math.sqrt(HEAD_DIM) * jnp.max(jnp.abs(q_gain)) * jnp.max(jnp.abs(k_gain))


def _attention(chains, groups, bounds, acc_ref):
    n = len(chains)

    def later(k):
        return tuple(i for _, ids in groups[k:] for i in ids)

    def exact_max():
        def half_max(i, dl):
            s = chains[i][0](dl)
            return jnp.maximum(s[:, :LANES], s[:, LANES:])

        mx = [half_max(i, 0) for i in range(n)]
        lo = 1
        for k, (last, _) in enumerate(groups):
            members = later(k)

            def body(dl, carry, members=members):
                return tuple(jnp.maximum(c, half_max(i, dl)) for c, i in zip(carry, members))

            for i, r in zip(members, lax.fori_loop(lo, last + 1, body, tuple(mx[i] for i in members))):
                mx[i] = r
            lo = last + 1
        return tuple(jnp.broadcast_to(jnp.max(m, axis=-1, keepdims=True), (m.shape[0], ATT_TILE))
                     for m in mx)

    shift = exact_max() if bounds is None else [b() for b in bounds]

    def weights(i, dl):
        return jnp.exp2(chains[i][0](dl) - shift[i]).astype(BF16)

    def product(i, dl):
        return chains[i][1](weights(i, dl), dl)

    for i in range(n):
        acc_ref[i] = product(i, 0)
    lo = 1
    for k, (last, _) in enumerate(groups):
        members = later(k)
        count = last + 1 - lo

        def body(j, carry, members=members, lo=lo):
            dl = lo + 2 * j
            for i in members:
                both = jnp.concatenate([weights(i, dl + 1), weights(i, dl)], axis=1)
                acc_ref[i] += chains[i][1](both, dl, 2)
            return carry

        lax.fori_loop(0, count // 2, body, 0)

        @pl.when(count % 2 == 1)
        def _(members=members, last=last):
            for i in members:
                acc_ref[i] += product(i, last)

        lo = last + 1
    return [acc_ref[i] for i in range(n)]


def _pair_lanes(first, second, normalise):
    low = lax.broadcasted_iota(jnp.int32, first.shape, 1) < HEAD_DIM
    pair = jnp.where(low, first, second)
    if not normalise:
        return pair
    sums = pltpu.roll(jnp.where(low, second, first), HEAD_DIM, axis=1)
    return pair * (1.0 / sums)


def _select_blocks(imp_t, t0, tq):
    blk = lax.broadcasted_iota(jnp.int32, imp_t.shape, 0)
    cur = (t0 + (lax.broadcasted_iota(jnp.int32, imp_t.shape, 1) & (tq - 1))) // SLC_BLOCK
    forced = (blk == 0) | (blk == cur) | (blk == cur - 1)
    score = jnp.where(blk <= cur, imp_t + jnp.where(forced, FORCE_BONUS, 0.0), NEG_INF)
    blk_f = blk.astype(F32)
    dead = -3.0e38
    picked = jnp.zeros(imp_t.shape, F32)
    for _ in range(SLC_TOPK):
        best = jnp.max(score, axis=0, keepdims=True)
        first = jnp.min(jnp.where(score == best, blk_f, float(LANES)), axis=0, keepdims=True)
        hit = blk_f == first
        picked = jnp.where(hit, 1.0, picked)
        score = jnp.where(hit, dead, score)
    return picked


def _nsa_kernel(bnd_ref, qa_ref, kcmp_ref, vcmp_ref, ksw_ref, vsw_ref, small_ref, tb_ref, bc_ref, ov_ref,
                e_ref, gx_ref, o_ref, amask_ref, acc_ref, *, n_blocks, exact):
    qt = pl.program_id(1)
    tq = ATT_TILE
    rq = NSA_Q_PER_GROUP
    lane = lax.broadcasted_iota(jnp.int32, (tq, LANES), 1)
    own_lanes = [jnp.where((lane // HEAD_DIM) == g, 1.0, 0.0).astype(BF16) for g in range(NSA_KV_GROUPS)]

    def key_rows(dl, n=1):
        return pl.ds(pl.multiple_of((qt - dl - (n - 1)) * tq, tq), n * tq)

    groups = range(NSA_KV_GROUPS)
    n_seq = qa_ref.shape[0]
    n_grp = NSA_KV_GROUPS

    def win_chain(b, g, q):
        def score(dl):
            s = _dot_nt(q, ksw_ref[b, key_rows(dl), LANES:2 * LANES]).reshape(rq, tq, tq)
            return (s + tb_ref[dl, g]).reshape(rq * tq, tq)

        def pv(p, dl, n=1):
            c0 = (n_grp + g) * LANES
            return _dot(p, vsw_ref[b, key_rows(dl, n), c0:c0 + LANES])

        return score, pv

    def slc_chain(b, g, q):
        def score(dl):
            tile = dl if isinstance(dl, int) else jnp.where(dl < N_WIN_TILES - 1, dl, N_WIN_TILES)
            rows = key_rows(dl)
            s = _dot_nt(q, ksw_ref[b, rows, 0:LANES]).reshape(rq, tq, tq)
            return (s + tb_ref[tile, g] + amask_ref[b * n_grp + g, :, rows][None]).reshape(rq * tq, tq)

        def pv(p, dl, n=1):
            return _dot(p, vsw_ref[b, key_rows(dl, n), g * LANES:(g + 1) * LANES])

        return score, pv

    overlap = ov_ref[...].astype(BF16)
    key_block = e_ref[...].astype(BF16)
    gate_expand = [gx_ref[br].astype(BF16) for br in range(3)]

    slc_chains, win_chains, o_cmps = [], [], []
    for b in range(n_seq):
        qs = [jnp.concatenate([qa_ref[b, :, r * LANES:(r + 1) * LANES] * own_lanes[g] for r in range(rq)],
                              axis=0) for g in groups]
        slc_chains += [slc_chain(b, g, qs[g]) for g in groups]
        win_chains += [win_chain(b, g, qs[g]) for g in groups]

        s = (_dot_nt(kcmp_ref[b], jnp.concatenate(qs, axis=0))
             + jnp.concatenate([bc_ref[0, g, r] for g in groups for r in range(rq)], axis=1))
        m = jnp.max(s, axis=0, keepdims=True)
        p = jnp.where(s > 0.5 * NEG_INF, jnp.exp2(s - m), 0.0)
        l = jnp.sum(p, axis=0, keepdims=True)
        p_c = p * (1.0 / jnp.where(l > 0.0, l, 1.0))
        o_cmp = _dot(p_c.T.astype(BF16), vcmp_ref[b])
        o_cmps.append([o_cmp[g * rq * tq:(g + 1) * rq * tq] for g in groups])

        p_sum = jnp.concatenate([sum(p_c[:, (g * rq + r) * tq:(g * rq + r + 1) * tq] for r in range(rq))
                                 for g in groups], axis=1)
        hi = p_sum.astype(BF16)
        lo = (p_sum - hi.astype(F32)).astype(BF16)
        imp_t = _dot(overlap, hi) + _dot(overlap, lo)
        sel_t = _select_blocks(imp_t[0:n_blocks], qt * tq, tq)
        block_bias = jnp.concatenate([jnp.where(sel_t > 0.0, 0.0, NEG_INF),
                                      jnp.zeros((LANES - n_blocks, n_grp * tq), F32)], axis=0)
        amask_ref[b * n_grp:(b + 1) * n_grp] = _dot(block_bias.T.astype(BF16), key_block).reshape(
            n_grp, tq, e_ref.shape[1])

    n_slc = len(slc_chains)
    slc_ids = tuple(range(n_slc))
    win_ids = tuple(range(n_slc, 2 * n_slc))
    bounds = None if exact else [lambda i=i: jnp.full((rq * tq, tq), bnd_ref[i // n_slc], F32)
                                 for i in range(2 * n_slc)]
    accs = _attention(slc_chains + win_chains,
                      [(jnp.minimum(qt, N_WIN_TILES - 1), win_ids), (qt, slc_ids)], bounds, acc_ref)

    for b in range(n_seq):
        gates = small_ref[b]
        g_hi = gates.astype(BF16)
        g_lo = (gates - g_hi.astype(F32)).astype(BF16)
        out = [jnp.zeros((tq, LANES), F32) for _ in range(rq)]
        lo_, hi_ = b * n_grp, (b + 1) * n_grp
        for br, per_group in enumerate((o_cmps[b], accs[lo_:hi_], accs[n_slc + lo_:n_slc + hi_])):
            gate = _dot(g_hi, gate_expand[br]) + _dot(g_lo, gate_expand[br])
            for r in range(rq):
                first, second = (a[r * tq:(r + 1) * tq] for a in per_group)
                pair = _pair_lanes(first, second, normalise=br > 0)
                out[r] = out[r] + gate[:, r * LANES:(r + 1) * LANES] * pair
        for r in range(rq):
            o_ref[b, :, r * LANES:(r + 1) * LANES] = out[r].astype(BF16)


def _nsa(bounds, qa, kcmp, vcmp, ksw, vsw, small, toeplitz, cmp_bias, batch, seq, exact):
    tq = ATT_TILE
    nq = seq // tq
    rq = NSA_Q_PER_GROUP
    n_blocks = seq // SLC_BLOCK
    n_cmp = (seq - CMP_BLOCK) // CMP_STRIDE + 1
    ci = np.arange(LANES)[:, None] * CMP_STRIDE
    sj = np.arange(LANES)[None, :] * SLC_BLOCK
    overlap = ((ci <= sj + SLC_BLOCK - 1) & (ci + CMP_BLOCK - 1 >= sj)
               & (np.arange(LANES)[:, None] < n_cmp) & (np.arange(LANES)[None, :] < n_blocks))
    expand = np.arange(LANES)[:, None] == (np.arange(seq)[None, :] // SLC_BLOCK)
    gate_expand = np.zeros((3, LANES, NSA_W), np.float32)
    for b in range(3):
        for g in range(NSA_KV_GROUPS):
            for r in range(rq):
                c0 = r * LANES + g * HEAD_DIM
                gate_expand[b, b * NSA_HEADS + g * rq + r, c0:c0 + HEAD_DIM] = 1.0
    per_step = 2 if batch % 2 == 0 else 1

    def rows(n_rows, width):
        return pl.BlockSpec((per_step, n_rows, width), lambda b, t: (b, 0, 0))

    def tile(width):
        return pl.BlockSpec((per_step, tq, width), lambda b, t: (b, t, 0))

    out = pl.pallas_call(
        functools.partial(_nsa_kernel, n_blocks=n_blocks, exact=exact),
        grid=(batch // per_step, nq),
        in_specs=[
            pl.BlockSpec(memory_space=pltpu.SMEM),
            tile(NSA_W),
            rows(LANES, NSA_KV_W),
            rows(LANES, NSA_KV_W),
            rows(seq, 2 * NSA_KV_W),
            rows(seq, 2 * NSA_KV_GROUPS * LANES),
            tile(LANES),
            _const_spec((N_WIN_TILES + 1, NSA_KV_GROUPS, rq, tq, tq)),
            pl.BlockSpec((1, NSA_KV_GROUPS, rq, LANES, tq), lambda b, t: (t, 0, 0, 0, 0)),
            _const_spec((LANES, LANES)),
            _const_spec((LANES, seq)),
            _const_spec((3, LANES, NSA_W)),
        ],
        out_specs=tile(NSA_W),
        out_shape=jax.ShapeDtypeStruct((batch, seq, NSA_W), BF16),
        scratch_shapes=[pltpu.VMEM((per_step * NSA_KV_GROUPS, tq, seq), F32),
                        pltpu.VMEM((per_step * 2 * NSA_KV_GROUPS, rq * tq, LANES), F32)],
        compiler_params=_params(2),
        name="nsa",
    )(bounds, qa.reshape(batch, seq, NSA_W), kcmp, vcmp, ksw.reshape(batch, seq, 2 * NSA_KV_W),
      vsw.reshape(batch, seq, 2 * NSA_KV_GROUPS * LANES), small.reshape(batch, seq, LANES),
      toeplitz, cmp_bias,
      jnp.asarray(overlap.T, F32), jnp.asarray(expand, F32), jnp.asarray(gate_expand, F32))
    return out.reshape(batch * seq, NSA_W)


def _fox_kernel(bnd_ref, q_ref, k_ref, v_ref, cum_ref, cum_t_ref, o_ref, acc_ref, *, exact):
    qt = pl.program_id(1)
    tq = ATT_TILE
    pairs = FOX_HEADS // 2
    row = lax.broadcasted_iota(jnp.int32, (2, tq, tq), 1)
    col = lax.broadcasted_iota(jnp.int32, (2, tq, tq), 2)
    lane = lax.broadcasted_iota(jnp.int32, (tq, LANES), 1)
    low = jnp.where(lane < HEAD_DIM, 1.0, 0.0).astype(BF16)
    high = jnp.where(lane < HEAD_DIM, 0.0, 1.0).astype(BF16)
    n_seq = q_ref.shape[0]

    def key_rows(dl, n=1):
        return pl.ds(pl.multiple_of((qt - dl - (n - 1)) * tq, tq), n * tq)

    bounds = []

    def pair_chain(b, p):
        pair = q_ref[b, :, p * LANES:(p + 1) * LANES]
        q = jnp.concatenate([pair * low, pair * high], axis=0)
        base = cum_ref[b, 2 * p:2 * p + 2, pl.ds(pl.multiple_of(qt * tq, tq), LANES)][:, 0:1]

        def decay_of(rows):
            return (base - cum_ref[b, 2 * p:2 * p + 2, rows]) * LOG2E

        def bound():
            cum_rows = cum_t_ref[b]
            c0 = FORGET_LANE + 2 * p
            own = jnp.concatenate([base[h:h + 1, :] - cum_rows[:, c0 + h:c0 + h + 1] for h in range(2)],
                                  axis=0)
            return jnp.broadcast_to(own * LOG2E + bnd_ref[0], (2 * tq, tq))

        bounds.append(bound)

        def score(dl):
            rows = key_rows(dl)
            decay = decay_of(rows)
            s = _dot_nt(q, k_ref[b, rows, p * LANES:(p + 1) * LANES]).reshape(2, tq, tq) + decay[:, None, :]
            if isinstance(dl, int):
                s = jnp.where(col <= row, s, NEG_INF)
            return s.reshape(2 * tq, tq)

        def pv(w, dl, n=1):
            rows = key_rows(dl, n)
            return jnp.concatenate(
                [_dot(w[0:tq], v_ref[b, rows, 2 * p * LANES:(2 * p + 1) * LANES]),
                 _dot(w[tq:2 * tq], v_ref[b, rows, (2 * p + 1) * LANES:(2 * p + 2) * LANES])], axis=0)

        return score, pv

    chains = [pair_chain(b, p) for b in range(n_seq) for p in range(pairs)]
    accs = _attention(chains, [(qt, tuple(range(len(chains))))], None if exact else bounds, acc_ref)
    for i, acc in enumerate(accs):
        b, p = divmod(i, pairs)
        o_ref[b, :, p * LANES:(p + 1) * LANES] = _pair_lanes(acc[0:tq], acc[tq:2 * tq], True).astype(BF16)


def _fox(bounds, qb, kb, vb, cum, cum_cols, batch, seq, exact):
    tq = ATT_TILE
    nq = seq // tq
    per_step = 2 if batch % 2 == 0 else 1
    out = pl.pallas_call(
        functools.partial(_fox_kernel, exact=exact),
        grid=(batch // per_step, nq),
        in_specs=[
            pl.BlockSpec(memory_space=pltpu.SMEM),
            pl.BlockSpec((per_step, tq, FOX_W), lambda b, t: (b, t, 0)),
            pl.BlockSpec((per_step, seq, FOX_W), lambda b, t: (b, 0, 0)),
            pl.BlockSpec((per_step, seq, FOX_HEADS * LANES), lambda b, t: (b, 0, 0)),
            pl.BlockSpec((per_step, FOX_HEADS, seq), lambda b, t: (b, 0, 0)),
            pl.BlockSpec((per_step, tq, LANES), lambda b, t: (b, t, 0)),
        ],
        out_specs=pl.BlockSpec((per_step, tq, FOX_W), lambda b, t: (b, t, 0)),
        out_shape=jax.ShapeDtypeStruct((batch, seq, FOX_W), BF16),
        scratch_shapes=[pltpu.VMEM((per_step * FOX_HEADS // 2, 2 * tq, LANES), F32)],
        compiler_params=_params(2),
        name="fox",
    )(bounds, qb.reshape(batch, seq, FOX_W), kb.reshape(batch, seq, FOX_W),
      vb.reshape(batch, seq, FOX_HEADS * LANES), cum, cum_cols.reshape(batch, seq, LANES))
    return out.reshape(batch * seq, FOX_W)


def _merge_ffn_kernel(x_ref, on_ref, of_ref, gmix_ref, wgm_ref, won_ref, wof_ref, wout_ref,
                      g2_ref, wup_ref, wdn_ref, o_ref):
    x = x_ref[...]
    u = _rms_rows(x, gmix_ref[...]).astype(BF16)
    gate = jax.nn.sigmoid(_dot(u, wgm_ref[...]))
    merged = (gate[:, :D_MODEL] * _dot(on_ref[...], won_ref[...])
              + gate[:, D_MODEL:] * _dot(of_ref[...], wof_ref[...]))
    x2 = x + _dot(merged.astype(BF16), wout_ref[...])
    o_ref[...] = _swiglu_residual(x2, g2_ref[...], wup_ref, wdn_ref)


def _merge_ffn(x2d, o_nsa, o_fox, mix_norm, w_gm, w_o_nsa, w_o_fox, w_out, gain2, w_up, w_down):
    n = x2d.shape[0]
    tm = TOKEN_TILE

    def row(width):
        return pl.BlockSpec((tm, width), lambda i: (i, 0))

    return pl.pallas_call(
        _merge_ffn_kernel,
        grid=(n // tm,),
        in_specs=[row(D_MODEL), row(NSA_W), row(FOX_W), _const_spec((1, D_MODEL)),
                  _const_spec((D_MODEL, 2 * D_MODEL)), _const_spec((NSA_W, D_MODEL)),
                  _const_spec((FOX_W, D_MODEL)), _const_spec((D_MODEL, D_MODEL)),
                  _const_spec((1, D_MODEL)), _const_spec((D_MODEL, 2 * D_FF)),
                  _const_spec((D_FF, D_MODEL))],
        out_specs=row(D_MODEL),
        out_shape=jax.ShapeDtypeStruct((n, D_MODEL), F32),
        compiler_params=_params(1),
        name="merge_ffn",
    )(x2d, o_nsa, o_fox, mix_norm.reshape(1, D_MODEL), w_gm, w_o_nsa.astype(BF16),
      w_o_fox.astype(BF16), w_out.astype(BF16), gain2.reshape(1, D_MODEL), w_up.astype(BF16),
      w_down.astype(BF16))


def _layer(x, ffn1_norm, ffn1_w_up, ffn1_w_down, mix_norm, w_in, b_forget, nsa_q_gain, nsa_k_gain,
           fox_q_gain, fox_k_gain, cmp_pos_k, cmp_pos_v, cmp_k_w1, cmp_k_w2, cmp_v_w1, cmp_v_w2,
           w_o_nsa, w_o_fox, w_out, ffn2_norm, ffn2_w_up, ffn2_w_down, rel_bias_table):
    batch, seq, d = x.shape
    assert d == D_MODEL and seq % ATT_TILE == 0 and (batch * seq) % TOKEN_TILE == 0
    assert seq // SLC_BLOCK <= LANES and (seq - CMP_BLOCK) // CMP_STRIDE + 1 == LANES - 1
    assert seq // (CMP_BLOCK // 2) == LANES
    x2d = x.reshape(batch * seq, D_MODEL)

    x1 = _ffn(x2d, ffn1_norm, ffn1_w_up, ffn1_w_down)

    w_packed, gain_row, small_bias, w_gm = _pack_in_proj(w_in, b_forget, nsa_q_gain, nsa_k_gain,
                                                         fox_q_gain, fox_k_gain)
    qa, qb, kb, ksw, vb, vsw, kc, vc, small = _in_proj(x1, mix_norm, w_packed, gain_row, small_bias,
                                                       batch, seq)

    kcmp, vcmp = _compress(kc, vc, batch, seq, _pack_compress(cmp_pos_k, cmp_k_w1, cmp_k_w2),
                           _pack_compress(cmp_pos_v, cmp_v_w1, cmp_v_w2), nsa_k_gain[0])
    toeplitz, cmp_bias = _bias_tables(rel_bias_table, seq // ATT_TILE)
    b_slc = _score_bound(nsa_q_gain, nsa_k_gain[1])
    b_win = _score_bound(nsa_q_gain, nsa_k_gain[2])
    t_hi, t_lo = jnp.max(rel_bias_table) * LOG2E, jnp.min(rel_bias_table) * LOG2E
    nsa_ok = 2.0 * jnp.maximum(b_slc, b_win) + (t_hi - t_lo) < EXP2_RANGE
    nsa_bounds = jnp.stack([b_slc + t_hi, b_win + t_hi])
    o_nsa = lax.cond(
        nsa_ok,
        lambda: _nsa(nsa_bounds, qa, kcmp, vcmp, ksw, vsw, small, toeplitz, cmp_bias, batch, seq, False),
        lambda: _nsa(nsa_bounds, qa, kcmp, vcmp, ksw, vsw, small, toeplitz, cmp_bias, batch, seq, True))

    cum_cols, cum = _cumsum(small, batch, seq)
    b_fox = _score_bound(fox_q_gain, fox_k_gain)
    fox_bounds = jnp.stack([b_fox])
    o_fox = lax.cond(2.0 * b_fox < EXP2_RANGE,
                     lambda: _fox(fox_bounds, qb, kb, vb, cum, cum_cols, batch, seq, False),
                     lambda: _fox(fox_bounds, qb, kb, vb, cum, cum_cols, batch, seq, True))

    w_o_nsa_p = w_o_nsa.reshape(NSA_KV_GROUPS, NSA_Q_PER_GROUP, HEAD_DIM, D_MODEL).transpose(1, 0, 2, 3)
    out = _merge_ffn(x1, o_nsa, o_fox, mix_norm, w_gm, w_o_nsa_p.reshape(NSA_W, D_MODEL), w_o_fox,
                     w_out, ffn2_norm, ffn2_w_up, ffn2_w_down)
    return out.reshape(batch, seq, D_MODEL)


def kernel(x, ffn1_norm, ffn1_w_up, ffn1_w_down, mix_norm, w_in, b_forget, nsa_q_gain, nsa_k_gain,
           fox_q_gain, fox_k_gain, cmp_pos_k, cmp_pos_v, cmp_k_w1, cmp_k_w2, cmp_v_w1, cmp_v_w2,
           w_o_nsa, w_o_fox, w_out, ffn2_norm, ffn2_w_up, ffn2_w_down, rel_bias_table):
    for layer in range(ffn1_norm.shape[0]):
        x = _layer(x, ffn1_norm[layer], ffn1_w_up[layer], ffn1_w_down[layer], mix_norm[layer],
                   w_in[layer], b_forget[layer], nsa_q_gain[layer], nsa_k_gain[layer],
                   fox_q_gain[layer], fox_k_gain[layer], cmp_pos_k[layer], cmp_pos_v[layer],
                   cmp_k_w1[layer], cmp_k_w2[layer], cmp_v_w1[layer], cmp_v_w2[layer],
                   w_o_nsa[layer], w_o_fox[layer], w_out[layer], ffn2_norm[layer],
                   ffn2_w_up[layer], ffn2_w_down[layer], rel_bias_table)
    return x
```

```python
import functools
import math

import numpy as np
import jax
import jax.numpy as jnp
from jax import lax
from jax.experimental import pallas as pl
from jax.experimental.pallas import tpu as pltpu

F32 = jnp.float32
BF16 = jnp.bfloat16

D_MODEL = 1024
HEAD_DIM = 64
NSA_HEADS = 8
NSA_KV_GROUPS = 2
NSA_Q_PER_GROUP = NSA_HEADS // NSA_KV_GROUPS
CMP_BLOCK = 32
CMP_STRIDE = 16
CMP_HIDDEN = 128
SLC_BLOCK = 64
SLC_TOPK = 8
WINDOW = 512
FOX_HEADS = 8
D_FF = 2816
N_BUCKETS = 32
MAX_DISTANCE = 128
RMS_EPS = 1e-6
NEG_INF = -1.0e30
FORCE_BONUS = 1.0e4

NSA_W = NSA_HEADS * HEAD_DIM
NSA_KV_W = NSA_KV_GROUPS * HEAD_DIM
FOX_W = FOX_HEADS * HEAD_DIM
IN_SPLITS = (NSA_W, NSA_KV_W, NSA_KV_W, NSA_KV_W, NSA_KV_W, NSA_KV_W, NSA_KV_W, 3 * NSA_HEADS,
             FOX_W, FOX_W, FOX_W, FOX_HEADS, 2 * D_MODEL)

LANES = 128
TOKEN_TILE = 512
FFN_CHUNK = D_FF // 2
ATT_TILE = 256
N_WIN_TILES = WINDOW // ATT_TILE + 1
VMEM_LIMIT = 56 * 1024 * 1024
LOG2E = 1.4426950408889634
EXP2_RANGE = 100.0

PQ_A = 0
PQ_B = PQ_A + NSA_W
PK_B = PQ_B + FOX_W
PK_SW = PK_B + FOX_W
P_NORM_END = PK_SW + 2 * NSA_KV_W
PV_B = P_NORM_END
PV_SW = PV_B + FOX_W
PKV_C = PV_SW + 2 * NSA_KV_W
P_SMALL = PKV_C + 2 * NSA_KV_W
P_END = P_SMALL + LANES
NORM_CHUNK = 256


def _dot(a, b):
    return jnp.dot(a, b, preferred_element_type=F32)


def _dot_nt(a, b):
    return lax.dot_general(a, b, (((1,), (1,)), ((), ())), preferred_element_type=F32)


def _split_dot(x, w):
    hi = x.astype(BF16)
    lo = (x - hi.astype(F32)).astype(BF16)
    return _dot(hi, w) + _dot(lo, w)


def _rms_rows(x, gain_row):
    ms = jnp.mean(x * x, axis=-1, keepdims=True)
    return x * lax.rsqrt(ms + RMS_EPS) * gain_row


def _const_spec(shape):
    nd = len(shape)
    return pl.BlockSpec(shape, lambda *_: (0,) * nd, pipeline_mode=pl.Buffered(1))


def _token_tile(n_rows, wide):
    tile = 2 * TOKEN_TILE if wide and n_rows % (2 * TOKEN_TILE) == 0 else TOKEN_TILE
    assert n_rows % tile == 0
    return tile


def _params(n_axes):
    return pltpu.CompilerParams(dimension_semantics=("arbitrary",) * n_axes,
                                vmem_limit_bytes=VMEM_LIMIT)


def _swiglu_residual(x, gain_row, wup_ref, wdn_ref):
    xn = _rms_rows(x, gain_row).astype(BF16)
    acc = jnp.zeros(x.shape, F32)
    for c in range(D_FF // FFN_CHUNK):
        lo = c * FFN_CHUNK
        gate = _dot(xn, wup_ref[:, lo:lo + FFN_CHUNK])
        up = _dot(xn, wup_ref[:, D_FF + lo:D_FF + lo + FFN_CHUNK])
        h = (gate * jax.nn.sigmoid(gate) * up).astype(BF16)
        acc = acc + _dot(h, wdn_ref[lo:lo + FFN_CHUNK, :])
    return x + 0.5 * acc


def _ffn_kernel(x_ref, g_ref, wup_ref, wdn_ref, o_ref):
    o_ref[...] = _swiglu_residual(x_ref[...], g_ref[...], wup_ref, wdn_ref)


def _ffn(x2d, gain, w_up, w_down):
    n = x2d.shape[0]
    tm = _token_tile(n, wide=True)
    row = pl.BlockSpec((tm, D_MODEL), lambda i: (i, 0))
    return pl.pallas_call(
        _ffn_kernel,
        grid=(n // tm,),
        in_specs=[row, _const_spec((1, D_MODEL)), _const_spec((D_MODEL, 2 * D_FF)),
                  _const_spec((D_FF, D_MODEL))],
        out_specs=row,
        out_shape=jax.ShapeDtypeStruct((n, D_MODEL), F32),
        compiler_params=_params(1),
        name="ffn",
    )(x2d, gain.reshape(1, D_MODEL), w_up.astype(BF16), w_down.astype(BF16))


def _block_ones(width, size=NORM_CHUNK):
    idx = np.arange(size) // width
    return jnp.asarray((idx[:, None] == idx[None, :]).astype(np.float32), BF16)


def _pack_in_proj(w_in, b_forget, nsa_q_gain, nsa_k_gain, fox_q_gain, fox_k_gain):
    scale = LOG2E / math.sqrt(HEAD_DIM)
    pts = np.cumsum(np.array(IN_SPLITS))[:-1].tolist()
    qa, kc, vc, ks, vs, kw, vw, ga, qb, kb, vb, fb, gm = jnp.split(w_in, pts, axis=1)
    cols = []
    for r in range(NSA_Q_PER_GROUP):
        for g in range(NSA_KV_GROUPS):
            h = g * NSA_Q_PER_GROUP + r
            cols.append(qa[:, h * HEAD_DIM:(h + 1) * HEAD_DIM])
    cols += [qb, kb, ks, kw]
    gains = [jnp.tile(nsa_q_gain * scale, NSA_HEADS), jnp.tile(fox_q_gain * scale, FOX_HEADS),
             jnp.tile(fox_k_gain, FOX_HEADS), jnp.tile(nsa_k_gain[1], NSA_KV_GROUPS),
             jnp.tile(nsa_k_gain[2], NSA_KV_GROUPS)]
    n_small_pad = LANES - ga.shape[1] - fb.shape[1]
    cols += [vb, vs, vw, kc, vc, ga, fb, jnp.zeros((D_MODEL, n_small_pad), F32)]
    w_packed = jnp.concatenate(cols, axis=1).astype(BF16)
    gain_row = jnp.concatenate(gains).reshape(1, P_NORM_END)
    small_bias = jnp.concatenate([jnp.zeros((ga.shape[1],), F32), b_forget,
                                  jnp.zeros((n_small_pad,), F32)]).reshape(1, LANES)
    return w_packed, gain_row, small_bias, gm.astype(BF16)


def _in_proj_kernel(x_ref, g_ref, w_ref, gain_ref, sbias_ref, bd64_ref,
                    qa_ref, qb_ref, kb_ref, ksw_ref, vb_ref, vsw_ref, kc_ref, vc_ref, small_ref, kv_scr):
    u = _rms_rows(x_ref[...], g_ref[...]).astype(BF16)
    tm = u.shape[0]

    y = _dot(u, w_ref[:, 0:P_NORM_END])
    n_chunks = P_NORM_END // NORM_CHUNK
    squares = jnp.concatenate([jnp.square(y[:, c * NORM_CHUNK:(c + 1) * NORM_CHUNK]).astype(BF16)
                               for c in range(n_chunks)], axis=0)
    ss = _dot(squares, bd64_ref[...])
    chunk = 0
    for o_ref in (qa_ref, qb_ref, kb_ref, ksw_ref):
        for i in range(o_ref.shape[1] // NORM_CHUNK):
            cols = slice(chunk * NORM_CHUNK, (chunk + 1) * NORM_CHUNK)
            inv = lax.rsqrt(ss[chunk * tm:(chunk + 1) * tm] * (1.0 / HEAD_DIM) + RMS_EPS)
            o_ref[:, i * NORM_CHUNK:(i + 1) * NORM_CHUNK] = (y[:, cols] * inv * gain_ref[:, cols]).astype(BF16)
            chunk += 1
    low = lax.broadcasted_iota(jnp.int32, (u.shape[0], LANES), 1) < HEAD_DIM

    def store_with_ones(v, o_ref):
        for i in range(v.shape[1] // LANES):
            pair = v[:, i * LANES:(i + 1) * LANES]
            o_ref[:, 2 * i * LANES:(2 * i + 1) * LANES] = jnp.where(low, pair, 1.0).astype(BF16)
            o_ref[:, (2 * i + 1) * LANES:(2 * i + 2) * LANES] = jnp.where(low, 1.0, pair).astype(BF16)

    rest = _dot(u, w_ref[:, P_NORM_END:P_END])
    store_with_ones(rest[:, PV_B - P_NORM_END:PV_SW - P_NORM_END], vb_ref)
    store_with_ones(rest[:, PV_SW - P_NORM_END:PKV_C - P_NORM_END], vsw_ref)
    for j, o_ref in enumerate((kc_ref, vc_ref)):
        c0 = PKV_C - P_NORM_END + j * NSA_KV_W
        kv_scr[j] = rest[:, c0:c0 + NSA_KV_W]
        for l in range(CMP_STRIDE):
            o_ref[0, :, l * NSA_KV_W:(l + 1) * NSA_KV_W] = kv_scr[j, pl.ds(l, tm // CMP_STRIDE,
                                                                           stride=CMP_STRIDE), :]
    z = rest[:, P_SMALL - P_NORM_END:] + sbias_ref[...]
    lane = lax.broadcasted_iota(jnp.int32, z.shape, 1)
    log_sig = jnp.minimum(z, 0.0) - jnp.log1p(jnp.exp(-jnp.abs(z)))
    small_ref[...] = jnp.where(lane < 3 * NSA_HEADS, jax.nn.sigmoid(z), log_sig)


def _in_proj(x2d, mix_norm, w_packed, gain_row, small_bias, batch, seq):
    n = x2d.shape[0]
    tm = _token_tile(seq, wide=True)
    steps_per_seq = seq // tm

    def row(width):
        return pl.BlockSpec((tm, width), lambda i: (i, 0))

    def out(width, dtype):
        return row(width), jax.ShapeDtypeStruct((n, width), dtype)

    grouped = (pl.BlockSpec((1, tm // CMP_STRIDE, CMP_STRIDE * NSA_KV_W),
                            lambda i: (i // steps_per_seq, i % steps_per_seq, 0)),
               jax.ShapeDtypeStruct((batch, seq // CMP_STRIDE, CMP_STRIDE * NSA_KV_W), F32))
    outs = [out(NSA_W, BF16), out(FOX_W, BF16), out(FOX_W, BF16), out(2 * NSA_KV_W, BF16),
            out(2 * FOX_W, BF16), out(4 * NSA_KV_W, BF16), grouped, grouped, out(LANES, F32)]
    return pl.pallas_call(
        _in_proj_kernel,
        grid=(n // tm,),
        in_specs=[row(D_MODEL), _const_spec((1, D_MODEL)), _const_spec((D_MODEL, P_END)),
                  _const_spec((1, P_NORM_END)), _const_spec((1, LANES)),
                  _const_spec((NORM_CHUNK, NORM_CHUNK))],
        out_specs=[spec for spec, _ in outs],
        out_shape=[shape for _, shape in outs],
        scratch_shapes=[pltpu.VMEM((2, tm, NSA_KV_W), F32)],
        compiler_params=_params(1),
        name="in_proj",
    )(x2d, mix_norm.reshape(1, D_MODEL), w_packed, gain_row, small_bias, _block_ones(HEAD_DIM))


def _pack_compress(pos, w1, w2):
    half = CMP_BLOCK // 2
    eye = jnp.eye(NSA_KV_GROUPS, dtype=F32)
    w1r = w1.reshape(CMP_BLOCK, HEAD_DIM, CMP_HIDDEN)

    def big(w):
        return jnp.einsum('ldh,pg->lpdgh', w, eye).reshape(half * NSA_KV_W, NSA_KV_GROUPS * CMP_HIDDEN)

    def posrow(p):
        return jnp.broadcast_to(p[:, None, :], (half, NSA_KV_GROUPS, HEAD_DIM)).reshape(1, half * NSA_KV_W)

    w2big = jnp.einsum('hd,pg->phgd', w2, eye).reshape(NSA_KV_GROUPS * CMP_HIDDEN, NSA_KV_W)
    return (posrow(pos[:half]), posrow(pos[half:]), big(w1r[:half]).astype(BF16),
            big(w1r[half:]).astype(BF16), w2big.astype(BF16))


def _compress_kernel(kc_ref, vc_ref, kp_lo, kp_hi, kw_lo, kw_hi, kw2, vp_lo, vp_hi, vw_lo, vw_hi, vw2,
                     kgain_ref, bd64_ref, ko_ref, vo_ref):
    def mlp(r, p_lo, p_hi, w_lo, w_hi, w2):
        first = _dot((r + p_lo[...]).astype(BF16), w_lo[...])
        second = _dot((r + p_hi[...]).astype(BF16), w_hi[...])
        h = first + pltpu.roll(second, second.shape[0] - 1, axis=0)
        return _dot((h * jax.nn.sigmoid(h)).astype(BF16), w2[...])

    k = mlp(kc_ref[0], kp_lo, kp_hi, kw_lo, kw_hi, kw2)
    ss = _split_dot(k * k, bd64_ref[...])
    ko_ref[0] = (k * lax.rsqrt(ss * (1.0 / HEAD_DIM) + RMS_EPS) * kgain_ref[...]).astype(BF16)
    vo_ref[0] = mlp(vc_ref[0], vp_lo, vp_hi, vw_lo, vw_hi, vw2).astype(BF16)


def _compress(kc, vc, batch, seq, k_pack, v_pack, k_gain):
    rows = seq // (CMP_BLOCK // 2)
    width = (CMP_BLOCK // 2) * NSA_KV_W
    blk = pl.BlockSpec((1, rows, width), lambda b: (b, 0, 0))
    out = pl.BlockSpec((1, rows, NSA_KV_W), lambda b: (b, 0, 0))
    pack_specs = [_const_spec((1, width)), _const_spec((1, width)),
                  _const_spec((width, NSA_KV_GROUPS * CMP_HIDDEN)),
                  _const_spec((width, NSA_KV_GROUPS * CMP_HIDDEN)),
                  _const_spec((NSA_KV_GROUPS * CMP_HIDDEN, NSA_KV_W))]
    bd = _block_ones(HEAD_DIM, NSA_KV_W)
    return pl.pallas_call(
        _compress_kernel,
        grid=(batch,),
        in_specs=[blk, blk] + pack_specs + pack_specs + [_const_spec((1, NSA_KV_W)),
                                                         _const_spec((NSA_KV_W, NSA_KV_W))],
        out_specs=[out, out],
        out_shape=[jax.ShapeDtypeStruct((batch, rows, NSA_KV_W), BF16)] * 2,
        compiler_params=_params(1),
        name="compress",
    )(kc.reshape(batch, rows, width), vc.reshape(batch, rows, width), *k_pack, *v_pack,
      jnp.tile(k_gain, NSA_KV_GROUPS).reshape(1, NSA_KV_W), bd)


FORGET_LANE = 3 * NSA_HEADS


def _cumsum_kernel(x_ref, cols_ref, rows_ref):
    x = x_ref[...]
    row = lax.broadcasted_iota(jnp.int32, x.shape, 0)
    k = 1
    while k < x.shape[0]:
        x = x + jnp.where(row >= k, pltpu.roll(x, k, axis=0), 0.0)
        k *= 2
    cols_ref[...] = x
    rows_ref[0] = x.T[FORGET_LANE:FORGET_LANE + FOX_HEADS]


def _cumsum(small, batch, seq):
    spec = pl.BlockSpec((seq, LANES), lambda b: (b, 0))
    return pl.pallas_call(
        _cumsum_kernel, grid=(batch,), in_specs=[spec],
        out_specs=[spec, pl.BlockSpec((1, FOX_HEADS, seq), lambda b: (b, 0, 0))],
        out_shape=[jax.ShapeDtypeStruct((batch * seq, LANES), F32),
                   jax.ShapeDtypeStruct((batch, FOX_HEADS, seq), F32)],
        compiler_params=_params(1), name="cumsum",
    )(small)


def _write_bias_rows(d, valid, tab_ref, o_ref, rows):
    max_exact = N_BUCKETS // 2
    n = jnp.maximum(d, 0)
    nf = jnp.maximum(n, 1).astype(F32)
    large = max_exact + (jnp.log(nf / max_exact) / math.log(MAX_DISTANCE / max_exact)
                         * (N_BUCKETS - max_exact)).astype(jnp.int32)
    bucket = jnp.where(n < max_exact, n, jnp.minimum(large, N_BUCKETS - 1))
    for h in range(NSA_HEADS):
        acc = jnp.zeros(d.shape, F32)
        for b in range(N_BUCKETS):
            acc = jnp.where(bucket == b, tab_ref[b, h], acc)
        g, r = divmod(h, NSA_Q_PER_GROUP)
        o_ref[0, g, r, rows, :] = jnp.where(valid, acc * LOG2E, NEG_INF)


def _write_far_tile(valid, tab_ref, o_ref):
    for h in range(NSA_HEADS):
        g, r = divmod(h, NSA_Q_PER_GROUP)
        far = jnp.where(valid, tab_ref[N_BUCKETS - 1, h], 0.0) * LOG2E
        o_ref[0, g, r] = jnp.where(valid, far, NEG_INF)


def _toeplitz_bias_kernel(tab_ref, o_ref):
    s = pl.program_id(0)
    tile = ATT_TILE

    def distances(n_rows):
        i = lax.broadcasted_iota(jnp.int32, (n_rows, tile), 0)
        j = lax.broadcasted_iota(jnp.int32, (n_rows, tile), 1)
        d = jnp.minimum(s, N_WIN_TILES - 1) * tile + i - j
        d_hi = jnp.where(s < N_WIN_TILES, WINDOW, jnp.int32(1 << 30))
        return d, (d >= 0) & (d < d_hi)

    _write_far_tile(distances(tile)[1], tab_ref, o_ref)
    for step in range(N_WIN_TILES + 1):
        near = min(tile, max(0, MAX_DISTANCE + tile - 1 - min(step, N_WIN_TILES - 1) * tile))
        near = -(-near // 8) * 8
        if near:
            @pl.when(s == step)
            def _(near=near):
                d, valid = distances(near)
                _write_bias_rows(d, valid, tab_ref, o_ref, slice(0, near))


def _cmp_bias_kernel(tab_ref, o_ref):
    t = pl.program_id(0)
    chunk = 8

    def distances(c0, n_rows):
        c = c0 + lax.broadcasted_iota(jnp.int32, (n_rows, ATT_TILE), 0)
        i = lax.broadcasted_iota(jnp.int32, (n_rows, ATT_TILE), 1)
        d = t * ATT_TILE + i - CMP_STRIDE * c - (CMP_BLOCK - 1)
        return d, (d >= 0) & (c < LANES - 1)

    _write_far_tile(distances(0, LANES)[1], tab_ref, o_ref)
    first = (t * ATT_TILE - (MAX_DISTANCE + CMP_BLOCK - 1)) // (chunk * CMP_STRIDE)
    n_chunks = (ATT_TILE + MAX_DISTANCE) // (chunk * CMP_STRIDE) + 2

    def refine(m, carry):
        c0 = pl.multiple_of(jnp.clip(first + m, 0, LANES // chunk - 1) * chunk, chunk)
        d, valid = distances(c0, chunk)
        _write_bias_rows(d, valid, tab_ref, o_ref, pl.ds(c0, chunk))
        return carry

    lax.fori_loop(0, n_chunks, refine, 0)


def _bias_tables(rel_bias_table, n_q_tiles):
    def call(body, steps, rows, name):
        shape = (steps, NSA_KV_GROUPS, NSA_Q_PER_GROUP, rows, ATT_TILE)
        return pl.pallas_call(
            body, grid=(steps,),
            in_specs=[pl.BlockSpec(memory_space=pltpu.SMEM)],
            out_specs=pl.BlockSpec((1,) + shape[1:], lambda s: (s, 0, 0, 0, 0)),
            out_shape=jax.ShapeDtypeStruct(shape, F32),
            compiler_params=_params(1), name=name,
        )(rel_bias_table)

    return (call(_toeplitz_bias_kernel, N_WIN_TILES + 1, ATT_TILE, "toeplitz_bias"),
            call(_cmp_bias_kernel, n_q_tiles, LANES, "cmp_bias"))


def _score_bound(q_gain, k_gain):
    return 1.02 * LOG2E * math.sqrt(HEAD_DIM) * jnp.max(jnp.abs(q_gain)) * jnp.max(jnp.abs(k_gain))


def _attention(chains, groups, bounds, acc_ref):
    n = len(chains)

    def later(k):
        return tuple(i for _, ids in groups[k:] for i in ids)

    def exact_max():
        def half_max(i, dl):
            s = chains[i][0](dl)
            return jnp.maximum(s[:, :LANES], s[:, LANES:])

        mx = [half_max(i, 0) for i in range(n)]
        lo = 1
        for k, (last, _) in enumerate(groups):
            members = later(k)

            def body(dl, carry, members=members):
                return tuple(jnp.maximum(c, half_max(i, dl)) for c, i in zip(carry, members))

            for i, r in zip(members, lax.fori_loop(lo, last + 1, body, tuple(mx[i] for i in members))):
                mx[i] = r
            lo = last + 1
        return tuple(jnp.broadcast_to(jnp.max(m, axis=-1, keepdims=True), (m.shape[0], ATT_TILE))
                     for m in mx)

    shift = exact_max() if bounds is None else [b() for b in bounds]

    def weights(i, dl):
        return jnp.exp2(chains[i][0](dl) - shift[i]).astype(BF16)

    def product(i, dl):
        return chains[i][1](weights(i, dl), dl)

    for i in range(n):
        acc_ref[i] = product(i, 0)
    lo = 1
    for k, (last, _) in enumerate(groups):
        members = later(k)
        count = last + 1 - lo

        def body(j, carry, members=members, lo=lo):
            dl = lo + 2 * j
            for i in members:
                both = jnp.concatenate([weights(i, dl + 1), weights(i, dl)], axis=1)
                acc_ref[i] += chains[i][1](both, dl, 2)
            return carry

        lax.fori_loop(0, count // 2, body, 0)

        @pl.when(count % 2 == 1)
        def _(members=members, last=last):
            for i in members:
                acc_ref[i] += product(i, last)

        lo = last + 1
    return [acc_ref[i] for i in range(n)]


def _pair_lanes(first, second, normalise):
    low = lax.broadcasted_iota(jnp.int32, first.shape, 1) < HEAD_DIM
    pair = jnp.where(low, first, second)
    if not normalise:
        return pair
    sums = pltpu.roll(jnp.where(low, second, first), HEAD_DIM, axis=1)
    return pair * (1.0 / sums)


def _select_blocks(imp_t, t0, tq):
    blk = lax.broadcasted_iota(jnp.int32, imp_t.shape, 0)
    cur = (t0 + (lax.broadcasted_iota(jnp.int32, imp_t.shape, 1) & (tq - 1))) // SLC_BLOCK
    forced = (blk == 0) | (blk == cur) | (blk == cur - 1)
    score = jnp.where(blk <= cur, imp_t + jnp.where(forced, FORCE_BONUS, 0.0), NEG_INF)
    blk_f = blk.astype(F32)
    dead = -3.0e38
    picked = jnp.zeros(imp_t.shape, F32)
    for _ in range(SLC_TOPK):
        best = jnp.max(score, axis=0, keepdims=True)
        first = jnp.min(jnp.where(score == best, blk_f, float(LANES)), axis=0, keepdims=True)
        hit = blk_f == first
        picked = jnp.where(hit, 1.0, picked)
        score = jnp.where(hit, dead, score)
    return picked


def _nsa_kernel(bnd_ref, qa_ref, kcmp_ref, vcmp_ref, ksw_ref, vsw_ref, small_ref, tb_ref, bc_ref, ov_ref,
                e_ref, gx_ref, o_ref, amask_ref, acc_ref, *, n_blocks, exact):
    qt = pl.program_id(1)
    tq = ATT_TILE
    rq = NSA_Q_PER_GROUP
    lane = lax.broadcasted_iota(jnp.int32, (tq, LANES), 1)
    own_lanes = [jnp.where((lane // HEAD_DIM) == g, 1.0, 0.0).astype(BF16) for g in range(NSA_KV_GROUPS)]

    def key_rows(dl, n=1):
        return pl.ds(pl.multiple_of((qt - dl - (n - 1)) * tq, tq), n * tq)

    groups = range(NSA_KV_GROUPS)
    n_seq = qa_ref.shape[0]
    n_grp = NSA_KV_GROUPS

    def win_chain(b, g, q):
        def score(dl):
            s = _dot_nt(q, ksw_ref[b, key_rows(dl), LANES:2 * LANES]).reshape(rq, tq, tq)
            return (s + tb_ref[dl, g]).reshape(rq * tq, tq)

        def pv(p, dl, n=1):
            c0 = (n_grp + g) * LANES
            return _dot(p, vsw_ref[b, key_rows(dl, n), c0:c0 + LANES])

        return score, pv

    def slc_chain(b, g, q):
        def score(dl):
            tile = dl if isinstance(dl, int) else jnp.where(dl < N_WIN_TILES - 1, dl, N_WIN_TILES)
            rows = key_rows(dl)
            s = _dot_nt(q, ksw_ref[b, rows, 0:LANES]).reshape(rq, tq, tq)
            return (s + tb_ref[tile, g] + amask_ref[b * n_grp + g, :, rows][None]).reshape(rq * tq, tq)

        def pv(p, dl, n=1):
            return _dot(p, vsw_ref[b, key_rows(dl, n), g * LANES:(g + 1) * LANES])

        return score, pv

    overlap = ov_ref[...].astype(BF16)
    key_block = e_ref[...].astype(BF16)
    gate_expand = [gx_ref[br].astype(BF16) for br in range(3)]

    slc_chains, win_chains, o_cmps = [], [], []
    for b in range(n_seq):
        qs = [jnp.concatenate([qa_ref[b, :, r * LANES:(r + 1) * LANES] * own_lanes[g] for r in range(rq)],
                              axis=0) for g in groups]
        slc_chains += [slc_chain(b, g, qs[g]) for g in groups]
        win_chains += [win_chain(b, g, qs[g]) for g in groups]

        s = (_dot_nt(kcmp_ref[b], jnp.concatenate(qs, axis=0))
             + jnp.concatenate([bc_ref[0, g, r] for g in groups for r in range(rq)], axis=1))
        m = jnp.max(s, axis=0, keepdims=True)
        p = jnp.where(s > 0.5 * NEG_INF, jnp.exp2(s - m), 0.0)
        l = jnp.sum(p, axis=0, keepdims=True)
        p_c = p * (1.0 / jnp.where(l > 0.0, l, 1.0))
        o_cmp = _dot(p_c.T.astype(BF16), vcmp_ref[b])
        o_cmps.append([o_cmp[g * rq * tq:(g + 1) * rq * tq] for g in groups])

        p_sum = jnp.concatenate([sum(p_c[:, (g * rq + r) * tq:(g * rq + r + 1) * tq] for r in range(rq))
                                 for g in groups], axis=1)
        hi = p_sum.astype(BF16)
        lo = (p_sum - hi.astype(F32)).astype(BF16)
        imp_t = _dot(overlap, hi) + _dot(overlap, lo)
        sel_t = _select_blocks(imp_t[0:n_blocks], qt * tq, tq)
        block_bias = jnp.concatenate([jnp.where(sel_t > 0.0, 0.0, NEG_INF),
                                      jnp.zeros((LANES - n_blocks, n_grp * tq), F32)], axis=0)
        amask_ref[b * n_grp:(b + 1) * n_grp] = _dot(block_bias.T.astype(BF16), key_block).reshape(
            n_grp, tq, e_ref.shape[1])

    n_slc = len(slc_chains)
    slc_ids = tuple(range(n_slc))
    win_ids = tuple(range(n_slc, 2 * n_slc))
    bounds = None if exact else [lambda i=i: jnp.full((rq * tq, tq), bnd_ref[i // n_slc], F32)
                                 for i in range(2 * n_slc)]
    accs = _attention(slc_chains + win_chains,
                      [(jnp.minimum(qt, N_WIN_TILES - 1), win_ids), (qt, slc_ids)], bounds, acc_ref)

    for b in range(n_seq):
        gates = small_ref[b]
        g_hi = gates.astype(BF16)
        g_lo = (gates - g_hi.astype(F32)).astype(BF16)
        out = [jnp.zeros((tq, LANES), F32) for _ in range(rq)]
        lo_, hi_ = b * n_grp, (b + 1) * n_grp
        for br, per_group in enumerate((o_cmps[b], accs[lo_:hi_], accs[n_slc + lo_:n_slc + hi_])):
            gate = _dot(g_hi, gate_expand[br]) + _dot(g_lo, gate_expand[br])
            for r in range(rq):
                first, second = (a[r * tq:(r + 1) * tq] for a in per_group)
                pair = _pair_lanes(first, second, normalise=br > 0)
                out[r] = out[r] + gate[:, r * LANES:(r + 1) * LANES] * pair
        for r in range(rq):
            o_ref[b, :, r * LANES:(r + 1) * LANES] = out[r].astype(BF16)


def _nsa(bounds, qa, kcmp, vcmp, ksw, vsw, small, toeplitz, cmp_bias, batch, seq, exact):
    tq = ATT_TILE
    nq = seq // tq
    rq = NSA_Q_PER_GROUP
    n_blocks = seq // SLC_BLOCK
    n_cmp = (seq - CMP_BLOCK) // CMP_STRIDE + 1
    ci = np.arange(LANES)[:, None] * CMP_STRIDE
    sj = np.arange(LANES)[None, :] * SLC_BLOCK
    overlap = ((ci <= sj + SLC_BLOCK - 1) & (ci + CMP_BLOCK - 1 >= sj)
               & (np.arange(LANES)[:, None] < n_cmp) & (np.arange(LANES)[None, :] < n_blocks))
    expand = np.arange(LANES)[:, None] == (np.arange(seq)[None, :] // SLC_BLOCK)
    gate_expand = np.zeros((3, LANES, NSA_W), np.float32)
    for b in range(3):
        for g in range(NSA_KV_GROUPS):
            for r in range(rq):
                c0 = r * LANES + g * HEAD_DIM
                gate_expand[b, b * NSA_HEADS + g * rq + r, c0:c0 + HEAD_DIM] = 1.0
    per_step = 2 if batch % 2 == 0 else 1

    def rows(n_rows, width):
        return pl.BlockSpec((per_step, n_rows, width), lambda b, t: (b, 0, 0))

    def tile(width):
        return pl.BlockSpec((per_step, tq, width), lambda b, t: (b, t, 0))

    out = pl.pallas_call(
        functools.partial(_nsa_kernel, n_blocks=n_blocks, exact=exact),
        grid=(batch // per_step, nq),
        in_specs=[
            pl.BlockSpec(memory_space=pltpu.SMEM),
            tile(NSA_W),
            rows(LANES, NSA_KV_W),
            rows(LANES, NSA_KV_W),
            rows(seq, 2 * NSA_KV_W),
            rows(seq, 2 * NSA_KV_GROUPS * LANES),
            tile(LANES),
            _const_spec((N_WIN_TILES + 1, NSA_KV_GROUPS, rq, tq, tq)),
            pl.BlockSpec((1, NSA_KV_GROUPS, rq, LANES, tq), lambda b, t: (t, 0, 0, 0, 0)),
            _const_spec((LANES, LANES)),
            _const_spec((LANES, seq)),
            _const_spec((3, LANES, NSA_W)),
        ],
        out_specs=tile(NSA_W),
        out_shape=jax.ShapeDtypeStruct((batch, seq, NSA_W), BF16),
        scratch_shapes=[pltpu.VMEM((per_step * NSA_KV_GROUPS, tq, seq), F32),
                        pltpu.VMEM((per_step * 2 * NSA_KV_GROUPS, rq * tq, LANES), F32)],
        compiler_params=_params(2),
        name="nsa",
    )(bounds, qa.reshape(batch, seq, NSA_W), kcmp, vcmp, ksw.reshape(batch, seq, 2 * NSA_KV_W),
      vsw.reshape(batch, seq, 2 * NSA_KV_GROUPS * LANES), small.reshape(batch, seq, LANES),
      toeplitz, cmp_bias,
      jnp.asarray(overlap.T, F32), jnp.asarray(expand, F32), jnp.asarray(gate_expand, F32))
    return out.reshape(batch * seq, NSA_W)


def _fox_kernel(bnd_ref, q_ref, k_ref, v_ref, cum_ref, cum_t_ref, o_ref, acc_ref, *, exact):
    qt = pl.program_id(1)
    tq = ATT_TILE
    pairs = FOX_HEADS // 2
    row = lax.broadcasted_iota(jnp.int32, (2, tq, tq), 1)
    col = lax.broadcasted_iota(jnp.int32, (2, tq, tq), 2)
    lane = lax.broadcasted_iota(jnp.int32, (tq, LANES), 1)
    low = jnp.where(lane < HEAD_DIM, 1.0, 0.0).astype(BF16)
    high = jnp.where(lane < HEAD_DIM, 0.0, 1.0).astype(BF16)
    n_seq = q_ref.shape[0]

    def key_rows(dl, n=1):
        return pl.ds(pl.multiple_of((qt - dl - (n - 1)) * tq, tq), n * tq)

    bounds = []

    def pair_chain(b, p):
        pair = q_ref[b, :, p * LANES:(p + 1) * LANES]
        q = jnp.concatenate([pair * low, pair * high], axis=0)
        base = cum_ref[b, 2 * p:2 * p + 2, pl.ds(pl.multiple_of(qt * tq, tq), LANES)][:, 0:1]

        def decay_of(rows):
            return (base - cum_ref[b, 2 * p:2 * p + 2, rows]) * LOG2E

        def bound():
            cum_rows = cum_t_ref[b]
            c0 = FORGET_LANE + 2 * p
            own = jnp.concatenate([base[h:h + 1, :] - cum_rows[:, c0 + h:c0 + h + 1] for h in range(2)],
                                  axis=0)
            return jnp.broadcast_to(own * LOG2E + bnd_ref[0], (2 * tq, tq))

        bounds.append(bound)

        def score(dl):
            rows = key_rows(dl)
            decay = decay_of(rows)
            s = _dot_nt(q, k_ref[b, rows, p * LANES:(p + 1) * LANES]).reshape(2, tq, tq) + decay[:, None, :]
            if isinstance(dl, int):
                s = jnp.where(col <= row, s, NEG_INF)
            return s.reshape(2 * tq, tq)

        def pv(w, dl, n=1):
            rows = key_rows(dl, n)
            return jnp.concatenate(
                [_dot(w[0:tq], v_ref[b, rows, 2 * p * LANES:(2 * p + 1) * LANES]),
                 _dot(w[tq:2 * tq], v_ref[b, rows, (2 * p + 1) * LANES:(2 * p + 2) * LANES])], axis=0)

        return score, pv

    chains = [pair_chain(b, p) for b in range(n_seq) for p in range(pairs)]
    accs = _attention(chains, [(qt, tuple(range(len(chains))))], None if exact else bounds, acc_ref)
    for i, acc in enumerate(accs):
        b, p = divmod(i, pairs)
        o_ref[b, :, p * LANES:(p + 1) * LANES] = _pair_lanes(acc[0:tq], acc[tq:2 * tq], True).astype(BF16)


def _fox(bounds, qb, kb, vb, cum, cum_cols, batch, seq, exact):
    tq = ATT_TILE
    nq = seq // tq
    per_step = 2 if batch % 2 == 0 else 1
    out = pl.pallas_call(
        functools.partial(_fox_kernel, exact=exact),
        grid=(batch // per_step, nq),
        in_specs=[
            pl.BlockSpec(memory_space=pltpu.SMEM),
            pl.BlockSpec((per_step, tq, FOX_W), lambda b, t: (b, t, 0)),
            pl.BlockSpec((per_step, seq, FOX_W), lambda b, t: (b, 0, 0)),
            pl.BlockSpec((per_step, seq, FOX_HEADS * LANES), lambda b, t: (b, 0, 0)),
            pl.BlockSpec((per_step, FOX_HEADS, seq), lambda b, t: (b, 0, 0)),
            pl.BlockSpec((per_step, tq, LANES), lambda b, t: (b, t, 0)),
        ],
        out_specs=pl.BlockSpec((per_step, tq, FOX_W), lambda b, t: (b, t, 0)),
        out_shape=jax.ShapeDtypeStruct((batch, seq, FOX_W), BF16),
        scratch_shapes=[pltpu.VMEM((per_step * FOX_HEADS // 2, 2 * tq, LANES), F32)],
        compiler_params=_params(2),
        name="fox",
    )(bounds, qb.reshape(batch, seq, FOX_W), kb.reshape(batch, seq, FOX_W),
      vb.reshape(batch, seq, FOX_HEADS * LANES), cum, cum_cols.reshape(batch, seq, LANES))
    return out.reshape(batch * seq, FOX_W)


def _merge_ffn_kernel(x_ref, on_ref, of_ref, gmix_ref, wgm_ref, won_ref, wof_ref, wout_ref,
                      g2_ref, wup_ref, wdn_ref, o_ref):
    x = x_ref[...]
    u = _rms_rows(x, gmix_ref[...]).astype(BF16)
    gate = jax.nn.sigmoid(_dot(u, wgm_ref[...]))
    merged = (gate[:, :D_MODEL] * _dot(on_ref[...], won_ref[...])
              + gate[:, D_MODEL:] * _dot(of_ref[...], wof_ref[...]))
    x2 = x + _dot(merged.astype(BF16), wout_ref[...])
    o_ref[...] = _swiglu_residual(x2, g2_ref[...], wup_ref, wdn_ref)


def _merge_ffn(x2d, o_nsa, o_fox, mix_norm, w_gm, w_o_nsa, w_o_fox, w_out, gain2, w_up, w_down):
    n = x2d.shape[0]
    tm = _token_tile(n, wide=False)

    def row(width):
        return pl.BlockSpec((tm, width), lambda i: (i, 0))

    return pl.pallas_call(
        _merge_ffn_kernel,
        grid=(n // tm,),
        in_specs=[row(D_MODEL), row(NSA_W), row(FOX_W), _const_spec((1, D_MODEL)),
                  _const_spec((D_MODEL, 2 * D_MODEL)), _const_spec((NSA_W, D_MODEL)),
                  _const_spec((FOX_W, D_MODEL)), _const_spec((D_MODEL, D_MODEL)),
                  _const_spec((1, D_MODEL)), _const_spec((D_MODEL, 2 * D_FF)),
                  _const_spec((D_FF, D_MODEL))],
        out_specs=row(D_MODEL),
        out_shape=jax.ShapeDtypeStruct((n, D_MODEL), F32),
        compiler_params=_params(1),
        name="merge_ffn",
    )(x2d, o_nsa, o_fox, mix_norm.reshape(1, D_MODEL), w_gm, w_o_nsa.astype(BF16),
      w_o_fox.astype(BF16), w_out.astype(BF16), gain2.reshape(1, D_MODEL), w_up.astype(BF16),
      w_down.astype(BF16))


def _layer(x, ffn1_norm, ffn1_w_up, ffn1_w_down, mix_norm, w_in, b_forget, nsa_q_gain, nsa_k_gain,
           fox_q_gain, fox_k_gain, cmp_pos_k, cmp_pos_v, cmp_k_w1, cmp_k_w2, cmp_v_w1, cmp_v_w2,
           w_o_nsa, w_o_fox, w_out, ffn2_norm, ffn2_w_up, ffn2_w_down, rel_bias_table):
    batch, seq, d = x.shape
    assert d == D_MODEL and seq % ATT_TILE == 0 and (batch * seq) % TOKEN_TILE == 0
    assert seq // SLC_BLOCK <= LANES and (seq - CMP_BLOCK) // CMP_STRIDE + 1 == LANES - 1
    assert seq // (CMP_BLOCK // 2) == LANES
    x2d = x.reshape(batch * seq, D_MODEL)

    x1 = _ffn(x2d, ffn1_norm, ffn1_w_up, ffn1_w_down)

    w_packed, gain_row, small_bias, w_gm = _pack_in_proj(w_in, b_forget, nsa_q_gain, nsa_k_gain,
                                                         fox_q_gain, fox_k_gain)
    qa, qb, kb, ksw, vb, vsw, kc, vc, small = _in_proj(x1, mix_norm, w_packed, gain_row, small_bias,
                                                       batch, seq)

    kcmp, vcmp = _compress(kc, vc, batch, seq, _pack_compress(cmp_pos_k, cmp_k_w1, cmp_k_w2),
                           _pack_compress(cmp_pos_v, cmp_v_w1, cmp_v_w2), nsa_k_gain[0])
    toeplitz, cmp_bias = _bias_tables(rel_bias_table, seq // ATT_TILE)
    b_slc = _score_bound(nsa_q_gain, nsa_k_gain[1])
    b_win = _score_bound(nsa_q_gain, nsa_k_gain[2])
    t_hi, t_lo = jnp.max(rel_bias_table) * LOG2E, jnp.min(rel_bias_table) * LOG2E
    nsa_ok = 2.0 * jnp.maximum(b_slc, b_win) + (t_hi - t_lo) < EXP2_RANGE
    nsa_bounds = jnp.stack([b_slc + t_hi, b_win + t_hi])
    o_nsa = lax.cond(
        nsa_ok,
        lambda: _nsa(nsa_bounds, qa, kcmp, vcmp, ksw, vsw, small, toeplitz, cmp_bias, batch, seq, False),
        lambda: _nsa(nsa_bounds, qa, kcmp, vcmp, ksw, vsw, small, toeplitz, cmp_bias, batch, seq, True))

    cum_cols, cum = _cumsum(small, batch, seq)
    b_fox = _score_bound(fox_q_gain, fox_k_gain)
    fox_bounds = jnp.stack([b_fox])
    o_fox = lax.cond(2.0 * b_fox < EXP2_RANGE,
                     lambda: _fox(fox_bounds, qb, kb, vb, cum, cum_cols, batch, seq, False),
                     lambda: _fox(fox_bounds, qb, kb, vb, cum, cum_cols, batch, seq, True))

    w_o_nsa_p = w_o_nsa.reshape(NSA_KV_GROUPS, NSA_Q_PER_GROUP, HEAD_DIM, D_MODEL).transpose(1, 0, 2, 3)
    out = _merge_ffn(x1, o_nsa, o_fox, mix_norm, w_gm, w_o_nsa_p.reshape(NSA_W, D_MODEL), w_o_fox,
                     w_out, ffn2_norm, ffn2_w_up, ffn2_w_down)
    return out.reshape(batch, seq, D_MODEL)


def kernel(x, ffn1_norm, ffn1_w_up, ffn1_w_down, mix_norm, w_in, b_forget, nsa_q_gain, nsa_k_gain,
           fox_q_gain, fox_k_gain, cmp_pos_k, cmp_pos_v, cmp_k_w1, cmp_k_w2, cmp_v_w1, cmp_v_w2,
           w_o_nsa, w_o_fox, w_out, ffn2_norm, ffn2_w_up, ffn2_w_down, rel_bias_table):
    for layer in range(ffn1_norm.shape[0]):
        x = _layer(x, ffn1_norm[layer], ffn1_w_up[layer], ffn1_w_down[layer], mix_norm[layer],
                   w_in[layer], b_forget[layer], nsa_q_gain[layer], nsa_k_gain[layer],
                   fox_q_gain[layer], fox_k_gain[layer], cmp_pos_k[layer], cmp_pos_v[layer],
                   cmp_k_w1[layer], cmp_k_w2[layer], cmp_v_w1[layer], cmp_v_w2[layer],
                   w_o_nsa[layer], w_o_fox[layer], w_out[layer], ffn2_norm[layer],
                   ffn2_w_up[layer], ffn2_w_down[layer], rel_bias_table)
    return x
```

```python
import functools
import math

import numpy as np
import jax
import jax.numpy as jnp
from jax import lax
from jax.experimental import pallas as pl
from jax.experimental.pallas import tpu as pltpu

F32 = jnp.float32
BF16 = jnp.bfloat16

D_MODEL = 1024
HEAD_DIM = 64
NSA_HEADS = 8
NSA_KV_GROUPS = 2
NSA_Q_PER_GROUP = NSA_HEADS // NSA_KV_GROUPS
CMP_BLOCK = 32
CMP_STRIDE = 16
CMP_HIDDEN = 128
SLC_BLOCK = 64
SLC_TOPK = 8
WINDOW = 512
FOX_HEADS = 8
D_FF = 2816
N_BUCKETS = 32
MAX_DISTANCE = 128
RMS_EPS = 1e-6
NEG_INF = -1.0e30
FORCE_BONUS = 1.0e4

NSA_W = NSA_HEADS * HEAD_DIM
NSA_KV_W = NSA_KV_GROUPS * HEAD_DIM
FOX_W = FOX_HEADS * HEAD_DIM
IN_SPLITS = (NSA_W, NSA_KV_W, NSA_KV_W, NSA_KV_W, NSA_KV_W, NSA_KV_W, NSA_KV_W, 3 * NSA_HEADS,
             FOX_W, FOX_W, FOX_W, FOX_HEADS, 2 * D_MODEL)

LANES = 128
TOKEN_TILE = 512
MXU_DIM = 256
FFN_CHUNKS = (-(-D_FF // (2 * MXU_DIM)) * MXU_DIM, D_FF - -(-D_FF // (2 * MXU_DIM)) * MXU_DIM)
ATT_TILE = 256
N_WIN_TILES = WINDOW // ATT_TILE + 1
VMEM_LIMIT = 56 * 1024 * 1024
LOG2E = 1.4426950408889634
EXP2_RANGE = 100.0

PQ_A = 0
PQ_B = PQ_A + NSA_W
PK_B = PQ_B + FOX_W
PK_SW = PK_B + FOX_W
P_NORM_END = PK_SW + 2 * NSA_KV_W
PV_B = P_NORM_END
PV_SW = PV_B + FOX_W
PKV_C = PV_SW + 2 * NSA_KV_W
P_SMALL = PKV_C + 2 * NSA_KV_W
P_END = P_SMALL + LANES
NORM_CHUNK = 256


def _dot(a, b):
    return jnp.dot(a, b, preferred_element_type=F32)


def _dot_nt(a, b):
    return lax.dot_general(a, b, (((1,), (1,)), ((), ())), preferred_element_type=F32)


def _split_dot(x, w):
    hi = x.astype(BF16)
    lo = (x - hi.astype(F32)).astype(BF16)
    return _dot(hi, w) + _dot(lo, w)


def _rms_rows(x, gain_row):
    ms = jnp.mean(x * x, axis=-1, keepdims=True)
    return x * lax.rsqrt(ms + RMS_EPS) * gain_row


def _const_spec(shape):
    nd = len(shape)
    return pl.BlockSpec(shape, lambda *_: (0,) * nd, pipeline_mode=pl.Buffered(1))


def _token_tile(n_rows, wide):
    tile = 2 * TOKEN_TILE if wide and n_rows % (2 * TOKEN_TILE) == 0 else TOKEN_TILE
    assert n_rows % tile == 0
    return tile


def _params(n_axes):
    return pltpu.CompilerParams(dimension_semantics=("arbitrary",) * n_axes,
                                vmem_limit_bytes=VMEM_LIMIT)


def _swiglu_residual(x, gain_row, wup_ref, wdn_ref):
    xn = _rms_rows(x, gain_row).astype(BF16)
    acc = jnp.zeros(x.shape, F32)
    lo = 0
    for width in FFN_CHUNKS:
        gate = _dot(xn, wup_ref[:, lo:lo + width])
        up = _dot(xn, wup_ref[:, D_FF + lo:D_FF + lo + width])
        h = (gate * jax.nn.sigmoid(gate) * up).astype(BF16)
        acc = acc + _dot(h, wdn_ref[lo:lo + width, :])
        lo += width
    return x + 0.5 * acc


def _ffn_kernel(x_ref, g_ref, wup_ref, wdn_ref, o_ref):
    o_ref[...] = _swiglu_residual(x_ref[...], g_ref[...], wup_ref, wdn_ref)


def _ffn(x2d, gain, w_up, w_down):
    n = x2d.shape[0]
    tm = _token_tile(n, wide=True)
    row = pl.BlockSpec((tm, D_MODEL), lambda i: (i, 0))
    return pl.pallas_call(
        _ffn_kernel,
        grid=(n // tm,),
        in_specs=[row, _const_spec((1, D_MODEL)), _const_spec((D_MODEL, 2 * D_FF)),
                  _const_spec((D_FF, D_MODEL))],
        out_specs=row,
        out_shape=jax.ShapeDtypeStruct((n, D_MODEL), F32),
        compiler_params=_params(1),
        name="ffn",
    )(x2d, gain.reshape(1, D_MODEL), w_up.astype(BF16), w_down.astype(BF16))


def _block_ones(width, size=NORM_CHUNK):
    idx = np.arange(size) // width
    return jnp.asarray((idx[:, None] == idx[None, :]).astype(np.float32), BF16)


def _pack_in_proj(w_in, b_forget, nsa_q_gain, nsa_k_gain, fox_q_gain, fox_k_gain):
    scale = LOG2E / math.sqrt(HEAD_DIM)
    pts = np.cumsum(np.array(IN_SPLITS))[:-1].tolist()
    qa, kc, vc, ks, vs, kw, vw, ga, qb, kb, vb, fb, gm = jnp.split(w_in, pts, axis=1)
    cols = []
    for r in range(NSA_Q_PER_GROUP):
        for g in range(NSA_KV_GROUPS):
            h = g * NSA_Q_PER_GROUP + r
            cols.append(qa[:, h * HEAD_DIM:(h + 1) * HEAD_DIM])
    cols += [qb, kb, ks, kw]
    gains = [jnp.tile(nsa_q_gain * scale, NSA_HEADS), jnp.tile(fox_q_gain * scale, FOX_HEADS),
             jnp.tile(fox_k_gain, FOX_HEADS), jnp.tile(nsa_k_gain[1], NSA_KV_GROUPS),
             jnp.tile(nsa_k_gain[2], NSA_KV_GROUPS)]
    n_small_pad = LANES - ga.shape[1] - fb.shape[1]
    cols += [vb, vs, vw, kc, vc, ga, fb, jnp.zeros((D_MODEL, n_small_pad), F32)]
    w_packed = jnp.concatenate(cols, axis=1).astype(BF16)
    gain_row = jnp.concatenate(gains).reshape(1, P_NORM_END)
    small_bias = jnp.concatenate([jnp.zeros((ga.shape[1],), F32), b_forget,
                                  jnp.zeros((n_small_pad,), F32)]).reshape(1, LANES)
    return w_packed, gain_row, small_bias, gm.astype(BF16)


def _in_proj_kernel(x_ref, g_ref, w_ref, gain_ref, sbias_ref, bd64_ref,
                    qa_ref, qb_ref, kb_ref, ksw_ref, vb_ref, vsw_ref, kc_ref, vc_ref, small_ref, kv_scr):
    u = _rms_rows(x_ref[...], g_ref[...]).astype(BF16)
    tm = u.shape[0]

    y = _dot(u, w_ref[:, 0:P_NORM_END])
    n_chunks = P_NORM_END // NORM_CHUNK
    squares = jnp.concatenate([jnp.square(y[:, c * NORM_CHUNK:(c + 1) * NORM_CHUNK]).astype(BF16)
                               for c in range(n_chunks)], axis=0)
    ss = _dot(squares, bd64_ref[...])
    chunk = 0
    for o_ref in (qa_ref, qb_ref, kb_ref, ksw_ref):
        for i in range(o_ref.shape[1] // NORM_CHUNK):
            cols = slice(chunk * NORM_CHUNK, (chunk + 1) * NORM_CHUNK)
            inv = lax.rsqrt(ss[chunk * tm:(chunk + 1) * tm] * (1.0 / HEAD_DIM) + RMS_EPS)
            o_ref[:, i * NORM_CHUNK:(i + 1) * NORM_CHUNK] = (y[:, cols] * inv * gain_ref[:, cols]).astype(BF16)
            chunk += 1
    low = lax.broadcasted_iota(jnp.int32, (u.shape[0], LANES), 1) < HEAD_DIM

    def store_with_ones(v, o_ref):
        for i in range(v.shape[1] // LANES):
            pair = v[:, i * LANES:(i + 1) * LANES]
            o_ref[:, 2 * i * LANES:(2 * i + 1) * LANES] = jnp.where(low, pair, 1.0).astype(BF16)
            o_ref[:, (2 * i + 1) * LANES:(2 * i + 2) * LANES] = jnp.where(low, 1.0, pair).astype(BF16)

    rest = _dot(u, w_ref[:, P_NORM_END:P_END])
    store_with_ones(rest[:, PV_B - P_NORM_END:PV_SW - P_NORM_END], vb_ref)
    store_with_ones(rest[:, PV_SW - P_NORM_END:PKV_C - P_NORM_END], vsw_ref)
    for j, o_ref in enumerate((kc_ref, vc_ref)):
        c0 = PKV_C - P_NORM_END + j * NSA_KV_W
        kv_scr[j] = rest[:, c0:c0 + NSA_KV_W]
        for l in range(CMP_STRIDE):
            o_ref[0, :, l * NSA_KV_W:(l + 1) * NSA_KV_W] = kv_scr[j, pl.ds(l, tm // CMP_STRIDE,
                                                                           stride=CMP_STRIDE), :]
    z = rest[:, P_SMALL - P_NORM_END:] + sbias_ref[...]
    lane = lax.broadcasted_iota(jnp.int32, z.shape, 1)
    log_sig = jnp.minimum(z, 0.0) - jnp.log1p(jnp.exp(-jnp.abs(z)))
    small_ref[...] = jnp.where(lane < 3 * NSA_HEADS, jax.nn.sigmoid(z), log_sig)


def _in_proj(x2d, mix_norm, w_packed, gain_row, small_bias, batch, seq):
    n = x2d.shape[0]
    tm = _token_tile(seq, wide=True)
    steps_per_seq = seq // tm

    def row(width):
        return pl.BlockSpec((tm, width), lambda i: (i, 0))

    def out(width, dtype):
        return row(width), jax.ShapeDtypeStruct((n, width), dtype)

    grouped = (pl.BlockSpec((1, tm // CMP_STRIDE, CMP_STRIDE * NSA_KV_W),
                            lambda i: (i // steps_per_seq, i % steps_per_seq, 0)),
               jax.ShapeDtypeStruct((batch, seq // CMP_STRIDE, CMP_STRIDE * NSA_KV_W), F32))
    outs = [out(NSA_W, BF16), out(FOX_W, BF16), out(FOX_W, BF16), out(2 * NSA_KV_W, BF16),
            out(2 * FOX_W, BF16), out(4 * NSA_KV_W, BF16), grouped, grouped, out(LANES, F32)]
    return pl.pallas_call(
        _in_proj_kernel,
        grid=(n // tm,),
        in_specs=[row(D_MODEL), _const_spec((1, D_MODEL)), _const_spec((D_MODEL, P_END)),
                  _const_spec((1, P_NORM_END)), _const_spec((1, LANES)),
                  _const_spec((NORM_CHUNK, NORM_CHUNK))],
        out_specs=[spec for spec, _ in outs],
        out_shape=[shape for _, shape in outs],
        scratch_shapes=[pltpu.VMEM((2, tm, NSA_KV_W), F32)],
        compiler_params=_params(1),
        name="in_proj",
    )(x2d, mix_norm.reshape(1, D_MODEL), w_packed, gain_row, small_bias, _block_ones(HEAD_DIM))


def _pack_compress(pos, w1, w2):
    half = CMP_BLOCK // 2
    eye = jnp.eye(NSA_KV_GROUPS, dtype=F32)
    w1r = w1.reshape(CMP_BLOCK, HEAD_DIM, CMP_HIDDEN)

    def big(w):
        return jnp.einsum('ldh,pg->lpdgh', w, eye).reshape(half * NSA_KV_W, NSA_KV_GROUPS * CMP_HIDDEN)

    def posrow(p):
        return jnp.broadcast_to(p[:, None, :], (half, NSA_KV_GROUPS, HEAD_DIM)).reshape(1, half * NSA_KV_W)

    w2big = jnp.einsum('hd,pg->phgd', w2, eye).reshape(NSA_KV_GROUPS * CMP_HIDDEN, NSA_KV_W)
    return (posrow(pos[:half]), posrow(pos[half:]), big(w1r[:half]).astype(BF16),
            big(w1r[half:]).astype(BF16), w2big.astype(BF16))


def _compress_kernel(kc_ref, vc_ref, kp_lo, kp_hi, kw_lo, kw_hi, kw2, vp_lo, vp_hi, vw_lo, vw_hi, vw2,
                     kgain_ref, bd64_ref, ko_ref, vo_ref):
    def mlp(r, p_lo, p_hi, w_lo, w_hi, w2):
        first = _dot((r + p_lo[...]).astype(BF16), w_lo[...])
        second = _dot((r + p_hi[...]).astype(BF16), w_hi[...])
        h = first + pltpu.roll(second, second.shape[0] - 1, axis=0)
        return _dot((h * jax.nn.sigmoid(h)).astype(BF16), w2[...])

    k = mlp(kc_ref[0], kp_lo, kp_hi, kw_lo, kw_hi, kw2)
    ss = _split_dot(k * k, bd64_ref[...])
    ko_ref[0] = (k * lax.rsqrt(ss * (1.0 / HEAD_DIM) + RMS_EPS) * kgain_ref[...]).astype(BF16)
    vo_ref[0] = mlp(vc_ref[0], vp_lo, vp_hi, vw_lo, vw_hi, vw2).astype(BF16)


def _compress(kc, vc, batch, seq, k_pack, v_pack, k_gain):
    rows = seq // (CMP_BLOCK // 2)
    width = (CMP_BLOCK // 2) * NSA_KV_W
    blk = pl.BlockSpec((1, rows, width), lambda b: (b, 0, 0))
    out = pl.BlockSpec((1, rows, NSA_KV_W), lambda b: (b, 0, 0))
    pack_specs = [_const_spec((1, width)), _const_spec((1, width)),
                  _const_spec((width, NSA_KV_GROUPS * CMP_HIDDEN)),
                  _const_spec((width, NSA_KV_GROUPS * CMP_HIDDEN)),
                  _const_spec((NSA_KV_GROUPS * CMP_HIDDEN, NSA_KV_W))]
    bd = _block_ones(HEAD_DIM, NSA_KV_W)
    return pl.pallas_call(
        _compress_kernel,
        grid=(batch,),
        in_specs=[blk, blk] + pack_specs + pack_specs + [_const_spec((1, NSA_KV_W)),
                                                         _const_spec((NSA_KV_W, NSA_KV_W))],
        out_specs=[out, out],
        out_shape=[jax.ShapeDtypeStruct((batch, rows, NSA_KV_W), BF16)] * 2,
        compiler_params=_params(1),
        name="compress",
    )(kc.reshape(batch, rows, width), vc.reshape(batch, rows, width), *k_pack, *v_pack,
      jnp.tile(k_gain, NSA_KV_GROUPS).reshape(1, NSA_KV_W), bd)


FORGET_LANE = 3 * NSA_HEADS


def _cumsum_kernel(x_ref, cols_ref, rows_ref):
    x = x_ref[...]
    row = lax.broadcasted_iota(jnp.int32, x.shape, 0)
    k = 1
    while k < x.shape[0]:
        x = x + jnp.where(row >= k, pltpu.roll(x, k, axis=0), 0.0)
        k *= 2
    cols_ref[...] = x
    rows_ref[0] = x.T[FORGET_LANE:FORGET_LANE + FOX_HEADS]


def _cumsum(small, batch, seq):
    spec = pl.BlockSpec((seq, LANES), lambda b: (b, 0))
    return pl.pallas_call(
        _cumsum_kernel, grid=(batch,), in_specs=[spec],
        out_specs=[spec, pl.BlockSpec((1, FOX_HEADS, seq), lambda b: (b, 0, 0))],
        out_shape=[jax.ShapeDtypeStruct((batch * seq, LANES), F32),
                   jax.ShapeDtypeStruct((batch, FOX_HEADS, seq), F32)],
        compiler_params=_params(1), name="cumsum",
    )(small)


def _write_bias_rows(d, valid, tab_ref, o_ref, rows):
    max_exact = N_BUCKETS // 2
    n = jnp.maximum(d, 0)
    nf = jnp.maximum(n, 1).astype(F32)
    large = max_exact + (jnp.log(nf / max_exact) / math.log(MAX_DISTANCE / max_exact)
                         * (N_BUCKETS - max_exact)).astype(jnp.int32)
    bucket = jnp.where(n < max_exact, n, jnp.minimum(large, N_BUCKETS - 1))
    for h in range(NSA_HEADS):
        acc = jnp.zeros(d.shape, F32)
        for b in range(N_BUCKETS):
            acc = jnp.where(bucket == b, tab_ref[b, h], acc)
        g, r = divmod(h, NSA_Q_PER_GROUP)
        o_ref[0, g, r, rows, :] = jnp.where(valid, acc * LOG2E, NEG_INF)


def _write_far_tile(valid, tab_ref, o_ref):
    for h in range(NSA_HEADS):
        g, r = divmod(h, NSA_Q_PER_GROUP)
        far = jnp.where(valid, tab_ref[N_BUCKETS - 1, h], 0.0) * LOG2E
        o_ref[0, g, r] = jnp.where(valid, far, NEG_INF)


def _toeplitz_bias_kernel(tab_ref, o_ref):
    s = pl.program_id(0)
    tile = ATT_TILE

    def distances(n_rows):
        i = lax.broadcasted_iota(jnp.int32, (n_rows, tile), 0)
        j = lax.broadcasted_iota(jnp.int32, (n_rows, tile), 1)
        d = jnp.minimum(s, N_WIN_TILES - 1) * tile + i - j
        d_hi = jnp.where(s < N_WIN_TILES, WINDOW, jnp.int32(1 << 30))
        return d, (d >= 0) & (d < d_hi)

    _write_far_tile(distances(tile)[1], tab_ref, o_ref)
    for step in range(N_WIN_TILES + 1):
        near = min(tile, max(0, MAX_DISTANCE + tile - 1 - min(step, N_WIN_TILES - 1) * tile))
        near = -(-near // 8) * 8
        if near:
            @pl.when(s == step)
            def _(near=near):
                d, valid = distances(near)
                _write_bias_rows(d, valid, tab_ref, o_ref, slice(0, near))


def _cmp_bias_kernel(tab_ref, o_ref):
    t = pl.program_id(0)
    chunk = 8

    def distances(c0, n_rows):
        c = c0 + lax.broadcasted_iota(jnp.int32, (n_rows, ATT_TILE), 0)
        i = lax.broadcasted_iota(jnp.int32, (n_rows, ATT_TILE), 1)
        d = t * ATT_TILE + i - CMP_STRIDE * c - (CMP_BLOCK - 1)
        return d, (d >= 0) & (c < LANES - 1)

    _write_far_tile(distances(0, LANES)[1], tab_ref, o_ref)
    first = (t * ATT_TILE - (MAX_DISTANCE + CMP_BLOCK - 1)) // (chunk * CMP_STRIDE)
    n_chunks = (ATT_TILE + MAX_DISTANCE) // (chunk * CMP_STRIDE) + 2

    def refine(m, carry):
        c0 = pl.multiple_of(jnp.clip(first + m, 0, LANES // chunk - 1) * chunk, chunk)
        d, valid = distances(c0, chunk)
        _write_bias_rows(d, valid, tab_ref, o_ref, pl.ds(c0, chunk))
        return carry

    lax.fori_loop(0, n_chunks, refine, 0)


def _bias_tables(rel_bias_table, n_q_tiles):
    def call(body, steps, rows, name):
        shape = (steps, NSA_KV_GROUPS, NSA_Q_PER_GROUP, rows, ATT_TILE)
        return pl.pallas_call(
            body, grid=(steps,),
            in_specs=[pl.BlockSpec(memory_space=pltpu.SMEM)],
            out_specs=pl.BlockSpec((1,) + shape[1:], lambda s: (s, 0, 0, 0, 0)),
            out_shape=jax.ShapeDtypeStruct(shape, F32),
            compiler_params=_params(1), name=name,
        )(rel_bias_table)

    return (call(_toeplitz_bias_kernel, N_WIN_TILES + 1, ATT_TILE, "toeplitz_bias"),
            call(_cmp_bias_kernel, n_q_tiles, LANES, "cmp_bias"))


def _score_bound(q_gain, k_gain):
    return 1.02 * LOG2E * math.sqrt(HEAD_DIM) * jnp.max(jnp.abs(q_gain)) * jnp.max(jnp.abs(k_gain))


def _attention(chains, groups, bounds, acc_ref):
    n = len(chains)

    def later(k):
        return tuple(i for _, ids in groups[k:] for i in ids)

    def exact_max():
        def half_max(i, dl):
            s = chains[i][0](dl)
            return jnp.maximum(s[:, :LANES], s[:, LANES:])

        mx = [half_max(i, 0) for i in range(n)]
        lo = 1
        for k, (last, _) in enumerate(groups):
            members = later(k)

            def body(dl, carry, members=members):
                return tuple(jnp.maximum(c, half_max(i, dl)) for c, i in zip(carry, members))

            for i, r in zip(members, lax.fori_loop(lo, last + 1, body, tuple(mx[i] for i in members))):
                mx[i] = r
            lo = last + 1
        return tuple(jnp.broadcast_to(jnp.max(m, axis=-1, keepdims=True), (m.shape[0], ATT_TILE))
                     for m in mx)

    shift = exact_max() if bounds is None else [b() for b in bounds]

    def weights(i, dl):
        return jnp.exp2(chains[i][0](dl) - shift[i]).astype(BF16)

    def product(i, dl):
        return chains[i][1](weights(i, dl), dl)

    for i in range(n):
        acc_ref[i] = product(i, 0)
    lo = 1
    for k, (last, _) in enumerate(groups):
        members = later(k)
        count = last + 1 - lo

        def body(j, carry, members=members, lo=lo):
            dl = lo + 2 * j
            for i in members:
                both = jnp.concatenate([weights(i, dl + 1), weights(i, dl)], axis=1)
                acc_ref[i] += chains[i][1](both, dl, 2)
            return carry

        lax.fori_loop(0, count // 2, body, 0)

        @pl.when(count % 2 == 1)
        def _(members=members, last=last):
            for i in members:
                acc_ref[i] += product(i, last)

        lo = last + 1
    return [acc_ref[i] for i in range(n)]


def _pair_lanes(first, second, normalise):
    low = lax.broadcasted_iota(jnp.int32, first.shape, 1) < HEAD_DIM
    pair = jnp.where(low, first, second)
    if not normalise:
        return pair
    sums = pltpu.roll(jnp.where(low, second, first), HEAD_DIM, axis=1)
    return pair * (1.0 / sums)


def _select_blocks(imp_t, t0, tq):
    blk = lax.broadcasted_iota(jnp.int32, imp_t.shape, 0)
    cur = (t0 + (lax.broadcasted_iota(jnp.int32, imp_t.shape, 1) & (tq - 1))) // SLC_BLOCK
    forced = (blk == 0) | (blk == cur) | (blk == cur - 1)
    score = jnp.where(blk <= cur, imp_t + jnp.where(forced, FORCE_BONUS, 0.0), NEG_INF)
    blk_f = blk.astype(F32)
    dead = -3.0e38
    picked = jnp.zeros(imp_t.shape, F32)
    for _ in range(SLC_TOPK):
        best = jnp.max(score, axis=0, keepdims=True)
        first = jnp.min(jnp.where(score == best, blk_f, float(LANES)), axis=0, keepdims=True)
        hit = blk_f == first
        picked = jnp.where(hit, 1.0, picked)
        score = jnp.where(hit, dead, score)
    return picked


def _nsa_kernel(bnd_ref, qa_ref, kcmp_ref, vcmp_ref, ksw_ref, vsw_ref, small_ref, tb_ref, bc_ref, ov_ref,
                e_ref, gx_ref, o_ref, amask_ref, acc_ref, *, n_blocks, exact):
    qt = pl.program_id(1)
    tq = ATT_TILE
    rq = NSA_Q_PER_GROUP
    lane = lax.broadcasted_iota(jnp.int32, (tq, LANES), 1)
    own_lanes = [jnp.where((lane // HEAD_DIM) == g, 1.0, 0.0).astype(BF16) for g in range(NSA_KV_GROUPS)]

    def key_rows(dl, n=1):
        return pl.ds(pl.multiple_of((qt - dl - (n - 1)) * tq, tq), n * tq)

    groups = range(NSA_KV_GROUPS)
    n_seq = qa_ref.shape[0]
    n_grp = NSA_KV_GROUPS

    def win_chain(b, g, q):
        def score(dl):
            s = _dot_nt(q, ksw_ref[b, key_rows(dl), LANES:2 * LANES]).reshape(rq, tq, tq)
            return (s + tb_ref[dl, g]).reshape(rq * tq, tq)

        def pv(p, dl, n=1):
            c0 = (n_grp + g) * LANES
            return _dot(p, vsw_ref[b, key_rows(dl, n), c0:c0 + LANES])

        return score, pv

    def slc_chain(b, g, q):
        def score(dl):
            tile = dl if isinstance(dl, int) else jnp.where(dl < N_WIN_TILES - 1, dl, N_WIN_TILES)
            rows = key_rows(dl)
            s = _dot_nt(q, ksw_ref[b, rows, 0:LANES]).reshape(rq, tq, tq)
            return (s + tb_ref[tile, g] + amask_ref[b * n_grp + g, :, rows][None]).reshape(rq * tq, tq)

        def pv(p, dl, n=1):
            return _dot(p, vsw_ref[b, key_rows(dl, n), g * LANES:(g + 1) * LANES])

        return score, pv

    overlap = ov_ref[...].astype(BF16)
    key_block = e_ref[...].astype(BF16)
    gate_expand = [gx_ref[br].astype(BF16) for br in range(3)]

    slc_chains, win_chains, o_cmps = [], [], []
    for b in range(n_seq):
        qs = [jnp.concatenate([qa_ref[b, :, r * LANES:(r + 1) * LANES] * own_lanes[g] for r in range(rq)],
                              axis=0) for g in groups]
        slc_chains += [slc_chain(b, g, qs[g]) for g in groups]
        win_chains += [win_chain(b, g, qs[g]) for g in groups]

        s = (_dot_nt(kcmp_ref[b], jnp.concatenate(qs, axis=0))
             + jnp.concatenate([bc_ref[0, g, r] for g in groups for r in range(rq)], axis=1))
        m = jnp.max(s, axis=0, keepdims=True)
        p = jnp.where(s > 0.5 * NEG_INF, jnp.exp2(s - m), 0.0)
        l = jnp.sum(p, axis=0, keepdims=True)
        p_c = p * (1.0 / jnp.where(l > 0.0, l, 1.0))
        o_cmp = _dot(p_c.T.astype(BF16), vcmp_ref[b])
        o_cmps.append([o_cmp[g * rq * tq:(g + 1) * rq * tq] for g in groups])

        p_sum = jnp.concatenate([sum(p_c[:, (g * rq + r) * tq:(g * rq + r + 1) * tq] for r in range(rq))
                                 for g in groups], axis=1)
        hi = p_sum.astype(BF16)
        lo = (p_sum - hi.astype(F32)).astype(BF16)
        imp_t = _dot(overlap, hi) + _dot(overlap, lo)
        sel_t = _select_blocks(imp_t[0:n_blocks], qt * tq, tq)
        block_bias = jnp.concatenate([jnp.where(sel_t > 0.0, 0.0, NEG_INF),
                                      jnp.zeros((LANES - n_blocks, n_grp * tq), F32)], axis=0)
        amask_ref[b * n_grp:(b + 1) * n_grp] = _dot(block_bias.T.astype(BF16), key_block).reshape(
            n_grp, tq, e_ref.shape[1])

    n_slc = len(slc_chains)
    slc_ids = tuple(range(n_slc))
    win_ids = tuple(range(n_slc, 2 * n_slc))
    bounds = None if exact else [lambda i=i: jnp.full((rq * tq, tq), bnd_ref[i // n_slc], F32)
                                 for i in range(2 * n_slc)]
    accs = _attention(slc_chains + win_chains,
                      [(jnp.minimum(qt, N_WIN_TILES - 1), win_ids), (qt, slc_ids)], bounds, acc_ref)

    for b in range(n_seq):
        gates = small_ref[b]
        g_hi = gates.astype(BF16)
        g_lo = (gates - g_hi.astype(F32)).astype(BF16)
        out = [jnp.zeros((tq, LANES), F32) for _ in range(rq)]
        lo_, hi_ = b * n_grp, (b + 1) * n_grp
        for br, per_group in enumerate((o_cmps[b], accs[lo_:hi_], accs[n_slc + lo_:n_slc + hi_])):
            gate = _dot(g_hi, gate_expand[br]) + _dot(g_lo, gate_expand[br])
            for r in range(rq):
                first, second = (a[r * tq:(r + 1) * tq] for a in per_group)
                pair = _pair_lanes(first, second, normalise=br > 0)
                out[r] = out[r] + gate[:, r * LANES:(r + 1) * LANES] * pair
        for r in range(rq):
            o_ref[b, :, r * LANES:(r + 1) * LANES] = out[r].astype(BF16)


def _nsa(bounds, qa, kcmp, vcmp, ksw, vsw, small, toeplitz, cmp_bias, batch, seq, exact):
    tq = ATT_TILE
    nq = seq // tq
    rq = NSA_Q_PER_GROUP
    n_blocks = seq // SLC_BLOCK
    n_cmp = (seq - CMP_BLOCK) // CMP_STRIDE + 1
    ci = np.arange(LANES)[:, None] * CMP_STRIDE
    sj = np.arange(LANES)[None, :] * SLC_BLOCK
    overlap = ((ci <= sj + SLC_BLOCK - 1) & (ci + CMP_BLOCK - 1 >= sj)
               & (np.arange(LANES)[:, None] < n_cmp) & (np.arange(LANES)[None, :] < n_blocks))
    expand = np.arange(LANES)[:, None] == (np.arange(seq)[None, :] // SLC_BLOCK)
    gate_expand = np.zeros((3, LANES, NSA_W), np.float32)
    for b in range(3):
        for g in range(NSA_KV_GROUPS):
            for r in range(rq):
                c0 = r * LANES + g * HEAD_DIM
                gate_expand[b, b * NSA_HEADS + g * rq + r, c0:c0 + HEAD_DIM] = 1.0
    per_step = 2 if batch % 2 == 0 else 1

    def rows(n_rows, width):
        return pl.BlockSpec((per_step, n_rows, width), lambda b, t: (b, 0, 0))

    def tile(width):
        return pl.BlockSpec((per_step, tq, width), lambda b, t: (b, t, 0))

    out = pl.pallas_call(
        functools.partial(_nsa_kernel, n_blocks=n_blocks, exact=exact),
        grid=(batch // per_step, nq),
        in_specs=[
            pl.BlockSpec(memory_space=pltpu.SMEM),
            tile(NSA_W),
            rows(LANES, NSA_KV_W),
            rows(LANES, NSA_KV_W),
            rows(seq, 2 * NSA_KV_W),
            rows(seq, 2 * NSA_KV_GROUPS * LANES),
            tile(LANES),
            _const_spec((N_WIN_TILES + 1, NSA_KV_GROUPS, rq, tq, tq)),
            pl.BlockSpec((1, NSA_KV_GROUPS, rq, LANES, tq), lambda b, t: (t, 0, 0, 0, 0)),
            _const_spec((LANES, LANES)),
            _const_spec((LANES, seq)),
            _const_spec((3, LANES, NSA_W)),
        ],
        out_specs=tile(NSA_W),
        out_shape=jax.ShapeDtypeStruct((batch, seq, NSA_W), BF16),
        scratch_shapes=[pltpu.VMEM((per_step * NSA_KV_GROUPS, tq, seq), F32),
                        pltpu.VMEM((per_step * 2 * NSA_KV_GROUPS, rq * tq, LANES), F32)],
        compiler_params=_params(2),
        name="nsa",
    )(bounds, qa.reshape(batch, seq, NSA_W), kcmp, vcmp, ksw.reshape(batch, seq, 2 * NSA_KV_W),
      vsw.reshape(batch, seq, 2 * NSA_KV_GROUPS * LANES), small.reshape(batch, seq, LANES),
      toeplitz, cmp_bias,
      jnp.asarray(overlap.T, F32), jnp.asarray(expand, F32), jnp.asarray(gate_expand, F32))
    return out.reshape(batch * seq, NSA_W)


def _fox_kernel(bnd_ref, q_ref, k_ref, v_ref, cum_ref, cum_t_ref, o_ref, acc_ref, *, exact):
    qt = pl.program_id(1)
    tq = ATT_TILE
    pairs = FOX_HEADS // 2
    row = lax.broadcasted_iota(jnp.int32, (2, tq, tq), 1)
    col = lax.broadcasted_iota(jnp.int32, (2, tq, tq), 2)
    lane = lax.broadcasted_iota(jnp.int32, (tq, LANES), 1)
    low = jnp.where(lane < HEAD_DIM, 1.0, 0.0).astype(BF16)
    high = jnp.where(lane < HEAD_DIM, 0.0, 1.0).astype(BF16)
    n_seq = q_ref.shape[0]

    def key_rows(dl, n=1):
        return pl.ds(pl.multiple_of((qt - dl - (n - 1)) * tq, tq), n * tq)

    bounds = []

    def pair_chain(b, p):
        pair = q_ref[b, :, p * LANES:(p + 1) * LANES]
        q = jnp.concatenate([pair * low, pair * high], axis=0)
        base = cum_ref[b, 2 * p:2 * p + 2, pl.ds(pl.multiple_of(qt * tq, tq), LANES)][:, 0:1]

        def decay_of(rows):
            return (base - cum_ref[b, 2 * p:2 * p + 2, rows]) * LOG2E

        def bound():
            cum_rows = cum_t_ref[b]
            c0 = FORGET_LANE + 2 * p
            own = jnp.concatenate([base[h:h + 1, :] - cum_rows[:, c0 + h:c0 + h + 1] for h in range(2)],
                                  axis=0)
            return jnp.broadcast_to(own * LOG2E + bnd_ref[0], (2 * tq, tq))

        bounds.append(bound)

        def score(dl):
            rows = key_rows(dl)
            decay = decay_of(rows)
            s = _dot_nt(q, k_ref[b, rows, p * LANES:(p + 1) * LANES]).reshape(2, tq, tq) + decay[:, None, :]
            if isinstance(dl, int):
                s = jnp.where(col <= row, s, NEG_INF)
            return s.reshape(2 * tq, tq)

        def pv(w, dl, n=1):
            rows = key_rows(dl, n)
            return jnp.concatenate(
                [_dot(w[0:tq], v_ref[b, rows, 2 * p * LANES:(2 * p + 1) * LANES]),
                 _dot(w[tq:2 * tq], v_ref[b, rows, (2 * p + 1) * LANES:(2 * p + 2) * LANES])], axis=0)

        return score, pv

    chains = [pair_chain(b, p) for b in range(n_seq) for p in range(pairs)]
    accs = _attention(chains, [(qt, tuple(range(len(chains))))], None if exact else bounds, acc_ref)
    for i, acc in enumerate(accs):
        b, p = divmod(i, pairs)
        o_ref[b, :, p * LANES:(p + 1) * LANES] = _pair_lanes(acc[0:tq], acc[tq:2 * tq], True).astype(BF16)


def _fox(bounds, qb, kb, vb, cum, cum_cols, batch, seq, exact):
    tq = ATT_TILE
    nq = seq // tq
    per_step = 2 if batch % 2 == 0 else 1
    out = pl.pallas_call(
        functools.partial(_fox_kernel, exact=exact),
        grid=(batch // per_step, nq),
        in_specs=[
            pl.BlockSpec(memory_space=pltpu.SMEM),
            pl.BlockSpec((per_step, tq, FOX_W), lambda b, t: (b, t, 0)),
            pl.BlockSpec((per_step, seq, FOX_W), lambda b, t: (b, 0, 0)),
            pl.BlockSpec((per_step, seq, FOX_HEADS * LANES), lambda b, t: (b, 0, 0)),
            pl.BlockSpec((per_step, FOX_HEADS, seq), lambda b, t: (b, 0, 0)),
            pl.BlockSpec((per_step, tq, LANES), lambda b, t: (b, t, 0)),
        ],
        out_specs=pl.BlockSpec((per_step, tq, FOX_W), lambda b, t: (b, t, 0)),
        out_shape=jax.ShapeDtypeStruct((batch, seq, FOX_W), BF16),
        scratch_shapes=[pltpu.VMEM((per_step * FOX_HEADS // 2, 2 * tq, LANES), F32)],
        compiler_params=_params(2),
        name="fox",
    )(bounds, qb.reshape(batch, seq, FOX_W), kb.reshape(batch, seq, FOX_W),
      vb.reshape(batch, seq, FOX_HEADS * LANES), cum, cum_cols.reshape(batch, seq, LANES))
    return out.reshape(batch * seq, FOX_W)


def _merge_ffn_kernel(x_ref, on_ref, of_ref, gmix_ref, wgm_ref, won_ref, wof_ref, wout_ref,
                      g2_ref, wup_ref, wdn_ref, o_ref):
    x = x_ref[...]
    u = _rms_rows(x, gmix_ref[...]).astype(BF16)
    gate = jax.nn.sigmoid(_dot(u, wgm_ref[...]))
    merged = (gate[:, :D_MODEL] * _dot(on_ref[...], won_ref[...])
              + gate[:, D_MODEL:] * _dot(of_ref[...], wof_ref[...]))
    x2 = x + _dot(merged.astype(BF16), wout_ref[...])
    o_ref[...] = _swiglu_residual(x2, g2_ref[...], wup_ref, wdn_ref)


def _merge_ffn(x2d, o_nsa, o_fox, mix_norm, w_gm, w_o_nsa, w_o_fox, w_out, gain2, w_up, w_down):
    n = x2d.shape[0]
    tm = _token_tile(n, wide=False)

    def row(width):
        return pl.BlockSpec((tm, width), lambda i: (i, 0))

    return pl.pallas_call(
        _merge_ffn_kernel,
        grid=(n // tm,),
        in_specs=[row(D_MODEL), row(NSA_W), row(FOX_W), _const_spec((1, D_MODEL)),
                  _const_spec((D_MODEL, 2 * D_MODEL)), _const_spec((NSA_W, D_MODEL)),
                  _const_spec((FOX_W, D_MODEL)), _const_spec((D_MODEL, D_MODEL)),
                  _const_spec((1, D_MODEL)), _const_spec((D_MODEL, 2 * D_FF)),
                  _const_spec((D_FF, D_MODEL))],
        out_specs=row(D_MODEL),
        out_shape=jax.ShapeDtypeStruct((n, D_MODEL), F32),
        compiler_params=_params(1),
        name="merge_ffn",
    )(x2d, o_nsa, o_fox, mix_norm.reshape(1, D_MODEL), w_gm, w_o_nsa.astype(BF16),
      w_o_fox.astype(BF16), w_out.astype(BF16), gain2.reshape(1, D_MODEL), w_up.astype(BF16),
      w_down.astype(BF16))


def _layer(x, ffn1_norm, ffn1_w_up, ffn1_w_down, mix_norm, w_in, b_forget, nsa_q_gain, nsa_k_gain,
           fox_q_gain, fox_k_gain, cmp_pos_k, cmp_pos_v, cmp_k_w1, cmp_k_w2, cmp_v_w1, cmp_v_w2,
           w_o_nsa, w_o_fox, w_out, ffn2_norm, ffn2_w_up, ffn2_w_down, rel_bias_table):
    batch, seq, d = x.shape
    assert d == D_MODEL and seq % ATT_TILE == 0 and (batch * seq) % TOKEN_TILE == 0
    assert seq // SLC_BLOCK <= LANES and (seq - CMP_BLOCK) // CMP_STRIDE + 1 == LANES - 1
    assert seq // (CMP_BLOCK // 2) == LANES
    x2d = x.reshape(batch * seq, D_MODEL)

    x1 = _ffn(x2d, ffn1_norm, ffn1_w_up, ffn1_w_down)

    w_packed, gain_row, small_bias, w_gm = _pack_in_proj(w_in, b_forget, nsa_q_gain, nsa_k_gain,
                                                         fox_q_gain, fox_k_gain)
    qa, qb, kb, ksw, vb, vsw, kc, vc, small = _in_proj(x1, mix_norm, w_packed, gain_row, small_bias,
                                                       batch, seq)

    kcmp, vcmp = _compress(kc, vc, batch, seq, _pack_compress(cmp_pos_k, cmp_k_w1, cmp_k_w2),
                           _pack_compress(cmp_pos_v, cmp_v_w1, cmp_v_w2), nsa_k_gain[0])
    toeplitz, cmp_bias = _bias_tables(rel_bias_table, seq // ATT_TILE)
    b_slc = _score_bound(nsa_q_gain, nsa_k_gain[1])
    b_win = _score_bound(nsa_q_gain, nsa_k_gain[2])
    t_hi, t_lo = jnp.max(rel_bias_table) * LOG2E, jnp.min(rel_bias_table) * LOG2E
    nsa_ok = 2.0 * jnp.maximum(b_slc, b_win) + (t_hi - t_lo) < EXP2_RANGE
    nsa_bounds = jnp.stack([b_slc + t_hi, b_win + t_hi])
    o_nsa = lax.cond(
        nsa_ok,
        lambda: _nsa(nsa_bounds, qa, kcmp, vcmp, ksw, vsw, small, toeplitz, cmp_bias, batch, seq, False),
        lambda: _nsa(nsa_bounds, qa, kcmp, vcmp, ksw, vsw, small, toeplitz, cmp_bias, batch, seq, True))

    cum_cols, cum = _cumsum(small, batch, seq)
    b_fox = _score_bound(fox_q_gain, fox_k_gain)
    fox_bounds = jnp.stack([b_fox])
    o_fox = lax.cond(2.0 * b_fox < EXP2_RANGE,
                     lambda: _fox(fox_bounds, qb, kb, vb, cum, cum_cols, batch, seq, False),
                     lambda: _fox(fox_bounds, qb, kb, vb, cum, cum_cols, batch, seq, True))

    w_o_nsa_p = w_o_nsa.reshape(NSA_KV_GROUPS, NSA_Q_PER_GROUP, HEAD_DIM, D_MODEL).transpose(1, 0, 2, 3)
    out = _merge_ffn(x1, o_nsa, o_fox, mix_norm, w_gm, w_o_nsa_p.reshape(NSA_W, D_MODEL), w_o_fox,
                     w_out, ffn2_norm, ffn2_w_up, ffn2_w_down)
    return out.reshape(batch, seq, D_MODEL)


def kernel(x, ffn1_norm, ffn1_w_up, ffn1_w_down, mix_norm, w_in, b_forget, nsa_q_gain, nsa_k_gain,
           fox_q_gain, fox_k_gain, cmp_pos_k, cmp_pos_v, cmp_k_w1, cmp_k_w2, cmp_v_w1, cmp_v_w2,
           w_o_nsa, w_o_fox, w_out, ffn2_norm, ffn2_w_up, ffn2_w_down, rel_bias_table):
    for layer in range(ffn1_norm.shape[0]):
        x = _layer(x, ffn1_norm[layer], ffn1_w_up[layer], ffn1_w_down[layer], mix_norm[layer],
                   w_in[layer], b_forget[layer], nsa_q_gain[layer], nsa_k_gain[layer],
                   fox_q_gain[layer], fox_k_gain[layer], cmp_pos_k[layer], cmp_pos_v[layer],
                   cmp_k_w1[layer], cmp_k_w2[layer], cmp_v_w1[layer], cmp_v_w2[layer],
                   w_o_nsa[layer], w_o_fox[layer], w_out[layer], ffn2_norm[layer],
                   ffn2_w_up[layer], ffn2_w_down[layer], rel_bias_table)
    return x
```

```python
import functools
import math

import numpy as np
import jax
import jax.numpy as jnp
from jax import lax
from jax.experimental import pallas as pl
from jax.experimental.pallas import tpu as pltpu

F32 = jnp.float32
BF16 = jnp.bfloat16

D_MODEL = 1024
HEAD_DIM = 64
NSA_HEADS = 8
NSA_KV_GROUPS = 2
NSA_Q_PER_GROUP = NSA_HEADS // NSA_KV_GROUPS
CMP_BLOCK = 32
CMP_STRIDE = 16
CMP_HIDDEN = 128
SLC_BLOCK = 64
SLC_TOPK = 8
WINDOW = 512
FOX_HEADS = 8
D_FF = 2816
N_BUCKETS = 32
MAX_DISTANCE = 128
RMS_EPS = 1e-6
NEG_INF = -1.0e30
FORCE_BONUS = 1.0e4

NSA_W = NSA_HEADS * HEAD_DIM
NSA_KV_W = NSA_KV_GROUPS * HEAD_DIM
FOX_W = FOX_HEADS * HEAD_DIM
IN_SPLITS = (NSA_W, NSA_KV_W, NSA_KV_W, NSA_KV_W, NSA_KV_W, NSA_KV_W, NSA_KV_W, 3 * NSA_HEADS,
             FOX_W, FOX_W, FOX_W, FOX_HEADS, 2 * D_MODEL)

LANES = 128
TOKEN_TILE = 512
MXU_DIM = 256
FFN_CHUNKS = (-(-D_FF // (2 * MXU_DIM)) * MXU_DIM, D_FF - -(-D_FF // (2 * MXU_DIM)) * MXU_DIM)
ATT_TILE = 256
N_WIN_TILES = WINDOW // ATT_TILE + 1
VMEM_LIMIT = 56 * 1024 * 1024
LOG2E = 1.4426950408889634
EXP2_RANGE = 100.0

PQ_A = 0
PQ_B = PQ_A + NSA_W
PK_B = PQ_B + FOX_W
PK_SW = PK_B + FOX_W
P_NORM_END = PK_SW + 2 * NSA_KV_W
PV_B = P_NORM_END
PV_SW = PV_B + FOX_W
PKV_C = PV_SW + 2 * NSA_KV_W
P_SMALL = PKV_C + 2 * NSA_KV_W
P_END = P_SMALL + LANES
NORM_CHUNK = 256


def _dot(a, b):
    return jnp.dot(a, b, preferred_element_type=F32)


def _dot_nt(a, b):
    return lax.dot_general(a, b, (((1,), (1,)), ((), ())), preferred_element_type=F32)


def _split_dot(x, w):
    hi = x.astype(BF16)
    lo = (x - hi.astype(F32)).astype(BF16)
    return _dot(hi, w) + _dot(lo, w)


def _rms_rows(x, gain_row):
    ms = jnp.mean(x * x, axis=-1, keepdims=True)
    return x * lax.rsqrt(ms + RMS_EPS) * gain_row


def _const_spec(shape):
    nd = len(shape)
    return pl.BlockSpec(shape, lambda *_: (0,) * nd, pipeline_mode=pl.Buffered(1))


def _token_tile(n_rows, wide):
    tile = 2 * TOKEN_TILE if wide and n_rows % (2 * TOKEN_TILE) == 0 else TOKEN_TILE
    assert n_rows % tile == 0
    return tile


def _params(n_axes):
    return pltpu.CompilerParams(dimension_semantics=("arbitrary",) * n_axes,
                                vmem_limit_bytes=VMEM_LIMIT)


def _swiglu_residual(x, gain_row, wup_ref, wdn_ref):
    xn = _rms_rows(x, gain_row).astype(BF16)
    acc = jnp.zeros(x.shape, F32)
    lo = 0
    for width in FFN_CHUNKS:
        gate = _dot(xn, wup_ref[:, lo:lo + width])
        up = _dot(xn, wup_ref[:, D_FF + lo:D_FF + lo + width])
        h = (gate * jax.nn.sigmoid(gate) * up).astype(BF16)
        acc = acc + _dot(h, wdn_ref[lo:lo + width, :])
        lo += width
    return x + 0.5 * acc


def _ffn_kernel(x_ref, g_ref, wup_ref, wdn_ref, o_ref):
    o_ref[...] = _swiglu_residual(x_ref[...], g_ref[...], wup_ref, wdn_ref)


def _ffn(x2d, gain, w_up, w_down):
    n = x2d.shape[0]
    tm = _token_tile(n, wide=True)
    row = pl.BlockSpec((tm, D_MODEL), lambda i: (i, 0))
    return pl.pallas_call(
        _ffn_kernel,
        grid=(n // tm,),
        in_specs=[row, _const_spec((1, D_MODEL)), _const_spec((D_MODEL, 2 * D_FF)),
                  _const_spec((D_FF, D_MODEL))],
        out_specs=row,
        out_shape=jax.ShapeDtypeStruct((n, D_MODEL), F32),
        compiler_params=_params(1),
        name="ffn",
    )(x2d, gain.reshape(1, D_MODEL), w_up.astype(BF16), w_down.astype(BF16))


def _block_ones(width, size=NORM_CHUNK):
    idx = np.arange(size) // width
    return jnp.asarray((idx[:, None] == idx[None, :]).astype(np.float32), BF16)


def _pack_in_proj(w_in, b_forget, nsa_q_gain, nsa_k_gain, fox_q_gain, fox_k_gain):
    scale = LOG2E / math.sqrt(HEAD_DIM)
    pts = np.cumsum(np.array(IN_SPLITS))[:-1].tolist()
    qa, kc, vc, ks, vs, kw, vw, ga, qb, kb, vb, fb, gm = jnp.split(w_in, pts, axis=1)
    cols = []
    for r in range(NSA_Q_PER_GROUP):
        for g in range(NSA_KV_GROUPS):
            h = g * NSA_Q_PER_GROUP + r
            cols.append(qa[:, h * HEAD_DIM:(h + 1) * HEAD_DIM])
    cols += [qb, kb, ks, kw]
    gains = [jnp.tile(nsa_q_gain * scale, NSA_HEADS), jnp.tile(fox_q_gain * scale, FOX_HEADS),
             jnp.tile(fox_k_gain, FOX_HEADS), jnp.tile(nsa_k_gain[1], NSA_KV_GROUPS),
             jnp.tile(nsa_k_gain[2], NSA_KV_GROUPS)]
    n_small_pad = LANES - ga.shape[1] - fb.shape[1]
    cols += [vb, vs, vw, kc, vc, ga, fb, jnp.zeros((D_MODEL, n_small_pad), F32)]
    w_packed = jnp.concatenate(cols, axis=1).astype(BF16)
    gain_row = jnp.concatenate(gains).reshape(1, P_NORM_END)
    small_bias = jnp.concatenate([jnp.zeros((ga.shape[1],), F32), b_forget,
                                  jnp.zeros((n_small_pad,), F32)]).reshape(1, LANES)
    return w_packed, gain_row, small_bias, gm.astype(BF16)


def _in_proj_kernel(x_ref, g_ref, w_ref, gain_ref, sbias_ref, bd64_ref,
                    qa_ref, qb_ref, kb_ref, ksw_ref, vb_ref, vsw_ref, kc_ref, vc_ref, small_ref, kv_scr):
    u = _rms_rows(x_ref[...], g_ref[...]).astype(BF16)
    tm = u.shape[0]

    y = _dot(u, w_ref[:, 0:P_NORM_END])
    n_chunks = P_NORM_END // NORM_CHUNK
    squares = jnp.concatenate([jnp.square(y[:, c * NORM_CHUNK:(c + 1) * NORM_CHUNK]).astype(BF16)
                               for c in range(n_chunks)], axis=0)
    ss = _dot(squares, bd64_ref[...])
    chunk = 0
    for o_ref in (qa_ref, qb_ref, kb_ref, ksw_ref):
        for i in range(o_ref.shape[1] // NORM_CHUNK):
            cols = slice(chunk * NORM_CHUNK, (chunk + 1) * NORM_CHUNK)
            inv = lax.rsqrt(ss[chunk * tm:(chunk + 1) * tm] * (1.0 / HEAD_DIM) + RMS_EPS)
            o_ref[:, i * NORM_CHUNK:(i + 1) * NORM_CHUNK] = (y[:, cols] * inv * gain_ref[:, cols]).astype(BF16)
            chunk += 1
    low = lax.broadcasted_iota(jnp.int32, (u.shape[0], LANES), 1) < HEAD_DIM

    def store_with_ones(v, o_ref):
        for i in range(v.shape[1] // LANES):
            pair = v[:, i * LANES:(i + 1) * LANES]
            o_ref[:, 2 * i * LANES:(2 * i + 1) * LANES] = jnp.where(low, pair, 1.0).astype(BF16)
            o_ref[:, (2 * i + 1) * LANES:(2 * i + 2) * LANES] = jnp.where(low, 1.0, pair).astype(BF16)

    rest = _dot(u, w_ref[:, P_NORM_END:P_END])
    store_with_ones(rest[:, PV_B - P_NORM_END:PV_SW - P_NORM_END], vb_ref)
    store_with_ones(rest[:, PV_SW - P_NORM_END:PKV_C - P_NORM_END], vsw_ref)
    for j, o_ref in enumerate((kc_ref, vc_ref)):
        c0 = PKV_C - P_NORM_END + j * NSA_KV_W
        kv_scr[j] = rest[:, c0:c0 + NSA_KV_W]
        for l in range(CMP_STRIDE):
            o_ref[0, :, l * NSA_KV_W:(l + 1) * NSA_KV_W] = kv_scr[j, pl.ds(l, tm // CMP_STRIDE,
                                                                           stride=CMP_STRIDE), :]
    z = rest[:, P_SMALL - P_NORM_END:] + sbias_ref[...]
    lane = lax.broadcasted_iota(jnp.int32, z.shape, 1)
    log_sig = jnp.minimum(z, 0.0) - jnp.log1p(jnp.exp(-jnp.abs(z)))
    small_ref[...] = jnp.where(lane < 3 * NSA_HEADS, jax.nn.sigmoid(z), log_sig)


def _in_proj(x2d, mix_norm, w_packed, gain_row, small_bias, batch, seq):
    n = x2d.shape[0]
    tm = _token_tile(seq, wide=True)
    steps_per_seq = seq // tm

    def row(width):
        return pl.BlockSpec((tm, width), lambda i: (i, 0))

    def out(width, dtype):
        return row(width), jax.ShapeDtypeStruct((n, width), dtype)

    grouped = (pl.BlockSpec((1, tm // CMP_STRIDE, CMP_STRIDE * NSA_KV_W),
                            lambda i: (i // steps_per_seq, i % steps_per_seq, 0)),
               jax.ShapeDtypeStruct((batch, seq // CMP_STRIDE, CMP_STRIDE * NSA_KV_W), F32))
    outs = [out(NSA_W, BF16), out(FOX_W, BF16), out(FOX_W, BF16), out(2 * NSA_KV_W, BF16),
            out(2 * FOX_W, BF16), out(4 * NSA_KV_W, BF16), grouped, grouped, out(LANES, F32)]
    return pl.pallas_call(
        _in_proj_kernel,
        grid=(n // tm,),
        in_specs=[row(D_MODEL), _const_spec((1, D_MODEL)), _const_spec((D_MODEL, P_END)),
                  _const_spec((1, P_NORM_END)), _const_spec((1, LANES)),
                  _const_spec((NORM_CHUNK, NORM_CHUNK))],
        out_specs=[spec for spec, _ in outs],
        out_shape=[shape for _, shape in outs],
        scratch_shapes=[pltpu.VMEM((2, tm, NSA_KV_W), F32)],
        compiler_params=_params(1),
        name="in_proj",
    )(x2d, mix_norm.reshape(1, D_MODEL), w_packed, gain_row, small_bias, _block_ones(HEAD_DIM))


def _pack_compress(pos, w1, w2):
    half = CMP_BLOCK // 2
    eye = jnp.eye(NSA_KV_GROUPS, dtype=F32)
    w1r = w1.reshape(CMP_BLOCK, HEAD_DIM, CMP_HIDDEN)

    def big(w):
        return jnp.einsum('ldh,pg->lpdgh', w, eye).reshape(half * NSA_KV_W, NSA_KV_GROUPS * CMP_HIDDEN)

    def posrow(p):
        return jnp.broadcast_to(p[:, None, :], (half, NSA_KV_GROUPS, HEAD_DIM)).reshape(1, half * NSA_KV_W)

    w2big = jnp.einsum('hd,pg->phgd', w2, eye).reshape(NSA_KV_GROUPS * CMP_HIDDEN, NSA_KV_W)
    return (posrow(pos[:half]), posrow(pos[half:]), big(w1r[:half]).astype(BF16),
            big(w1r[half:]).astype(BF16), w2big.astype(BF16))


def _compress_kernel(kc_ref, vc_ref, kp_lo, kp_hi, kw_lo, kw_hi, kw2, vp_lo, vp_hi, vw_lo, vw_hi, vw2,
                     kgain_ref, bd64_ref, ko_ref, vo_ref):
    n_seq, rows, width = kc_ref.shape

    def mlp(x_ref, p_lo, p_hi, w_lo, w_hi, w2):
        r = x_ref[...].reshape(n_seq * rows, width)
        first = _dot((r + p_lo[...]).astype(BF16), w_lo[...])
        second = _dot((r + p_hi[...]).astype(BF16), w_hi[...])
        h = first + pltpu.roll(second, second.shape[0] - 1, axis=0)
        return _dot((h * jax.nn.sigmoid(h)).astype(BF16), w2[...])

    k = mlp(kc_ref, kp_lo, kp_hi, kw_lo, kw_hi, kw2)
    ss = _split_dot(k * k, bd64_ref[...])
    k = (k * lax.rsqrt(ss * (1.0 / HEAD_DIM) + RMS_EPS) * kgain_ref[...]).astype(BF16)
    ko_ref[...] = k.reshape(ko_ref.shape)
    vo_ref[...] = mlp(vc_ref, vp_lo, vp_hi, vw_lo, vw_hi, vw2).astype(BF16).reshape(vo_ref.shape)


def _compress(kc, vc, batch, seq, k_pack, v_pack, k_gain):
    rows = seq // (CMP_BLOCK // 2)
    width = (CMP_BLOCK // 2) * NSA_KV_W
    per_step = next(n for n in (4, 2, 1) if batch % n == 0)
    blk = pl.BlockSpec((per_step, rows, width), lambda b: (b, 0, 0))
    out = pl.BlockSpec((per_step, rows, NSA_KV_W), lambda b: (b, 0, 0))
    pack_specs = [_const_spec((1, width)), _const_spec((1, width)),
                  _const_spec((width, NSA_KV_GROUPS * CMP_HIDDEN)),
                  _const_spec((width, NSA_KV_GROUPS * CMP_HIDDEN)),
                  _const_spec((NSA_KV_GROUPS * CMP_HIDDEN, NSA_KV_W))]
    bd = _block_ones(HEAD_DIM, NSA_KV_W)
    return pl.pallas_call(
        _compress_kernel,
        grid=(batch // per_step,),
        in_specs=[blk, blk] + pack_specs + pack_specs + [_const_spec((1, NSA_KV_W)),
                                                         _const_spec((NSA_KV_W, NSA_KV_W))],
        out_specs=[out, out],
        out_shape=[jax.ShapeDtypeStruct((batch, rows, NSA_KV_W), BF16)] * 2,
        compiler_params=_params(1),
        name="compress",
    )(kc.reshape(batch, rows, width), vc.reshape(batch, rows, width), *k_pack, *v_pack,
      jnp.tile(k_gain, NSA_KV_GROUPS).reshape(1, NSA_KV_W), bd)


FORGET_LANE = 3 * NSA_HEADS


def _cumsum_kernel(x_ref, cols_ref, rows_ref):
    x = x_ref[...]
    row = lax.broadcasted_iota(jnp.int32, x.shape, 0)
    k = 1
    while k < x.shape[0]:
        x = x + jnp.where(row >= k, pltpu.roll(x, k, axis=0), 0.0)
        k *= 2
    cols_ref[...] = x
    rows_ref[0] = x.T[FORGET_LANE:FORGET_LANE + FOX_HEADS]


def _cumsum(small, batch, seq):
    spec = pl.BlockSpec((seq, LANES), lambda b: (b, 0))
    return pl.pallas_call(
        _cumsum_kernel, grid=(batch,), in_specs=[spec],
        out_specs=[spec, pl.BlockSpec((1, FOX_HEADS, seq), lambda b: (b, 0, 0))],
        out_shape=[jax.ShapeDtypeStruct((batch * seq, LANES), F32),
                   jax.ShapeDtypeStruct((batch, FOX_HEADS, seq), F32)],
        compiler_params=_params(1), name="cumsum",
    )(small)


def _write_bias_rows(d, valid, tab_ref, o_ref, rows):
    max_exact = N_BUCKETS // 2
    n = jnp.maximum(d, 0)
    nf = jnp.maximum(n, 1).astype(F32)
    large = max_exact + (jnp.log(nf / max_exact) / math.log(MAX_DISTANCE / max_exact)
                         * (N_BUCKETS - max_exact)).astype(jnp.int32)
    bucket = jnp.where(n < max_exact, n, jnp.minimum(large, N_BUCKETS - 1))
    for h in range(NSA_HEADS):
        acc = jnp.zeros(d.shape, F32)
        for b in range(N_BUCKETS):
            acc = jnp.where(bucket == b, tab_ref[b, h], acc)
        g, r = divmod(h, NSA_Q_PER_GROUP)
        o_ref[0, g, r, rows, :] = jnp.where(valid, acc * LOG2E, NEG_INF)


def _write_far_tile(valid, tab_ref, o_ref):
    for h in range(NSA_HEADS):
        g, r = divmod(h, NSA_Q_PER_GROUP)
        far = jnp.where(valid, tab_ref[N_BUCKETS - 1, h], 0.0) * LOG2E
        o_ref[0, g, r] = jnp.where(valid, far, NEG_INF)


def _toeplitz_bias_kernel(tab_ref, o_ref):
    s = pl.program_id(0)
    tile = ATT_TILE

    def distances(n_rows):
        i = lax.broadcasted_iota(jnp.int32, (n_rows, tile), 0)
        j = lax.broadcasted_iota(jnp.int32, (n_rows, tile), 1)
        d = jnp.minimum(s, N_WIN_TILES - 1) * tile + i - j
        d_hi = jnp.where(s < N_WIN_TILES, WINDOW, jnp.int32(1 << 30))
        return d, (d >= 0) & (d < d_hi)

    _write_far_tile(distances(tile)[1], tab_ref, o_ref)
    for step in range(N_WIN_TILES + 1):
        near = min(tile, max(0, MAX_DISTANCE + tile - 1 - min(step, N_WIN_TILES - 1) * tile))
        near = -(-near // 8) * 8
        if near:
            @pl.when(s == step)
            def _(near=near):
                d, valid = distances(near)
                _write_bias_rows(d, valid, tab_ref, o_ref, slice(0, near))


def _cmp_bias_kernel(tab_ref, o_ref):
    t = pl.program_id(0)
    chunk = 8

    def distances(c0, n_rows):
        c = c0 + lax.broadcasted_iota(jnp.int32, (n_rows, ATT_TILE), 0)
        i = lax.broadcasted_iota(jnp.int32, (n_rows, ATT_TILE), 1)
        d = t * ATT_TILE + i - CMP_STRIDE * c - (CMP_BLOCK - 1)
        return d, (d >= 0) & (c < LANES - 1)

    _write_far_tile(distances(0, LANES)[1], tab_ref, o_ref)
    first = (t * ATT_TILE - (MAX_DISTANCE + CMP_BLOCK - 1)) // (chunk * CMP_STRIDE)
    n_chunks = (ATT_TILE + MAX_DISTANCE) // (chunk * CMP_STRIDE) + 2

    def refine(m, carry):
        c0 = pl.multiple_of(jnp.clip(first + m, 0, LANES // chunk - 1) * chunk, chunk)
        d, valid = distances(c0, chunk)
        _write_bias_rows(d, valid, tab_ref, o_ref, pl.ds(c0, chunk))
        return carry

    lax.fori_loop(0, n_chunks, refine, 0)


def _bias_tables(rel_bias_table, n_q_tiles):
    def call(body, steps, rows, name):
        shape = (steps, NSA_KV_GROUPS, NSA_Q_PER_GROUP, rows, ATT_TILE)
        return pl.pallas_call(
            body, grid=(steps,),
            in_specs=[pl.BlockSpec(memory_space=pltpu.SMEM)],
            out_specs=pl.BlockSpec((1,) + shape[1:], lambda s: (s, 0, 0, 0, 0)),
            out_shape=jax.ShapeDtypeStruct(shape, F32),
            compiler_params=_params(1), name=name,
        )(rel_bias_table)

    return (call(_toeplitz_bias_kernel, N_WIN_TILES + 1, ATT_TILE, "toeplitz_bias"),
            call(_cmp_bias_kernel, n_q_tiles, LANES, "cmp_bias"))


def _score_bound(q_gain, k_gain):
    return 1.02 * LOG2E * math.sqrt(HEAD_DIM) * jnp.max(jnp.abs(q_gain)) * jnp.max(jnp.abs(k_gain))


def _attention(chains, groups, bounds, acc_ref):
    n = len(chains)

    def later(k):
        return tuple(i for _, ids in groups[k:] for i in ids)

    def exact_max():
        def half_max(i, dl):
            s = chains[i][0](dl)
            return jnp.maximum(s[:, :LANES], s[:, LANES:])

        mx = [half_max(i, 0) for i in range(n)]
        lo = 1
        for k, (last, _) in enumerate(groups):
            members = later(k)

            def body(dl, carry, members=members):
                return tuple(jnp.maximum(c, half_max(i, dl)) for c, i in zip(carry, members))

            for i, r in zip(members, lax.fori_loop(lo, last + 1, body, tuple(mx[i] for i in members))):
                mx[i] = r
            lo = last + 1
        return tuple(jnp.broadcast_to(jnp.max(m, axis=-1, keepdims=True), (m.shape[0], ATT_TILE))
                     for m in mx)

    shift = exact_max() if bounds is None else [b() for b in bounds]

    def weights(i, dl):
        return jnp.exp2(chains[i][0](dl) - shift[i]).astype(BF16)

    def product(i, dl):
        return chains[i][1](weights(i, dl), dl)

    for i in range(n):
        acc_ref[i] = product(i, 0)
    lo = 1
    for k, (last, _) in enumerate(groups):
        members = later(k)
        count = last + 1 - lo

        def body(j, carry, members=members, lo=lo):
            dl = lo + 2 * j
            for i in members:
                both = jnp.concatenate([weights(i, dl + 1), weights(i, dl)], axis=1)
                acc_ref[i] += chains[i][1](both, dl, 2)
            return carry

        lax.fori_loop(0, count // 2, body, 0)

        @pl.when(count % 2 == 1)
        def _(members=members, last=last):
            for i in members:
                acc_ref[i] += product(i, last)

        lo = last + 1
    return [acc_ref[i] for i in range(n)]


def _pair_lanes(first, second, normalise):
    low = lax.broadcasted_iota(jnp.int32, first.shape, 1) < HEAD_DIM
    pair = jnp.where(low, first, second)
    if not normalise:
        return pair
    sums = pltpu.roll(jnp.where(low, second, first), HEAD_DIM, axis=1)
    return pair * (1.0 / sums)


def _select_blocks(imp_t, t0, tq):
    blk = lax.broadcasted_iota(jnp.int32, imp_t.shape, 0)
    cur = (t0 + (lax.broadcasted_iota(jnp.int32, imp_t.shape, 1) & (tq - 1))) // SLC_BLOCK
    forced = (blk == 0) | (blk == cur) | (blk == cur - 1)
    score = jnp.where(blk <= cur, imp_t + jnp.where(forced, FORCE_BONUS, 0.0), NEG_INF)
    blk_f = blk.astype(F32)
    dead = -3.0e38
    picked = jnp.zeros(imp_t.shape, F32)
    for _ in range(SLC_TOPK):
        best = jnp.max(score, axis=0, keepdims=True)
        first = jnp.min(jnp.where(score == best, blk_f, float(LANES)), axis=0, keepdims=True)
        hit = blk_f == first
        picked = jnp.where(hit, 1.0, picked)
        score = jnp.where(hit, dead, score)
    return picked


def _nsa_kernel(bnd_ref, qa_ref, kcmp_ref, vcmp_ref, ksw_ref, vsw_ref, small_ref, tb_ref, bc_ref, ov_ref,
                e_ref, gx_ref, o_ref, amask_ref, acc_ref, *, n_blocks, exact):
    qt = pl.program_id(1)
    tq = ATT_TILE
    rq = NSA_Q_PER_GROUP
    lane = lax.broadcasted_iota(jnp.int32, (tq, LANES), 1)
    own_lanes = [jnp.where((lane // HEAD_DIM) == g, 1.0, 0.0).astype(BF16) for g in range(NSA_KV_GROUPS)]

    def key_rows(dl, n=1):
        return pl.ds(pl.multiple_of((qt - dl - (n - 1)) * tq, tq), n * tq)

    groups = range(NSA_KV_GROUPS)
    n_seq = qa_ref.shape[0]
    n_grp = NSA_KV_GROUPS

    def win_chain(b, g, q):
        def score(dl):
            s = _dot_nt(q, ksw_ref[b, key_rows(dl), LANES:2 * LANES]).reshape(rq, tq, tq)
            return (s + tb_ref[dl, g]).reshape(rq * tq, tq)

        def pv(p, dl, n=1):
            c0 = (n_grp + g) * LANES
            return _dot(p, vsw_ref[b, key_rows(dl, n), c0:c0 + LANES])

        return score, pv

    def slc_chain(b, g, q):
        def score(dl):
            tile = dl if isinstance(dl, int) else jnp.where(dl < N_WIN_TILES - 1, dl, N_WIN_TILES)
            rows = key_rows(dl)
            s = _dot_nt(q, ksw_ref[b, rows, 0:LANES]).reshape(rq, tq, tq)
            return (s + tb_ref[tile, g] + amask_ref[b * n_grp + g, :, rows][None]).reshape(rq * tq, tq)

        def pv(p, dl, n=1):
            return _dot(p, vsw_ref[b, key_rows(dl, n), g * LANES:(g + 1) * LANES])

        return score, pv

    overlap = ov_ref[...].astype(BF16)
    key_block = e_ref[...].astype(BF16)
    gate_expand = [gx_ref[br].astype(BF16) for br in range(3)]

    slc_chains, win_chains, o_cmps = [], [], []
    for b in range(n_seq):
        qs = [jnp.concatenate([qa_ref[b, :, r * LANES:(r + 1) * LANES] * own_lanes[g] for r in range(rq)],
                              axis=0) for g in groups]
        slc_chains += [slc_chain(b, g, qs[g]) for g in groups]
        win_chains += [win_chain(b, g, qs[g]) for g in groups]

        s = (_dot_nt(kcmp_ref[b], jnp.concatenate(qs, axis=0))
             + jnp.concatenate([bc_ref[0, g, r] for g in groups for r in range(rq)], axis=1))
        m = jnp.max(s, axis=0, keepdims=True)
        p = jnp.where(s > 0.5 * NEG_INF, jnp.exp2(s - m), 0.0)
        l = jnp.sum(p, axis=0, keepdims=True)
        p_c = p * (1.0 / jnp.where(l > 0.0, l, 1.0))
        o_cmp = _dot(p_c.T.astype(BF16), vcmp_ref[b])
        o_cmps.append([o_cmp[g * rq * tq:(g + 1) * rq * tq] for g in groups])

        p_sum = jnp.concatenate([sum(p_c[:, (g * rq + r) * tq:(g * rq + r + 1) * tq] for r in range(rq))
                                 for g in groups], axis=1)
        hi = p_sum.astype(BF16)
        lo = (p_sum - hi.astype(F32)).astype(BF16)
        imp_t = _dot(overlap, hi) + _dot(overlap, lo)
        sel_t = _select_blocks(imp_t[0:n_blocks], qt * tq, tq)
        block_bias = jnp.concatenate([jnp.where(sel_t > 0.0, 0.0, NEG_INF),
                                      jnp.zeros((LANES - n_blocks, n_grp * tq), F32)], axis=0)
        amask_ref[b * n_grp:(b + 1) * n_grp] = _dot(block_bias.T.astype(BF16), key_block).reshape(
            n_grp, tq, e_ref.shape[1])

    n_slc = len(slc_chains)
    slc_ids = tuple(range(n_slc))
    win_ids = tuple(range(n_slc, 2 * n_slc))
    bounds = None if exact else [lambda i=i: jnp.full((rq * tq, tq), bnd_ref[i // n_slc], F32)
                                 for i in range(2 * n_slc)]
    accs = _attention(slc_chains + win_chains,
                      [(jnp.minimum(qt, N_WIN_TILES - 1), win_ids), (qt, slc_ids)], bounds, acc_ref)

    for b in range(n_seq):
        gates = small_ref[b]
        g_hi = gates.astype(BF16)
        g_lo = (gates - g_hi.astype(F32)).astype(BF16)
        out = [jnp.zeros((tq, LANES), F32) for _ in range(rq)]
        lo_, hi_ = b * n_grp, (b + 1) * n_grp
        for br, per_group in enumerate((o_cmps[b], accs[lo_:hi_], accs[n_slc + lo_:n_slc + hi_])):
            gate = _dot(g_hi, gate_expand[br]) + _dot(g_lo, gate_expand[br])
            for r in range(rq):
                first, second = (a[r * tq:(r + 1) * tq] for a in per_group)
                pair = _pair_lanes(first, second, normalise=br > 0)
                out[r] = out[r] + gate[:, r * LANES:(r + 1) * LANES] * pair
        for r in range(rq):
            o_ref[b, :, r * LANES:(r + 1) * LANES] = out[r].astype(BF16)


def _nsa(bounds, qa, kcmp, vcmp, ksw, vsw, small, toeplitz, cmp_bias, batch, seq, exact):
    tq = ATT_TILE
    nq = seq // tq
    rq = NSA_Q_PER_GROUP
    n_blocks = seq // SLC_BLOCK
    n_cmp = (seq - CMP_BLOCK) // CMP_STRIDE + 1
    ci = np.arange(LANES)[:, None] * CMP_STRIDE
    sj = np.arange(LANES)[None, :] * SLC_BLOCK
    overlap = ((ci <= sj + SLC_BLOCK - 1) & (ci + CMP_BLOCK - 1 >= sj)
               & (np.arange(LANES)[:, None] < n_cmp) & (np.arange(LANES)[None, :] < n_blocks))
    expand = np.arange(LANES)[:, None] == (np.arange(seq)[None, :] // SLC_BLOCK)
    gate_expand = np.zeros((3, LANES, NSA_W), np.float32)
    for b in range(3):
        for g in range(NSA_KV_GROUPS):
            for r in range(rq):
                c0 = r * LANES + g * HEAD_DIM
                gate_expand[b, b * NSA_HEADS + g * rq + r, c0:c0 + HEAD_DIM] = 1.0
    per_step = 2 if batch % 2 == 0 else 1

    def rows(n_rows, width):
        return pl.BlockSpec((per_step, n_rows, width), lambda b, t: (b, 0, 0))

    def tile(width):
        return pl.BlockSpec((per_step, tq, width), lambda b, t: (b, t, 0))

    out = pl.pallas_call(
        functools.partial(_nsa_kernel, n_blocks=n_blocks, exact=exact),
        grid=(batch // per_step, nq),
        in_specs=[
            pl.BlockSpec(memory_space=pltpu.SMEM),
            tile(NSA_W),
            rows(LANES, NSA_KV_W),
            rows(LANES, NSA_KV_W),
            rows(seq, 2 * NSA_KV_W),
            rows(seq, 2 * NSA_KV_GROUPS * LANES),
            tile(LANES),
            _const_spec((N_WIN_TILES + 1, NSA_KV_GROUPS, rq, tq, tq)),
            pl.BlockSpec((1, NSA_KV_GROUPS, rq, LANES, tq), lambda b, t: (t, 0, 0, 0, 0)),
            _const_spec((LANES, LANES)),
            _const_spec((LANES, seq)),
            _const_spec((3, LANES, NSA_W)),
        ],
        out_specs=tile(NSA_W),
        out_shape=jax.ShapeDtypeStruct((batch, seq, NSA_W), BF16),
        scratch_shapes=[pltpu.VMEM((per_step * NSA_KV_GROUPS, tq, seq), F32),
                        pltpu.VMEM((per_step * 2 * NSA_KV_GROUPS, rq * tq, LANES), F32)],
        compiler_params=_params(2),
        name="nsa",
    )(bounds, qa.reshape(batch, seq, NSA_W), kcmp, vcmp, ksw.reshape(batch, seq, 2 * NSA_KV_W),
      vsw.reshape(batch, seq, 2 * NSA_KV_GROUPS * LANES), small.reshape(batch, seq, LANES),
      toeplitz, cmp_bias,
      jnp.asarray(overlap.T, F32), jnp.asarray(expand, F32), jnp.asarray(gate_expand, F32))
    return out.reshape(batch * seq, NSA_W)


def _fox_kernel(bnd_ref, q_ref, k_ref, v_ref, cum_ref, cum_t_ref, o_ref, acc_ref, *, exact):
    qt = pl.program_id(1)
    tq = ATT_TILE
    pairs = FOX_HEADS // 2
    row = lax.broadcasted_iota(jnp.int32, (2, tq, tq), 1)
    col = lax.broadcasted_iota(jnp.int32, (2, tq, tq), 2)
    lane = lax.broadcasted_iota(jnp.int32, (tq, LANES), 1)
    low = jnp.where(lane < HEAD_DIM, 1.0, 0.0).astype(BF16)
    high = jnp.where(lane < HEAD_DIM, 0.0, 1.0).astype(BF16)
    n_seq = q_ref.shape[0]

    def key_rows(dl, n=1):
        return pl.ds(pl.multiple_of((qt - dl - (n - 1)) * tq, tq), n * tq)

    bounds = []

    def pair_chain(b, p):
        pair = q_ref[b, :, p * LANES:(p + 1) * LANES]
        q = jnp.concatenate([pair * low, pair * high], axis=0)
        base = cum_ref[b, 2 * p:2 * p + 2, pl.ds(pl.multiple_of(qt * tq, tq), LANES)][:, 0:1]

        def decay_of(rows):
            return (base - cum_ref[b, 2 * p:2 * p + 2, rows]) * LOG2E

        def bound():
            cum_rows = cum_t_ref[b]
            c0 = FORGET_LANE + 2 * p
            own = jnp.concatenate([base[h:h + 1, :] - cum_rows[:, c0 + h:c0 + h + 1] for h in range(2)],
                                  axis=0)
            return jnp.broadcast_to(own * LOG2E + bnd_ref[0], (2 * tq, tq))

        bounds.append(bound)

        def score(dl):
            rows = key_rows(dl)
            decay = decay_of(rows)
            s = _dot_nt(q, k_ref[b, rows, p * LANES:(p + 1) * LANES]).reshape(2, tq, tq) + decay[:, None, :]
            if isinstance(dl, int):
                s = jnp.where(col <= row, s, NEG_INF)
            return s.reshape(2 * tq, tq)

        def pv(w, dl, n=1):
            rows = key_rows(dl, n)
            return jnp.concatenate(
                [_dot(w[0:tq], v_ref[b, rows, 2 * p * LANES:(2 * p + 1) * LANES]),
                 _dot(w[tq:2 * tq], v_ref[b, rows, (2 * p + 1) * LANES:(2 * p + 2) * LANES])], axis=0)

        return score, pv

    chains = [pair_chain(b, p) for b in range(n_seq) for p in range(pairs)]
    accs = _attention(chains, [(qt, tuple(range(len(chains))))], None if exact else bounds, acc_ref)
    for i, acc in enumerate(accs):
        b, p = divmod(i, pairs)
        o_ref[b, :, p * LANES:(p + 1) * LANES] = _pair_lanes(acc[0:tq], acc[tq:2 * tq], True).astype(BF16)


def _fox(bounds, qb, kb, vb, cum, cum_cols, batch, seq, exact):
    tq = ATT_TILE
    nq = seq // tq
    per_step = 2 if batch % 2 == 0 else 1
    out = pl.pallas_call(
        functools.partial(_fox_kernel, exact=exact),
        grid=(batch // per_step, nq),
        in_specs=[
            pl.BlockSpec(memory_space=pltpu.SMEM),
            pl.BlockSpec((per_step, tq, FOX_W), lambda b, t: (b, t, 0)),
            pl.BlockSpec((per_step, seq, FOX_W), lambda b, t: (b, 0, 0)),
            pl.BlockSpec((per_step, seq, FOX_HEADS * LANES), lambda b, t: (b, 0, 0)),
            pl.BlockSpec((per_step, FOX_HEADS, seq), lambda b, t: (b, 0, 0)),
            pl.BlockSpec((per_step, tq, LANES), lambda b, t: (b, t, 0)),
        ],
        out_specs=pl.BlockSpec((per_step, tq, FOX_W), lambda b, t: (b, t, 0)),
        out_shape=jax.ShapeDtypeStruct((batch, seq, FOX_W), BF16),
        scratch_shapes=[pltpu.VMEM((per_step * FOX_HEADS // 2, 2 * tq, LANES), F32)],
        compiler_params=_params(2),
        name="fox",
    )(bounds, qb.reshape(batch, seq, FOX_W), kb.reshape(batch, seq, FOX_W),
      vb.reshape(batch, seq, FOX_HEADS * LANES), cum, cum_cols.reshape(batch, seq, LANES))
    return out.reshape(batch * seq, FOX_W)


def _merge_ffn_kernel(x_ref, on_ref, of_ref, gmix_ref, wgm_ref, won_ref, wof_ref, wout_ref,
                      g2_ref, wup_ref, wdn_ref, o_ref):
    x = x_ref[...]
    u = _rms_rows(x, gmix_ref[...]).astype(BF16)
    gate = jax.nn.sigmoid(_dot(u, wgm_ref[...]))
    merged = (gate[:, :D_MODEL] * _dot(on_ref[...], won_ref[...])
              + gate[:, D_MODEL:] * _dot(of_ref[...], wof_ref[...]))
    x2 = x + _dot(merged.astype(BF16), wout_ref[...])
    o_ref[...] = _swiglu_residual(x2, g2_ref[...], wup_ref, wdn_ref)


def _merge_ffn(x2d, o_nsa, o_fox, mix_norm, w_gm, w_o_nsa, w_o_fox, w_out, gain2, w_up, w_down):
    n = x2d.shape[0]
    tm = _token_tile(n, wide=False)

    def row(width):
        return pl.BlockSpec((tm, width), lambda i: (i, 0))

    return pl.pallas_call(
        _merge_ffn_kernel,
        grid=(n // tm,),
        in_specs=[row(D_MODEL), row(NSA_W), row(FOX_W), _const_spec((1, D_MODEL)),
                  _const_spec((D_MODEL, 2 * D_MODEL)), _const_spec((NSA_W, D_MODEL)),
                  _const_spec((FOX_W, D_MODEL)), _const_spec((D_MODEL, D_MODEL)),
                  _const_spec((1, D_MODEL)), _const_spec((D_MODEL, 2 * D_FF)),
                  _const_spec((D_FF, D_MODEL))],
        out_specs=row(D_MODEL),
        out_shape=jax.ShapeDtypeStruct((n, D_MODEL), F32),
        compiler_params=_params(1),
        name="merge_ffn",
    )(x2d, o_nsa, o_fox, mix_norm.reshape(1, D_MODEL), w_gm, w_o_nsa.astype(BF16),
      w_o_fox.astype(BF16), w_out.astype(BF16), gain2.reshape(1, D_MODEL), w_up.astype(BF16),
      w_down.astype(BF16))


def _layer(x, ffn1_norm, ffn1_w_up, ffn1_w_down, mix_norm, w_in, b_forget, nsa_q_gain, nsa_k_gain,
           fox_q_gain, fox_k_gain, cmp_pos_k, cmp_pos_v, cmp_k_w1, cmp_k_w2, cmp_v_w1, cmp_v_w2,
           w_o_nsa, w_o_fox, w_out, ffn2_norm, ffn2_w_up, ffn2_w_down, rel_bias_table):
    batch, seq, d = x.shape
    assert d == D_MODEL and seq % ATT_TILE == 0 and (batch * seq) % TOKEN_TILE == 0
    assert seq // SLC_BLOCK <= LANES and (seq - CMP_BLOCK) // CMP_STRIDE + 1 == LANES - 1
    assert seq // (CMP_BLOCK // 2) == LANES
    x2d = x.reshape(batch * seq, D_MODEL)

    x1 = _ffn(x2d, ffn1_norm, ffn1_w_up, ffn1_w_down)

    w_packed, gain_row, small_bias, w_gm = _pack_in_proj(w_in, b_forget, nsa_q_gain, nsa_k_gain,
                                                         fox_q_gain, fox_k_gain)
    qa, qb, kb, ksw, vb, vsw, kc, vc, small = _in_proj(x1, mix_norm, w_packed, gain_row, small_bias,
                                                       batch, seq)

    kcmp, vcmp = _compress(kc, vc, batch, seq, _pack_compress(cmp_pos_k, cmp_k_w1, cmp_k_w2),
                           _pack_compress(cmp_pos_v, cmp_v_w1, cmp_v_w2), nsa_k_gain[0])
    toeplitz, cmp_bias = _bias_tables(rel_bias_table, seq // ATT_TILE)
    b_slc = _score_bound(nsa_q_gain, nsa_k_gain[1])
    b_win = _score_bound(nsa_q_gain, nsa_k_gain[2])
    t_hi, t_lo = jnp.max(rel_bias_table) * LOG2E, jnp.min(rel_bias_table) * LOG2E
    nsa_ok = 2.0 * jnp.maximum(b_slc, b_win) + (t_hi - t_lo) < EXP2_RANGE
    nsa_bounds = jnp.stack([b_slc + t_hi, b_win + t_hi])
    o_nsa = lax.cond(
        nsa_ok,
        lambda: _nsa(nsa_bounds, qa, kcmp, vcmp, ksw, vsw, small, toeplitz, cmp_bias, batch, seq, False),
        lambda: _nsa(nsa_bounds, qa, kcmp, vcmp, ksw, vsw, small, toeplitz, cmp_bias, batch, seq, True))

    cum_cols, cum = _cumsum(small, batch, seq)
    b_fox = _score_bound(fox_q_gain, fox_k_gain)
    fox_bounds = jnp.stack([b_fox])
    o_fox = lax.cond(2.0 * b_fox < EXP2_RANGE,
                     lambda: _fox(fox_bounds, qb, kb, vb, cum, cum_cols, batch, seq, False),
                     lambda: _fox(fox_bounds, qb, kb, vb, cum, cum_cols, batch, seq, True))

    w_o_nsa_p = w_o_nsa.reshape(NSA_KV_GROUPS, NSA_Q_PER_GROUP, HEAD_DIM, D_MODEL).transpose(1, 0, 2, 3)
    out = _merge_ffn(x1, o_nsa, o_fox, mix_norm, w_gm, w_o_nsa_p.reshape(NSA_W, D_MODEL), w_o_fox,
                     w_out, ffn2_norm, ffn2_w_up, ffn2_w_down)
    return out.reshape(batch, seq, D_MODEL)


def kernel(x, ffn1_norm, ffn1_w_up, ffn1_w_down, mix_norm, w_in, b_forget, nsa_q_gain, nsa_k_gain,
           fox_q_gain, fox_k_gain, cmp_pos_k, cmp_pos_v, cmp_k_w1, cmp_k_w2, cmp_v_w1, cmp_v_w2,
           w_o_nsa, w_o_fox, w_out, ffn2_norm, ffn2_w_up, ffn2_w_down, rel_bias_table):
    for layer in range(ffn1_norm.shape[0]):
        x = _layer(x, ffn1_norm[layer], ffn1_w_up[layer], ffn1_w_down[layer], mix_norm[layer],
                   w_in[layer], b_forget[layer], nsa_q_gain[layer], nsa_k_gain[layer],
                   fox_q_gain[layer], fox_k_gain[layer], cmp_pos_k[layer], cmp_pos_v[layer],
                   cmp_k_w1[layer], cmp_k_w2[layer], cmp_v_w1[layer], cmp_v_w2[layer],
                   w_o_nsa[layer], w_o_fox[layer], w_out[layer], ffn2_norm[layer],
                   ffn2_w_up[layer], ffn2_w_down[layer], rel_bias_table)
    return x
```

```python
import functools
import math

import numpy as np
import jax
import jax.numpy as jnp
from jax import lax
from jax.experimental import pallas as pl
from jax.experimental.pallas import tpu as pltpu

F32 = jnp.float32
BF16 = jnp.bfloat16

D_MODEL = 1024
HEAD_DIM = 64
NSA_HEADS = 8
NSA_KV_GROUPS = 2
NSA_Q_PER_GROUP = NSA_HEADS // NSA_KV_GROUPS
CMP_BLOCK = 32
CMP_STRIDE = 16
CMP_HIDDEN = 128
SLC_BLOCK = 64
SLC_TOPK = 8
WINDOW = 512
FOX_HEADS = 8
D_FF = 2816
N_BUCKETS = 32
MAX_DISTANCE = 128
RMS_EPS = 1e-6
NEG_INF = -1.0e30
FORCE_BONUS = 1.0e4

NSA_W = NSA_HEADS * HEAD_DIM
NSA_KV_W = NSA_KV_GROUPS * HEAD_DIM
FOX_W = FOX_HEADS * HEAD_DIM
IN_SPLITS = (NSA_W, NSA_KV_W, NSA_KV_W, NSA_KV_W, NSA_KV_W, NSA_KV_W, NSA_KV_W, 3 * NSA_HEADS,
             FOX_W, FOX_W, FOX_W, FOX_HEADS, 2 * D_MODEL)

LANES = 128
TOKEN_TILE = 512
MXU_DIM = 256
FFN_CHUNKS = (-(-D_FF // (2 * MXU_DIM)) * MXU_DIM, D_FF - -(-D_FF // (2 * MXU_DIM)) * MXU_DIM)
ATT_TILE = 256
N_WIN_TILES = WINDOW // ATT_TILE + 1
VMEM_LIMIT = 56 * 1024 * 1024
LOG2E = 1.4426950408889634
EXP2_RANGE = 100.0

PQ_A = 0
PQ_B = PQ_A + NSA_W
PK_B = PQ_B + FOX_W
PK_SW = PK_B + FOX_W
P_NORM_END = PK_SW + 2 * NSA_KV_W
PV_B = P_NORM_END
PV_SW = PV_B + FOX_W
PKV_C = PV_SW + 2 * NSA_KV_W
P_SMALL = PKV_C + 2 * NSA_KV_W
P_END = P_SMALL + LANES
NORM_CHUNK = 256


def _dot(a, b):
    return jnp.dot(a, b, preferred_element_type=F32)


def _dot_nt(a, b):
    return lax.dot_general(a, b, (((1,), (1,)), ((), ())), preferred_element_type=F32)


def _split_dot(x, w):
    hi = x.astype(BF16)
    lo = (x - hi.astype(F32)).astype(BF16)
    return _dot(hi, w) + _dot(lo, w)


def _rms_rows(x, gain_row):
    ms = jnp.mean(x * x, axis=-1, keepdims=True)
    return x * lax.rsqrt(ms + RMS_EPS) * gain_row


def _const_spec(shape):
    nd = len(shape)
    return pl.BlockSpec(shape, lambda *_: (0,) * nd, pipeline_mode=pl.Buffered(1))


def _token_tile(n_rows, wide):
    tile = 2 * TOKEN_TILE if wide and n_rows % (2 * TOKEN_TILE) == 0 else TOKEN_TILE
    assert n_rows % tile == 0
    return tile


def _params(n_axes):
    return pltpu.CompilerParams(dimension_semantics=("arbitrary",) * n_axes,
                                vmem_limit_bytes=VMEM_LIMIT)


def _swiglu_residual(x, gain_row, wup_ref, wdn_ref):
    xn = _rms_rows(x, gain_row).astype(BF16)
    acc = jnp.zeros(x.shape, F32)
    lo = 0
    for width in FFN_CHUNKS:
        gate = _dot(xn, wup_ref[:, lo:lo + width])
        up = _dot(xn, wup_ref[:, D_FF + lo:D_FF + lo + width])
        h = (gate * jax.nn.sigmoid(gate) * up).astype(BF16)
        acc = acc + _dot(h, wdn_ref[lo:lo + width, :])
        lo += width
    return x + 0.5 * acc


def _ffn_kernel(x_ref, g_ref, wup_ref, wdn_ref, o_ref):
    o_ref[...] = _swiglu_residual(x_ref[...], g_ref[...], wup_ref, wdn_ref)


def _ffn(x2d, gain, w_up, w_down):
    n = x2d.shape[0]
    tm = _token_tile(n, wide=True)
    row = pl.BlockSpec((tm, D_MODEL), lambda i: (i, 0))
    return pl.pallas_call(
        _ffn_kernel,
        grid=(n // tm,),
        in_specs=[row, _const_spec((1, D_MODEL)), _const_spec((D_MODEL, 2 * D_FF)),
                  _const_spec((D_FF, D_MODEL))],
        out_specs=row,
        out_shape=jax.ShapeDtypeStruct((n, D_MODEL), F32),
        compiler_params=_params(1),
        name="ffn",
    )(x2d, gain.reshape(1, D_MODEL), w_up.astype(BF16), w_down.astype(BF16))


def _block_ones(width, size=NORM_CHUNK):
    idx = np.arange(size) // width
    return jnp.asarray((idx[:, None] == idx[None, :]).astype(np.float32), BF16)


def _pack_in_proj(w_in, b_forget, nsa_q_gain, nsa_k_gain, fox_q_gain, fox_k_gain):
    scale = LOG2E / math.sqrt(HEAD_DIM)
    pts = np.cumsum(np.array(IN_SPLITS))[:-1].tolist()
    qa, kc, vc, ks, vs, kw, vw, ga, qb, kb, vb, fb, gm = jnp.split(w_in, pts, axis=1)
    cols = []
    for r in range(NSA_Q_PER_GROUP):
        for g in range(NSA_KV_GROUPS):
            h = g * NSA_Q_PER_GROUP + r
            cols.append(qa[:, h * HEAD_DIM:(h + 1) * HEAD_DIM])
    cols += [qb, kb, ks, kw]
    gains = [jnp.tile(nsa_q_gain * scale, NSA_HEADS), jnp.tile(fox_q_gain * scale, FOX_HEADS),
             jnp.tile(fox_k_gain, FOX_HEADS), jnp.tile(nsa_k_gain[1], NSA_KV_GROUPS),
             jnp.tile(nsa_k_gain[2], NSA_KV_GROUPS)]
    n_small_pad = LANES - ga.shape[1] - fb.shape[1]
    cols += [vb, vs, vw, kc, vc, ga, fb, jnp.zeros((D_MODEL, n_small_pad), F32)]
    w_packed = jnp.concatenate(cols, axis=1).astype(BF16)
    gain_row = jnp.concatenate(gains).reshape(1, P_NORM_END)
    small_bias = jnp.concatenate([jnp.zeros((ga.shape[1],), F32), b_forget,
                                  jnp.zeros((n_small_pad,), F32)]).reshape(1, LANES)
    return w_packed, gain_row, small_bias, gm.astype(BF16)


def _in_proj_kernel(x_ref, g_ref, w_ref, gain_ref, sbias_ref, bd64_ref,
                    qa_ref, qb_ref, kb_ref, ksw_ref, vb_ref, vsw_ref, kc_ref, vc_ref, small_ref, kv_scr):
    u = _rms_rows(x_ref[...], g_ref[...]).astype(BF16)
    tm = u.shape[0]

    y = _dot(u, w_ref[:, 0:P_NORM_END])
    n_chunks = P_NORM_END // NORM_CHUNK
    squares = jnp.concatenate([jnp.square(y[:, c * NORM_CHUNK:(c + 1) * NORM_CHUNK]).astype(BF16)
                               for c in range(n_chunks)], axis=0)
    ss = _dot(squares, bd64_ref[...])
    chunk = 0
    for o_ref in (qa_ref, qb_ref, kb_ref, ksw_ref):
        for i in range(o_ref.shape[1] // NORM_CHUNK):
            cols = slice(chunk * NORM_CHUNK, (chunk + 1) * NORM_CHUNK)
            inv = lax.rsqrt(ss[chunk * tm:(chunk + 1) * tm] * (1.0 / HEAD_DIM) + RMS_EPS)
            o_ref[:, i * NORM_CHUNK:(i + 1) * NORM_CHUNK] = (y[:, cols] * inv * gain_ref[:, cols]).astype(BF16)
            chunk += 1
    low = lax.broadcasted_iota(jnp.int32, (u.shape[0], LANES), 1) < HEAD_DIM

    def store_with_ones(v, o_ref):
        for i in range(v.shape[1] // LANES):
            pair = v[:, i * LANES:(i + 1) * LANES]
            o_ref[:, 2 * i * LANES:(2 * i + 1) * LANES] = jnp.where(low, pair, 1.0).astype(BF16)
            o_ref[:, (2 * i + 1) * LANES:(2 * i + 2) * LANES] = jnp.where(low, 1.0, pair).astype(BF16)

    rest = _dot(u, w_ref[:, P_NORM_END:P_END])
    store_with_ones(rest[:, PV_B - P_NORM_END:PV_SW - P_NORM_END], vb_ref)
    store_with_ones(rest[:, PV_SW - P_NORM_END:PKV_C - P_NORM_END], vsw_ref)
    for j, o_ref in enumerate((kc_ref, vc_ref)):
        c0 = PKV_C - P_NORM_END + j * NSA_KV_W
        kv_scr[j] = rest[:, c0:c0 + NSA_KV_W]
        for l in range(CMP_STRIDE):
            o_ref[0, :, l * NSA_KV_W:(l + 1) * NSA_KV_W] = kv_scr[j, pl.ds(l, tm // CMP_STRIDE,
                                                                           stride=CMP_STRIDE), :]
    z = rest[:, P_SMALL - P_NORM_END:] + sbias_ref[...]
    lane = lax.broadcasted_iota(jnp.int32, z.shape, 1)
    log_sig = jnp.minimum(z, 0.0) - jnp.log1p(jnp.exp(-jnp.abs(z)))
    small_ref[...] = jnp.where(lane < 3 * NSA_HEADS, jax.nn.sigmoid(z), log_sig)


def _in_proj(x2d, mix_norm, w_packed, gain_row, small_bias, batch, seq):
    n = x2d.shape[0]
    tm = _token_tile(seq, wide=True)
    steps_per_seq = seq // tm

    def row(width):
        return pl.BlockSpec((tm, width), lambda i: (i, 0))

    def out(width, dtype):
        return row(width), jax.ShapeDtypeStruct((n, width), dtype)

    grouped = (pl.BlockSpec((1, tm // CMP_STRIDE, CMP_STRIDE * NSA_KV_W),
                            lambda i: (i // steps_per_seq, i % steps_per_seq, 0)),
               jax.ShapeDtypeStruct((batch, seq // CMP_STRIDE, CMP_STRIDE * NSA_KV_W), F32))
    outs = [out(NSA_W, BF16), out(FOX_W, BF16), out(FOX_W, BF16), out(2 * NSA_KV_W, BF16),
            out(2 * FOX_W, BF16), out(4 * NSA_KV_W, BF16), grouped, grouped, out(LANES, F32)]
    return pl.pallas_call(
        _in_proj_kernel,
        grid=(n // tm,),
        in_specs=[row(D_MODEL), _const_spec((1, D_MODEL)), _const_spec((D_MODEL, P_END)),
                  _const_spec((1, P_NORM_END)), _const_spec((1, LANES)),
                  _const_spec((NORM_CHUNK, NORM_CHUNK))],
        out_specs=[spec for spec, _ in outs],
        out_shape=[shape for _, shape in outs],
        scratch_shapes=[pltpu.VMEM((2, tm, NSA_KV_W), F32)],
        compiler_params=_params(1),
        name="in_proj",
    )(x2d, mix_norm.reshape(1, D_MODEL), w_packed, gain_row, small_bias, _block_ones(HEAD_DIM))


def _pack_compress(pos, w1, w2):
    half = CMP_BLOCK // 2
    eye = jnp.eye(NSA_KV_GROUPS, dtype=F32)
    w1r = w1.reshape(CMP_BLOCK, HEAD_DIM, CMP_HIDDEN)

    def big(w):
        return jnp.einsum('ldh,pg->lpdgh', w, eye).reshape(half * NSA_KV_W, NSA_KV_GROUPS * CMP_HIDDEN)

    def posrow(p):
        return jnp.broadcast_to(p[:, None, :], (half, NSA_KV_GROUPS, HEAD_DIM)).reshape(1, half * NSA_KV_W)

    w2big = jnp.einsum('hd,pg->phgd', w2, eye).reshape(NSA_KV_GROUPS * CMP_HIDDEN, NSA_KV_W)
    return (posrow(pos[:half]), posrow(pos[half:]), big(w1r[:half]).astype(BF16),
            big(w1r[half:]).astype(BF16), w2big.astype(BF16))


def _compress_kernel(kc_ref, vc_ref, kp_lo, kp_hi, kw_lo, kw_hi, kw2, vp_lo, vp_hi, vw_lo, vw_hi, vw2,
                     kgain_ref, bd64_ref, ko_ref, vo_ref):
    n_seq, rows, width = kc_ref.shape

    def mlp(x_ref, p_lo, p_hi, w_lo, w_hi, w2):
        r = x_ref[...].reshape(n_seq * rows, width)
        first = _dot((r + p_lo[...]).astype(BF16), w_lo[...])
        second = _dot((r + p_hi[...]).astype(BF16), w_hi[...])
        h = first + pltpu.roll(second, second.shape[0] - 1, axis=0)
        return _dot((h * jax.nn.sigmoid(h)).astype(BF16), w2[...])

    k = mlp(kc_ref, kp_lo, kp_hi, kw_lo, kw_hi, kw2)
    ss = _split_dot(k * k, bd64_ref[...])
    k = (k * lax.rsqrt(ss * (1.0 / HEAD_DIM) + RMS_EPS) * kgain_ref[...]).astype(BF16)
    ko_ref[...] = k.reshape(ko_ref.shape)
    vo_ref[...] = mlp(vc_ref, vp_lo, vp_hi, vw_lo, vw_hi, vw2).astype(BF16).reshape(vo_ref.shape)


def _compress(kc, vc, batch, seq, k_pack, v_pack, k_gain):
    rows = seq // (CMP_BLOCK // 2)
    width = (CMP_BLOCK // 2) * NSA_KV_W
    per_step = next(n for n in (4, 2, 1) if batch % n == 0)
    blk = pl.BlockSpec((per_step, rows, width), lambda b: (b, 0, 0))
    out = pl.BlockSpec((per_step, rows, NSA_KV_W), lambda b: (b, 0, 0))
    pack_specs = [_const_spec((1, width)), _const_spec((1, width)),
                  _const_spec((width, NSA_KV_GROUPS * CMP_HIDDEN)),
                  _const_spec((width, NSA_KV_GROUPS * CMP_HIDDEN)),
                  _const_spec((NSA_KV_GROUPS * CMP_HIDDEN, NSA_KV_W))]
    bd = _block_ones(HEAD_DIM, NSA_KV_W)
    return pl.pallas_call(
        _compress_kernel,
        grid=(batch // per_step,),
        in_specs=[blk, blk] + pack_specs + pack_specs + [_const_spec((1, NSA_KV_W)),
                                                         _const_spec((NSA_KV_W, NSA_KV_W))],
        out_specs=[out, out],
        out_shape=[jax.ShapeDtypeStruct((batch, rows, NSA_KV_W), BF16)] * 2,
        compiler_params=_params(1),
        name="compress",
    )(kc.reshape(batch, rows, width), vc.reshape(batch, rows, width), *k_pack, *v_pack,
      jnp.tile(k_gain, NSA_KV_GROUPS).reshape(1, NSA_KV_W), bd)


FORGET_LANE = 3 * NSA_HEADS


def _cumsum_kernel(x_ref, cols_ref, rows_ref):
    x = x_ref[...]
    row = lax.broadcasted_iota(jnp.int32, x.shape, 0)
    k = 1
    while k < x.shape[0]:
        x = x + jnp.where(row >= k, pltpu.roll(x, k, axis=0), 0.0)
        k *= 2
    cols_ref[...] = x
    rows_ref[0] = x.T[FORGET_LANE:FORGET_LANE + FOX_HEADS]


def _cumsum(small, batch, seq):
    spec = pl.BlockSpec((seq, LANES), lambda b: (b, 0))
    return pl.pallas_call(
        _cumsum_kernel, grid=(batch,), in_specs=[spec],
        out_specs=[spec, pl.BlockSpec((1, FOX_HEADS, seq), lambda b: (b, 0, 0))],
        out_shape=[jax.ShapeDtypeStruct((batch * seq, LANES), F32),
                   jax.ShapeDtypeStruct((batch, FOX_HEADS, seq), F32)],
        compiler_params=_params(1), name="cumsum",
    )(small)


def _write_bias_rows(d, valid, tab_ref, o_ref, rows):
    max_exact = N_BUCKETS // 2
    n = jnp.maximum(d, 0)
    nf = jnp.maximum(n, 1).astype(F32)
    large = max_exact + (jnp.log(nf / max_exact) / math.log(MAX_DISTANCE / max_exact)
                         * (N_BUCKETS - max_exact)).astype(jnp.int32)
    bucket = jnp.where(n < max_exact, n, jnp.minimum(large, N_BUCKETS - 1))
    for h in range(NSA_HEADS):
        acc = jnp.zeros(d.shape, F32)
        for b in range(N_BUCKETS):
            acc = jnp.where(bucket == b, tab_ref[b, h], acc)
        g, r = divmod(h, NSA_Q_PER_GROUP)
        o_ref[0, g, r, rows, :] = jnp.where(valid, acc * LOG2E, NEG_INF)


def _write_far_tile(valid, tab_ref, o_ref):
    for h in range(NSA_HEADS):
        g, r = divmod(h, NSA_Q_PER_GROUP)
        far = jnp.where(valid, tab_ref[N_BUCKETS - 1, h], 0.0) * LOG2E
        o_ref[0, g, r] = jnp.where(valid, far, NEG_INF)


def _toeplitz_bias_kernel(tab_ref, o_ref):
    s = pl.program_id(0)
    tile = ATT_TILE

    def distances(n_rows):
        i = lax.broadcasted_iota(jnp.int32, (n_rows, tile), 0)
        j = lax.broadcasted_iota(jnp.int32, (n_rows, tile), 1)
        d = jnp.minimum(s, N_WIN_TILES - 1) * tile + i - j
        d_hi = jnp.where(s < N_WIN_TILES, WINDOW, jnp.int32(1 << 30))
        return d, (d >= 0) & (d < d_hi)

    _write_far_tile(distances(tile)[1], tab_ref, o_ref)
    for step in range(N_WIN_TILES + 1):
        near = min(tile, max(0, MAX_DISTANCE + tile - 1 - min(step, N_WIN_TILES - 1) * tile))
        near = -(-near // 8) * 8
        if near:
            @pl.when(s == step)
            def _(near=near):
                d, valid = distances(near)
                _write_bias_rows(d, valid, tab_ref, o_ref, slice(0, near))


def _cmp_bias_kernel(tab_ref, o_ref):
    t = pl.program_id(0)
    chunk = 8

    def distances(c0, n_rows):
        c = c0 + lax.broadcasted_iota(jnp.int32, (n_rows, ATT_TILE), 0)
        i = lax.broadcasted_iota(jnp.int32, (n_rows, ATT_TILE), 1)
        d = t * ATT_TILE + i - CMP_STRIDE * c - (CMP_BLOCK - 1)
        return d, (d >= 0) & (c < LANES - 1)

    _write_far_tile(distances(0, LANES)[1], tab_ref, o_ref)
    first = (t * ATT_TILE - (MAX_DISTANCE + CMP_BLOCK - 1)) // (chunk * CMP_STRIDE)
    n_chunks = (ATT_TILE + MAX_DISTANCE) // (chunk * CMP_STRIDE) + 2

    def refine(m, carry):
        c0 = pl.multiple_of(jnp.clip(first + m, 0, LANES // chunk - 1) * chunk, chunk)
        d, valid = distances(c0, chunk)
        _write_bias_rows(d, valid, tab_ref, o_ref, pl.ds(c0, chunk))
        return carry

    lax.fori_loop(0, n_chunks, refine, 0)


def _bias_tables(rel_bias_table, n_q_tiles):
    def call(body, steps, rows, name):
        shape = (steps, NSA_KV_GROUPS, NSA_Q_PER_GROUP, rows, ATT_TILE)
        return pl.pallas_call(
            body, grid=(steps,),
            in_specs=[pl.BlockSpec(memory_space=pltpu.SMEM)],
            out_specs=pl.BlockSpec((1,) + shape[1:], lambda s: (s, 0, 0, 0, 0)),
            out_shape=jax.ShapeDtypeStruct(shape, F32),
            compiler_params=_params(1), name=name,
        )(rel_bias_table)

    return (call(_toeplitz_bias_kernel, N_WIN_TILES + 1, ATT_TILE, "toeplitz_bias"),
            call(_cmp_bias_kernel, n_q_tiles, LANES, "cmp_bias"))


def _score_bound(q_gain, k_gain):
    return 1.02 * LOG2E * math.sqrt(HEAD_DIM) * jnp.max(jnp.abs(q_gain)) * jnp.max(jnp.abs(k_gain))


def _attention(chains, groups, bounds, acc_ref):
    n = len(chains)

    def later(k):
        return tuple(i for _, ids in groups[k:] for i in ids)

    def exact_max():
        def half_max(i, dl):
            s = chains[i][0](dl)
            return jnp.maximum(s[:, :LANES], s[:, LANES:])

        mx = [half_max(i, 0) for i in range(n)]
        lo = 1
        for k, (last, _) in enumerate(groups):
            members = later(k)

            def body(dl, carry, members=members):
                return tuple(jnp.maximum(c, half_max(i, dl)) for c, i in zip(carry, members))

            for i, r in zip(members, lax.fori_loop(lo, last + 1, body, tuple(mx[i] for i in members))):
                mx[i] = r
            lo = last + 1
        return tuple(jnp.broadcast_to(jnp.max(m, axis=-1, keepdims=True), (m.shape[0], ATT_TILE))
                     for m in mx)

    shift = exact_max() if bounds is None else [b() for b in bounds]

    def weights(i, dl):
        return jnp.exp2(chains[i][0](dl) - shift[i]).astype(BF16)

    def product(i, dl):
        return chains[i][1](weights(i, dl), dl)

    for i in range(n):
        acc_ref[i] = product(i, 0)
    lo = 1
    for k, (last, _) in enumerate(groups):
        members = later(k)
        count = last + 1 - lo

        def body(j, carry, members=members, lo=lo):
            dl = lo + 2 * j
            for i in members:
                both = jnp.concatenate([weights(i, dl + 1), weights(i, dl)], axis=1)
                acc_ref[i] += chains[i][1](both, dl, 2)
            return carry

        lax.fori_loop(0, count // 2, body, 0)

        @pl.when(count % 2 == 1)
        def _(members=members, last=last):
            for i in members:
                acc_ref[i] += product(i, last)

        lo = last + 1
    return [acc_ref[i] for i in range(n)]


def _pair_lanes(first, second, normalise):
    low = lax.broadcasted_iota(jnp.int32, first.shape, 1) < HEAD_DIM
    pair = jnp.where(low, first, second)
    if not normalise:
        return pair
    sums = pltpu.roll(jnp.where(low, second, first), HEAD_DIM, axis=1)
    return pair * (1.0 / sums)


def _select_blocks(imp_t, t0, tq):
    blk = lax.broadcasted_iota(jnp.int32, imp_t.shape, 0)
    cur = (t0 + (lax.broadcasted_iota(jnp.int32, imp_t.shape, 1) & (tq - 1))) // SLC_BLOCK
    forced = (blk == 0) | (blk == cur) | (blk == cur - 1)
    score = jnp.where(blk <= cur, imp_t + jnp.where(forced, FORCE_BONUS, 0.0), NEG_INF)
    blk_f = blk.astype(F32)
    dead = -3.0e38
    picked = jnp.zeros(imp_t.shape, F32)
    for _ in range(SLC_TOPK):
        best = jnp.max(score, axis=0, keepdims=True)
        first = jnp.min(jnp.where(score == best, blk_f, float(LANES)), axis=0, keepdims=True)
        hit = blk_f == first
        picked = jnp.where(hit, 1.0, picked)
        score = jnp.where(hit, dead, score)
    return picked


def _nsa_kernel(bnd_ref, qa_ref, kcmp_ref, vcmp_ref, ksw_ref, vsw_ref, small_ref, tb_ref, bc_ref, ov_ref,
                e_ref, gx_ref, o_ref, amask_ref, acc_ref, *, n_blocks, exact):
    qt = pl.program_id(1)
    tq = ATT_TILE
    rq = NSA_Q_PER_GROUP
    lane = lax.broadcasted_iota(jnp.int32, (tq, LANES), 1)
    own_lanes = [jnp.where((lane // HEAD_DIM) == g, 1.0, 0.0).astype(BF16) for g in range(NSA_KV_GROUPS)]

    def key_rows(dl, n=1):
        return pl.ds(pl.multiple_of((qt - dl - (n - 1)) * tq, tq), n * tq)

    groups = range(NSA_KV_GROUPS)
    n_seq = qa_ref.shape[0]
    n_grp = NSA_KV_GROUPS

    def win_chain(b, g, q):
        def score(dl):
            s = _dot_nt(q, ksw_ref[b, key_rows(dl), LANES:2 * LANES]).reshape(rq, tq, tq)
            return (s + tb_ref[dl, g]).reshape(rq * tq, tq)

        def pv(p, dl, n=1):
            c0 = (n_grp + g) * LANES
            return _dot(p, vsw_ref[b, key_rows(dl, n), c0:c0 + LANES])

        return score, pv

    def slc_chain(b, g, q):
        def score(dl):
            tile = dl if isinstance(dl, int) else jnp.where(dl < N_WIN_TILES - 1, dl, N_WIN_TILES)
            rows = key_rows(dl)
            s = _dot_nt(q, ksw_ref[b, rows, 0:LANES]).reshape(rq, tq, tq)
            return (s + tb_ref[tile, g] + amask_ref[b * n_grp + g, :, rows][None]).reshape(rq * tq, tq)

        def pv(p, dl, n=1):
            return _dot(p, vsw_ref[b, key_rows(dl, n), g * LANES:(g + 1) * LANES])

        return score, pv

    overlap = ov_ref[...].astype(BF16)
    key_block = e_ref[...].astype(BF16)
    gate_expand = [gx_ref[br].astype(BF16) for br in range(3)]

    slc_chains, win_chains, o_cmps = [], [], []
    for b in range(n_seq):
        qs = [jnp.concatenate([qa_ref[b, :, r * LANES:(r + 1) * LANES] * own_lanes[g] for r in range(rq)],
                              axis=0) for g in groups]
        slc_chains += [slc_chain(b, g, qs[g]) for g in groups]
        win_chains += [win_chain(b, g, qs[g]) for g in groups]

        s = (_dot_nt(kcmp_ref[b], jnp.concatenate(qs, axis=0))
             + jnp.concatenate([bc_ref[0, g, r] for g in groups for r in range(rq)], axis=1))
        m = jnp.max(s, axis=0, keepdims=True)
        p = jnp.where(s > 0.5 * NEG_INF, jnp.exp2(s - m), 0.0)
        l = jnp.sum(p, axis=0, keepdims=True)
        p_c = p * (1.0 / jnp.where(l > 0.0, l, 1.0))
        o_cmp = _dot(p_c.T.astype(BF16), vcmp_ref[b])
        o_cmps.append([o_cmp[g * rq * tq:(g + 1) * rq * tq] for g in groups])

        p_sum = jnp.concatenate([sum(p_c[:, (g * rq + r) * tq:(g * rq + r + 1) * tq] for r in range(rq))
                                 for g in groups], axis=1)
        hi = p_sum.astype(BF16)
        lo = (p_sum - hi.astype(F32)).astype(BF16)
        imp_t = _dot(overlap, hi) + _dot(overlap, lo)
        sel_t = _select_blocks(imp_t[0:n_blocks], qt * tq, tq)
        block_bias = jnp.concatenate([jnp.where(sel_t > 0.0, 0.0, NEG_INF),
                                      jnp.zeros((LANES - n_blocks, n_grp * tq), F32)], axis=0)
        amask_ref[b * n_grp:(b + 1) * n_grp] = _dot(block_bias.T.astype(BF16), key_block).reshape(
            n_grp, tq, e_ref.shape[1])

    n_slc = len(slc_chains)
    slc_ids = tuple(range(n_slc))
    win_ids = tuple(range(n_slc, 2 * n_slc))
    bounds = None if exact else [lambda i=i: jnp.full((rq * tq, tq), bnd_ref[i // n_slc], F32)
                                 for i in range(2 * n_slc)]
    accs = _attention(slc_chains + win_chains,
                      [(jnp.minimum(qt, N_WIN_TILES - 1), win_ids), (qt, slc_ids)], bounds, acc_ref)

    for b in range(n_seq):
        gates = small_ref[b]
        g_hi = gates.astype(BF16)
        g_lo = (gates - g_hi.astype(F32)).astype(BF16)
        out = [jnp.zeros((tq, LANES), F32) for _ in range(rq)]
        lo_, hi_ = b * n_grp, (b + 1) * n_grp
        for br, per_group in enumerate((o_cmps[b], accs[lo_:hi_], accs[n_slc + lo_:n_slc + hi_])):
            gate = _dot(g_hi, gate_expand[br]) + _dot(g_lo, gate_expand[br])
            for r in range(rq):
                first, second = (a[r * tq:(r + 1) * tq] for a in per_group)
                pair = _pair_lanes(first, second, normalise=br > 0)
                out[r] = out[r] + gate[:, r * LANES:(r + 1) * LANES] * pair
        for r in range(rq):
            o_ref[b, :, r * LANES:(r + 1) * LANES] = out[r].astype(BF16)


def _nsa(bounds, qa, kcmp, vcmp, ksw, vsw, small, toeplitz, cmp_bias, batch, seq, exact):
    tq = ATT_TILE
    nq = seq // tq
    rq = NSA_Q_PER_GROUP
    n_blocks = seq // SLC_BLOCK
    n_cmp = (seq - CMP_BLOCK) // CMP_STRIDE + 1
    ci = np.arange(LANES)[:, None] * CMP_STRIDE
    sj = np.arange(LANES)[None, :] * SLC_BLOCK
    overlap = ((ci <= sj + SLC_BLOCK - 1) & (ci + CMP_BLOCK - 1 >= sj)
               & (np.arange(LANES)[:, None] < n_cmp) & (np.arange(LANES)[None, :] < n_blocks))
    expand = np.arange(LANES)[:, None] == (np.arange(seq)[None, :] // SLC_BLOCK)
    gate_expand = np.zeros((3, LANES, NSA_W), np.float32)
    for b in range(3):
        for g in range(NSA_KV_GROUPS):
            for r in range(rq):
                c0 = r * LANES + g * HEAD_DIM
                gate_expand[b, b * NSA_HEADS + g * rq + r, c0:c0 + HEAD_DIM] = 1.0
    per_step = 2 if batch % 2 == 0 else 1

    def rows(n_rows, width):
        return pl.BlockSpec((per_step, n_rows, width), lambda b, t: (b, 0, 0))

    def tile(width):
        return pl.BlockSpec((per_step, tq, width), lambda b, t: (b, t, 0))

    out = pl.pallas_call(
        functools.partial(_nsa_kernel, n_blocks=n_blocks, exact=exact),
        grid=(batch // per_step, nq),
        in_specs=[
            pl.BlockSpec(memory_space=pltpu.SMEM),
            tile(NSA_W),
            rows(LANES, NSA_KV_W),
            rows(LANES, NSA_KV_W),
            rows(seq, 2 * NSA_KV_W),
            rows(seq, 2 * NSA_KV_GROUPS * LANES),
            tile(LANES),
            _const_spec((N_WIN_TILES + 1, NSA_KV_GROUPS, rq, tq, tq)),
            pl.BlockSpec((1, NSA_KV_GROUPS, rq, LANES, tq), lambda b, t: (t, 0, 0, 0, 0)),
            _const_spec((LANES, LANES)),
            _const_spec((LANES, seq)),
            _const_spec((3, LANES, NSA_W)),
        ],
        out_specs=tile(NSA_W),
        out_shape=jax.ShapeDtypeStruct((batch, seq, NSA_W), BF16),
        scratch_shapes=[pltpu.VMEM((per_step * NSA_KV_GROUPS, tq, seq), F32),
                        pltpu.VMEM((per_step * 2 * NSA_KV_GROUPS, rq * tq, LANES), F32)],
        compiler_params=_params(2),
        name="nsa",
    )(bounds, qa.reshape(batch, seq, NSA_W), kcmp, vcmp, ksw.reshape(batch, seq, 2 * NSA_KV_W),
      vsw.reshape(batch, seq, 2 * NSA_KV_GROUPS * LANES), small.reshape(batch, seq, LANES),
      toeplitz, cmp_bias,
      jnp.asarray(overlap.T, F32), jnp.asarray(expand, F32), jnp.asarray(gate_expand, F32))
    return out.reshape(batch * seq, NSA_W)


def _fox_kernel(bnd_ref, q_ref, k_ref, v_ref, cum_ref, cum_t_ref, o_ref, acc_ref, *, exact):
    qt = pl.program_id(1)
    tq = ATT_TILE
    pairs = FOX_HEADS // 2
    row = lax.broadcasted_iota(jnp.int32, (2, tq, tq), 1)
    col = lax.broadcasted_iota(jnp.int32, (2, tq, tq), 2)
    lane = lax.broadcasted_iota(jnp.int32, (tq, LANES), 1)
    low = jnp.where(lane < HEAD_DIM, 1.0, 0.0).astype(BF16)
    high = jnp.where(lane < HEAD_DIM, 0.0, 1.0).astype(BF16)
    n_seq = q_ref.shape[0]

    def key_rows(dl, n=1):
        return pl.ds(pl.multiple_of((qt - dl - (n - 1)) * tq, tq), n * tq)

    bounds = []

    def pair_chain(b, p):
        pair = q_ref[b, :, p * LANES:(p + 1) * LANES]
        q = jnp.concatenate([pair * low, pair * high], axis=0)
        base = cum_ref[b, 2 * p:2 * p + 2, pl.ds(pl.multiple_of(qt * tq, tq), LANES)][:, 0:1]

        def decay_of(rows):
            return (base - cum_ref[b, 2 * p:2 * p + 2, rows]) * LOG2E

        def bound():
            cum_rows = cum_t_ref[b]
            c0 = FORGET_LANE + 2 * p
            own = jnp.concatenate([base[h:h + 1, :] - cum_rows[:, c0 + h:c0 + h + 1] for h in range(2)],
                                  axis=0)
            return jnp.broadcast_to(own * LOG2E + bnd_ref[0], (2 * tq, tq))

        bounds.append(bound)

        def score(dl):
            rows = key_rows(dl)
            decay = decay_of(rows)
            s = _dot_nt(q, k_ref[b, rows, p * LANES:(p + 1) * LANES]).reshape(2, tq, tq) + decay[:, None, :]
            if isinstance(dl, int):
                s = jnp.where(col <= row, s, NEG_INF)
            return s.reshape(2 * tq, tq)

        def pv(w, dl, n=1):
            rows = key_rows(dl, n)
            return jnp.concatenate(
                [_dot(w[0:tq], v_ref[b, rows, 2 * p * LANES:(2 * p + 1) * LANES]),
                 _dot(w[tq:2 * tq], v_ref[b, rows, (2 * p + 1) * LANES:(2 * p + 2) * LANES])], axis=0)

        return score, pv

    chains = [pair_chain(b, p) for b in range(n_seq) for p in range(pairs)]
    accs = _attention(chains, [(qt, tuple(range(len(chains))))], None if exact else bounds, acc_ref)
    for i, acc in enumerate(accs):
        b, p = divmod(i, pairs)
        o_ref[b, :, p * LANES:(p + 1) * LANES] = _pair_lanes(acc[0:tq], acc[tq:2 * tq], True).astype(BF16)


def _fox(bounds, qb, kb, vb, cum, cum_cols, batch, seq, exact):
    tq = ATT_TILE
    nq = seq // tq
    per_step = 2 if batch % 2 == 0 else 1
    out = pl.pallas_call(
        functools.partial(_fox_kernel, exact=exact),
        grid=(batch // per_step, nq),
        in_specs=[
            pl.BlockSpec(memory_space=pltpu.SMEM),
            pl.BlockSpec((per_step, tq, FOX_W), lambda b, t: (b, t, 0)),
            pl.BlockSpec((per_step, seq, FOX_W), lambda b, t: (b, 0, 0)),
            pl.BlockSpec((per_step, seq, FOX_HEADS * LANES), lambda b, t: (b, 0, 0)),
            pl.BlockSpec((per_step, FOX_HEADS, seq), lambda b, t: (b, 0, 0)),
            pl.BlockSpec((per_step, tq, LANES), lambda b, t: (b, t, 0)),
        ],
        out_specs=pl.BlockSpec((per_step, tq, FOX_W), lambda b, t: (b, t, 0)),
        out_shape=jax.ShapeDtypeStruct((batch, seq, FOX_W), BF16),
        scratch_shapes=[pltpu.VMEM((per_step * FOX_HEADS // 2, 2 * tq, LANES), F32)],
        compiler_params=_params(2),
        name="fox",
    )(bounds, qb.reshape(batch, seq, FOX_W), kb.reshape(batch, seq, FOX_W),
      vb.reshape(batch, seq, FOX_HEADS * LANES), cum, cum_cols.reshape(batch, seq, LANES))
    return out.reshape(batch * seq, FOX_W)


def _merge_ffn_kernel(x_ref, on_ref, of_ref, gmix_ref, wgm_ref, won_ref, wof_ref, wout_ref,
                      g2_ref, wup_ref, wdn_ref, o_ref):
    x = x_ref[...]
    u = _rms_rows(x, gmix_ref[...]).astype(BF16)
    gate = jax.nn.sigmoid(_dot(u, wgm_ref[...]))
    merged = (gate[:, :D_MODEL] * _dot(on_ref[...], won_ref[...])
              + gate[:, D_MODEL:] * _dot(of_ref[...], wof_ref[...]))
    x2 = x + _dot(merged.astype(BF16), wout_ref[...])
    o_ref[...] = _swiglu_residual(x2, g2_ref[...], wup_ref, wdn_ref)


def _merge_ffn(x2d, o_nsa, o_fox, mix_norm, w_gm, w_o_nsa, w_o_fox, w_out, gain2, w_up, w_down):
    n = x2d.shape[0]
    tm = _token_tile(n, wide=False)

    def row(width):
        return pl.BlockSpec((tm, width), lambda i: (i, 0))

    return pl.pallas_call(
        _merge_ffn_kernel,
        grid=(n // tm,),
        in_specs=[row(D_MODEL), row(NSA_W), row(FOX_W), _const_spec((1, D_MODEL)),
                  _const_spec((D_MODEL, 2 * D_MODEL)), _const_spec((NSA_W, D_MODEL)),
                  _const_spec((FOX_W, D_MODEL)), _const_spec((D_MODEL, D_MODEL)),
                  _const_spec((1, D_MODEL)), _const_spec((D_MODEL, 2 * D_FF)),
                  _const_spec((D_FF, D_MODEL))],
        out_specs=row(D_MODEL),
        out_shape=jax.ShapeDtypeStruct((n, D_MODEL), F32),
        compiler_params=_params(1),
        name="merge_ffn",
    )(x2d, o_nsa, o_fox, mix_norm.reshape(1, D_MODEL), w_gm, w_o_nsa.astype(BF16),
      w_o_fox.astype(BF16), w_out.astype(BF16), gain2.reshape(1, D_MODEL), w_up.astype(BF16),
      w_down.astype(BF16))


def _layer(x, ffn1_norm, ffn1_w_up, ffn1_w_down, mix_norm, w_in, b_forget, nsa_q_gain, nsa_k_gain,
           fox_q_gain, fox_k_gain, cmp_pos_k, cmp_pos_v, cmp_k_w1, cmp_k_w2, cmp_v_w1, cmp_v_w2,
           w_o_nsa, w_o_fox, w_out, ffn2_norm, ffn2_w_up, ffn2_w_down, rel_bias_table):
    batch, seq, d = x.shape
    assert d == D_MODEL and seq % ATT_TILE == 0 and (batch * seq) % TOKEN_TILE == 0
    assert seq // SLC_BLOCK <= LANES and (seq - CMP_BLOCK) // CMP_STRIDE + 1 == LANES - 1
    assert seq // (CMP_BLOCK // 2) == LANES
    x2d = x.reshape(batch * seq, D_MODEL)

    x1 = _ffn(x2d, ffn1_norm, ffn1_w_up, ffn1_w_down)

    w_packed, gain_row, small_bias, w_gm = _pack_in_proj(w_in, b_forget, nsa_q_gain, nsa_k_gain,
                                                         fox_q_gain, fox_k_gain)
    qa, qb, kb, ksw, vb, vsw, kc, vc, small = _in_proj(x1, mix_norm, w_packed, gain_row, small_bias,
                                                       batch, seq)

    kcmp, vcmp = _compress(kc, vc, batch, seq, _pack_compress(cmp_pos_k, cmp_k_w1, cmp_k_w2),
                           _pack_compress(cmp_pos_v, cmp_v_w1, cmp_v_w2), nsa_k_gain[0])
    b_slc = _score_bound(nsa_q_gain, nsa_k_gain[1])
    b_win = _score_bound(nsa_q_gain, nsa_k_gain[2])
    t_hi, t_lo = jnp.max(rel_bias_table) * LOG2E, jnp.min(rel_bias_table) * LOG2E
    nsa_ok = 2.0 * jnp.maximum(b_slc, b_win) + (t_hi - t_lo) < EXP2_RANGE
    nsa_bounds = jnp.stack([b_slc + t_hi, b_win + t_hi])
    def nsa_branch(exact):
        toeplitz, cmp_bias = _bias_tables(rel_bias_table, seq // ATT_TILE)
        return _nsa(nsa_bounds, qa, kcmp, vcmp, ksw, vsw, small, toeplitz, cmp_bias, batch, seq, exact)

    o_nsa = lax.cond(nsa_ok, lambda: nsa_branch(False), lambda: nsa_branch(True))

    cum_cols, cum = _cumsum(small, batch, seq)
    b_fox = _score_bound(fox_q_gain, fox_k_gain)
    fox_bounds = jnp.stack([b_fox])
    o_fox = lax.cond(2.0 * b_fox < EXP2_RANGE,
                     lambda: _fox(fox_bounds, qb, kb, vb, cum, cum_cols, batch, seq, False),
                     lambda: _fox(fox_bounds, qb, kb, vb, cum, cum_cols, batch, seq, True))

    w_o_nsa_p = w_o_nsa.reshape(NSA_KV_GROUPS, NSA_Q_PER_GROUP, HEAD_DIM, D_MODEL).transpose(1, 0, 2, 3)
    out = _merge_ffn(x1, o_nsa, o_fox, mix_norm, w_gm, w_o_nsa_p.reshape(NSA_W, D_MODEL), w_o_fox,
                     w_out, ffn2_norm, ffn2_w_up, ffn2_w_down)
    return out.reshape(batch, seq, D_MODEL)


def kernel(x, ffn1_norm, ffn1_w_up, ffn1_w_down, mix_norm, w_in, b_forget, nsa_q_gain, nsa_k_gain,
           fox_q_gain, fox_k_gain, cmp_pos_k, cmp_pos_v, cmp_k_w1, cmp_k_w2, cmp_v_w1, cmp_v_w2,
           w_o_nsa, w_o_fox, w_out, ffn2_norm, ffn2_w_up, ffn2_w_down, rel_bias_table):
    for layer in range(ffn1_norm.shape[0]):
        x = _layer(x, ffn1_norm[layer], ffn1_w_up[layer], ffn1_w_down[layer], mix_norm[layer],
                   w_in[layer], b_forget[layer], nsa_q_gain[layer], nsa_k_gain[layer],
                   fox_q_gain[layer], fox_k_gain[layer], cmp_pos_k[layer], cmp_pos_v[layer],
                   cmp_k_w1[layer], cmp_k_w2[layer], cmp_v_w1[layer], cmp_v_w2[layer],
                   w_o_nsa[layer], w_o_fox[layer], w_out[layer], ffn2_norm[layer],
                   ffn2_w_up[layer], ffn2_w_down[layer], rel_bias_table)
    return x
```

```python
import functools
import math

import numpy as np
import jax
import jax.numpy as jnp
from jax import lax
from jax.experimental import pallas as pl
from jax.experimental.pallas import tpu as pltpu

F32 = jnp.float32
BF16 = jnp.bfloat16

D_MODEL = 1024
HEAD_DIM = 64
NSA_HEADS = 8
NSA_KV_GROUPS = 2
NSA_Q_PER_GROUP = NSA_HEADS // NSA_KV_GROUPS
CMP_BLOCK = 32
CMP_STRIDE = 16
CMP_HIDDEN = 128
SLC_BLOCK = 64
SLC_TOPK = 8
WINDOW = 512
FOX_HEADS = 8
D_FF = 2816
N_BUCKETS = 32
MAX_DISTANCE = 128
RMS_EPS = 1e-6
NEG_INF = -1.0e30
FORCE_BONUS = 1.0e4

NSA_W = NSA_HEADS * HEAD_DIM
NSA_KV_W = NSA_KV_GROUPS * HEAD_DIM
FOX_W = FOX_HEADS * HEAD_DIM
IN_SPLITS = (NSA_W, NSA_KV_W, NSA_KV_W, NSA_KV_W, NSA_KV_W, NSA_KV_W, NSA_KV_W, 3 * NSA_HEADS,
             FOX_W, FOX_W, FOX_W, FOX_HEADS, 2 * D_MODEL)

LANES = 128
TOKEN_TILE = 512
MXU_DIM = 256
FFN_CHUNKS = (-(-D_FF // (2 * MXU_DIM)) * MXU_DIM, D_FF - -(-D_FF // (2 * MXU_DIM)) * MXU_DIM)
ATT_TILE = 256
N_WIN_TILES = WINDOW // ATT_TILE + 1
VMEM_LIMIT = 56 * 1024 * 1024
LOG2E = 1.4426950408889634
EXP2_RANGE = 100.0

PQ_A = 0
PQ_B = PQ_A + NSA_W
PK_B = PQ_B + FOX_W
PK_SW = PK_B + FOX_W
P_NORM_END = PK_SW + 2 * NSA_KV_W
PV_B = P_NORM_END
PV_SW = PV_B + FOX_W
PKV_C = PV_SW + 2 * NSA_KV_W
P_SMALL = PKV_C + 2 * NSA_KV_W
P_END = P_SMALL + LANES
NORM_CHUNK = 256


def _dot(a, b):
    return jnp.dot(a, b, preferred_element_type=F32)


def _dot_nt(a, b):
    return lax.dot_general(a, b, (((1,), (1,)), ((), ())), preferred_element_type=F32)


def _split_dot(x, w):
    hi = x.astype(BF16)
    lo = (x - hi.astype(F32)).astype(BF16)
    return _dot(hi, w) + _dot(lo, w)


def _rms_rows(x, gain_row):
    ms = jnp.mean(x * x, axis=-1, keepdims=True)
    return x * lax.rsqrt(ms + RMS_EPS) * gain_row


def _const_spec(shape):
    nd = len(shape)
    return pl.BlockSpec(shape, lambda *_: (0,) * nd, pipeline_mode=pl.Buffered(1))


def _token_tile(n_rows, wide):
    tile = 2 * TOKEN_TILE if wide and n_rows % (2 * TOKEN_TILE) == 0 else TOKEN_TILE
    assert n_rows % tile == 0
    return tile


def _params(n_axes):
    return pltpu.CompilerParams(dimension_semantics=("arbitrary",) * n_axes,
                                vmem_limit_bytes=VMEM_LIMIT)


def _swiglu_residual(x, gain_row, wup_ref, wdn_ref):
    xn = _rms_rows(x, gain_row).astype(BF16)
    acc = jnp.zeros(x.shape, F32)
    lo = 0
    for width in FFN_CHUNKS:
        gate = _dot(xn, wup_ref[:, lo:lo + width])
        up = _dot(xn, wup_ref[:, D_FF + lo:D_FF + lo + width])
        h = (gate * jax.nn.sigmoid(gate) * up).astype(BF16)
        acc = acc + _dot(h, wdn_ref[lo:lo + width, :])
        lo += width
    return x + 0.5 * acc


def _ffn_kernel(x_ref, g_ref, wup_ref, wdn_ref, o_ref):
    o_ref[...] = _swiglu_residual(x_ref[...], g_ref[...], wup_ref, wdn_ref)


def _ffn(x2d, gain, w_up, w_down):
    n = x2d.shape[0]
    tm = _token_tile(n, wide=True)
    row = pl.BlockSpec((tm, D_MODEL), lambda i: (i, 0))
    return pl.pallas_call(
        _ffn_kernel,
        grid=(n // tm,),
        in_specs=[row, _const_spec((1, D_MODEL)), _const_spec((D_MODEL, 2 * D_FF)),
                  _const_spec((D_FF, D_MODEL))],
        out_specs=row,
        out_shape=jax.ShapeDtypeStruct((n, D_MODEL), F32),
        compiler_params=_params(1),
        name="ffn",
    )(x2d, gain.reshape(1, D_MODEL), w_up.astype(BF16), w_down.astype(BF16))


def _block_ones(width, size=NORM_CHUNK):
    idx = np.arange(size) // width
    return jnp.asarray((idx[:, None] == idx[None, :]).astype(np.float32), BF16)


def _pack_in_proj(w_in, b_forget, nsa_q_gain, nsa_k_gain, fox_q_gain, fox_k_gain):
    scale = LOG2E / math.sqrt(HEAD_DIM)
    pts = np.cumsum(np.array(IN_SPLITS))[:-1].tolist()
    qa, kc, vc, ks, vs, kw, vw, ga, qb, kb, vb, fb, gm = jnp.split(w_in, pts, axis=1)
    cols = []
    for r in range(NSA_Q_PER_GROUP):
        for g in range(NSA_KV_GROUPS):
            h = g * NSA_Q_PER_GROUP + r
            cols.append(qa[:, h * HEAD_DIM:(h + 1) * HEAD_DIM])
    cols += [qb, kb, ks, kw]
    gains = [jnp.tile(nsa_q_gain * scale, NSA_HEADS), jnp.tile(fox_q_gain * scale, FOX_HEADS),
             jnp.tile(fox_k_gain, FOX_HEADS), jnp.tile(nsa_k_gain[1], NSA_KV_GROUPS),
             jnp.tile(nsa_k_gain[2], NSA_KV_GROUPS)]
    n_small_pad = LANES - ga.shape[1] - fb.shape[1]
    cols += [vb, vs, vw, kc, vc, ga, fb, jnp.zeros((D_MODEL, n_small_pad), F32)]
    w_packed = jnp.concatenate(cols, axis=1).astype(BF16)
    gain_row = jnp.concatenate(gains).reshape(1, P_NORM_END)
    small_bias = jnp.concatenate([jnp.zeros((ga.shape[1],), F32), b_forget,
                                  jnp.zeros((n_small_pad,), F32)]).reshape(1, LANES)
    return w_packed, gain_row, small_bias, gm.astype(BF16)


def _in_proj_kernel(x_ref, g_ref, w_ref, gain_ref, sbias_ref, bd64_ref,
                    qa_ref, qb_ref, kb_ref, ksw_ref, vb_ref, vsw_ref, kc_ref, vc_ref, small_ref, kv_scr):
    u = _rms_rows(x_ref[...], g_ref[...]).astype(BF16)
    tm = u.shape[0]

    y = _dot(u, w_ref[:, 0:P_NORM_END])
    n_chunks = P_NORM_END // NORM_CHUNK
    squares = jnp.concatenate([jnp.square(y[:, c * NORM_CHUNK:(c + 1) * NORM_CHUNK]).astype(BF16)
                               for c in range(n_chunks)], axis=0)
    ss = _dot(squares, bd64_ref[...])
    chunk = 0
    for o_ref in (qa_ref, qb_ref, kb_ref, ksw_ref):
        for i in range(o_ref.shape[1] // NORM_CHUNK):
            cols = slice(chunk * NORM_CHUNK, (chunk + 1) * NORM_CHUNK)
            inv = lax.rsqrt(ss[chunk * tm:(chunk + 1) * tm] * (1.0 / HEAD_DIM) + RMS_EPS)
            o_ref[:, i * NORM_CHUNK:(i + 1) * NORM_CHUNK] = (y[:, cols] * inv * gain_ref[:, cols]).astype(BF16)
            chunk += 1
    low = lax.broadcasted_iota(jnp.int32, (u.shape[0], LANES), 1) < HEAD_DIM

    def store_with_ones(v, o_ref):
        for i in range(v.shape[1] // LANES):
            pair = v[:, i * LANES:(i + 1) * LANES]
            o_ref[:, 2 * i * LANES:(2 * i + 1) * LANES] = jnp.where(low, pair, 1.0).astype(BF16)
            o_ref[:, (2 * i + 1) * LANES:(2 * i + 2) * LANES] = jnp.where(low, 1.0, pair).astype(BF16)

    rest = _dot(u, w_ref[:, P_NORM_END:P_END])
    store_with_ones(rest[:, PV_B - P_NORM_END:PV_SW - P_NORM_END], vb_ref)
    store_with_ones(rest[:, PV_SW - P_NORM_END:PKV_C - P_NORM_END], vsw_ref)
    for j, o_ref in enumerate((kc_ref, vc_ref)):
        c0 = PKV_C - P_NORM_END + j * NSA_KV_W
        kv_scr[j] = rest[:, c0:c0 + NSA_KV_W]
        for l in range(CMP_STRIDE):
            o_ref[0, :, l * NSA_KV_W:(l + 1) * NSA_KV_W] = kv_scr[j, pl.ds(l, tm // CMP_STRIDE,
                                                                           stride=CMP_STRIDE), :]
    z = rest[:, P_SMALL - P_NORM_END:] + sbias_ref[...]
    lane = lax.broadcasted_iota(jnp.int32, z.shape, 1)
    log_sig = jnp.minimum(z, 0.0) - jnp.log1p(jnp.exp(-jnp.abs(z)))
    small_ref[...] = jnp.where(lane < 3 * NSA_HEADS, jax.nn.sigmoid(z), log_sig)


def _in_proj(x2d, mix_norm, w_packed, gain_row, small_bias, batch, seq):
    n = x2d.shape[0]
    tm = _token_tile(seq, wide=True)
    steps_per_seq = seq // tm

    def row(width):
        return pl.BlockSpec((tm, width), lambda i: (i, 0))

    def out(width, dtype):
        return row(width), jax.ShapeDtypeStruct((n, width), dtype)

    grouped = (pl.BlockSpec((1, tm // CMP_STRIDE, CMP_STRIDE * NSA_KV_W),
                            lambda i: (i // steps_per_seq, i % steps_per_seq, 0)),
               jax.ShapeDtypeStruct((batch, seq // CMP_STRIDE, CMP_STRIDE * NSA_KV_W), F32))
    outs = [out(NSA_W, BF16), out(FOX_W, BF16), out(FOX_W, BF16), out(2 * NSA_KV_W, BF16),
            out(2 * FOX_W, BF16), out(4 * NSA_KV_W, BF16), grouped, grouped, out(LANES, F32)]
    return pl.pallas_call(
        _in_proj_kernel,
        grid=(n // tm,),
        in_specs=[row(D_MODEL), _const_spec((1, D_MODEL)), _const_spec((D_MODEL, P_END)),
                  _const_spec((1, P_NORM_END)), _const_spec((1, LANES)),
                  _const_spec((NORM_CHUNK, NORM_CHUNK))],
        out_specs=[spec for spec, _ in outs],
        out_shape=[shape for _, shape in outs],
        scratch_shapes=[pltpu.VMEM((2, tm, NSA_KV_W), F32)],
        compiler_params=_params(1),
        name="in_proj",
    )(x2d, mix_norm.reshape(1, D_MODEL), w_packed, gain_row, small_bias, _block_ones(HEAD_DIM))


def _pack_compress(pos, w1, w2):
    half = CMP_BLOCK // 2
    eye = jnp.eye(NSA_KV_GROUPS, dtype=F32)
    w1r = w1.reshape(CMP_BLOCK, HEAD_DIM, CMP_HIDDEN)

    def big(w):
        return jnp.einsum('ldh,pg->lpdgh', w, eye).reshape(half * NSA_KV_W, NSA_KV_GROUPS * CMP_HIDDEN)

    def posrow(p):
        return jnp.broadcast_to(p[:, None, :], (half, NSA_KV_GROUPS, HEAD_DIM)).reshape(1, half * NSA_KV_W)

    w2big = jnp.einsum('hd,pg->phgd', w2, eye).reshape(NSA_KV_GROUPS * CMP_HIDDEN, NSA_KV_W)
    return (posrow(pos[:half]), posrow(pos[half:]), big(w1r[:half]).astype(BF16),
            big(w1r[half:]).astype(BF16), w2big.astype(BF16))


def _compress_kernel(kc_ref, vc_ref, kp_lo, kp_hi, kw_lo, kw_hi, kw2, vp_lo, vp_hi, vw_lo, vw_hi, vw2,
                     kgain_ref, bd64_ref, ko_ref, vo_ref):
    n_seq, rows, width = kc_ref.shape

    def mlp(x_ref, p_lo, p_hi, w_lo, w_hi, w2):
        r = x_ref[...].reshape(n_seq * rows, width)
        first = _dot((r + p_lo[...]).astype(BF16), w_lo[...])
        second = _dot((r + p_hi[...]).astype(BF16), w_hi[...])
        h = first + pltpu.roll(second, second.shape[0] - 1, axis=0)
        return _dot((h * jax.nn.sigmoid(h)).astype(BF16), w2[...])

    k = mlp(kc_ref, kp_lo, kp_hi, kw_lo, kw_hi, kw2)
    ss = _split_dot(k * k, bd64_ref[...])
    k = (k * lax.rsqrt(ss * (1.0 / HEAD_DIM) + RMS_EPS) * kgain_ref[...]).astype(BF16)
    ko_ref[...] = k.reshape(ko_ref.shape)
    vo_ref[...] = mlp(vc_ref, vp_lo, vp_hi, vw_lo, vw_hi, vw2).astype(BF16).reshape(vo_ref.shape)


def _compress(kc, vc, batch, seq, k_pack, v_pack, k_gain):
    rows = seq // (CMP_BLOCK // 2)
    width = (CMP_BLOCK // 2) * NSA_KV_W
    per_step = next(n for n in (4, 2, 1) if batch % n == 0)
    blk = pl.BlockSpec((per_step, rows, width), lambda b: (b, 0, 0))
    out = pl.BlockSpec((per_step, rows, NSA_KV_W), lambda b: (b, 0, 0))
    pack_specs = [_const_spec((1, width)), _const_spec((1, width)),
                  _const_spec((width, NSA_KV_GROUPS * CMP_HIDDEN)),
                  _const_spec((width, NSA_KV_GROUPS * CMP_HIDDEN)),
                  _const_spec((NSA_KV_GROUPS * CMP_HIDDEN, NSA_KV_W))]
    bd = _block_ones(HEAD_DIM, NSA_KV_W)
    return pl.pallas_call(
        _compress_kernel,
        grid=(batch // per_step,),
        in_specs=[blk, blk] + pack_specs + pack_specs + [_const_spec((1, NSA_KV_W)),
                                                         _const_spec((NSA_KV_W, NSA_KV_W))],
        out_specs=[out, out],
        out_shape=[jax.ShapeDtypeStruct((batch, rows, NSA_KV_W), BF16)] * 2,
        compiler_params=_params(1),
        name="compress",
    )(kc.reshape(batch, rows, width), vc.reshape(batch, rows, width), *k_pack, *v_pack,
      jnp.tile(k_gain, NSA_KV_GROUPS).reshape(1, NSA_KV_W), bd)


FORGET_LANE = 3 * NSA_HEADS


def _cumsum_kernel(x_ref, cols_ref, rows_ref):
    n_seq, _, seq = rows_ref.shape
    x = x_ref[...]
    row = lax.broadcasted_iota(jnp.int32, x.shape, 0) % seq
    k = 1
    while k < seq:
        x = x + jnp.where(row >= k, pltpu.roll(x, k, axis=0), 0.0)
        k *= 2
    cols_ref[...] = x
    x_t = x.T
    for j in range(n_seq):
        rows_ref[j] = x_t[FORGET_LANE:FORGET_LANE + FOX_HEADS, j * seq:(j + 1) * seq]


def _cumsum(small, batch, seq):
    per_step = next(n for n in (4, 2, 1) if batch % n == 0)
    spec = pl.BlockSpec((per_step * seq, LANES), lambda b: (b, 0))
    return pl.pallas_call(
        _cumsum_kernel, grid=(batch // per_step,), in_specs=[spec],
        out_specs=[spec, pl.BlockSpec((per_step, FOX_HEADS, seq), lambda b: (b, 0, 0))],
        out_shape=[jax.ShapeDtypeStruct((batch * seq, LANES), F32),
                   jax.ShapeDtypeStruct((batch, FOX_HEADS, seq), F32)],
        compiler_params=_params(1), name="cumsum",
    )(small)


def _write_bias_rows(d, valid, tab_ref, o_ref, rows):
    max_exact = N_BUCKETS // 2
    n = jnp.maximum(d, 0)
    nf = jnp.maximum(n, 1).astype(F32)
    large = max_exact + (jnp.log(nf / max_exact) / math.log(MAX_DISTANCE / max_exact)
                         * (N_BUCKETS - max_exact)).astype(jnp.int32)
    bucket = jnp.where(n < max_exact, n, jnp.minimum(large, N_BUCKETS - 1))
    for h in range(NSA_HEADS):
        acc = jnp.zeros(d.shape, F32)
        for b in range(N_BUCKETS):
            acc = jnp.where(bucket == b, tab_ref[b, h], acc)
        g, r = divmod(h, NSA_Q_PER_GROUP)
        o_ref[0, g, r, rows, :] = jnp.where(valid, acc * LOG2E, NEG_INF)


def _write_far_tile(valid, tab_ref, o_ref):
    for h in range(NSA_HEADS):
        g, r = divmod(h, NSA_Q_PER_GROUP)
        far = jnp.where(valid, tab_ref[N_BUCKETS - 1, h], 0.0) * LOG2E
        o_ref[0, g, r] = jnp.where(valid, far, NEG_INF)


def _toeplitz_bias_kernel(tab_ref, o_ref):
    s = pl.program_id(0)
    tile = ATT_TILE

    def distances(n_rows):
        i = lax.broadcasted_iota(jnp.int32, (n_rows, tile), 0)
        j = lax.broadcasted_iota(jnp.int32, (n_rows, tile), 1)
        d = jnp.minimum(s, N_WIN_TILES - 1) * tile + i - j
        d_hi = jnp.where(s < N_WIN_TILES, WINDOW, jnp.int32(1 << 30))
        return d, (d >= 0) & (d < d_hi)

    _write_far_tile(distances(tile)[1], tab_ref, o_ref)
    for step in range(N_WIN_TILES + 1):
        near = min(tile, max(0, MAX_DISTANCE + tile - 1 - min(step, N_WIN_TILES - 1) * tile))
        near = -(-near // 8) * 8
        if near:
            @pl.when(s == step)
            def _(near=near):
                d, valid = distances(near)
                _write_bias_rows(d, valid, tab_ref, o_ref, slice(0, near))


def _cmp_bias_kernel(tab_ref, o_ref):
    t = pl.program_id(0)
    chunk = 8

    def distances(c0, n_rows):
        c = c0 + lax.broadcasted_iota(jnp.int32, (n_rows, ATT_TILE), 0)
        i = lax.broadcasted_iota(jnp.int32, (n_rows, ATT_TILE), 1)
        d = t * ATT_TILE + i - CMP_STRIDE * c - (CMP_BLOCK - 1)
        return d, (d >= 0) & (c < LANES - 1)

    _write_far_tile(distances(0, LANES)[1], tab_ref, o_ref)
    first = (t * ATT_TILE - (MAX_DISTANCE + CMP_BLOCK - 1)) // (chunk * CMP_STRIDE)
    n_chunks = (ATT_TILE + MAX_DISTANCE) // (chunk * CMP_STRIDE) + 2

    def refine(m, carry):
        c0 = pl.multiple_of(jnp.clip(first + m, 0, LANES // chunk - 1) * chunk, chunk)
        d, valid = distances(c0, chunk)
        _write_bias_rows(d, valid, tab_ref, o_ref, pl.ds(c0, chunk))
        return carry

    lax.fori_loop(0, n_chunks, refine, 0)


def _bias_tables(rel_bias_table, n_q_tiles):
    def call(body, steps, rows, name):
        shape = (steps, NSA_KV_GROUPS, NSA_Q_PER_GROUP, rows, ATT_TILE)
        return pl.pallas_call(
            body, grid=(steps,),
            in_specs=[pl.BlockSpec(memory_space=pltpu.SMEM)],
            out_specs=pl.BlockSpec((1,) + shape[1:], lambda s: (s, 0, 0, 0, 0)),
            out_shape=jax.ShapeDtypeStruct(shape, F32),
            compiler_params=_params(1), name=name,
        )(rel_bias_table)

    return (call(_toeplitz_bias_kernel, N_WIN_TILES + 1, ATT_TILE, "toeplitz_bias"),
            call(_cmp_bias_kernel, n_q_tiles, LANES, "cmp_bias"))


def _score_bound(q_gain, k_gain):
    return 1.02 * LOG2E * math.sqrt(HEAD_DIM) * jnp.max(jnp.abs(q_gain)) * jnp.max(jnp.abs(k_gain))


def _attention(chains, groups, bounds, acc_ref):
    n = len(chains)

    def later(k):
        return tuple(i for _, ids in groups[k:] for i in ids)

    def exact_max():
        def half_max(i, dl):
            s = chains[i][0](dl)
            return jnp.maximum(s[:, :LANES], s[:, LANES:])

        mx = [half_max(i, 0) for i in range(n)]
        lo = 1
        for k, (last, _) in enumerate(groups):
            members = later(k)

            def body(dl, carry, members=members):
                return tuple(jnp.maximum(c, half_max(i, dl)) for c, i in zip(carry, members))

            for i, r in zip(members, lax.fori_loop(lo, last + 1, body, tuple(mx[i] for i in members))):
                mx[i] = r
            lo = last + 1
        return tuple(jnp.broadcast_to(jnp.max(m, axis=-1, keepdims=True), (m.shape[0], ATT_TILE))
                     for m in mx)

    shift = exact_max() if bounds is None else [b() for b in bounds]

    def weights(i, dl):
        return jnp.exp2(chains[i][0](dl) - shift[i]).astype(BF16)

    def product(i, dl):
        return chains[i][1](weights(i, dl), dl)

    for i in range(n):
        acc_ref[i] = product(i, 0)
    lo = 1
    for k, (last, _) in enumerate(groups):
        members = later(k)
        count = last + 1 - lo

        def body(j, carry, members=members, lo=lo):
            dl = lo + 2 * j
            for i in members:
                both = jnp.concatenate([weights(i, dl + 1), weights(i, dl)], axis=1)
                acc_ref[i] += chains[i][1](both, dl, 2)
            return carry

        lax.fori_loop(0, count // 2, body, 0)

        @pl.when(count % 2 == 1)
        def _(members=members, last=last):
            for i in members:
                acc_ref[i] += product(i, last)

        lo = last + 1
    return [acc_ref[i] for i in range(n)]


def _pair_lanes(first, second, normalise):
    low = lax.broadcasted_iota(jnp.int32, first.shape, 1) < HEAD_DIM
    pair = jnp.where(low, first, second)
    if not normalise:
        return pair
    sums = pltpu.roll(jnp.where(low, second, first), HEAD_DIM, axis=1)
    return pair * (1.0 / sums)


def _select_blocks(imp_t, t0, tq):
    blk = lax.broadcasted_iota(jnp.int32, imp_t.shape, 0)
    cur = (t0 + (lax.broadcasted_iota(jnp.int32, imp_t.shape, 1) & (tq - 1))) // SLC_BLOCK
    forced = (blk == 0) | (blk == cur) | (blk == cur - 1)
    score = jnp.where(blk <= cur, imp_t + jnp.where(forced, FORCE_BONUS, 0.0), NEG_INF)
    blk_f = blk.astype(F32)
    dead = -3.0e38
    picked = jnp.zeros(imp_t.shape, F32)
    for _ in range(SLC_TOPK):
        best = jnp.max(score, axis=0, keepdims=True)
        first = jnp.min(jnp.where(score == best, blk_f, float(LANES)), axis=0, keepdims=True)
        hit = blk_f == first
        picked = jnp.where(hit, 1.0, picked)
        score = jnp.where(hit, dead, score)
    return picked


def _nsa_kernel(bnd_ref, qa_ref, kcmp_ref, vcmp_ref, ksw_ref, vsw_ref, small_ref, tb_ref, bc_ref, ov_ref,
                e_ref, gx_ref, o_ref, amask_ref, acc_ref, *, n_blocks, exact):
    qt = pl.program_id(1)
    tq = ATT_TILE
    rq = NSA_Q_PER_GROUP
    lane = lax.broadcasted_iota(jnp.int32, (tq, LANES), 1)
    own_lanes = [jnp.where((lane // HEAD_DIM) == g, 1.0, 0.0).astype(BF16) for g in range(NSA_KV_GROUPS)]

    def key_rows(dl, n=1):
        return pl.ds(pl.multiple_of((qt - dl - (n - 1)) * tq, tq), n * tq)

    groups = range(NSA_KV_GROUPS)
    n_seq = qa_ref.shape[0]
    n_grp = NSA_KV_GROUPS

    def win_chain(b, g, q):
        def score(dl):
            s = _dot_nt(q, ksw_ref[b, key_rows(dl), LANES:2 * LANES]).reshape(rq, tq, tq)
            return (s + tb_ref[dl, g]).reshape(rq * tq, tq)

        def pv(p, dl, n=1):
            c0 = (n_grp + g) * LANES
            return _dot(p, vsw_ref[b, key_rows(dl, n), c0:c0 + LANES])

        return score, pv

    def slc_chain(b, g, q):
        def score(dl):
            tile = dl if isinstance(dl, int) else jnp.where(dl < N_WIN_TILES - 1, dl, N_WIN_TILES)
            rows = key_rows(dl)
            s = _dot_nt(q, ksw_ref[b, rows, 0:LANES]).reshape(rq, tq, tq)
            return (s + tb_ref[tile, g] + amask_ref[b * n_grp + g, :, rows][None]).reshape(rq * tq, tq)

        def pv(p, dl, n=1):
            return _dot(p, vsw_ref[b, key_rows(dl, n), g * LANES:(g + 1) * LANES])

        return score, pv

    overlap = ov_ref[...].astype(BF16)
    key_block = e_ref[...].astype(BF16)
    gate_expand = [gx_ref[br].astype(BF16) for br in range(3)]

    slc_chains, win_chains, o_cmps = [], [], []
    for b in range(n_seq):
        qs = [jnp.concatenate([qa_ref[b, :, r * LANES:(r + 1) * LANES] * own_lanes[g] for r in range(rq)],
                              axis=0) for g in groups]
        slc_chains += [slc_chain(b, g, qs[g]) for g in groups]
        win_chains += [win_chain(b, g, qs[g]) for g in groups]

        s = (_dot_nt(kcmp_ref[b], jnp.concatenate(qs, axis=0))
             + jnp.concatenate([bc_ref[0, g, r] for g in groups for r in range(rq)], axis=1))
        m = jnp.max(s, axis=0, keepdims=True)
        p = jnp.where(s > 0.5 * NEG_INF, jnp.exp2(s - m), 0.0)
        l = jnp.sum(p, axis=0, keepdims=True)
        p_c = p * (1.0 / jnp.where(l > 0.0, l, 1.0))
        o_cmp = _dot(p_c.T.astype(BF16), vcmp_ref[b])
        o_cmps.append([o_cmp[g * rq * tq:(g + 1) * rq * tq] for g in groups])

        p_sum = jnp.concatenate([sum(p_c[:, (g * rq + r) * tq:(g * rq + r + 1) * tq] for r in range(rq))
                                 for g in groups], axis=1)
        hi = p_sum.astype(BF16)
        lo = (p_sum - hi.astype(F32)).astype(BF16)
        imp_t = _dot(overlap, hi) + _dot(overlap, lo)
        sel_t = _select_blocks(imp_t[0:n_blocks], qt * tq, tq)
        block_bias = jnp.concatenate([jnp.where(sel_t > 0.0, 0.0, NEG_INF),
                                      jnp.zeros((LANES - n_blocks, n_grp * tq), F32)], axis=0)
        amask_ref[b * n_grp:(b + 1) * n_grp] = _dot(block_bias.T.astype(BF16), key_block).reshape(
            n_grp, tq, e_ref.shape[1])

    n_slc = len(slc_chains)
    slc_ids = tuple(range(n_slc))
    win_ids = tuple(range(n_slc, 2 * n_slc))
    bounds = None if exact else [lambda i=i: jnp.full((rq * tq, tq), bnd_ref[i // n_slc], F32)
                                 for i in range(2 * n_slc)]
    accs = _attention(slc_chains + win_chains,
                      [(jnp.minimum(qt, N_WIN_TILES - 1), win_ids), (qt, slc_ids)], bounds, acc_ref)

    for b in range(n_seq):
        gates = small_ref[b]
        g_hi = gates.astype(BF16)
        g_lo = (gates - g_hi.astype(F32)).astype(BF16)
        out = [jnp.zeros((tq, LANES), F32) for _ in range(rq)]
        lo_, hi_ = b * n_grp, (b + 1) * n_grp
        for br, per_group in enumerate((o_cmps[b], accs[lo_:hi_], accs[n_slc + lo_:n_slc + hi_])):
            gate = _dot(g_hi, gate_expand[br]) + _dot(g_lo, gate_expand[br])
            for r in range(rq):
                first, second = (a[r * tq:(r + 1) * tq] for a in per_group)
                pair = _pair_lanes(first, second, normalise=br > 0)
                out[r] = out[r] + gate[:, r * LANES:(r + 1) * LANES] * pair
        for r in range(rq):
            o_ref[b, :, r * LANES:(r + 1) * LANES] = out[r].astype(BF16)


def _nsa(bounds, qa, kcmp, vcmp, ksw, vsw, small, toeplitz, cmp_bias, batch, seq, exact):
    tq = ATT_TILE
    nq = seq // tq
    rq = NSA_Q_PER_GROUP
    n_blocks = seq // SLC_BLOCK
    n_cmp = (seq - CMP_BLOCK) // CMP_STRIDE + 1
    ci = np.arange(LANES)[:, None] * CMP_STRIDE
    sj = np.arange(LANES)[None, :] * SLC_BLOCK
    overlap = ((ci <= sj + SLC_BLOCK - 1) & (ci + CMP_BLOCK - 1 >= sj)
               & (np.arange(LANES)[:, None] < n_cmp) & (np.arange(LANES)[None, :] < n_blocks))
    expand = np.arange(LANES)[:, None] == (np.arange(seq)[None, :] // SLC_BLOCK)
    gate_expand = np.zeros((3, LANES, NSA_W), np.float32)
    for b in range(3):
        for g in range(NSA_KV_GROUPS):
            for r in range(rq):
                c0 = r * LANES + g * HEAD_DIM
                gate_expand[b, b * NSA_HEADS + g * rq + r, c0:c0 + HEAD_DIM] = 1.0
    per_step = 2 if batch % 2 == 0 else 1

    def rows(n_rows, width):
        return pl.BlockSpec((per_step, n_rows, width), lambda b, t: (b, 0, 0))

    def tile(width):
        return pl.BlockSpec((per_step, tq, width), lambda b, t: (b, t, 0))

    out = pl.pallas_call(
        functools.partial(_nsa_kernel, n_blocks=n_blocks, exact=exact),
        grid=(batch // per_step, nq),
        in_specs=[
            pl.BlockSpec(memory_space=pltpu.SMEM),
            tile(NSA_W),
            rows(LANES, NSA_KV_W),
            rows(LANES, NSA_KV_W),
            rows(seq, 2 * NSA_KV_W),
            rows(seq, 2 * NSA_KV_GROUPS * LANES),
            tile(LANES),
            _const_spec((N_WIN_TILES + 1, NSA_KV_GROUPS, rq, tq, tq)),
            pl.BlockSpec((1, NSA_KV_GROUPS, rq, LANES, tq), lambda b, t: (t, 0, 0, 0, 0)),
            _const_spec((LANES, LANES)),
            _const_spec((LANES, seq)),
            _const_spec((3, LANES, NSA_W)),
        ],
        out_specs=tile(NSA_W),
        out_shape=jax.ShapeDtypeStruct((batch, seq, NSA_W), BF16),
        scratch_shapes=[pltpu.VMEM((per_step * NSA_KV_GROUPS, tq, seq), F32),
                        pltpu.VMEM((per_step * 2 * NSA_KV_GROUPS, rq * tq, LANES), F32)],
        compiler_params=_params(2),
        name="nsa",
    )(bounds, qa.reshape(batch, seq, NSA_W), kcmp, vcmp, ksw.reshape(batch, seq, 2 * NSA_KV_W),
      vsw.reshape(batch, seq, 2 * NSA_KV_GROUPS * LANES), small.reshape(batch, seq, LANES),
      toeplitz, cmp_bias,
      jnp.asarray(overlap.T, F32), jnp.asarray(expand, F32), jnp.asarray(gate_expand, F32))
    return out.reshape(batch * seq, NSA_W)


def _fox_kernel(bnd_ref, q_ref, k_ref, v_ref, cum_ref, cum_t_ref, o_ref, acc_ref, *, exact):
    qt = pl.program_id(1)
    tq = ATT_TILE
    pairs = FOX_HEADS // 2
    row = lax.broadcasted_iota(jnp.int32, (2, tq, tq), 1)
    col = lax.broadcasted_iota(jnp.int32, (2, tq, tq), 2)
    lane = lax.broadcasted_iota(jnp.int32, (tq, LANES), 1)
    low = jnp.where(lane < HEAD_DIM, 1.0, 0.0).astype(BF16)
    high = jnp.where(lane < HEAD_DIM, 0.0, 1.0).astype(BF16)
    n_seq = q_ref.shape[0]

    def key_rows(dl, n=1):
        return pl.ds(pl.multiple_of((qt - dl - (n - 1)) * tq, tq), n * tq)

    bounds = []

    def pair_chain(b, p):
        pair = q_ref[b, :, p * LANES:(p + 1) * LANES]
        q = jnp.concatenate([pair * low, pair * high], axis=0)
        base = cum_ref[b, 2 * p:2 * p + 2, pl.ds(pl.multiple_of(qt * tq, tq), LANES)][:, 0:1]

        def decay_of(rows):
            return (base - cum_ref[b, 2 * p:2 * p + 2, rows]) * LOG2E

        def bound():
            cum_rows = cum_t_ref[b]
            c0 = FORGET_LANE + 2 * p
            own = jnp.concatenate([base[h:h + 1, :] - cum_rows[:, c0 + h:c0 + h + 1] for h in range(2)],
                                  axis=0)
            return jnp.broadcast_to(own * LOG2E + bnd_ref[0], (2 * tq, tq))

        bounds.append(bound)

        def score(dl):
            rows = key_rows(dl)
            decay = decay_of(rows)
            s = _dot_nt(q, k_ref[b, rows, p * LANES:(p + 1) * LANES]).reshape(2, tq, tq) + decay[:, None, :]
            if isinstance(dl, int):
                s = jnp.where(col <= row, s, NEG_INF)
            return s.reshape(2 * tq, tq)

        def pv(w, dl, n=1):
            rows = key_rows(dl, n)
            return jnp.concatenate(
                [_dot(w[0:tq], v_ref[b, rows, 2 * p * LANES:(2 * p + 1) * LANES]),
                 _dot(w[tq:2 * tq], v_ref[b, rows, (2 * p + 1) * LANES:(2 * p + 2) * LANES])], axis=0)

        return score, pv

    chains = [pair_chain(b, p) for b in range(n_seq) for p in range(pairs)]
    accs = _attention(chains, [(qt, tuple(range(len(chains))))], None if exact else bounds, acc_ref)
    for i, acc in enumerate(accs):
        b, p = divmod(i, pairs)
        o_ref[b, :, p * LANES:(p + 1) * LANES] = _pair_lanes(acc[0:tq], acc[tq:2 * tq], True).astype(BF16)


def _fox(bounds, qb, kb, vb, cum, cum_cols, batch, seq, exact):
    tq = ATT_TILE
    nq = seq // tq
    per_step = 2 if batch % 2 == 0 else 1
    out = pl.pallas_call(
        functools.partial(_fox_kernel, exact=exact),
        grid=(batch // per_step, nq),
        in_specs=[
            pl.BlockSpec(memory_space=pltpu.SMEM),
            pl.BlockSpec((per_step, tq, FOX_W), lambda b, t: (b, t, 0)),
            pl.BlockSpec((per_step, seq, FOX_W), lambda b, t: (b, 0, 0)),
            pl.BlockSpec((per_step, seq, FOX_HEADS * LANES), lambda b, t: (b, 0, 0)),
            pl.BlockSpec((per_step, FOX_HEADS, seq), lambda b, t: (b, 0, 0)),
            pl.BlockSpec((per_step, tq, LANES), lambda b, t: (b, t, 0)),
        ],
        out_specs=pl.BlockSpec((per_step, tq, FOX_W), lambda b, t: (b, t, 0)),
        out_shape=jax.ShapeDtypeStruct((batch, seq, FOX_W), BF16),
        scratch_shapes=[pltpu.VMEM((per_step * FOX_HEADS // 2, 2 * tq, LANES), F32)],
        compiler_params=_params(2),
        name="fox",
    )(bounds, qb.reshape(batch, seq, FOX_W), kb.reshape(batch, seq, FOX_W),
      vb.reshape(batch, seq, FOX_HEADS * LANES), cum, cum_cols.reshape(batch, seq, LANES))
    return out.reshape(batch * seq, FOX_W)


def _merge_ffn_kernel(x_ref, on_ref, of_ref, gmix_ref, wgm_ref, won_ref, wof_ref, wout_ref,
                      g2_ref, wup_ref, wdn_ref, o_ref):
    x = x_ref[...]
    u = _rms_rows(x, gmix_ref[...]).astype(BF16)
    gate = jax.nn.sigmoid(_dot(u, wgm_ref[...]))
    merged = (gate[:, :D_MODEL] * _dot(on_ref[...], won_ref[...])
              + gate[:, D_MODEL:] * _dot(of_ref[...], wof_ref[...]))
    x2 = x + _dot(merged.astype(BF16), wout_ref[...])
    o_ref[...] = _swiglu_residual(x2, g2_ref[...], wup_ref, wdn_ref)


def _merge_ffn(x2d, o_nsa, o_fox, mix_norm, w_gm, w_o_nsa, w_o_fox, w_out, gain2, w_up, w_down):
    n = x2d.shape[0]
    tm = _token_tile(n, wide=False)

    def row(width):
        return pl.BlockSpec((tm, width), lambda i: (i, 0))

    return pl.pallas_call(
        _merge_ffn_kernel,
        grid=(n // tm,),
        in_specs=[row(D_MODEL), row(NSA_W), row(FOX_W), _const_spec((1, D_MODEL)),
                  _const_spec((D_MODEL, 2 * D_MODEL)), _const_spec((NSA_W, D_MODEL)),
                  _const_spec((FOX_W, D_MODEL)), _const_spec((D_MODEL, D_MODEL)),
                  _const_spec((1, D_MODEL)), _const_spec((D_MODEL, 2 * D_FF)),
                  _const_spec((D_FF, D_MODEL))],
        out_specs=row(D_MODEL),
        out_shape=jax.ShapeDtypeStruct((n, D_MODEL), F32),
        compiler_params=_params(1),
        name="merge_ffn",
    )(x2d, o_nsa, o_fox, mix_norm.reshape(1, D_MODEL), w_gm, w_o_nsa.astype(BF16),
      w_o_fox.astype(BF16), w_out.astype(BF16), gain2.reshape(1, D_MODEL), w_up.astype(BF16),
      w_down.astype(BF16))


def _layer(x, ffn1_norm, ffn1_w_up, ffn1_w_down, mix_norm, w_in, b_forget, nsa_q_gain, nsa_k_gain,
           fox_q_gain, fox_k_gain, cmp_pos_k, cmp_pos_v, cmp_k_w1, cmp_k_w2, cmp_v_w1, cmp_v_w2,
           w_o_nsa, w_o_fox, w_out, ffn2_norm, ffn2_w_up, ffn2_w_down, rel_bias_table):
    batch, seq, d = x.shape
    assert d == D_MODEL and seq % ATT_TILE == 0 and (batch * seq) % TOKEN_TILE == 0
    assert seq // SLC_BLOCK <= LANES and (seq - CMP_BLOCK) // CMP_STRIDE + 1 == LANES - 1
    assert seq // (CMP_BLOCK // 2) == LANES
    x2d = x.reshape(batch * seq, D_MODEL)

    x1 = _ffn(x2d, ffn1_norm, ffn1_w_up, ffn1_w_down)

    w_packed, gain_row, small_bias, w_gm = _pack_in_proj(w_in, b_forget, nsa_q_gain, nsa_k_gain,
                                                         fox_q_gain, fox_k_gain)
    qa, qb, kb, ksw, vb, vsw, kc, vc, small = _in_proj(x1, mix_norm, w_packed, gain_row, small_bias,
                                                       batch, seq)

    kcmp, vcmp = _compress(kc, vc, batch, seq, _pack_compress(cmp_pos_k, cmp_k_w1, cmp_k_w2),
                           _pack_compress(cmp_pos_v, cmp_v_w1, cmp_v_w2), nsa_k_gain[0])
    b_slc = _score_bound(nsa_q_gain, nsa_k_gain[1])
    b_win = _score_bound(nsa_q_gain, nsa_k_gain[2])
    t_hi, t_lo = jnp.max(rel_bias_table) * LOG2E, jnp.min(rel_bias_table) * LOG2E
    nsa_ok = 2.0 * jnp.maximum(b_slc, b_win) + (t_hi - t_lo) < EXP2_RANGE
    nsa_bounds = jnp.stack([b_slc + t_hi, b_win + t_hi])
    def nsa_branch(exact):
        toeplitz, cmp_bias = _bias_tables(rel_bias_table, seq // ATT_TILE)
        return _nsa(nsa_bounds, qa, kcmp, vcmp, ksw, vsw, small, toeplitz, cmp_bias, batch, seq, exact)

    o_nsa = lax.cond(nsa_ok, lambda: nsa_branch(False), lambda: nsa_branch(True))

    cum_cols, cum = _cumsum(small, batch, seq)
    b_fox = _score_bound(fox_q_gain, fox_k_gain)
    fox_bounds = jnp.stack([b_fox])
    o_fox = lax.cond(2.0 * b_fox < EXP2_RANGE,
                     lambda: _fox(fox_bounds, qb, kb, vb, cum, cum_cols, batch, seq, False),
                     lambda: _fox(fox_bounds, qb, kb, vb, cum, cum_cols, batch, seq, True))

    w_o_nsa_p = w_o_nsa.reshape(NSA_KV_GROUPS, NSA_Q_PER_GROUP, HEAD_DIM, D_MODEL).transpose(1, 0, 2, 3)
    out = _merge_ffn(x1, o_nsa, o_fox, mix_norm, w_gm, w_o_nsa_p.reshape(NSA_W, D_MODEL), w_o_fox,
                     w_out, ffn2_norm, ffn2_w_up, ffn2_w_down)
    return out.reshape(batch, seq, D_MODEL)


def kernel(x, ffn1_norm, ffn1_w_up, ffn1_w_down, mix_norm, w_in, b_forget, nsa_q_gain, nsa_k_gain,
           fox_q_gain, fox_k_gain, cmp_pos_k, cmp_pos_v, cmp_k_w1, cmp_k_w2, cmp_v_w1, cmp_v_w2,
           w_o_nsa, w_o_fox, w_out, ffn2_norm, ffn2_w_up, ffn2_w_down, rel_bias_table):
    for layer in range(ffn1_norm.shape[0]):
        x = _layer(x, ffn1_norm[layer], ffn1_w_up[layer], ffn1_w_down[layer], mix_norm[layer],
                   w_in[layer], b_forget[layer], nsa_q_gain[layer], nsa_k_gain[layer],
                   fox_q_gain[layer], fox_k_gain[layer], cmp_pos_k[layer], cmp_pos_v[layer],
                   cmp_k_w1[layer], cmp_k_w2[layer], cmp_v_w1[layer], cmp_v_w2[layer],
                   w_o_nsa[layer], w_o_fox[layer], w_out[layer], ffn2_norm[layer],
                   ffn2_w_up[layer], ffn2_w_down[layer], rel_bias_table)
    return x
```

```python
import functools
import math

import numpy as np
import jax
import jax.numpy as jnp
from jax import lax
from jax.experimental import pallas as pl
from jax.experimental.pallas import tpu as pltpu

F32 = jnp.float32
BF16 = jnp.bfloat16

D_MODEL = 1024
HEAD_DIM = 64
NSA_HEADS = 8
NSA_KV_GROUPS = 2
NSA_Q_PER_GROUP = NSA_HEADS // NSA_KV_GROUPS
CMP_BLOCK = 32
CMP_STRIDE = 16
CMP_HIDDEN = 128
SLC_BLOCK = 64
SLC_TOPK = 8
WINDOW = 512
FOX_HEADS = 8
D_FF = 2816
N_BUCKETS = 32
MAX_DISTANCE = 128
RMS_EPS = 1e-6
NEG_INF = -1.0e30
FORCE_BONUS = 1.0e4

NSA_W = NSA_HEADS * HEAD_DIM
NSA_KV_W = NSA_KV_GROUPS * HEAD_DIM
FOX_W = FOX_HEADS * HEAD_DIM
IN_SPLITS = (NSA_W, NSA_KV_W, NSA_KV_W, NSA_KV_W, NSA_KV_W, NSA_KV_W, NSA_KV_W, 3 * NSA_HEADS,
             FOX_W, FOX_W, FOX_W, FOX_HEADS, 2 * D_MODEL)

LANES = 128
TOKEN_TILE = 512
MXU_DIM = 256
FFN_CHUNKS = (-(-D_FF // (2 * MXU_DIM)) * MXU_DIM, D_FF - -(-D_FF // (2 * MXU_DIM)) * MXU_DIM)
ATT_TILE = 256
N_WIN_TILES = WINDOW // ATT_TILE + 1
VMEM_LIMIT = 56 * 1024 * 1024
LOG2E = 1.4426950408889634
EXP2_RANGE = 100.0

PQ_A = 0
PQ_B = PQ_A + NSA_W
PK_B = PQ_B + FOX_W
PK_SW = PK_B + FOX_W
P_NORM_END = PK_SW + 2 * NSA_KV_W
PV_B = P_NORM_END
PV_SW = PV_B + FOX_W
PKV_C = PV_SW + 2 * NSA_KV_W
P_SMALL = PKV_C + 2 * NSA_KV_W
P_END = P_SMALL + LANES
NORM_CHUNK = 256


def _dot(a, b):
    return jnp.dot(a, b, preferred_element_type=F32)


def _dot_nt(a, b):
    return lax.dot_general(a, b, (((1,), (1,)), ((), ())), preferred_element_type=F32)


def _split_dot(x, w):
    hi = x.astype(BF16)
    lo = (x - hi.astype(F32)).astype(BF16)
    return _dot(hi, w) + _dot(lo, w)


def _rms_rows(x, gain_row):
    ms = jnp.mean(x * x, axis=-1, keepdims=True)
    return x * lax.rsqrt(ms + RMS_EPS) * gain_row


def _const_spec(shape):
    nd = len(shape)
    return pl.BlockSpec(shape, lambda *_: (0,) * nd, pipeline_mode=pl.Buffered(1))


def _token_tile(n_rows, wide):
    tile = 2 * TOKEN_TILE if wide and n_rows % (2 * TOKEN_TILE) == 0 else TOKEN_TILE
    assert n_rows % tile == 0
    return tile


def _params(n_axes):
    return pltpu.CompilerParams(dimension_semantics=("arbitrary",) * n_axes,
                                vmem_limit_bytes=VMEM_LIMIT)


def _swiglu_residual(x, gain_row, wup_ref, wdn_ref):
    xn = _rms_rows(x, gain_row).astype(BF16)
    acc = jnp.zeros(x.shape, F32)
    lo = 0
    for width in FFN_CHUNKS:
        gate = _dot(xn, wup_ref[:, lo:lo + width])
        up = _dot(xn, wup_ref[:, D_FF + lo:D_FF + lo + width])
        h = (gate * jax.nn.sigmoid(gate) * up).astype(BF16)
        acc = acc + _dot(h, wdn_ref[lo:lo + width, :])
        lo += width
    return x + 0.5 * acc


def _ffn_kernel(x_ref, g_ref, wup_ref, wdn_ref, o_ref):
    o_ref[...] = _swiglu_residual(x_ref[...], g_ref[...], wup_ref, wdn_ref)


def _ffn(x2d, gain, w_up, w_down):
    n = x2d.shape[0]
    tm = _token_tile(n, wide=True)
    row = pl.BlockSpec((tm, D_MODEL), lambda i: (i, 0))
    return pl.pallas_call(
        _ffn_kernel,
        grid=(n // tm,),
        in_specs=[row, _const_spec((1, D_MODEL)), _const_spec((D_MODEL, 2 * D_FF)),
                  _const_spec((D_FF, D_MODEL))],
        out_specs=row,
        out_shape=jax.ShapeDtypeStruct((n, D_MODEL), F32),
        compiler_params=_params(1),
        name="ffn",
    )(x2d, gain.reshape(1, D_MODEL), w_up.astype(BF16), w_down.astype(BF16))


def _block_ones(width, size=NORM_CHUNK):
    idx = np.arange(size) // width
    return jnp.asarray((idx[:, None] == idx[None, :]).astype(np.float32), BF16)


def _pack_in_proj(w_in, b_forget, nsa_q_gain, nsa_k_gain, fox_q_gain, fox_k_gain):
    scale = LOG2E / math.sqrt(HEAD_DIM)
    pts = np.cumsum(np.array(IN_SPLITS))[:-1].tolist()
    qa, kc, vc, ks, vs, kw, vw, ga, qb, kb, vb, fb, gm = jnp.split(w_in, pts, axis=1)
    cols = []
    for r in range(NSA_Q_PER_GROUP):
        for g in range(NSA_KV_GROUPS):
            h = g * NSA_Q_PER_GROUP + r
            cols.append(qa[:, h * HEAD_DIM:(h + 1) * HEAD_DIM])
    cols += [qb, kb, ks, kw]
    gains = [jnp.tile(nsa_q_gain * scale, NSA_HEADS), jnp.tile(fox_q_gain * scale, FOX_HEADS),
             jnp.tile(fox_k_gain, FOX_HEADS), jnp.tile(nsa_k_gain[1], NSA_KV_GROUPS),
             jnp.tile(nsa_k_gain[2], NSA_KV_GROUPS)]
    n_small_pad = LANES - ga.shape[1] - fb.shape[1]
    cols += [vb, vs, vw, kc, vc, ga, fb, jnp.zeros((D_MODEL, n_small_pad), F32)]
    w_packed = jnp.concatenate(cols, axis=1).astype(BF16)
    gain_row = jnp.concatenate(gains).reshape(1, P_NORM_END)
    small_bias = jnp.concatenate([jnp.zeros((ga.shape[1],), F32), b_forget,
                                  jnp.zeros((n_small_pad,), F32)]).reshape(1, LANES)
    return w_packed, gain_row, small_bias, gm.astype(BF16)


def _in_proj_kernel(x_ref, g_ref, w_ref, gain_ref, sbias_ref, bd64_ref,
                    qa_ref, qb_ref, kb_ref, ksw_ref, vb_ref, vsw_ref, kc_ref, vc_ref, small_ref, kv_scr):
    u = _rms_rows(x_ref[...], g_ref[...]).astype(BF16)
    tm = u.shape[0]

    y = _dot(u, w_ref[:, 0:P_NORM_END])
    n_chunks = P_NORM_END // NORM_CHUNK
    squares = jnp.concatenate([jnp.square(y[:, c * NORM_CHUNK:(c + 1) * NORM_CHUNK]).astype(BF16)
                               for c in range(n_chunks)], axis=0)
    ss = _dot(squares, bd64_ref[...])
    chunk = 0
    for o_ref in (qa_ref, qb_ref, kb_ref, ksw_ref):
        for i in range(o_ref.shape[1] // NORM_CHUNK):
            cols = slice(chunk * NORM_CHUNK, (chunk + 1) * NORM_CHUNK)
            inv = lax.rsqrt(ss[chunk * tm:(chunk + 1) * tm] * (1.0 / HEAD_DIM) + RMS_EPS)
            o_ref[:, i * NORM_CHUNK:(i + 1) * NORM_CHUNK] = (y[:, cols] * inv * gain_ref[:, cols]).astype(BF16)
            chunk += 1
    low = lax.broadcasted_iota(jnp.int32, (u.shape[0], LANES), 1) < HEAD_DIM

    def store_with_ones(v, o_ref):
        for i in range(v.shape[1] // LANES):
            pair = v[:, i * LANES:(i + 1) * LANES]
            o_ref[:, 2 * i * LANES:(2 * i + 1) * LANES] = jnp.where(low, pair, 1.0).astype(BF16)
            o_ref[:, (2 * i + 1) * LANES:(2 * i + 2) * LANES] = jnp.where(low, 1.0, pair).astype(BF16)

    rest = _dot(u, w_ref[:, P_NORM_END:P_END])
    store_with_ones(rest[:, PV_B - P_NORM_END:PV_SW - P_NORM_END], vb_ref)
    store_with_ones(rest[:, PV_SW - P_NORM_END:PKV_C - P_NORM_END], vsw_ref)
    for j, o_ref in enumerate((kc_ref, vc_ref)):
        c0 = PKV_C - P_NORM_END + j * NSA_KV_W
        kv_scr[j] = rest[:, c0:c0 + NSA_KV_W]
        for l in range(CMP_STRIDE):
            o_ref[0, :, l * NSA_KV_W:(l + 1) * NSA_KV_W] = kv_scr[j, pl.ds(l, tm // CMP_STRIDE,
                                                                           stride=CMP_STRIDE), :]
    z = rest[:, P_SMALL - P_NORM_END:] + sbias_ref[...]
    lane = lax.broadcasted_iota(jnp.int32, z.shape, 1)
    log_sig = jnp.minimum(z, 0.0) - jnp.log1p(jnp.exp(-jnp.abs(z)))
    small_ref[...] = jnp.where(lane < 3 * NSA_HEADS, jax.nn.sigmoid(z), log_sig)


def _in_proj(x2d, mix_norm, w_packed, gain_row, small_bias, batch, seq):
    n = x2d.shape[0]
    tm = _token_tile(seq, wide=True)
    steps_per_seq = seq // tm

    def row(width):
        return pl.BlockSpec((tm, width), lambda i: (i, 0))

    def out(width, dtype):
        return row(width), jax.ShapeDtypeStruct((n, width), dtype)

    grouped = (pl.BlockSpec((1, tm // CMP_STRIDE, CMP_STRIDE * NSA_KV_W),
                            lambda i: (i // steps_per_seq, i % steps_per_seq, 0)),
               jax.ShapeDtypeStruct((batch, seq // CMP_STRIDE, CMP_STRIDE * NSA_KV_W), F32))
    outs = [out(NSA_W, BF16), out(FOX_W, BF16), out(FOX_W, BF16), out(2 * NSA_KV_W, BF16),
            out(2 * FOX_W, BF16), out(4 * NSA_KV_W, BF16), grouped, grouped, out(LANES, F32)]
    return pl.pallas_call(
        _in_proj_kernel,
        grid=(n // tm,),
        in_specs=[row(D_MODEL), _const_spec((1, D_MODEL)), _const_spec((D_MODEL, P_END)),
                  _const_spec((1, P_NORM_END)), _const_spec((1, LANES)),
                  _const_spec((NORM_CHUNK, NORM_CHUNK))],
        out_specs=[spec for spec, _ in outs],
        out_shape=[shape for _, shape in outs],
        scratch_shapes=[pltpu.VMEM((2, tm, NSA_KV_W), F32)],
        compiler_params=_params(1),
        name="in_proj",
    )(x2d, mix_norm.reshape(1, D_MODEL), w_packed, gain_row, small_bias, _block_ones(HEAD_DIM))


def _pack_compress(pos, w1, w2):
    half = CMP_BLOCK // 2
    eye = jnp.eye(NSA_KV_GROUPS, dtype=F32)
    w1r = w1.reshape(CMP_BLOCK, HEAD_DIM, CMP_HIDDEN)

    def big(w):
        return jnp.einsum('ldh,pg->lpdgh', w, eye).reshape(half * NSA_KV_W, NSA_KV_GROUPS * CMP_HIDDEN)

    def posrow(p):
        return jnp.broadcast_to(p[:, None, :], (half, NSA_KV_GROUPS, HEAD_DIM)).reshape(1, half * NSA_KV_W)

    w2big = jnp.einsum('hd,pg->phgd', w2, eye).reshape(NSA_KV_GROUPS * CMP_HIDDEN, NSA_KV_W)
    return (posrow(pos[:half]), posrow(pos[half:]), big(w1r[:half]).astype(BF16),
            big(w1r[half:]).astype(BF16), w2big.astype(BF16))


def _compress_kernel(kc_ref, vc_ref, kp_lo, kp_hi, kw_lo, kw_hi, kw2, vp_lo, vp_hi, vw_lo, vw_hi, vw2,
                     kgain_ref, bd64_ref, ko_ref, vo_ref):
    n_seq, rows, width = kc_ref.shape

    def mlp(x_ref, p_lo, p_hi, w_lo, w_hi, w2):
        r = x_ref[...].reshape(n_seq * rows, width)
        first = _dot((r + p_lo[...]).astype(BF16), w_lo[...])
        second = _dot((r + p_hi[...]).astype(BF16), w_hi[...])
        h = first + pltpu.roll(second, second.shape[0] - 1, axis=0)
        return _dot((h * jax.nn.sigmoid(h)).astype(BF16), w2[...])

    k = mlp(kc_ref, kp_lo, kp_hi, kw_lo, kw_hi, kw2)
    ss = _split_dot(k * k, bd64_ref[...])
    k = (k * lax.rsqrt(ss * (1.0 / HEAD_DIM) + RMS_EPS) * kgain_ref[...]).astype(BF16)
    ko_ref[...] = k.reshape(ko_ref.shape)
    vo_ref[...] = mlp(vc_ref, vp_lo, vp_hi, vw_lo, vw_hi, vw2).astype(BF16).reshape(vo_ref.shape)


def _compress(kc, vc, batch, seq, k_pack, v_pack, k_gain):
    rows = seq // (CMP_BLOCK // 2)
    width = (CMP_BLOCK // 2) * NSA_KV_W
    per_step = next(n for n in (4, 2, 1) if batch % n == 0)
    blk = pl.BlockSpec((per_step, rows, width), lambda b: (b, 0, 0))
    out = pl.BlockSpec((per_step, rows, NSA_KV_W), lambda b: (b, 0, 0))
    pack_specs = [_const_spec((1, width)), _const_spec((1, width)),
                  _const_spec((width, NSA_KV_GROUPS * CMP_HIDDEN)),
                  _const_spec((width, NSA_KV_GROUPS * CMP_HIDDEN)),
                  _const_spec((NSA_KV_GROUPS * CMP_HIDDEN, NSA_KV_W))]
    bd = _block_ones(HEAD_DIM, NSA_KV_W)
    return pl.pallas_call(
        _compress_kernel,
        grid=(batch // per_step,),
        in_specs=[blk, blk] + pack_specs + pack_specs + [_const_spec((1, NSA_KV_W)),
                                                         _const_spec((NSA_KV_W, NSA_KV_W))],
        out_specs=[out, out],
        out_shape=[jax.ShapeDtypeStruct((batch, rows, NSA_KV_W), BF16)] * 2,
        compiler_params=_params(1),
        name="compress",
    )(kc.reshape(batch, rows, width), vc.reshape(batch, rows, width), *k_pack, *v_pack,
      jnp.tile(k_gain, NSA_KV_GROUPS).reshape(1, NSA_KV_W), bd)


FORGET_LANE = 3 * NSA_HEADS


def _cumsum_kernel(x_ref, cols_ref, rows_ref):
    n_seq, _, seq = rows_ref.shape
    x = x_ref[...]
    row = lax.broadcasted_iota(jnp.int32, x.shape, 0) % seq
    k = 1
    while k < seq:
        x = x + jnp.where(row >= k, pltpu.roll(x, k, axis=0), 0.0)
        k *= 2
    cols_ref[...] = x
    x_t = x.T
    for j in range(n_seq):
        rows_ref[j] = x_t[FORGET_LANE:FORGET_LANE + FOX_HEADS, j * seq:(j + 1) * seq]


def _cumsum(small, batch, seq):
    per_step = next(n for n in (4, 2, 1) if batch % n == 0)
    spec = pl.BlockSpec((per_step * seq, LANES), lambda b: (b, 0))
    return pl.pallas_call(
        _cumsum_kernel, grid=(batch // per_step,), in_specs=[spec],
        out_specs=[spec, pl.BlockSpec((per_step, FOX_HEADS, seq), lambda b: (b, 0, 0))],
        out_shape=[jax.ShapeDtypeStruct((batch * seq, LANES), F32),
                   jax.ShapeDtypeStruct((batch, FOX_HEADS, seq), F32)],
        compiler_params=_params(1), name="cumsum",
    )(small)


def _write_bias_rows(d, valid, tab_ref, o_ref, rows):
    max_exact = N_BUCKETS // 2
    n = jnp.maximum(d, 0)
    nf = jnp.maximum(n, 1).astype(F32)
    large = max_exact + (jnp.log(nf / max_exact) / math.log(MAX_DISTANCE / max_exact)
                         * (N_BUCKETS - max_exact)).astype(jnp.int32)
    bucket = jnp.where(n < max_exact, n, jnp.minimum(large, N_BUCKETS - 1))
    for h in range(NSA_HEADS):
        acc = jnp.zeros(d.shape, F32)
        for b in range(N_BUCKETS):
            acc = jnp.where(bucket == b, tab_ref[b, h], acc)
        g, r = divmod(h, NSA_Q_PER_GROUP)
        o_ref[0, g, r, rows, :] = jnp.where(valid, acc * LOG2E, NEG_INF)


def _write_far_tile(valid, tab_ref, o_ref):
    for h in range(NSA_HEADS):
        g, r = divmod(h, NSA_Q_PER_GROUP)
        far = jnp.where(valid, tab_ref[N_BUCKETS - 1, h], 0.0) * LOG2E
        o_ref[0, g, r] = jnp.where(valid, far, NEG_INF)


def _toeplitz_bias_kernel(tab_ref, o_ref):
    s = pl.program_id(0)
    tile = ATT_TILE

    def distances(n_rows):
        i = lax.broadcasted_iota(jnp.int32, (n_rows, tile), 0)
        j = lax.broadcasted_iota(jnp.int32, (n_rows, tile), 1)
        d = jnp.minimum(s, N_WIN_TILES - 1) * tile + i - j
        d_hi = jnp.where(s < N_WIN_TILES, WINDOW, jnp.int32(1 << 30))
        return d, (d >= 0) & (d < d_hi)

    _write_far_tile(distances(tile)[1], tab_ref, o_ref)
    for step in range(N_WIN_TILES + 1):
        near = min(tile, max(0, MAX_DISTANCE + tile - 1 - min(step, N_WIN_TILES - 1) * tile))
        near = -(-near // 8) * 8
        if near:
            @pl.when(s == step)
            def _(near=near):
                d, valid = distances(near)
                _write_bias_rows(d, valid, tab_ref, o_ref, slice(0, near))


def _cmp_bias_kernel(tab_ref, o_ref):
    t = pl.program_id(0)
    chunk = 8

    def distances(c0, n_rows):
        c = c0 + lax.broadcasted_iota(jnp.int32, (n_rows, ATT_TILE), 0)
        i = lax.broadcasted_iota(jnp.int32, (n_rows, ATT_TILE), 1)
        d = t * ATT_TILE + i - CMP_STRIDE * c - (CMP_BLOCK - 1)
        return d, (d >= 0) & (c < LANES - 1)

    _write_far_tile(distances(0, LANES)[1], tab_ref, o_ref)
    first = (t * ATT_TILE - (MAX_DISTANCE + CMP_BLOCK - 1)) // (chunk * CMP_STRIDE)
    n_chunks = (ATT_TILE + MAX_DISTANCE) // (chunk * CMP_STRIDE) + 2

    def refine(m, carry):
        c0 = pl.multiple_of(jnp.clip(first + m, 0, LANES // chunk - 1) * chunk, chunk)
        d, valid = distances(c0, chunk)
        _write_bias_rows(d, valid, tab_ref, o_ref, pl.ds(c0, chunk))
        return carry

    lax.fori_loop(0, n_chunks, refine, 0)


def _bias_tables(rel_bias_table, n_q_tiles):
    def call(body, steps, rows, name):
        shape = (steps, NSA_KV_GROUPS, NSA_Q_PER_GROUP, rows, ATT_TILE)
        return pl.pallas_call(
            body, grid=(steps,),
            in_specs=[pl.BlockSpec(memory_space=pltpu.SMEM)],
            out_specs=pl.BlockSpec((1,) + shape[1:], lambda s: (s, 0, 0, 0, 0)),
            out_shape=jax.ShapeDtypeStruct(shape, F32),
            compiler_params=_params(1), name=name,
        )(rel_bias_table)

    return (call(_toeplitz_bias_kernel, N_WIN_TILES + 1, ATT_TILE, "toeplitz_bias"),
            call(_cmp_bias_kernel, n_q_tiles, LANES, "cmp_bias"))


def _score_bound(q_gain, k_gain):
    return 1.02 * LOG2E * math.sqrt(HEAD_DIM) * jnp.max(jnp.abs(q_gain)) * jnp.max(jnp.abs(k_gain))


def _attention(chains, groups, bounds, acc_ref):
    n = len(chains)

    def later(k):
        return tuple(i for _, ids in groups[k:] for i in ids)

    def exact_max():
        def half_max(i, dl):
            s = chains[i][0](dl)
            return jnp.maximum(s[:, :LANES], s[:, LANES:])

        mx = [half_max(i, 0) for i in range(n)]
        lo = 1
        for k, (last, _) in enumerate(groups):
            members = later(k)

            def body(dl, carry, members=members):
                return tuple(jnp.maximum(c, half_max(i, dl)) for c, i in zip(carry, members))

            for i, r in zip(members, lax.fori_loop(lo, last + 1, body, tuple(mx[i] for i in members))):
                mx[i] = r
            lo = last + 1
        return tuple(jnp.broadcast_to(jnp.max(m, axis=-1, keepdims=True), (m.shape[0], ATT_TILE))
                     for m in mx)

    shift = exact_max() if bounds is None else [b() for b in bounds]

    def weights(i, dl):
        return jnp.exp2(chains[i][0](dl) - shift[i]).astype(BF16)

    def product(i, dl):
        return chains[i][1](weights(i, dl), dl)

    for i in range(n):
        acc_ref[i] = product(i, 0)
    lo = 1
    for k, (last, _) in enumerate(groups):
        members = later(k)
        count = last + 1 - lo

        def body(j, carry, members=members, lo=lo):
            dl = lo + 2 * j
            for i in members:
                both = jnp.concatenate([weights(i, dl + 1), weights(i, dl)], axis=1)
                acc_ref[i] += chains[i][1](both, dl, 2)
            return carry

        lax.fori_loop(0, count // 2, body, 0)

        @pl.when(count % 2 == 1)
        def _(members=members, last=last):
            for i in members:
                acc_ref[i] += product(i, last)

        lo = last + 1
    return [acc_ref[i] for i in range(n)]


def _pair_lanes(first, second, normalise):
    low = lax.broadcasted_iota(jnp.int32, first.shape, 1) < HEAD_DIM
    pair = jnp.where(low, first, second)
    if not normalise:
        return pair
    sums = pltpu.roll(jnp.where(low, second, first), HEAD_DIM, axis=1)
    return pair * (1.0 / sums)


def _select_blocks(imp_t, t0, tq):
    blk = lax.broadcasted_iota(jnp.int32, imp_t.shape, 0)
    cur = (t0 + (lax.broadcasted_iota(jnp.int32, imp_t.shape, 1) & (tq - 1))) // SLC_BLOCK
    forced = (blk == 0) | (blk == cur) | (blk == cur - 1)
    score = jnp.where(blk <= cur, imp_t + jnp.where(forced, FORCE_BONUS, 0.0), NEG_INF)
    blk_f = blk.astype(F32)
    dead = -3.0e38
    picked = jnp.zeros(imp_t.shape, F32)
    for _ in range(SLC_TOPK):
        best = jnp.max(score, axis=0, keepdims=True)
        first = jnp.min(jnp.where(score == best, blk_f, float(LANES)), axis=0, keepdims=True)
        hit = blk_f == first
        picked = jnp.where(hit, 1.0, picked)
        score = jnp.where(hit, dead, score)
    return picked


def _nsa_kernel(bnd_ref, qa_ref, kcmp_ref, vcmp_ref, ksw_ref, vsw_ref, small_ref, tb_ref, bc_ref, ov_ref,
                e_ref, gx_ref, o_ref, amask_ref, acc_ref, *, n_blocks, exact):
    qt = pl.program_id(1)
    tq = ATT_TILE
    rq = NSA_Q_PER_GROUP
    lane = lax.broadcasted_iota(jnp.int32, (tq, LANES), 1)
    own_lanes = [jnp.where((lane // HEAD_DIM) == g, 1.0, 0.0).astype(BF16) for g in range(NSA_KV_GROUPS)]

    def key_rows(dl, n=1):
        return pl.ds(pl.multiple_of((qt - dl - (n - 1)) * tq, tq), n * tq)

    groups = range(NSA_KV_GROUPS)
    n_seq = qa_ref.shape[0]
    n_grp = NSA_KV_GROUPS

    def win_chain(b, g, q):
        def score(dl):
            s = _dot_nt(q, ksw_ref[b, key_rows(dl), LANES:2 * LANES]).reshape(rq, tq, tq)
            return (s + tb_ref[dl, g]).reshape(rq * tq, tq)

        def pv(p, dl, n=1):
            c0 = (n_grp + g) * LANES
            return _dot(p, vsw_ref[b, key_rows(dl, n), c0:c0 + LANES])

        return score, pv

    def slc_chain(b, g, q):
        def score(dl):
            tile = dl if isinstance(dl, int) else jnp.where(dl < N_WIN_TILES - 1, dl, N_WIN_TILES)
            rows = key_rows(dl)
            s = _dot_nt(q, ksw_ref[b, rows, 0:LANES]).reshape(rq, tq, tq)
            return (s + tb_ref[tile, g] + amask_ref[b * n_grp + g, :, rows][None]).reshape(rq * tq, tq)

        def pv(p, dl, n=1):
            return _dot(p, vsw_ref[b, key_rows(dl, n), g * LANES:(g + 1) * LANES])

        return score, pv

    overlap = ov_ref[...].astype(BF16)
    key_block = e_ref[...].astype(BF16)
    gate_expand = [gx_ref[br].astype(BF16) for br in range(3)]

    slc_chains, win_chains, o_cmps = [], [], []
    for b in range(n_seq):
        qs = [jnp.concatenate([qa_ref[b, :, r * LANES:(r + 1) * LANES] * own_lanes[g] for r in range(rq)],
                              axis=0) for g in groups]
        slc_chains += [slc_chain(b, g, qs[g]) for g in groups]
        win_chains += [win_chain(b, g, qs[g]) for g in groups]

        s = (_dot_nt(kcmp_ref[b], jnp.concatenate(qs, axis=0))
             + jnp.concatenate([bc_ref[0, g, r] for g in groups for r in range(rq)], axis=1))
        if exact:
            m = jnp.max(s, axis=0, keepdims=True)
            p = jnp.where(s > 0.5 * NEG_INF, jnp.exp2(s - m), 0.0)
        else:
            p = jnp.exp2(s - bnd_ref[2])
        l = jnp.sum(p, axis=0, keepdims=True)
        p_c = p * (1.0 / jnp.where(l > 0.0, l, 1.0))
        o_cmp = _dot(p_c.T.astype(BF16), vcmp_ref[b])
        o_cmps.append([o_cmp[g * rq * tq:(g + 1) * rq * tq] for g in groups])

        p_sum = jnp.concatenate([sum(p_c[:, (g * rq + r) * tq:(g * rq + r + 1) * tq] for r in range(rq))
                                 for g in groups], axis=1)
        hi = p_sum.astype(BF16)
        lo = (p_sum - hi.astype(F32)).astype(BF16)
        imp_t = _dot(overlap, hi) + _dot(overlap, lo)
        sel_t = _select_blocks(imp_t[0:n_blocks], qt * tq, tq)
        block_bias = jnp.concatenate([jnp.where(sel_t > 0.0, 0.0, NEG_INF),
                                      jnp.zeros((LANES - n_blocks, n_grp * tq), F32)], axis=0)
        amask_ref[b * n_grp:(b + 1) * n_grp] = _dot(block_bias.T.astype(BF16), key_block).reshape(
            n_grp, tq, e_ref.shape[1])

    n_slc = len(slc_chains)
    slc_ids = tuple(range(n_slc))
    win_ids = tuple(range(n_slc, 2 * n_slc))
    bounds = None if exact else [lambda i=i: jnp.full((rq * tq, tq), bnd_ref[i // n_slc], F32)
                                 for i in range(2 * n_slc)]
    accs = _attention(slc_chains + win_chains,
                      [(jnp.minimum(qt, N_WIN_TILES - 1), win_ids), (qt, slc_ids)], bounds, acc_ref)

    for b in range(n_seq):
        gates = small_ref[b]
        g_hi = gates.astype(BF16)
        g_lo = (gates - g_hi.astype(F32)).astype(BF16)
        out = [jnp.zeros((tq, LANES), F32) for _ in range(rq)]
        lo_, hi_ = b * n_grp, (b + 1) * n_grp
        for br, per_group in enumerate((o_cmps[b], accs[lo_:hi_], accs[n_slc + lo_:n_slc + hi_])):
            gate = _dot(g_hi, gate_expand[br]) + _dot(g_lo, gate_expand[br])
            for r in range(rq):
                first, second = (a[r * tq:(r + 1) * tq] for a in per_group)
                pair = _pair_lanes(first, second, normalise=br > 0)
                out[r] = out[r] + gate[:, r * LANES:(r + 1) * LANES] * pair
        for r in range(rq):
            o_ref[b, :, r * LANES:(r + 1) * LANES] = out[r].astype(BF16)


def _nsa(bounds, qa, kcmp, vcmp, ksw, vsw, small, toeplitz, cmp_bias, batch, seq, exact):
    tq = ATT_TILE
    nq = seq // tq
    rq = NSA_Q_PER_GROUP
    n_blocks = seq // SLC_BLOCK
    n_cmp = (seq - CMP_BLOCK) // CMP_STRIDE + 1
    ci = np.arange(LANES)[:, None] * CMP_STRIDE
    sj = np.arange(LANES)[None, :] * SLC_BLOCK
    overlap = ((ci <= sj + SLC_BLOCK - 1) & (ci + CMP_BLOCK - 1 >= sj)
               & (np.arange(LANES)[:, None] < n_cmp) & (np.arange(LANES)[None, :] < n_blocks))
    expand = np.arange(LANES)[:, None] == (np.arange(seq)[None, :] // SLC_BLOCK)
    gate_expand = np.zeros((3, LANES, NSA_W), np.float32)
    for b in range(3):
        for g in range(NSA_KV_GROUPS):
            for r in range(rq):
                c0 = r * LANES + g * HEAD_DIM
                gate_expand[b, b * NSA_HEADS + g * rq + r, c0:c0 + HEAD_DIM] = 1.0
    per_step = 2 if batch % 2 == 0 else 1

    def rows(n_rows, width):
        return pl.BlockSpec((per_step, n_rows, width), lambda b, t: (b, 0, 0))

    def tile(width):
        return pl.BlockSpec((per_step, tq, width), lambda b, t: (b, t, 0))

    out = pl.pallas_call(
        functools.partial(_nsa_kernel, n_blocks=n_blocks, exact=exact),
        grid=(batch // per_step, nq),
        in_specs=[
            pl.BlockSpec(memory_space=pltpu.SMEM),
            tile(NSA_W),
            rows(LANES, NSA_KV_W),
            rows(LANES, NSA_KV_W),
            rows(seq, 2 * NSA_KV_W),
            rows(seq, 2 * NSA_KV_GROUPS * LANES),
            tile(LANES),
            _const_spec((N_WIN_TILES + 1, NSA_KV_GROUPS, rq, tq, tq)),
            pl.BlockSpec((1, NSA_KV_GROUPS, rq, LANES, tq), lambda b, t: (t, 0, 0, 0, 0)),
            _const_spec((LANES, LANES)),
            _const_spec((LANES, seq)),
            _const_spec((3, LANES, NSA_W)),
        ],
        out_specs=tile(NSA_W),
        out_shape=jax.ShapeDtypeStruct((batch, seq, NSA_W), BF16),
        scratch_shapes=[pltpu.VMEM((per_step * NSA_KV_GROUPS, tq, seq), F32),
                        pltpu.VMEM((per_step * 2 * NSA_KV_GROUPS, rq * tq, LANES), F32)],
        compiler_params=_params(2),
        name="nsa",
    )(bounds, qa.reshape(batch, seq, NSA_W), kcmp, vcmp, ksw.reshape(batch, seq, 2 * NSA_KV_W),
      vsw.reshape(batch, seq, 2 * NSA_KV_GROUPS * LANES), small.reshape(batch, seq, LANES),
      toeplitz, cmp_bias,
      jnp.asarray(overlap.T, F32), jnp.asarray(expand, F32), jnp.asarray(gate_expand, F32))
    return out.reshape(batch * seq, NSA_W)


def _fox_kernel(bnd_ref, q_ref, k_ref, v_ref, cum_ref, cum_t_ref, o_ref, acc_ref, *, exact):
    qt = pl.program_id(1)
    tq = ATT_TILE
    pairs = FOX_HEADS // 2
    row = lax.broadcasted_iota(jnp.int32, (2, tq, tq), 1)
    col = lax.broadcasted_iota(jnp.int32, (2, tq, tq), 2)
    lane = lax.broadcasted_iota(jnp.int32, (tq, LANES), 1)
    low = jnp.where(lane < HEAD_DIM, 1.0, 0.0).astype(BF16)
    high = jnp.where(lane < HEAD_DIM, 0.0, 1.0).astype(BF16)
    n_seq = q_ref.shape[0]

    def key_rows(dl, n=1):
        return pl.ds(pl.multiple_of((qt - dl - (n - 1)) * tq, tq), n * tq)

    bounds = []

    def pair_chain(b, p):
        pair = q_ref[b, :, p * LANES:(p + 1) * LANES]
        q = jnp.concatenate([pair * low, pair * high], axis=0)
        base = cum_ref[b, 2 * p:2 * p + 2, pl.ds(pl.multiple_of(qt * tq, tq), LANES)][:, 0:1]

        def decay_of(rows):
            return (base - cum_ref[b, 2 * p:2 * p + 2, rows]) * LOG2E

        def bound():
            cum_rows = cum_t_ref[b]
            c0 = FORGET_LANE + 2 * p
            own = jnp.concatenate([base[h:h + 1, :] - cum_rows[:, c0 + h:c0 + h + 1] for h in range(2)],
                                  axis=0)
            return jnp.broadcast_to(own * LOG2E + bnd_ref[0], (2 * tq, tq))

        bounds.append(bound)

        def score(dl):
            rows = key_rows(dl)
            decay = decay_of(rows)
            s = _dot_nt(q, k_ref[b, rows, p * LANES:(p + 1) * LANES]).reshape(2, tq, tq) + decay[:, None, :]
            if isinstance(dl, int):
                s = jnp.where(col <= row, s, NEG_INF)
            return s.reshape(2 * tq, tq)

        def pv(w, dl, n=1):
            rows = key_rows(dl, n)
            return jnp.concatenate(
                [_dot(w[0:tq], v_ref[b, rows, 2 * p * LANES:(2 * p + 1) * LANES]),
                 _dot(w[tq:2 * tq], v_ref[b, rows, (2 * p + 1) * LANES:(2 * p + 2) * LANES])], axis=0)

        return score, pv

    chains = [pair_chain(b, p) for b in range(n_seq) for p in range(pairs)]
    accs = _attention(chains, [(qt, tuple(range(len(chains))))], None if exact else bounds, acc_ref)
    for i, acc in enumerate(accs):
        b, p = divmod(i, pairs)
        o_ref[b, :, p * LANES:(p + 1) * LANES] = _pair_lanes(acc[0:tq], acc[tq:2 * tq], True).astype(BF16)


def _fox(bounds, qb, kb, vb, cum, cum_cols, batch, seq, exact):
    tq = ATT_TILE
    nq = seq // tq
    per_step = 2 if batch % 2 == 0 else 1
    out = pl.pallas_call(
        functools.partial(_fox_kernel, exact=exact),
        grid=(batch // per_step, nq),
        in_specs=[
            pl.BlockSpec(memory_space=pltpu.SMEM),
            pl.BlockSpec((per_step, tq, FOX_W), lambda b, t: (b, t, 0)),
            pl.BlockSpec((per_step, seq, FOX_W), lambda b, t: (b, 0, 0)),
            pl.BlockSpec((per_step, seq, FOX_HEADS * LANES), lambda b, t: (b, 0, 0)),
            pl.BlockSpec((per_step, FOX_HEADS, seq), lambda b, t: (b, 0, 0)),
            pl.BlockSpec((per_step, tq, LANES), lambda b, t: (b, t, 0)),
        ],
        out_specs=pl.BlockSpec((per_step, tq, FOX_W), lambda b, t: (b, t, 0)),
        out_shape=jax.ShapeDtypeStruct((batch, seq, FOX_W), BF16),
        scratch_shapes=[pltpu.VMEM((per_step * FOX_HEADS // 2, 2 * tq, LANES), F32)],
        compiler_params=_params(2),
        name="fox",
    )(bounds, qb.reshape(batch, seq, FOX_W), kb.reshape(batch, seq, FOX_W),
      vb.reshape(batch, seq, FOX_HEADS * LANES), cum, cum_cols.reshape(batch, seq, LANES))
    return out.reshape(batch * seq, FOX_W)


def _merge_ffn_kernel(x_ref, on_ref, of_ref, gmix_ref, wgm_ref, won_ref, wof_ref, wout_ref,
                      g2_ref, wup_ref, wdn_ref, o_ref):
    x = x_ref[...]
    u = _rms_rows(x, gmix_ref[...]).astype(BF16)
    gate = jax.nn.sigmoid(_dot(u, wgm_ref[...]))
    merged = (gate[:, :D_MODEL] * _dot(on_ref[...], won_ref[...])
              + gate[:, D_MODEL:] * _dot(of_ref[...], wof_ref[...]))
    x2 = x + _dot(merged.astype(BF16), wout_ref[...])
    o_ref[...] = _swiglu_residual(x2, g2_ref[...], wup_ref, wdn_ref)


def _merge_ffn(x2d, o_nsa, o_fox, mix_norm, w_gm, w_o_nsa, w_o_fox, w_out, gain2, w_up, w_down):
    n = x2d.shape[0]
    tm = _token_tile(n, wide=False)

    def row(width):
        return pl.BlockSpec((tm, width), lambda i: (i, 0))

    return pl.pallas_call(
        _merge_ffn_kernel,
        grid=(n // tm,),
        in_specs=[row(D_MODEL), row(NSA_W), row(FOX_W), _const_spec((1, D_MODEL)),
                  _const_spec((D_MODEL, 2 * D_MODEL)), _const_spec((NSA_W, D_MODEL)),
                  _const_spec((FOX_W, D_MODEL)), _const_spec((D_MODEL, D_MODEL)),
                  _const_spec((1, D_MODEL)), _const_spec((D_MODEL, 2 * D_FF)),
                  _const_spec((D_FF, D_MODEL))],
        out_specs=row(D_MODEL),
        out_shape=jax.ShapeDtypeStruct((n, D_MODEL), F32),
        compiler_params=_params(1),
        name="merge_ffn",
    )(x2d, o_nsa, o_fox, mix_norm.reshape(1, D_MODEL), w_gm, w_o_nsa.astype(BF16),
      w_o_fox.astype(BF16), w_out.astype(BF16), gain2.reshape(1, D_MODEL), w_up.astype(BF16),
      w_down.astype(BF16))


def _layer(x, ffn1_norm, ffn1_w_up, ffn1_w_down, mix_norm, w_in, b_forget, nsa_q_gain, nsa_k_gain,
           fox_q_gain, fox_k_gain, cmp_pos_k, cmp_pos_v, cmp_k_w1, cmp_k_w2, cmp_v_w1, cmp_v_w2,
           w_o_nsa, w_o_fox, w_out, ffn2_norm, ffn2_w_up, ffn2_w_down, rel_bias_table):
    batch, seq, d = x.shape
    assert d == D_MODEL and seq % ATT_TILE == 0 and (batch * seq) % TOKEN_TILE == 0
    assert seq // SLC_BLOCK <= LANES and (seq - CMP_BLOCK) // CMP_STRIDE + 1 == LANES - 1
    assert seq // (CMP_BLOCK // 2) == LANES
    x2d = x.reshape(batch * seq, D_MODEL)

    x1 = _ffn(x2d, ffn1_norm, ffn1_w_up, ffn1_w_down)

    w_packed, gain_row, small_bias, w_gm = _pack_in_proj(w_in, b_forget, nsa_q_gain, nsa_k_gain,
                                                         fox_q_gain, fox_k_gain)
    qa, qb, kb, ksw, vb, vsw, kc, vc, small = _in_proj(x1, mix_norm, w_packed, gain_row, small_bias,
                                                       batch, seq)

    kcmp, vcmp = _compress(kc, vc, batch, seq, _pack_compress(cmp_pos_k, cmp_k_w1, cmp_k_w2),
                           _pack_compress(cmp_pos_v, cmp_v_w1, cmp_v_w2), nsa_k_gain[0])
    b_slc = _score_bound(nsa_q_gain, nsa_k_gain[1])
    b_win = _score_bound(nsa_q_gain, nsa_k_gain[2])
    t_hi, t_lo = jnp.max(rel_bias_table) * LOG2E, jnp.min(rel_bias_table) * LOG2E
    b_cmp = _score_bound(nsa_q_gain, nsa_k_gain[0])
    nsa_ok = 2.0 * jnp.maximum(jnp.maximum(b_slc, b_win), b_cmp) + (t_hi - t_lo) < EXP2_RANGE
    nsa_bounds = jnp.stack([b_slc + t_hi, b_win + t_hi, b_cmp + t_hi])
    def nsa_branch(exact):
        toeplitz, cmp_bias = _bias_tables(rel_bias_table, seq // ATT_TILE)
        return _nsa(nsa_bounds, qa, kcmp, vcmp, ksw, vsw, small, toeplitz, cmp_bias, batch, seq, exact)

    o_nsa = lax.cond(nsa_ok, lambda: nsa_branch(False), lambda: nsa_branch(True))

    cum_cols, cum = _cumsum(small, batch, seq)
    b_fox = _score_bound(fox_q_gain, fox_k_gain)
    fox_bounds = jnp.stack([b_fox])
    o_fox = lax.cond(2.0 * b_fox < EXP2_RANGE,
                     lambda: _fox(fox_bounds, qb, kb, vb, cum, cum_cols, batch, seq, False),
                     lambda: _fox(fox_bounds, qb, kb, vb, cum, cum_cols, batch, seq, True))

    w_o_nsa_p = w_o_nsa.reshape(NSA_KV_GROUPS, NSA_Q_PER_GROUP, HEAD_DIM, D_MODEL).transpose(1, 0, 2, 3)
    out = _merge_ffn(x1, o_nsa, o_fox, mix_norm, w_gm, w_o_nsa_p.reshape(NSA_W, D_MODEL), w_o_fox,
                     w_out, ffn2_norm, ffn2_w_up, ffn2_w_down)
    return out.reshape(batch, seq, D_MODEL)


def kernel(x, ffn1_norm, ffn1_w_up, ffn1_w_down, mix_norm, w_in, b_forget, nsa_q_gain, nsa_k_gain,
           fox_q_gain, fox_k_gain, cmp_pos_k, cmp_pos_v, cmp_k_w1, cmp_k_w2, cmp_v_w1, cmp_v_w2,
           w_o_nsa, w_o_fox, w_out, ffn2_norm, ffn2_w_up, ffn2_w_down, rel_bias_table):
    for layer in range(ffn1_norm.shape[0]):
        x = _layer(x, ffn1_norm[layer], ffn1_w_up[layer], ffn1_w_down[layer], mix_norm[layer],
                   w_in[layer], b_forget[layer], nsa_q_gain[layer], nsa_k_gain[layer],
                   fox_q_gain[layer], fox_k_gain[layer], cmp_pos_k[layer], cmp_pos_v[layer],
                   cmp_k_w1[layer], cmp_k_w2[layer], cmp_v_w1[layer], cmp_v_w2[layer],
                   w_o_nsa[layer], w_o_fox[layer], w_out[layer], ffn2_norm[layer],
                   ffn2_w_up[layer], ffn2_w_down[layer], rel_bias_table)
    return x
```

```python
import functools
import math

import numpy as np
import jax
import jax.numpy as jnp
from jax import lax
from jax.experimental import pallas as pl
from jax.experimental.pallas import tpu as pltpu

F32 = jnp.float32
BF16 = jnp.bfloat16

D_MODEL = 1024
HEAD_DIM = 64
NSA_HEADS = 8
NSA_KV_GROUPS = 2
NSA_Q_PER_GROUP = NSA_HEADS // NSA_KV_GROUPS
CMP_BLOCK = 32
CMP_STRIDE = 16
CMP_HIDDEN = 128
SLC_BLOCK = 64
SLC_TOPK = 8
WINDOW = 512
FOX_HEADS = 8
D_FF = 2816
N_BUCKETS = 32
MAX_DISTANCE = 128
RMS_EPS = 1e-6
NEG_INF = -1.0e30
FORCE_BONUS = 1.0e4

NSA_W = NSA_HEADS * HEAD_DIM
NSA_KV_W = NSA_KV_GROUPS * HEAD_DIM
FOX_W = FOX_HEADS * HEAD_DIM
IN_SPLITS = (NSA_W, NSA_KV_W, NSA_KV_W, NSA_KV_W, NSA_KV_W, NSA_KV_W, NSA_KV_W, 3 * NSA_HEADS,
             FOX_W, FOX_W, FOX_W, FOX_HEADS, 2 * D_MODEL)

LANES = 128
TOKEN_TILE = 512
MXU_DIM = 256
FFN_CHUNKS = (-(-D_FF // (2 * MXU_DIM)) * MXU_DIM, D_FF - -(-D_FF // (2 * MXU_DIM)) * MXU_DIM)
ATT_TILE = 256
N_WIN_TILES = WINDOW // ATT_TILE + 1
VMEM_LIMIT = 56 * 1024 * 1024
LOG2E = 1.4426950408889634
EXP2_RANGE = 100.0

PQ_A = 0
PQ_B = PQ_A + NSA_W
PK_B = PQ_B + FOX_W
PK_SW = PK_B + FOX_W
P_NORM_END = PK_SW + 2 * NSA_KV_W
PV_B = P_NORM_END
PV_SW = PV_B + FOX_W
PKV_C = PV_SW + 2 * NSA_KV_W
P_SMALL = PKV_C + 2 * NSA_KV_W
P_END = P_SMALL + LANES
NORM_CHUNK = 256


def _dot(a, b):
    return jnp.dot(a, b, preferred_element_type=F32)


def _dot_nt(a, b):
    return lax.dot_general(a, b, (((1,), (1,)), ((), ())), preferred_element_type=F32)


def _split_dot(x, w):
    hi = x.astype(BF16)
    lo = (x - hi.astype(F32)).astype(BF16)
    return _dot(hi, w) + _dot(lo, w)


def _rms_rows(x, gain_row):
    ms = jnp.mean(x * x, axis=-1, keepdims=True)
    return x * lax.rsqrt(ms + RMS_EPS) * gain_row


def _const_spec(shape):
    nd = len(shape)
    return pl.BlockSpec(shape, lambda *_: (0,) * nd, pipeline_mode=pl.Buffered(1))


def _token_tile(n_rows, wide):
    tile = 2 * TOKEN_TILE if wide and n_rows % (2 * TOKEN_TILE) == 0 else TOKEN_TILE
    assert n_rows % tile == 0
    return tile


def _params(n_axes):
    return pltpu.CompilerParams(dimension_semantics=("arbitrary",) * n_axes,
                                vmem_limit_bytes=VMEM_LIMIT)


def _swiglu_residual(x, gain_row, wup_ref, wdn_ref):
    xn = _rms_rows(x, gain_row).astype(BF16)
    acc = jnp.zeros(x.shape, F32)
    lo = 0
    for width in FFN_CHUNKS:
        gate = _dot(xn, wup_ref[:, lo:lo + width])
        up = _dot(xn, wup_ref[:, D_FF + lo:D_FF + lo + width])
        h = (gate * jax.nn.sigmoid(gate) * up).astype(BF16)
        acc = acc + _dot(h, wdn_ref[lo:lo + width, :])
        lo += width
    return x + 0.5 * acc


def _ffn_kernel(x_ref, g_ref, wup_ref, wdn_ref, o_ref):
    o_ref[...] = _swiglu_residual(x_ref[...], g_ref[...], wup_ref, wdn_ref)


def _ffn(x2d, gain, w_up, w_down):
    n = x2d.shape[0]
    tm = _token_tile(n, wide=True)
    row = pl.BlockSpec((tm, D_MODEL), lambda i: (i, 0))
    return pl.pallas_call(
        _ffn_kernel,
        grid=(n // tm,),
        in_specs=[row, _const_spec((1, D_MODEL)), _const_spec((D_MODEL, 2 * D_FF)),
                  _const_spec((D_FF, D_MODEL))],
        out_specs=row,
        out_shape=jax.ShapeDtypeStruct((n, D_MODEL), F32),
        compiler_params=_params(1),
        name="ffn",
    )(x2d, gain.reshape(1, D_MODEL), w_up.astype(BF16), w_down.astype(BF16))


def _block_ones(width, size=NORM_CHUNK):
    idx = np.arange(size) // width
    return jnp.asarray((idx[:, None] == idx[None, :]).astype(np.float32), BF16)


def _pack_in_proj(w_in, b_forget, nsa_q_gain, nsa_k_gain, fox_q_gain, fox_k_gain):
    scale = LOG2E / math.sqrt(HEAD_DIM)
    pts = np.cumsum(np.array(IN_SPLITS))[:-1].tolist()
    qa, kc, vc, ks, vs, kw, vw, ga, qb, kb, vb, fb, gm = jnp.split(w_in, pts, axis=1)
    cols = []
    for r in range(NSA_Q_PER_GROUP):
        for g in range(NSA_KV_GROUPS):
            h = g * NSA_Q_PER_GROUP + r
            cols.append(qa[:, h * HEAD_DIM:(h + 1) * HEAD_DIM])
    cols += [qb, kb, ks, kw]
    gains = [jnp.tile(nsa_q_gain * scale, NSA_HEADS), jnp.tile(fox_q_gain * scale, FOX_HEADS),
             jnp.tile(fox_k_gain, FOX_HEADS), jnp.tile(nsa_k_gain[1], NSA_KV_GROUPS),
             jnp.tile(nsa_k_gain[2], NSA_KV_GROUPS)]
    n_small_pad = LANES - ga.shape[1] - fb.shape[1]
    cols += [vb, vs, vw, kc, vc, ga, fb, jnp.zeros((D_MODEL, n_small_pad), F32)]
    w_packed = jnp.concatenate(cols, axis=1).astype(BF16)
    gain_row = jnp.concatenate(gains).reshape(1, P_NORM_END)
    small_bias = jnp.concatenate([jnp.zeros((ga.shape[1],), F32), b_forget,
                                  jnp.zeros((n_small_pad,), F32)]).reshape(1, LANES)
    return w_packed, gain_row, small_bias, gm.astype(BF16)


def _in_proj_kernel(x_ref, g_ref, w_ref, gain_ref, sbias_ref, bd64_ref,
                    qa_ref, qb_ref, kb_ref, ksw_ref, vb_ref, vsw_ref, kc_ref, vc_ref, small_ref, kv_scr):
    u = _rms_rows(x_ref[...], g_ref[...]).astype(BF16)
    tm = u.shape[0]

    y = _dot(u, w_ref[:, 0:P_NORM_END])
    n_chunks = P_NORM_END // NORM_CHUNK
    squares = jnp.concatenate([jnp.square(y[:, c * NORM_CHUNK:(c + 1) * NORM_CHUNK]).astype(BF16)
                               for c in range(n_chunks)], axis=0)
    ss = _dot(squares, bd64_ref[...])
    chunk = 0
    for o_ref in (qa_ref, qb_ref, kb_ref, ksw_ref):
        for i in range(o_ref.shape[1] // NORM_CHUNK):
            cols = slice(chunk * NORM_CHUNK, (chunk + 1) * NORM_CHUNK)
            inv = lax.rsqrt(ss[chunk * tm:(chunk + 1) * tm] * (1.0 / HEAD_DIM) + RMS_EPS)
            o_ref[:, i * NORM_CHUNK:(i + 1) * NORM_CHUNK] = (y[:, cols] * inv * gain_ref[:, cols]).astype(BF16)
            chunk += 1
    low = lax.broadcasted_iota(jnp.int32, (u.shape[0], LANES), 1) < HEAD_DIM

    def store_with_ones(v, o_ref):
        for i in range(v.shape[1] // LANES):
            pair = v[:, i * LANES:(i + 1) * LANES]
            o_ref[:, 2 * i * LANES:(2 * i + 1) * LANES] = jnp.where(low, pair, 1.0).astype(BF16)
            o_ref[:, (2 * i + 1) * LANES:(2 * i + 2) * LANES] = jnp.where(low, 1.0, pair).astype(BF16)

    rest = _dot(u, w_ref[:, P_NORM_END:P_END])
    store_with_ones(rest[:, PV_B - P_NORM_END:PV_SW - P_NORM_END], vb_ref)
    store_with_ones(rest[:, PV_SW - P_NORM_END:PKV_C - P_NORM_END], vsw_ref)
    for j, o_ref in enumerate((kc_ref, vc_ref)):
        c0 = PKV_C - P_NORM_END + j * NSA_KV_W
        kv_scr[j] = rest[:, c0:c0 + NSA_KV_W]
        for l in range(CMP_STRIDE):
            o_ref[0, :, l * NSA_KV_W:(l + 1) * NSA_KV_W] = kv_scr[j, pl.ds(l, tm // CMP_STRIDE,
                                                                           stride=CMP_STRIDE), :]
    z = rest[:, P_SMALL - P_NORM_END:] + sbias_ref[...]
    lane = lax.broadcasted_iota(jnp.int32, z.shape, 1)
    log_sig = jnp.minimum(z, 0.0) - jnp.log1p(jnp.exp(-jnp.abs(z)))
    small_ref[...] = jnp.where(lane < 3 * NSA_HEADS, jax.nn.sigmoid(z), log_sig)


def _in_proj(x2d, mix_norm, w_packed, gain_row, small_bias, batch, seq):
    n = x2d.shape[0]
    tm = _token_tile(seq, wide=True)
    steps_per_seq = seq // tm

    def row(width):
        return pl.BlockSpec((tm, width), lambda i: (i, 0))

    def out(width, dtype):
        return row(width), jax.ShapeDtypeStruct((n, width), dtype)

    grouped = (pl.BlockSpec((1, tm // CMP_STRIDE, CMP_STRIDE * NSA_KV_W),
                            lambda i: (i // steps_per_seq, i % steps_per_seq, 0)),
               jax.ShapeDtypeStruct((batch, seq // CMP_STRIDE, CMP_STRIDE * NSA_KV_W), F32))
    outs = [out(NSA_W, BF16), out(FOX_W, BF16), out(FOX_W, BF16), out(2 * NSA_KV_W, BF16),
            out(2 * FOX_W, BF16), out(4 * NSA_KV_W, BF16), grouped, grouped, out(LANES, F32)]
    return pl.pallas_call(
        _in_proj_kernel,
        grid=(n // tm,),
        in_specs=[row(D_MODEL), _const_spec((1, D_MODEL)), _const_spec((D_MODEL, P_END)),
                  _const_spec((1, P_NORM_END)), _const_spec((1, LANES)),
                  _const_spec((NORM_CHUNK, NORM_CHUNK))],
        out_specs=[spec for spec, _ in outs],
        out_shape=[shape for _, shape in outs],
        scratch_shapes=[pltpu.VMEM((2, tm, NSA_KV_W), F32)],
        compiler_params=_params(1),
        name="in_proj",
    )(x2d, mix_norm.reshape(1, D_MODEL), w_packed, gain_row, small_bias, _block_ones(HEAD_DIM))


def _pack_compress(pos, w1, w2):
    half = CMP_BLOCK // 2
    eye = jnp.eye(NSA_KV_GROUPS, dtype=F32)
    w1r = w1.reshape(CMP_BLOCK, HEAD_DIM, CMP_HIDDEN)

    def big(w):
        return jnp.einsum('ldh,pg->lpdgh', w, eye).reshape(half * NSA_KV_W, NSA_KV_GROUPS * CMP_HIDDEN)

    def posrow(p):
        return jnp.broadcast_to(p[:, None, :], (half, NSA_KV_GROUPS, HEAD_DIM)).reshape(1, half * NSA_KV_W)

    w2big = jnp.einsum('hd,pg->phgd', w2, eye).reshape(NSA_KV_GROUPS * CMP_HIDDEN, NSA_KV_W)
    return (posrow(pos[:half]), posrow(pos[half:]), big(w1r[:half]).astype(BF16),
            big(w1r[half:]).astype(BF16), w2big.astype(BF16))


def _compress_kernel(kc_ref, vc_ref, kp_lo, kp_hi, kw_lo, kw_hi, kw2, vp_lo, vp_hi, vw_lo, vw_hi, vw2,
                     kgain_ref, bd64_ref, ko_ref, vo_ref):
    n_seq, rows, width = kc_ref.shape

    def mlp(x_ref, p_lo, p_hi, w_lo, w_hi, w2):
        r = x_ref[...].reshape(n_seq * rows, width)
        first = _dot((r + p_lo[...]).astype(BF16), w_lo[...])
        second = _dot((r + p_hi[...]).astype(BF16), w_hi[...])
        h = first + pltpu.roll(second, second.shape[0] - 1, axis=0)
        return _dot((h * jax.nn.sigmoid(h)).astype(BF16), w2[...])

    k = mlp(kc_ref, kp_lo, kp_hi, kw_lo, kw_hi, kw2)
    ss = _split_dot(k * k, bd64_ref[...])
    k = (k * lax.rsqrt(ss * (1.0 / HEAD_DIM) + RMS_EPS) * kgain_ref[...]).astype(BF16)
    ko_ref[...] = k.reshape(ko_ref.shape)
    vo_ref[...] = mlp(vc_ref, vp_lo, vp_hi, vw_lo, vw_hi, vw2).astype(BF16).reshape(vo_ref.shape)


def _compress(kc, vc, batch, seq, k_pack, v_pack, k_gain):
    rows = seq // (CMP_BLOCK // 2)
    width = (CMP_BLOCK // 2) * NSA_KV_W
    per_step = next(n for n in (4, 2, 1) if batch % n == 0)
    blk = pl.BlockSpec((per_step, rows, width), lambda b: (b, 0, 0))
    out = pl.BlockSpec((per_step, rows, NSA_KV_W), lambda b: (b, 0, 0))
    pack_specs = [_const_spec((1, width)), _const_spec((1, width)),
                  _const_spec((width, NSA_KV_GROUPS * CMP_HIDDEN)),
                  _const_spec((width, NSA_KV_GROUPS * CMP_HIDDEN)),
                  _const_spec((NSA_KV_GROUPS * CMP_HIDDEN, NSA_KV_W))]
    bd = _block_ones(HEAD_DIM, NSA_KV_W)
    return pl.pallas_call(
        _compress_kernel,
        grid=(batch // per_step,),
        in_specs=[blk, blk] + pack_specs + pack_specs + [_const_spec((1, NSA_KV_W)),
                                                         _const_spec((NSA_KV_W, NSA_KV_W))],
        out_specs=[out, out],
        out_shape=[jax.ShapeDtypeStruct((batch, rows, NSA_KV_W), BF16)] * 2,
        compiler_params=_params(1),
        name="compress",
    )(kc.reshape(batch, rows, width), vc.reshape(batch, rows, width), *k_pack, *v_pack,
      jnp.tile(k_gain, NSA_KV_GROUPS).reshape(1, NSA_KV_W), bd)


FORGET_LANE = 3 * NSA_HEADS


def _cumsum_kernel(x_ref, cols_ref, rows_ref):
    n_seq, _, seq = rows_ref.shape
    x = x_ref[...]
    row = lax.broadcasted_iota(jnp.int32, x.shape, 0) % seq
    k = 1
    while k < seq:
        x = x + jnp.where(row >= k, pltpu.roll(x, k, axis=0), 0.0)
        k *= 2
    cols_ref[...] = x
    x_t = x.T
    for j in range(n_seq):
        rows_ref[j] = x_t[FORGET_LANE:FORGET_LANE + FOX_HEADS, j * seq:(j + 1) * seq]


def _cumsum(small, batch, seq):
    per_step = next(n for n in (4, 2, 1) if batch % n == 0)
    spec = pl.BlockSpec((per_step * seq, LANES), lambda b: (b, 0))
    return pl.pallas_call(
        _cumsum_kernel, grid=(batch // per_step,), in_specs=[spec],
        out_specs=[spec, pl.BlockSpec((per_step, FOX_HEADS, seq), lambda b: (b, 0, 0))],
        out_shape=[jax.ShapeDtypeStruct((batch * seq, LANES), F32),
                   jax.ShapeDtypeStruct((batch, FOX_HEADS, seq), F32)],
        compiler_params=_params(1), name="cumsum",
    )(small)


def _write_bias_rows(d, valid, tab_ref, o_ref, rows):
    max_exact = N_BUCKETS // 2
    n = jnp.maximum(d, 0)
    nf = jnp.maximum(n, 1).astype(F32)
    large = max_exact + (jnp.log(nf / max_exact) / math.log(MAX_DISTANCE / max_exact)
                         * (N_BUCKETS - max_exact)).astype(jnp.int32)
    bucket = jnp.where(n < max_exact, n, jnp.minimum(large, N_BUCKETS - 1))
    for h in range(NSA_HEADS):
        acc = jnp.zeros(d.shape, F32)
        for b in range(N_BUCKETS):
            acc = jnp.where(bucket == b, tab_ref[b, h], acc)
        g, r = divmod(h, NSA_Q_PER_GROUP)
        o_ref[0, g, r, rows, :] = jnp.where(valid, acc * LOG2E, NEG_INF)


def _write_far_tile(valid, tab_ref, o_ref):
    for h in range(NSA_HEADS):
        g, r = divmod(h, NSA_Q_PER_GROUP)
        far = jnp.where(valid, tab_ref[N_BUCKETS - 1, h], 0.0) * LOG2E
        o_ref[0, g, r] = jnp.where(valid, far, NEG_INF)


def _toeplitz_bias_kernel(tab_ref, o_ref):
    s = pl.program_id(0)
    tile = ATT_TILE

    def distances(n_rows):
        i = lax.broadcasted_iota(jnp.int32, (n_rows, tile), 0)
        j = lax.broadcasted_iota(jnp.int32, (n_rows, tile), 1)
        d = jnp.minimum(s, N_WIN_TILES - 1) * tile + i - j
        d_hi = jnp.where(s < N_WIN_TILES, WINDOW, jnp.int32(1 << 30))
        return d, (d >= 0) & (d < d_hi)

    _write_far_tile(distances(tile)[1], tab_ref, o_ref)
    for step in range(N_WIN_TILES + 1):
        near = min(tile, max(0, MAX_DISTANCE + tile - 1 - min(step, N_WIN_TILES - 1) * tile))
        near = -(-near // 8) * 8
        if near:
            @pl.when(s == step)
            def _(near=near):
                d, valid = distances(near)
                _write_bias_rows(d, valid, tab_ref, o_ref, slice(0, near))


def _cmp_bias_kernel(tab_ref, o_ref):
    t = pl.program_id(0)
    chunk = 8

    def distances(c0, n_rows):
        c = c0 + lax.broadcasted_iota(jnp.int32, (n_rows, ATT_TILE), 0)
        i = lax.broadcasted_iota(jnp.int32, (n_rows, ATT_TILE), 1)
        d = t * ATT_TILE + i - CMP_STRIDE * c - (CMP_BLOCK - 1)
        return d, (d >= 0) & (c < LANES - 1)

    _write_far_tile(distances(0, LANES)[1], tab_ref, o_ref)
    first = (t * ATT_TILE - (MAX_DISTANCE + CMP_BLOCK - 1)) // (chunk * CMP_STRIDE)
    n_chunks = (ATT_TILE + MAX_DISTANCE) // (chunk * CMP_STRIDE) + 2

    def refine(m, carry):
        c0 = pl.multiple_of(jnp.clip(first + m, 0, LANES // chunk - 1) * chunk, chunk)
        d, valid = distances(c0, chunk)
        _write_bias_rows(d, valid, tab_ref, o_ref, pl.ds(c0, chunk))
        return carry

    lax.fori_loop(0, n_chunks, refine, 0)


def _bias_tables(rel_bias_table, n_q_tiles):
    def call(body, steps, rows, name):
        shape = (steps, NSA_KV_GROUPS, NSA_Q_PER_GROUP, rows, ATT_TILE)
        return pl.pallas_call(
            body, grid=(steps,),
            in_specs=[pl.BlockSpec(memory_space=pltpu.SMEM)],
            out_specs=pl.BlockSpec((1,) + shape[1:], lambda s: (s, 0, 0, 0, 0)),
            out_shape=jax.ShapeDtypeStruct(shape, F32),
            compiler_params=_params(1), name=name,
        )(rel_bias_table)

    return (call(_toeplitz_bias_kernel, N_WIN_TILES + 1, ATT_TILE, "toeplitz_bias"),
            call(_cmp_bias_kernel, n_q_tiles, LANES, "cmp_bias"))


def _score_bound(q_gain, k_gain):
    return 1.02 * LOG2E * math.sqrt(HEAD_DIM) * jnp.max(jnp.abs(q_gain)) * jnp.max(jnp.abs(k_gain))


def _attention(chains, groups, bounds, acc_ref):
    n = len(chains)

    def later(k):
        return tuple(i for _, ids in groups[k:] for i in ids)

    def exact_max():
        def half_max(i, dl):
            s = chains[i][0](dl)
            return jnp.maximum(s[:, :LANES], s[:, LANES:])

        mx = [half_max(i, 0) for i in range(n)]
        lo = 1
        for k, (last, _) in enumerate(groups):
            members = later(k)

            def body(dl, carry, members=members):
                return tuple(jnp.maximum(c, half_max(i, dl)) for c, i in zip(carry, members))

            for i, r in zip(members, lax.fori_loop(lo, last + 1, body, tuple(mx[i] for i in members))):
                mx[i] = r
            lo = last + 1
        return tuple(jnp.broadcast_to(jnp.max(m, axis=-1, keepdims=True), (m.shape[0], ATT_TILE))
                     for m in mx)

    shift = exact_max() if bounds is None else [b() for b in bounds]

    def weights(i, dl):
        return jnp.exp2(chains[i][0](dl) - shift[i]).astype(BF16)

    def product(i, dl):
        return chains[i][1](weights(i, dl), dl)

    for i in range(n):
        acc_ref[i] = product(i, 0)
    lo = 1
    for k, (last, _) in enumerate(groups):
        members = later(k)
        count = last + 1 - lo

        def body(j, carry, members=members, lo=lo):
            dl = lo + 2 * j
            for i in members:
                both = jnp.concatenate([weights(i, dl + 1), weights(i, dl)], axis=1)
                acc_ref[i] += chains[i][1](both, dl, 2)
            return carry

        lax.fori_loop(0, count // 2, body, 0)

        @pl.when(count % 2 == 1)
        def _(members=members, last=last):
            for i in members:
                acc_ref[i] += product(i, last)

        lo = last + 1
    return [acc_ref[i] for i in range(n)]


def _pair_lanes(first, second, normalise):
    low = lax.broadcasted_iota(jnp.int32, first.shape, 1) < HEAD_DIM
    pair = jnp.where(low, first, second)
    if not normalise:
        return pair
    sums = pltpu.roll(jnp.where(low, second, first), HEAD_DIM, axis=1)
    return pair * (1.0 / sums)


def _select_blocks(imp_t, t0, tq):
    blk = lax.broadcasted_iota(jnp.int32, imp_t.shape, 0)
    cur = (t0 + (lax.broadcasted_iota(jnp.int32, imp_t.shape, 1) & (tq - 1))) // SLC_BLOCK
    forced = (blk == 0) | (blk == cur) | (blk == cur - 1)
    score = jnp.where(blk <= cur, imp_t + jnp.where(forced, FORCE_BONUS, 0.0), NEG_INF)
    blk_f = blk.astype(F32)
    dead = -3.0e38
    picked = jnp.zeros(imp_t.shape, F32)
    for _ in range(SLC_TOPK):
        best = jnp.max(score, axis=0, keepdims=True)
        first = jnp.min(jnp.where(score == best, blk_f, float(LANES)), axis=0, keepdims=True)
        hit = blk_f == first
        picked = jnp.where(hit, 1.0, picked)
        score = jnp.where(hit, dead, score)
    return picked


def _nsa_kernel(bnd_ref, qa_ref, kcmp_ref, vcmp_ref, ksw_ref, vsw_ref, small_ref, tb_ref, bc_ref, ov_ref,
                e_ref, gx_ref, o_ref, amask_ref, acc_ref, *, n_blocks, exact):
    qt = pl.program_id(1)
    tq = ATT_TILE
    rq = NSA_Q_PER_GROUP
    lane = lax.broadcasted_iota(jnp.int32, (tq, LANES), 1)
    own_lanes = [jnp.where((lane // HEAD_DIM) == g, 1.0, 0.0).astype(BF16) for g in range(NSA_KV_GROUPS)]

    def key_rows(dl, n=1):
        return pl.ds(pl.multiple_of((qt - dl - (n - 1)) * tq, tq), n * tq)

    groups = range(NSA_KV_GROUPS)
    n_seq = qa_ref.shape[0]
    n_grp = NSA_KV_GROUPS

    def win_chain(b, g, q):
        def score(dl):
            s = _dot_nt(q, ksw_ref[b, key_rows(dl), LANES:2 * LANES]).reshape(rq, tq, tq)
            return (s + tb_ref[dl, g]).reshape(rq * tq, tq)

        def pv(p, dl, n=1):
            c0 = (n_grp + g) * LANES
            return _dot(p, vsw_ref[b, key_rows(dl, n), c0:c0 + LANES])

        return score, pv

    def slc_chain(b, g, q):
        def score(dl):
            tile = dl if isinstance(dl, int) else jnp.where(dl < N_WIN_TILES - 1, dl, N_WIN_TILES)
            rows = key_rows(dl)
            s = _dot_nt(q, ksw_ref[b, rows, 0:LANES]).reshape(rq, tq, tq)
            return (s + tb_ref[tile, g] + amask_ref[b * n_grp + g, :, rows][None]).reshape(rq * tq, tq)

        def pv(p, dl, n=1):
            return _dot(p, vsw_ref[b, key_rows(dl, n), g * LANES:(g + 1) * LANES])

        return score, pv

    overlap = ov_ref[...].astype(BF16)
    key_block = e_ref[...].astype(BF16)
    gate_expand = [gx_ref[br].astype(BF16) for br in range(3)]

    slc_chains, win_chains, scores = [], [], []
    bias_c = jnp.concatenate([bc_ref[0, g, r] for g in groups for r in range(rq)], axis=1)
    for b in range(n_seq):
        qs = [jnp.concatenate([qa_ref[b, :, r * LANES:(r + 1) * LANES] * own_lanes[g] for r in range(rq)],
                              axis=0) for g in groups]
        slc_chains += [slc_chain(b, g, qs[g]) for g in groups]
        win_chains += [win_chain(b, g, qs[g]) for g in groups]
        scores.append(_dot_nt(kcmp_ref[b], jnp.concatenate(qs, axis=0)) + bias_c)

    s = jnp.concatenate(scores, axis=1)
    if exact:
        m = jnp.max(s, axis=0, keepdims=True)
        p = jnp.where(s > 0.5 * NEG_INF, jnp.exp2(s - m), 0.0)
    else:
        p = jnp.exp2(s - bnd_ref[2])
    l = jnp.sum(p, axis=0, keepdims=True)
    p_c = p * (1.0 / jnp.where(l > 0.0, l, 1.0))
    per_seq = n_grp * rq * tq
    o_cmps = []
    for b in range(n_seq):
        o_cmp = _dot(p_c[:, b * per_seq:(b + 1) * per_seq].T.astype(BF16), vcmp_ref[b])
        o_cmps.append([o_cmp[g * rq * tq:(g + 1) * rq * tq] for g in groups])

    p_sum = jnp.concatenate([sum(p_c[:, (j * rq + r) * tq:(j * rq + r + 1) * tq] for r in range(rq))
                             for j in range(n_seq * n_grp)], axis=1)
    hi = p_sum.astype(BF16)
    lo = (p_sum - hi.astype(F32)).astype(BF16)
    imp_t = _dot(overlap, hi) + _dot(overlap, lo)
    sel_t = _select_blocks(imp_t[0:n_blocks], qt * tq, tq)
    block_bias = jnp.concatenate([jnp.where(sel_t > 0.0, 0.0, NEG_INF),
                                  jnp.zeros((LANES - n_blocks, n_seq * n_grp * tq), F32)], axis=0)
    amask_ref[...] = _dot(block_bias.T.astype(BF16), key_block).reshape(amask_ref.shape)

    n_slc = len(slc_chains)
    slc_ids = tuple(range(n_slc))
    win_ids = tuple(range(n_slc, 2 * n_slc))
    bounds = None if exact else [lambda i=i: jnp.full((rq * tq, tq), bnd_ref[i // n_slc], F32)
                                 for i in range(2 * n_slc)]
    accs = _attention(slc_chains + win_chains,
                      [(jnp.minimum(qt, N_WIN_TILES - 1), win_ids), (qt, slc_ids)], bounds, acc_ref)

    for b in range(n_seq):
        gates = small_ref[b]
        g_hi = gates.astype(BF16)
        g_lo = (gates - g_hi.astype(F32)).astype(BF16)
        out = [jnp.zeros((tq, LANES), F32) for _ in range(rq)]
        lo_, hi_ = b * n_grp, (b + 1) * n_grp
        for br, per_group in enumerate((o_cmps[b], accs[lo_:hi_], accs[n_slc + lo_:n_slc + hi_])):
            gate = _dot(g_hi, gate_expand[br]) + _dot(g_lo, gate_expand[br])
            for r in range(rq):
                first, second = (a[r * tq:(r + 1) * tq] for a in per_group)
                pair = _pair_lanes(first, second, normalise=br > 0)
                out[r] = out[r] + gate[:, r * LANES:(r + 1) * LANES] * pair
        for r in range(rq):
            o_ref[b, :, r * LANES:(r + 1) * LANES] = out[r].astype(BF16)


def _nsa(bounds, qa, kcmp, vcmp, ksw, vsw, small, toeplitz, cmp_bias, batch, seq, exact):
    tq = ATT_TILE
    nq = seq // tq
    rq = NSA_Q_PER_GROUP
    n_blocks = seq // SLC_BLOCK
    n_cmp = (seq - CMP_BLOCK) // CMP_STRIDE + 1
    ci = np.arange(LANES)[:, None] * CMP_STRIDE
    sj = np.arange(LANES)[None, :] * SLC_BLOCK
    overlap = ((ci <= sj + SLC_BLOCK - 1) & (ci + CMP_BLOCK - 1 >= sj)
               & (np.arange(LANES)[:, None] < n_cmp) & (np.arange(LANES)[None, :] < n_blocks))
    expand = np.arange(LANES)[:, None] == (np.arange(seq)[None, :] // SLC_BLOCK)
    gate_expand = np.zeros((3, LANES, NSA_W), np.float32)
    for b in range(3):
        for g in range(NSA_KV_GROUPS):
            for r in range(rq):
                c0 = r * LANES + g * HEAD_DIM
                gate_expand[b, b * NSA_HEADS + g * rq + r, c0:c0 + HEAD_DIM] = 1.0
    per_step = 2 if batch % 2 == 0 else 1

    def rows(n_rows, width):
        return pl.BlockSpec((per_step, n_rows, width), lambda b, t: (b, 0, 0))

    def tile(width):
        return pl.BlockSpec((per_step, tq, width), lambda b, t: (b, t, 0))

    out = pl.pallas_call(
        functools.partial(_nsa_kernel, n_blocks=n_blocks, exact=exact),
        grid=(batch // per_step, nq),
        in_specs=[
            pl.BlockSpec(memory_space=pltpu.SMEM),
            tile(NSA_W),
            rows(LANES, NSA_KV_W),
            rows(LANES, NSA_KV_W),
            rows(seq, 2 * NSA_KV_W),
            rows(seq, 2 * NSA_KV_GROUPS * LANES),
            tile(LANES),
            _const_spec((N_WIN_TILES + 1, NSA_KV_GROUPS, rq, tq, tq)),
            pl.BlockSpec((1, NSA_KV_GROUPS, rq, LANES, tq), lambda b, t: (t, 0, 0, 0, 0)),
            _const_spec((LANES, LANES)),
            _const_spec((LANES, seq)),
            _const_spec((3, LANES, NSA_W)),
        ],
        out_specs=tile(NSA_W),
        out_shape=jax.ShapeDtypeStruct((batch, seq, NSA_W), BF16),
        scratch_shapes=[pltpu.VMEM((per_step * NSA_KV_GROUPS, tq, seq), F32),
                        pltpu.VMEM((per_step * 2 * NSA_KV_GROUPS, rq * tq, LANES), F32)],
        compiler_params=_params(2),
        name="nsa",
    )(bounds, qa.reshape(batch, seq, NSA_W), kcmp, vcmp, ksw.reshape(batch, seq, 2 * NSA_KV_W),
      vsw.reshape(batch, seq, 2 * NSA_KV_GROUPS * LANES), small.reshape(batch, seq, LANES),
      toeplitz, cmp_bias,
      jnp.asarray(overlap.T, F32), jnp.asarray(expand, F32), jnp.asarray(gate_expand, F32))
    return out.reshape(batch * seq, NSA_W)


def _fox_kernel(bnd_ref, q_ref, k_ref, v_ref, cum_ref, cum_t_ref, o_ref, acc_ref, *, exact):
    qt = pl.program_id(1)
    tq = ATT_TILE
    pairs = FOX_HEADS // 2
    row = lax.broadcasted_iota(jnp.int32, (2, tq, tq), 1)
    col = lax.broadcasted_iota(jnp.int32, (2, tq, tq), 2)
    lane = lax.broadcasted_iota(jnp.int32, (tq, LANES), 1)
    low = jnp.where(lane < HEAD_DIM, 1.0, 0.0).astype(BF16)
    high = jnp.where(lane < HEAD_DIM, 0.0, 1.0).astype(BF16)
    n_seq = q_ref.shape[0]

    def key_rows(dl, n=1):
        return pl.ds(pl.multiple_of((qt - dl - (n - 1)) * tq, tq), n * tq)

    bounds = []

    def pair_chain(b, p):
        pair = q_ref[b, :, p * LANES:(p + 1) * LANES]
        q = jnp.concatenate([pair * low, pair * high], axis=0)
        base = cum_ref[b, 2 * p:2 * p + 2, pl.ds(pl.multiple_of(qt * tq, tq), LANES)][:, 0:1]

        def decay_of(rows):
            return (base - cum_ref[b, 2 * p:2 * p + 2, rows]) * LOG2E

        def bound():
            cum_rows = cum_t_ref[b]
            c0 = FORGET_LANE + 2 * p
            own = jnp.concatenate([base[h:h + 1, :] - cum_rows[:, c0 + h:c0 + h + 1] for h in range(2)],
                                  axis=0)
            return jnp.broadcast_to(own * LOG2E + bnd_ref[0], (2 * tq, tq))

        bounds.append(bound)

        def score(dl):
            rows = key_rows(dl)
            decay = decay_of(rows)
            s = _dot_nt(q, k_ref[b, rows, p * LANES:(p + 1) * LANES]).reshape(2, tq, tq) + decay[:, None, :]
            if isinstance(dl, int):
                s = jnp.where(col <= row, s, NEG_INF)
            return s.reshape(2 * tq, tq)

        def pv(w, dl, n=1):
            rows = key_rows(dl, n)
            return jnp.concatenate(
                [_dot(w[0:tq], v_ref[b, rows, 2 * p * LANES:(2 * p + 1) * LANES]),
                 _dot(w[tq:2 * tq], v_ref[b, rows, (2 * p + 1) * LANES:(2 * p + 2) * LANES])], axis=0)

        return score, pv

    chains = [pair_chain(b, p) for b in range(n_seq) for p in range(pairs)]
    accs = _attention(chains, [(qt, tuple(range(len(chains))))], None if exact else bounds, acc_ref)
    for i, acc in enumerate(accs):
        b, p = divmod(i, pairs)
        o_ref[b, :, p * LANES:(p + 1) * LANES] = _pair_lanes(acc[0:tq], acc[tq:2 * tq], True).astype(BF16)


def _fox(bounds, qb, kb, vb, cum, cum_cols, batch, seq, exact):
    tq = ATT_TILE
    nq = seq // tq
    per_step = 2 if batch % 2 == 0 else 1
    out = pl.pallas_call(
        functools.partial(_fox_kernel, exact=exact),
        grid=(batch // per_step, nq),
        in_specs=[
            pl.BlockSpec(memory_space=pltpu.SMEM),
            pl.BlockSpec((per_step, tq, FOX_W), lambda b, t: (b, t, 0)),
            pl.BlockSpec((per_step, seq, FOX_W), lambda b, t: (b, 0, 0)),
            pl.BlockSpec((per_step, seq, FOX_HEADS * LANES), lambda b, t: (b, 0, 0)),
            pl.BlockSpec((per_step, FOX_HEADS, seq), lambda b, t: (b, 0, 0)),
            pl.BlockSpec((per_step, tq, LANES), lambda b, t: (b, t, 0)),
        ],
        out_specs=pl.BlockSpec((per_step, tq, FOX_W), lambda b, t: (b, t, 0)),
        out_shape=jax.ShapeDtypeStruct((batch, seq, FOX_W), BF16),
        scratch_shapes=[pltpu.VMEM((per_step * FOX_HEADS // 2, 2 * tq, LANES), F32)],
        compiler_params=_params(2),
        name="fox",
    )(bounds, qb.reshape(batch, seq, FOX_W), kb.reshape(batch, seq, FOX_W),
      vb.reshape(batch, seq, FOX_HEADS * LANES), cum, cum_cols.reshape(batch, seq, LANES))
    return out.reshape(batch * seq, FOX_W)


def _merge_ffn_kernel(x_ref, on_ref, of_ref, gmix_ref, wgm_ref, won_ref, wof_ref, wout_ref,
                      g2_ref, wup_ref, wdn_ref, o_ref):
    x = x_ref[...]
    u = _rms_rows(x, gmix_ref[...]).astype(BF16)
    gate = jax.nn.sigmoid(_dot(u, wgm_ref[...]))
    merged = (gate[:, :D_MODEL] * _dot(on_ref[...], won_ref[...])
              + gate[:, D_MODEL:] * _dot(of_ref[...], wof_ref[...]))
    x2 = x + _dot(merged.astype(BF16), wout_ref[...])
    o_ref[...] = _swiglu_residual(x2, g2_ref[...], wup_ref, wdn_ref)


def _merge_ffn(x2d, o_nsa, o_fox, mix_norm, w_gm, w_o_nsa, w_o_fox, w_out, gain2, w_up, w_down):
    n = x2d.shape[0]
    tm = _token_tile(n, wide=False)

    def row(width):
        return pl.BlockSpec((tm, width), lambda i: (i, 0))

    return pl.pallas_call(
        _merge_ffn_kernel,
        grid=(n // tm,),
        in_specs=[row(D_MODEL), row(NSA_W), row(FOX_W), _const_spec((1, D_MODEL)),
                  _const_spec((D_MODEL, 2 * D_MODEL)), _const_spec((NSA_W, D_MODEL)),
                  _const_spec((FOX_W, D_MODEL)), _const_spec((D_MODEL, D_MODEL)),
                  _const_spec((1, D_MODEL)), _const_spec((D_MODEL, 2 * D_FF)),
                  _const_spec((D_FF, D_MODEL))],
        out_specs=row(D_MODEL),
        out_shape=jax.ShapeDtypeStruct((n, D_MODEL), F32),
        compiler_params=_params(1),
        name="merge_ffn",
    )(x2d, o_nsa, o_fox, mix_norm.reshape(1, D_MODEL), w_gm, w_o_nsa.astype(BF16),
      w_o_fox.astype(BF16), w_out.astype(BF16), gain2.reshape(1, D_MODEL), w_up.astype(BF16),
      w_down.astype(BF16))


def _layer(x, ffn1_norm, ffn1_w_up, ffn1_w_down, mix_norm, w_in, b_forget, nsa_q_gain, nsa_k_gain,
           fox_q_gain, fox_k_gain, cmp_pos_k, cmp_pos_v, cmp_k_w1, cmp_k_w2, cmp_v_w1, cmp_v_w2,
           w_o_nsa, w_o_fox, w_out, ffn2_norm, ffn2_w_up, ffn2_w_down, rel_bias_table):
    batch, seq, d = x.shape
    assert d == D_MODEL and seq % ATT_TILE == 0 and (batch * seq) % TOKEN_TILE == 0
    assert seq // SLC_BLOCK <= LANES and (seq - CMP_BLOCK) // CMP_STRIDE + 1 == LANES - 1
    assert seq // (CMP_BLOCK // 2) == LANES
    x2d = x.reshape(batch * seq, D_MODEL)

    x1 = _ffn(x2d, ffn1_norm, ffn1_w_up, ffn1_w_down)

    w_packed, gain_row, small_bias, w_gm = _pack_in_proj(w_in, b_forget, nsa_q_gain, nsa_k_gain,
                                                         fox_q_gain, fox_k_gain)
    qa, qb, kb, ksw, vb, vsw, kc, vc, small = _in_proj(x1, mix_norm, w_packed, gain_row, small_bias,
                                                       batch, seq)

    kcmp, vcmp = _compress(kc, vc, batch, seq, _pack_compress(cmp_pos_k, cmp_k_w1, cmp_k_w2),
                           _pack_compress(cmp_pos_v, cmp_v_w1, cmp_v_w2), nsa_k_gain[0])
    b_slc = _score_bound(nsa_q_gain, nsa_k_gain[1])
    b_win = _score_bound(nsa_q_gain, nsa_k_gain[2])
    t_hi, t_lo = jnp.max(rel_bias_table) * LOG2E, jnp.min(rel_bias_table) * LOG2E
    b_cmp = _score_bound(nsa_q_gain, nsa_k_gain[0])
    nsa_ok = 2.0 * jnp.maximum(jnp.maximum(b_slc, b_win), b_cmp) + (t_hi - t_lo) < EXP2_RANGE
    nsa_bounds = jnp.stack([b_slc + t_hi, b_win + t_hi, b_cmp + t_hi])
    def nsa_branch(exact):
        toeplitz, cmp_bias = _bias_tables(rel_bias_table, seq // ATT_TILE)
        return _nsa(nsa_bounds, qa, kcmp, vcmp, ksw, vsw, small, toeplitz, cmp_bias, batch, seq, exact)

    o_nsa = lax.cond(nsa_ok, lambda: nsa_branch(False), lambda: nsa_branch(True))

    cum_cols, cum = _cumsum(small, batch, seq)
    b_fox = _score_bound(fox_q_gain, fox_k_gain)
    fox_bounds = jnp.stack([b_fox])
    o_fox = lax.cond(2.0 * b_fox < EXP2_RANGE,
                     lambda: _fox(fox_bounds, qb, kb, vb, cum, cum_cols, batch, seq, False),
                     lambda: _fox(fox_bounds, qb, kb, vb, cum, cum_cols, batch, seq, True))

    w_o_nsa_p = w_o_nsa.reshape(NSA_KV_GROUPS, NSA_Q_PER_GROUP, HEAD_DIM, D_MODEL).transpose(1, 0, 2, 3)
    out = _merge_ffn(x1, o_nsa, o_fox, mix_norm, w_gm, w_o_nsa_p.reshape(NSA_W, D_MODEL), w_o_fox,
                     w_out, ffn2_norm, ffn2_w_up, ffn2_w_down)
    return out.reshape(batch, seq, D_MODEL)


def kernel(x, ffn1_norm, ffn1_w_up, ffn1_w_down, mix_norm, w_in, b_forget, nsa_q_gain, nsa_k_gain,
           fox_q_gain, fox_k_gain, cmp_pos_k, cmp_pos_v, cmp_k_w1, cmp_k_w2, cmp_v_w1, cmp_v_w2,
           w_o_nsa, w_o_fox, w_out, ffn2_norm, ffn2_w_up, ffn2_w_down, rel_bias_table):
    for layer in range(ffn1_norm.shape[0]):
        x = _layer(x, ffn1_norm[layer], ffn1_w_up[layer], ffn1_w_down[layer], mix_norm[layer],
                   w_in[layer], b_forget[layer], nsa_q_gain[layer], nsa_k_gain[layer],
                   fox_q_gain[layer], fox_k_gain[layer], cmp_pos_k[layer], cmp_pos_v[layer],
                   cmp_k_w1[layer], cmp_k_w2[layer], cmp_v_w1[layer], cmp_v_w2[layer],
                   w_o_nsa[layer], w_o_fox[layer], w_out[layer], ffn2_norm[layer],
                   ffn2_w_up[layer], ffn2_w_down[layer], rel_bias_table)
    return x
```
